```python
import math
import jax, jax.numpy as jnp
from jax import lax
import numpy as np

D_MODEL = 2048
BATCH = 2
SEQ = 8192
DEPTH = 2

PLE_DIM = 256
D_FF = 4 * D_MODEL
NORM_EPS = 1e-6
ROPE_THETA = 500000.0
ROPE_FRACTION = 4
QBLK = 128
A_WIDTH = D_MODEL // 2
A_DV = 128
A_HEADS = A_WIDTH // A_DV
A_DC = A_DV // 2
A_IN = 3 * A_WIDTH
DIFF_SUBLN_EPS = 1e-5
B_WIDTH = D_MODEL // 2
B_HEAD = 64
B_HEADS = B_WIDTH // B_HEAD
DECAY_LORA = max(32, int(round(1.8 * D_MODEL ** 0.5 / 32)) * 32)
AAA_LORA = max(32, int(round(1.8 * D_MODEL ** 0.5 / 32)) * 32)
GATE_LORA = max(32, int(round(0.6 * D_MODEL ** 0.8 / 32)) * 32)
B_IN = 3 * B_WIDTH + DECAY_LORA + AAA_LORA + GATE_LORA
LNX_EPS = 64e-5
AB_IN = A_IN + B_IN
C_HEADS = 16
C_HEAD_DIM = D_MODEL // C_HEADS
C_KV_HEADS = 4
C_REP = C_HEADS // C_KV_HEADS
C_WIDTH = C_HEADS * C_HEAD_DIM
IDX_HEADS = 16
IDX_DIM = 64
TOPK_MAX = 256
LN_EPS = 1e-6
C_IN = C_WIDTH + 2 * C_KV_HEADS * C_HEAD_DIM + IDX_HEADS * IDX_DIM + IDX_DIM + IDX_HEADS

kernel_name = 'hybrid_diffattn_rwkv7_dsa_block'


def rms_norm(x, g, eps=NORM_EPS):
    xf = x.astype(jnp.float32)
    y = xf * lax.rsqrt(jnp.mean(xf * xf, axis=-1, keepdims=True) + eps)
    return (y * g.astype(jnp.float32)).astype(x.dtype)


def layer_norm(x, g, b, eps=LN_EPS):
    xf = x.astype(jnp.float32)
    mu = jnp.mean(xf, axis=-1, keepdims=True)
    var = jnp.mean(jnp.square(xf - mu), axis=-1, keepdims=True)
    y = (xf - mu) * lax.rsqrt(var + eps)
    return (y * g.astype(jnp.float32) + b.astype(jnp.float32)).astype(x.dtype)


def partial_rope(x, pos):
    dh = x.shape[-1]
    rot = dh // ROPE_FRACTION
    half = rot // 2
    inv_freq = ROPE_THETA ** (-(jnp.arange(half, dtype=jnp.float32) * 2.0 / rot))
    ang = pos.astype(jnp.float32)[..., None] * inv_freq
    cos = jnp.cos(ang)[:, :, None, :]
    sin = jnp.sin(ang)[:, :, None, :]
    x1 = x[..., :half].astype(jnp.float32)
    x2 = x[..., half:rot].astype(jnp.float32)
    r1 = (x1 * cos - x2 * sin).astype(x.dtype)
    r2 = (x2 * cos + x1 * sin).astype(x.dtype)
    return jnp.concatenate([r1, r2, x[..., rot:]], axis=-1)


def to_blocks(t, nb):
    return jnp.moveaxis(t.reshape((t.shape[0], nb, QBLK) + t.shape[2:]), 1, 0)


def diff_attention(q, k, v, lam, subln_g, lambda_init):
    B, T = q.shape[0], q.shape[1]
    nb = T // QBLK
    qb = to_blocks(q * (A_DC ** -0.5), nb)
    kpos = jnp.arange(T)

    def block(args):
        qblk, start = args
        s = jnp.einsum('bqhcd,bshcd->bhcqs', qblk, k).astype(jnp.float32)
        qpos = start + jnp.arange(QBLK)
        causal = kpos[None, :] <= qpos[:, None]
        s = jnp.where(causal, s, -jnp.inf)
        pm = jax.nn.softmax(s, axis=-1)
        attn = pm[:, :, 0] - lam * pm[:, :, 1]
        return jnp.einsum('bhqs,bshd->bqhd', attn.astype(v.dtype), v)

    out = lax.map(block, (qb, jnp.arange(nb) * QBLK))
    out = jnp.moveaxis(out, 0, 1).reshape(B, T, A_HEADS, A_DV)
    out = rms_norm(out, subln_g, DIFF_SUBLN_EPS) * (1.0 - lambda_init)
    return out.reshape(B, T, A_WIDTH)


def wkv7_scan(r, w, k, v, a, b):
    B, T, H, N = r.shape
    xs = tuple(jnp.moveaxis(t.astype(jnp.float32), 1, 0) for t in (r, w, k, v, a, b))

    def step(S, inp):
        r_t, w_t, k_t, v_t, a_t, b_t = inp
        sa = jnp.einsum('bhvk,bhk->bhv', S, a_t)
        S = S * w_t[:, :, None, :] + sa[..., None] * b_t[:, :, None, :] + v_t[..., None] * k_t[:, :, None, :]
        y = jnp.einsum('bhvk,bhk->bhv', S, r_t)
        return S, y

    S0 = jnp.zeros((B, H, N, N), jnp.float32)
    _, ys = lax.scan(step, S0, xs)
    return jnp.moveaxis(ys, 0, 1)


def rwkv7_time_mix(zb, mu, w0, w2, a0, a2, g2, k_k, k_a, r_k, lnx_g, lnx_b):
    B, T, _ = zb.shape
    prev = jnp.pad(zb, ((0, 0), (1, 0), (0, 0)))[:, :T]
    zb = zb + (prev - zb) * mu
    cuts = [B_WIDTH, 2 * B_WIDTH, 3 * B_WIDTH, 3 * B_WIDTH + DECAY_LORA, 3 * B_WIDTH + DECAY_LORA + AAA_LORA]
    r, k, v, xw, xa, xg = jnp.split(zb, cuts, axis=-1)
    w = -jax.nn.softplus(-(w0 + jnp.tanh(xw) @ w2)) - 0.5
    a = jax.nn.sigmoid(a0 + xa @ a2)
    g = jax.nn.sigmoid(xg) @ g2
    heads = lambda t: t.reshape(B, T, B_HEADS, B_HEAD)
    kk = heads(k * k_k).astype(jnp.float32)
    kk = kk / jnp.maximum(jnp.sqrt(jnp.sum(kk * kk, axis=-1, keepdims=True)), 1e-12)
    k = k * (1.0 + (a - 1.0) * k_a)
    decay = jnp.exp(-jnp.exp(w.astype(jnp.float32)))
    y = wkv7_scan(heads(r), heads(decay), heads(k), heads(v), -kk, kk * heads(a).astype(jnp.float32))
    mu_y = jnp.mean(y, axis=-1, keepdims=True)
    var_y = jnp.mean(jnp.square(y - mu_y), axis=-1, keepdims=True)
    y = (y - mu_y) * lax.rsqrt(var_y + LNX_EPS)
    y = y * lnx_g.reshape(B_HEADS, B_HEAD).astype(jnp.float32) + lnx_b.reshape(B_HEADS, B_HEAD).astype(jnp.float32)
    bonus = jnp.sum(heads(r) * heads(k) * r_k, axis=-1, keepdims=True) * heads(v)
    y = (y + bonus.astype(jnp.float32)).reshape(B, T, B_WIDTH) * g.astype(jnp.float32)
    return y.astype(zb.dtype)


def mixer_ab(u, pos, w_in, w_out, lq1, lk1, lq2, lk2, subln_g, mu, w0, w2, a0, a2, g2,
             k_k, k_a, r_k, lnx_g, lnx_b, lambda_init):
    B, T, _ = u.shape
    z = u @ w_in
    za, zb = z[..., :A_IN], z[..., A_IN:]
    qa, ka, va = jnp.split(za, 3, axis=-1)
    qa = partial_rope(qa.reshape(B, T, 2 * A_HEADS, A_DC), pos).reshape(B, T, A_HEADS, 2, A_DC)
    ka = partial_rope(ka.reshape(B, T, 2 * A_HEADS, A_DC), pos).reshape(B, T, A_HEADS, 2, A_DC)
    va = va.reshape(B, T, A_HEADS, A_DV)
    lam = (jnp.exp(jnp.sum(lq1.astype(jnp.float32) * lk1.astype(jnp.float32)))
           - jnp.exp(jnp.sum(lq2.astype(jnp.float32) * lk2.astype(jnp.float32))) + lambda_init)
    ya = diff_attention(qa, ka, va, lam, subln_g, lambda_init)
    yb = rwkv7_time_mix(zb, mu, w0, w2, a0, a2, g2, k_k, k_a, r_k, lnx_g, lnx_b)
    return jnp.concatenate([ya.astype(u.dtype), yb.astype(u.dtype)], axis=-1) @ w_out


def mixer_c(u, pos, w_in, w_out, idx_k_g, idx_k_b, topk):
    B, T, _ = u.shape
    z = u @ w_in
    c1 = C_WIDTH
    c2 = c1 + C_KV_HEADS * C_HEAD_DIM
    c3 = c2 + C_KV_HEADS * C_HEAD_DIM
    c4 = c3 + IDX_HEADS * IDX_DIM
    c5 = c4 + IDX_DIM
    q, k, v, qi, ki, wi = jnp.split(z, [c1, c2, c3, c4, c5], axis=-1)
    q = partial_rope(q.reshape(B, T, C_HEADS, C_HEAD_DIM), pos)
    k = partial_rope(k.reshape(B, T, C_KV_HEADS, C_HEAD_DIM), pos)
    v = v.reshape(B, T, C_KV_HEADS, C_HEAD_DIM)
    qi = partial_rope(qi.reshape(B, T, IDX_HEADS, IDX_DIM), pos)
    ki = partial_rope(layer_norm(ki, idx_k_g, idx_k_b)[:, :, None, :], pos)[:, :, 0]
    wi = wi * ((IDX_HEADS * IDX_DIM) ** -0.5)
    nb = T // QBLK
    kpos = jnp.arange(T)
    scale = C_HEAD_DIM ** -0.5
    gather = jax.vmap(lambda src, ix: src[ix])

    def block(args):
        qb, qib, wib, start = args
        qpos = start + jnp.arange(QBLK)
        causal = kpos[None, :] <= qpos[:, None]
        isc = jax.nn.relu(jnp.einsum('bqhd,bsd->bqhs', qib, ki))
        score = jnp.einsum('bqhs,bqh->bqs', isc, wib).astype(jnp.float32)
        score = jnp.where(causal[None], score, -jnp.inf)
        _, sel = lax.top_k(score, topk)
        valid = sel <= qpos[None, :, None]
        ks = gather(k, sel)
        vs = gather(v, sel)
        qg = qb.reshape(B, QBLK, C_KV_HEADS, C_REP, C_HEAD_DIM)
        s = jnp.einsum('bqgrd,bqkgd->bqgrk', qg, ks).astype(jnp.float32) * scale
        s = jnp.where(valid[:, :, None, None, :], s, -jnp.inf)
        pr = jax.nn.softmax(s, axis=-1)
        o = jnp.einsum('bqgrk,bqkgd->bqgrd', pr.astype(vs.dtype), vs)
        return o.reshape(B, QBLK, C_WIDTH)

    o = lax.map(block, (to_blocks(q, nb), to_blocks(qi, nb), to_blocks(wi, nb), jnp.arange(nb) * QBLK))
    o = jnp.moveaxis(o, 0, 1).reshape(B, T, C_WIDTH)
    return o @ w_out


def sq_relu_mlp(u, w_up, w_down):
    return jnp.square(jax.nn.relu(u @ w_up)) @ w_down


def setup_inputs(seed: int = 0) -> dict:
    key = jax.random.key(seed)
    ks = iter(jax.random.split(key, 48))
    f32 = jnp.float32
    NE = (DEPTH + 1) // 2
    NO = DEPTH // 2

    def dense(shape, fan_in, scale=1.0):
        return jax.random.normal(next(ks), shape, f32) * (scale * fan_in ** -0.5)

    def gain(shape):
        return 1.0 + 0.05 * jax.random.normal(next(ks), shape, f32)

    def small(shape, s=0.02):
        return s * jax.random.normal(next(ks), shape, f32)

    x = jax.random.normal(next(ks), (BATCH, SEQ, D_MODEL), f32)
    p = jax.random.normal(next(ks), (DEPTH, BATCH, SEQ, PLE_DIM), f32)
    offsets = jax.random.randint(next(ks), (BATCH, 1), 0, 4096, dtype=jnp.int32)
    positions = offsets + jnp.arange(SEQ, dtype=jnp.int32)[None, :]
    return {
        'x': x,
        'p': p,
        'positions': positions,
        'mix_pre_g': gain((DEPTH, D_MODEL)),
        'mix_post_g': gain((DEPTH, D_MODEL)),
        'mlp_pre_g': gain((DEPTH, D_MODEL)),
        'mlp_post_g': gain((DEPTH, D_MODEL)),
        'w_mlp_up': dense((DEPTH, D_MODEL, D_FF), D_MODEL),
        'w_mlp_down': dense((DEPTH, D_FF, D_MODEL), D_FF),
        'w_ple_proj': dense((DEPTH, PLE_DIM, D_MODEL), PLE_DIM),
        'w_ple_gate': dense((DEPTH, D_MODEL, D_MODEL), D_MODEL),
        'ple_post_g': gain((DEPTH, D_MODEL)),
        'ab_w_in': dense((NE, D_MODEL, AB_IN), D_MODEL),
        'ab_w_out': dense((NE, A_WIDTH + B_WIDTH, D_MODEL), A_WIDTH + B_WIDTH),
        'diff_lq1': small((NE, A_DC), 0.1),
        'diff_lk1': small((NE, A_DC), 0.1),
        'diff_lq2': small((NE, A_DC), 0.1),
        'diff_lk2': small((NE, A_DC), 0.1),
        'diff_subln_g': gain((NE, A_DV)),
        'rwkv_mu': jax.random.uniform(next(ks), (NE, B_IN), f32, 0.0, 1.0),
        'rwkv_w0': jax.random.uniform(next(ks), (NE, B_WIDTH), f32, -5.0, 1.0),
        'rwkv_w2': dense((NE, DECAY_LORA, B_WIDTH), DECAY_LORA, 0.1),
        'rwkv_a0': small((NE, B_WIDTH), 0.1),
        'rwkv_a2': dense((NE, AAA_LORA, B_WIDTH), AAA_LORA, 0.1),
        'rwkv_g2': dense((NE, GATE_LORA, B_WIDTH), GATE_LORA),
        'rwkv_k_k': 0.85 + small((NE, B_WIDTH), 0.05),
        'rwkv_k_a': gain((NE, B_WIDTH)),
        'rwkv_r_k': small((NE, B_HEADS, B_HEAD), 0.1),
        'rwkv_lnx_g': gain((NE, B_WIDTH)),
        'rwkv_lnx_b': small((NE, B_WIDTH), 0.02),
        'c_w_in': dense((NO, D_MODEL, C_IN), D_MODEL),
        'c_w_out': dense((NO, C_WIDTH, D_MODEL), C_WIDTH),
        'idx_k_g': gain((NO, IDX_DIM)),
        'idx_k_b': small((NO, IDX_DIM), 0.02),
    }


def reference(x, p, positions, mix_pre_g, mix_post_g, mlp_pre_g, mlp_post_g, w_mlp_up, w_mlp_down,
              w_ple_proj, w_ple_gate, ple_post_g, ab_w_in, ab_w_out, diff_lq1, diff_lk1, diff_lq2,
              diff_lk2, diff_subln_g, rwkv_mu, rwkv_w0, rwkv_w2, rwkv_a0, rwkv_a2, rwkv_g2, rwkv_k_k,
              rwkv_k_a, rwkv_r_k, rwkv_lnx_g, rwkv_lnx_b, c_w_in, c_w_out, idx_k_g, idx_k_b):
    T = x.shape[1]
    topk = min(TOPK_MAX, T // 4)
    h = x
    for i in range(DEPTH):
        j = i // 2
        u = rms_norm(h, mix_pre_g[i])
        if i % 2 == 0:
            lambda_init = 0.8 - 0.6 * math.exp(-0.3 * i)
            m = mixer_ab(u, positions, ab_w_in[j], ab_w_out[j], diff_lq1[j], diff_lk1[j], diff_lq2[j],
                         diff_lk2[j], diff_subln_g[j], rwkv_mu[j], rwkv_w0[j], rwkv_w2[j], rwkv_a0[j],
                         rwkv_a2[j], rwkv_g2[j], rwkv_k_k[j], rwkv_k_a[j], rwkv_r_k[j], rwkv_lnx_g[j],
                         rwkv_lnx_b[j], lambda_init)
        else:
            m = mixer_c(u, positions, c_w_in[j], c_w_out[j], idx_k_g[j], idx_k_b[j], topk)
        h = h + rms_norm(m, mix_post_g[i])
        u = rms_norm(h, mlp_pre_g[i])
        h = h + rms_norm(sq_relu_mlp(u, w_mlp_up[i], w_mlp_down[i]), mlp_post_g[i])
        e = p[i] @ w_ple_proj[i]
        gate = jax.nn.sigmoid(h @ w_ple_gate[i])
        h = h + rms_norm(e * gate, ple_post_g[i])
    return h
```

```python
import functools
import math

import jax
import jax.numpy as jnp
from jax import lax
from jax.experimental import pallas as pl
from jax.experimental.pallas import tpu as pltpu

F32 = jnp.float32
BF16 = jnp.bfloat16
I32 = jnp.int32

V7X_VMEM_LIMIT_BYTES = 48 * 1024 * 1024
NEG = -1e30
INT_MIN = -2147483648

NORM_EPS = 1e-6
ROPE_THETA = 500000.0
ROPE_FRACTION = 4
A_DV = 128
A_DC = 64
DIFF_SUBLN_EPS = 1e-5
B_HEAD = 64
LNX_EPS = 64e-5
C_HEAD_DIM = 128
C_KV_HEADS = 4
IDX_HEADS = 16
IDX_DIM = 64
TOPK_MAX = 256
LN_EPS = 1e-6
WKV_CHUNK = 64


def _params(sem):
    return pltpu.CompilerParams(dimension_semantics=sem, vmem_limit_bytes=V7X_VMEM_LIMIT_BYTES)


def _pick(n, prefs):
    for t in prefs:
        if n % t == 0:
            return t
    return n


def _mm_body(a_ref, w_ref, o_ref, *scratch, nk, act):
    def epilogue(acc):
        if act == "relu2":
            r = jnp.maximum(acc, 0.0)
            acc = r * r
        return acc.astype(o_ref.dtype)

    if nk == 1:
        o_ref[...] = epilogue(jnp.dot(a_ref[...], w_ref[...], preferred_element_type=F32))
        return
    (acc_ref,) = scratch
    k = pl.program_id(2)

    @pl.when(k == 0)
    def _():
        acc_ref[...] = jnp.zeros_like(acc_ref)

    acc_ref[...] += jnp.dot(a_ref[...], w_ref[...], preferred_element_type=F32)

    @pl.when(k == nk - 1)
    def _():
        o_ref[...] = epilogue(acc_ref[...])


def matmul(a, w, out_dtype=F32, act=None):
    a = a.astype(BF16)
    w = w.astype(BF16)
    m, kdim = a.shape
    n = w.shape[1]
    tm = _pick(m, (1024, 512, 256, 128, 64, 32, 16, 8))
    tn = _pick(n, (1024, 512, 256, 128))
    tk = kdim if kdim <= 2048 else _pick(kdim, (2048, 1024, 512))
    nk = kdim // tk
    scratch = [pltpu.VMEM((tm, tn), F32)] if nk > 1 else []
    return pl.pallas_call(
        functools.partial(_mm_body, nk=nk, act=act),
        grid=(m // tm, n // tn, nk),
        in_specs=[pl.BlockSpec((tm, tk), lambda i, j, k: (i, k)),
                  pl.BlockSpec((tk, tn), lambda i, j, k: (k, j))],
        out_specs=pl.BlockSpec((tm, tn), lambda i, j, k: (i, j)),
        out_shape=jax.ShapeDtypeStruct((m, n), out_dtype),
        scratch_shapes=scratch,
        compiler_params=_params(("parallel", "parallel", "arbitrary")),
        name="dense_matmul",
    )(a, w)


def _softmax_step(s, vt, m_ref, l_ref, acc_ref, idx):
    m_old = m_ref[idx]
    m_new = jnp.maximum(m_old, jnp.max(s, axis=0, keepdims=True))
    alpha = jnp.exp(m_old - m_new)
    p = jnp.exp(s - m_new)
    l_ref[idx] = alpha * l_ref[idx] + jnp.sum(p, axis=0, keepdims=True)
    acc_ref[idx] = alpha * acc_ref[idx] + jnp.dot(vt, p.astype(BF16), preferred_element_type=F32)
    m_ref[idx] = m_new


def _diff_attn_body(lam_ref, k_ref, q_ref, vt_ref, o_ref, m_ref, l_ref, acc_ref, *, tq):
    i = pl.program_id(2)
    m_ref[...] = jnp.full(m_ref.shape, NEG, F32)
    l_ref[...] = jnp.zeros(l_ref.shape, F32)
    acc_ref[...] = jnp.zeros(acc_ref.shape, F32)

    def chunk(j, masked):
        kj = k_ref[pl.ds(pl.multiple_of(j * tq, tq), tq), :]
        vj = vt_ref[j]
        for c in range(2):
            s = jnp.dot(kj, q_ref[c], preferred_element_type=F32)
            if masked:
                row = lax.broadcasted_iota(I32, (tq, tq), 0)
                col = lax.broadcasted_iota(I32, (tq, tq), 1)
                s = jnp.where(row <= col, s, NEG)
            _softmax_step(s, vj, m_ref, l_ref, acc_ref, c)

    def full_chunk(j, carry):
        chunk(j, False)
        return carry

    lax.fori_loop(0, i, full_chunk, 0)
    chunk(i, True)
    lam = lam_ref[0]
    o_ref[...] = acc_ref[0] / l_ref[0] - lam * (acc_ref[1] / l_ref[1])


def diff_attention_core(q, k, v, lam):
    b, t, h = q.shape[0], q.shape[1], q.shape[2]
    tq = _pick(t, (512, 256, 128))
    nq = t // tq
    qs = (q * (A_DC ** -0.5)).astype(BF16)
    qt = jnp.transpose(qs.reshape(b, nq, tq, h, 2, A_DC), (0, 3, 1, 4, 5, 2))
    zeros = jnp.zeros_like(qt[:, :, :, 0])
    qt = jnp.stack([jnp.concatenate([qt[:, :, :, 0], zeros], axis=-2),
                    jnp.concatenate([zeros, qt[:, :, :, 1]], axis=-2)], axis=3)
    kk = k.reshape(b, t, h * 2 * A_DC).astype(BF16)
    vt = jnp.transpose(v.astype(BF16).reshape(b, nq, tq, h, A_DV), (0, 3, 1, 4, 2))
    out = pl.pallas_call(
        functools.partial(_diff_attn_body, tq=tq),
        grid=(b, h, nq),
        in_specs=[pl.BlockSpec(memory_space=pltpu.SMEM),
                  pl.BlockSpec((None, t, 2 * A_DC), lambda bi, hi, qi: (bi, 0, hi)),
                  pl.BlockSpec((None, None, None, 2, 2 * A_DC, tq), lambda bi, hi, qi: (bi, hi, qi, 0, 0, 0)),
                  pl.BlockSpec((None, None, nq, A_DV, tq), lambda bi, hi, qi: (bi, hi, 0, 0, 0))],
        out_specs=pl.BlockSpec((None, None, None, A_DV, tq), lambda bi, hi, qi: (bi, hi, qi, 0, 0)),
        out_shape=jax.ShapeDtypeStruct((b, h, nq, A_DV, tq), F32),
        scratch_shapes=[pltpu.VMEM((2, 1, tq), F32), pltpu.VMEM((2, 1, tq), F32),
                        pltpu.VMEM((2, A_DV, tq), F32)],
        compiler_params=_params(("parallel", "parallel", "arbitrary")),
        name="diff_attention",
    )(lam.reshape(1).astype(F32), kk, qt, vt)
    return jnp.transpose(out, (0, 2, 4, 1, 3)).reshape(b, t, h, A_DV)


def _dot(a, b, dims, exact):
    if exact:
        return lax.dot_general(a, b, (dims, ((), ())), precision=lax.Precision.HIGHEST,
                               preferred_element_type=F32)
    return lax.dot_general(a.astype(BF16), b.astype(BF16), (dims, ((), ())), preferred_element_type=F32)


_NN = ((1,), (0,))
_NT = ((1,), (1,))
_TN = ((0,), (0,))


def _wkv_body(r_ref, lw_ref, k_ref, v_ref, a_ref, b_ref, y_ref, s_ref, *, hb, c, exact):
    @pl.when(pl.program_id(2) == 0)
    def _():
        s_ref[...] = jnp.zeros(s_ref.shape, F32)

    row = lax.broadcasted_iota(I32, (c, c), 0)
    col = lax.broadcasted_iota(I32, (c, c), 1)
    incl = row >= col
    strict = row > col
    tri = jnp.where(incl, 1.0, 0.0).astype(F32)
    eye = jnp.where(row == col, 1.0, 0.0).astype(F32)
    nsq = int(math.log2(c)) - 1

    for h in range(hb):
        lw = lw_ref[h]
        cum = _dot(tri, lw, _NN, True)
        tot = cum[c - 1:c, :]
        p = jnp.exp(cum)
        pinv = jnp.exp(-cum)
        at = a_ref[h] * jnp.exp(cum - lw)
        rt = r_ref[h] * p
        bt = b_ref[h] * pinv
        kt = k_ref[h] * pinv
        pend = jnp.exp(tot - cum)
        bh = b_ref[h] * pend
        kh = k_ref[h] * pend
        v = v_ref[h]
        a_ab = jnp.where(strict, _dot(at, bt, _NT, exact), 0.0)
        a_ak = jnp.where(strict, _dot(at, kt, _NT, exact), 0.0)
        a_rb = jnp.where(incl, _dot(rt, bt, _NT, exact), 0.0)
        a_rk = jnp.where(incl, _dot(rt, kt, _NT, exact), 0.0)
        x = a_ab
        minv = eye + x
        for _ in range(nsq):
            x = _dot(x, x, _NN, exact)
            minv = minv + _dot(minv, x, _NN, exact)
        s0 = s_ref[h]
        rhs = _dot(at, s0, _NT, exact) + _dot(a_ak, v, _NN, exact)
        u = _dot(minv, rhs, _NN, exact)
        y = _dot(rt, s0, _NT, exact) + _dot(a_rb, u, _NN, exact) + _dot(a_rk, v, _NN, exact)
        y_ref[h] = y
        s_ref[h] = s0 * jnp.exp(tot) + _dot(u, bh, _TN, exact) + _dot(v, kh, _TN, exact)


def wkv7(r, lw, k, v, a, b, exact=True):
    bsz, h, t, n = r.shape
    c = min(WKV_CHUNK, t)
    hb = _pick(h, (8, 4, 2, 1))
    spec = pl.BlockSpec((None, hb, c, n), lambda bi, hi, ci: (bi, hi, ci, 0))
    return pl.pallas_call(
        functools.partial(_wkv_body, hb=hb, c=c, exact=exact),
        grid=(bsz, h // hb, t // c),
        in_specs=[spec] * 6,
        out_specs=spec,
        out_shape=jax.ShapeDtypeStruct((bsz, h, t, n), F32),
        scratch_shapes=[pltpu.VMEM((hb, n, n), F32)],
        compiler_params=_params(("parallel", "parallel", "arbitrary")),
        name="wkv7_chunked",
    )(r, lw, k, v, a, b)


def _dsa_index_body(ki_ref, qit_ref, wit_ref, bias_ref, key_ref, cut_ref, *, tq, tkc, topk, nheads, t):
    i = pl.program_id(1)
    nch = ((i + 1) * tq + tkc - 1) // tkc
    qpos = i * tq + lax.broadcasted_iota(I32, (1, tq), 1)

    def kpos_of(ci):
        return ci * tkc + lax.broadcasted_iota(I32, (tkc, 1), 0)

    def rows(ci):
        return pl.ds(pl.multiple_of(ci * tkc, tkc), tkc)

    def score_chunk(ci, carry):
        kc = ki_ref[rows(ci), :]
        acc = jnp.zeros((tkc, tq), F32)
        for h in range(nheads):
            s = jnp.dot(kc, qit_ref[h], preferred_element_type=F32)
            acc = acc + jnp.maximum(s, 0.0) * wit_ref[h]
        acc = jnp.where(kpos_of(ci) <= qpos, acc, -jnp.inf)
        bits = lax.bitcast_convert_type(acc, I32)
        key_ref[rows(ci), :] = jnp.where(bits >= 0, bits, bits ^ 0x7FFFFFFF)
        return carry

    lax.fori_loop(0, nch, score_chunk, 0)

    def count(pred):
        def body(ci, cnt):
            ones = pred(key_ref[rows(ci), :], kpos_of(ci))
            return cnt + jnp.sum(ones, axis=0, keepdims=True)
        return lax.fori_loop(0, nch, body, jnp.zeros((1, tq), I32))

    def value_bit(step, prefix):
        cand = prefix | lax.shift_left(jnp.int32(1), 31 - step)
        cand_signed = cand ^ INT_MIN
        cnt = count(lambda key, kpos: jnp.where(key >= cand_signed, 1, 0))
        return jnp.where(cnt >= topk, cand, prefix)

    thr = lax.fori_loop(0, 32, value_bit, jnp.zeros((1, tq), I32)) ^ INT_MIN
    n_ge = count(lambda key, kpos: jnp.where(key >= thr, 1, 0))
    n_gt = count(lambda key, kpos: jnp.where(key > thr, 1, 0))
    need = topk - n_gt
    cut_ref[...] = jnp.full((1, tq), t, I32)

    @pl.when(jnp.max(n_ge) > topk)
    def _():
        nbits = max(1, (t - 1).bit_length())

        def index_bit(step, x):
            cand = x | lax.shift_left(jnp.int32(1), nbits - 1 - step)
            g = count(lambda key, kpos: jnp.where(key == thr, jnp.where(kpos < cand, 1, 0), 0))
            return jnp.where(g < need, cand, x)

        cut_ref[...] = lax.fori_loop(0, nbits, index_bit, jnp.zeros((1, tq), I32))

    cutoff = cut_ref[...]

    def write_chunk(ci, carry):
        key = key_ref[rows(ci), :]
        kpos = kpos_of(ci)
        tie = jnp.where(kpos <= cutoff, 0.0, NEG)
        sel = jnp.where(key > thr, 0.0, jnp.where(key == thr, tie, NEG))
        bias_ref[rows(ci), :] = jnp.where(kpos <= qpos, sel, NEG).astype(BF16)
        return carry

    lax.fori_loop(0, nch, write_chunk, 0)

    def fill_chunk(ci, carry):
        bias_ref[rows(ci), :] = jnp.full((tkc, tq), NEG, BF16)
        return carry

    lax.fori_loop(nch, t // tkc, fill_chunk, 0)


def dsa_select(ki, qi, wi, topk):
    b, t, nh, d = qi.shape
    tq = _pick(t, (256, 128))
    tkc = _pick(t, (512, 256, 128))
    nq = t // tq
    qit = jnp.transpose(qi.astype(BF16).reshape(b, nq, tq, nh, d), (0, 1, 3, 4, 2))
    wit = jnp.transpose(wi.astype(F32).reshape(b, nq, tq, nh), (0, 1, 3, 2))[:, :, :, None, :]
    return pl.pallas_call(
        functools.partial(_dsa_index_body, tq=tq, tkc=tkc, topk=topk, nheads=nh, t=t),
        grid=(b, nq),
        in_specs=[pl.BlockSpec((None, t, d), lambda bi, qi_: (bi, 0, 0)),
                  pl.BlockSpec((None, None, nh, d, tq), lambda bi, qi_: (bi, qi_, 0, 0, 0)),
                  pl.BlockSpec((None, None, nh, 1, tq), lambda bi, qi_: (bi, qi_, 0, 0, 0))],
        out_specs=pl.BlockSpec((None, t, tq), lambda bi, qi_: (bi, 0, qi_)),
        out_shape=jax.ShapeDtypeStruct((b, t, t), BF16),
        scratch_shapes=[pltpu.VMEM((t, tq), I32), pltpu.VMEM((1, tq), I32)],
        compiler_params=_params(("parallel", "arbitrary")),
        name="dsa_index_topk",
    )(ki.astype(BF16), qit, wit)


def _dsa_attn_body(k_ref, vt_ref, q_ref, bias_ref, o_ref, m_ref, l_ref, acc_ref, *, tq, tk, rep):
    i = pl.program_id(2)
    nch = ((i + 1) * tq + tk - 1) // tk
    m_ref[...] = jnp.full(m_ref.shape, NEG, F32)
    l_ref[...] = jnp.zeros(l_ref.shape, F32)
    acc_ref[...] = jnp.zeros(acc_ref.shape, F32)

    def chunk(j, carry):
        rows = pl.ds(pl.multiple_of(j * tk, tk), tk)
        kj = k_ref[rows, :]
        vj = vt_ref[j]
        bj = bias_ref[rows, :].astype(F32)
        for r in range(rep):
            s = jnp.dot(kj, q_ref[r], preferred_element_type=F32) + bj
            _softmax_step(s, vj, m_ref, l_ref, acc_ref, r)
        return carry

    lax.fori_loop(0, nch, chunk, 0)
    for r in range(rep):
        o_ref[r] = acc_ref[r] / l_ref[r]


def dsa_attention(q, k, v, bias):
    b, t, hq, d = q.shape
    g = k.shape[2]
    rep = hq // g
    tq = _pick(t, (256, 128))
    tk = _pick(t, (512, 256, 128))
    nq, nk = t // tq, t // tk
    qs = (q * (d ** -0.5)).astype(BF16)
    qt = jnp.transpose(qs.reshape(b, nq, tq, g, rep, d), (0, 3, 1, 4, 5, 2))
    kk = k.reshape(b, t, g * d).astype(BF16)
    vt = jnp.transpose(v.astype(BF16).reshape(b, nk, tk, g, d), (0, 3, 1, 4, 2))
    out = pl.pallas_call(
        functools.partial(_dsa_attn_body, tq=tq, tk=tk, rep=rep),
        grid=(b, g, nq),
        in_specs=[pl.BlockSpec((None, t, d), lambda bi, gi, qi: (bi, 0, gi)),
                  pl.BlockSpec((None, None, nk, d, tk), lambda bi, gi, qi: (bi, gi, 0, 0, 0)),
                  pl.BlockSpec((None, None, None, rep, d, tq), lambda bi, gi, qi: (bi, gi, qi, 0, 0, 0)),
                  pl.BlockSpec((None, t, tq), lambda bi, gi, qi: (bi, 0, qi))],
        out_specs=pl.BlockSpec((None, None, None, rep, d, tq), lambda bi, gi, qi: (bi, gi, qi, 0, 0, 0)),
        out_shape=jax.ShapeDtypeStruct((b, g, nq, rep, d, tq), F32),
        scratch_shapes=[pltpu.VMEM((rep, 1, tq), F32), pltpu.VMEM((rep, 1, tq), F32),
                        pltpu.VMEM((rep, d, tq), F32)],
        compiler_params=_params(("parallel", "parallel", "arbitrary")),
        name="dsa_attention",
    )(kk, vt, qt, bias)
    return jnp.transpose(out, (0, 2, 5, 1, 3, 4)).reshape(b, t, hq * d)


def _rms_norm(x, g, eps=NORM_EPS):
    xf = x.astype(F32)
    return xf * lax.rsqrt(jnp.mean(xf * xf, axis=-1, keepdims=True) + eps) * g.astype(F32)


def _layer_norm(x, g, b, eps=LN_EPS):
    mu = jnp.mean(x, axis=-1, keepdims=True)
    var = jnp.mean(jnp.square(x - mu), axis=-1, keepdims=True)
    return (x - mu) * lax.rsqrt(var + eps) * g + b


def _partial_rope(x, pos):
    dh = x.shape[-1]
    rot = dh // ROPE_FRACTION
    half = rot // 2
    inv_freq = ROPE_THETA ** (-(jnp.arange(half, dtype=F32) * 2.0 / rot))
    ang = pos.astype(F32)[..., None] * inv_freq
    cos = jnp.cos(ang)[:, :, None, :]
    sin = jnp.sin(ang)[:, :, None, :]
    x1 = x[..., :half]
    x2 = x[..., half:rot]
    return jnp.concatenate([x1 * cos - x2 * sin, x2 * cos + x1 * sin, x[..., rot:]], axis=-1)


def _pad_cols(w, n):
    return jnp.pad(w, ((0, 0), (0, n - w.shape[1])))


def _pad_rows(w, n):
    return jnp.pad(w, ((0, n - w.shape[0]), (0, 0)))


def _round_up(n, m):
    return (n + m - 1) // m * m


def _mm3(x, w, **kw):
    b, t, _ = x.shape
    return matmul(x.reshape(b * t, x.shape[-1]), w, **kw).reshape(b, t, w.shape[1])


def _lora(x, w2):
    r = _round_up(x.shape[-1], 128)
    xp = jnp.pad(x, ((0, 0), (0, 0), (0, r - x.shape[-1])))
    return _mm3(xp, _pad_rows(w2, r))


def _mixer_ab(u, pos, w_in, w_out, lq1, lk1, lq2, lk2, subln_g, mu, w0, w2, a0, a2, g2,
              k_k, k_a, r_k, lnx_g, lnx_b, lambda_init):
    b, t, d = u.shape
    a_width = d // 2
    b_width = d // 2
    a_heads = a_width // A_DV
    b_heads = b_width // B_HEAD
    a_in = 3 * a_width
    n_in = w_in.shape[1]
    z = _mm3(u, _pad_cols(w_in, _round_up(n_in, 512)))[..., :n_in]
    za, zb = z[..., :a_in], z[..., a_in:]

    qa, ka, va = jnp.split(za, 3, axis=-1)
    qa = _partial_rope(qa.reshape(b, t, 2 * a_heads, A_DC), pos).reshape(b, t, a_heads, 2, A_DC)
    ka = _partial_rope(ka.reshape(b, t, 2 * a_heads, A_DC), pos).reshape(b, t, a_heads, 2, A_DC)
    va = va.reshape(b, t, a_heads, A_DV)
    lam = jnp.exp(jnp.sum(lq1 * lk1)) - jnp.exp(jnp.sum(lq2 * lk2)) + lambda_init
    ya = diff_attention_core(qa, ka, va, lam)
    ya = (_rms_norm(ya, subln_g, DIFF_SUBLN_EPS) * (1.0 - lambda_init)).reshape(b, t, a_width)

    prev = jnp.pad(zb, ((0, 0), (1, 0), (0, 0)))[:, :t]
    zb = zb + (prev - zb) * mu
    dl = w2.shape[0]
    al = a2.shape[0]
    cuts = [b_width, 2 * b_width, 3 * b_width, 3 * b_width + dl, 3 * b_width + dl + al]
    r, k, v, xw, xa, xg = jnp.split(zb, cuts, axis=-1)
    w = -jax.nn.softplus(-(w0 + _lora(jnp.tanh(xw), w2))) - 0.5
    a = jax.nn.sigmoid(a0 + _lora(xa, a2))
    g = _lora(jax.nn.sigmoid(xg), g2)
    heads = lambda x: jnp.transpose(x.reshape(b, t, b_heads, B_HEAD), (0, 2, 1, 3))
    kk = heads(k * k_k)
    kk = kk / jnp.maximum(jnp.sqrt(jnp.sum(kk * kk, axis=-1, keepdims=True)), 1e-12)
    k = k * (1.0 + (a - 1.0) * k_a)
    log_decay = -jnp.exp(w)
    y = wkv7(heads(r), heads(log_decay), heads(k), heads(v), -kk, kk * heads(a))
    y = jnp.transpose(y, (0, 2, 1, 3))
    mu_y = jnp.mean(y, axis=-1, keepdims=True)
    var_y = jnp.mean(jnp.square(y - mu_y), axis=-1, keepdims=True)
    y = (y - mu_y) * lax.rsqrt(var_y + LNX_EPS)
    y = y * lnx_g.reshape(b_heads, B_HEAD) + lnx_b.reshape(b_heads, B_HEAD)
    hd = lambda x: x.reshape(b, t, b_heads, B_HEAD)
    bonus = jnp.sum(hd(r) * hd(k) * r_k, axis=-1, keepdims=True) * hd(v)
    yb = (y + bonus).reshape(b, t, b_width) * g
    return _mm3(jnp.concatenate([ya, yb], axis=-1), w_out)


def _mixer_c(u, pos, w_in, w_out, idx_k_g, idx_k_b, topk):
    b, t, d = u.shape
    c_heads = d // C_HEAD_DIM
    n_in = w_in.shape[1]
    z = _mm3(u, _pad_cols(w_in, _round_up(n_in, 512)))[..., :n_in]
    c1 = c_heads * C_HEAD_DIM
    c2 = c1 + C_KV_HEADS * C_HEAD_DIM
    c3 = c2 + C_KV_HEADS * C_HEAD_DIM
    c4 = c3 + IDX_HEADS * IDX_DIM
    c5 = c4 + IDX_DIM
    q, k, v, qi, ki, wi = jnp.split(z, [c1, c2, c3, c4, c5], axis=-1)
    q = _partial_rope(q.reshape(b, t, c_heads, C_HEAD_DIM), pos)
    k = _partial_rope(k.reshape(b, t, C_KV_HEADS, C_HEAD_DIM), pos)
    v = v.reshape(b, t, C_KV_HEADS, C_HEAD_DIM)
    qi = _partial_rope(qi.reshape(b, t, IDX_HEADS, IDX_DIM), pos)
    ki = _partial_rope(_layer_norm(ki, idx_k_g, idx_k_b)[:, :, None, :], pos)[:, :, 0]
    wi = wi * ((IDX_HEADS * IDX_DIM) ** -0.5)
    bias = dsa_select(ki, qi, wi, topk)
    o = dsa_attention(q, k, v, bias)
    return _mm3(o, w_out)


def kernel(x, p, positions, mix_pre_g, mix_post_g, mlp_pre_g, mlp_post_g, w_mlp_up, w_mlp_down, w_ple_proj, w_ple_gate, ple_post_g, ab_w_in, ab_w_out, diff_lq1, diff_lk1, diff_lq2, diff_lk2, diff_subln_g, rwkv_mu, rwkv_w0, rwkv_w2, rwkv_a0, rwkv_a2, rwkv_g2, rwkv_k_k, rwkv_k_a, rwkv_r_k, rwkv_lnx_g, rwkv_lnx_b, c_w_in, c_w_out, idx_k_g, idx_k_b):
    depth = mix_pre_g.shape[0]
    t = x.shape[1]
    topk = min(TOPK_MAX, t // 4)
    h = x
    for i in range(depth):
        j = i // 2
        u = _rms_norm(h, mix_pre_g[i])
        if i % 2 == 0:
            lambda_init = 0.8 - 0.6 * math.exp(-0.3 * i)
            m = _mixer_ab(u, positions, ab_w_in[j], ab_w_out[j], diff_lq1[j], diff_lk1[j], diff_lq2[j],
                          diff_lk2[j], diff_subln_g[j], rwkv_mu[j], rwkv_w0[j], rwkv_w2[j], rwkv_a0[j],
                          rwkv_a2[j], rwkv_g2[j], rwkv_k_k[j], rwkv_k_a[j], rwkv_r_k[j], rwkv_lnx_g[j],
                          rwkv_lnx_b[j], lambda_init)
        else:
            m = _mixer_c(u, positions, c_w_in[j], c_w_out[j], idx_k_g[j], idx_k_b[j], topk)
        h = h + _rms_norm(m, mix_post_g[i])
        u = _rms_norm(h, mlp_pre_g[i])
        up = _mm3(u, w_mlp_up[i], out_dtype=BF16, act="relu2")
        h = h + _rms_norm(_mm3(up, w_mlp_down[i]), mlp_post_g[i])
        e = _mm3(p[i], w_ple_proj[i])
        gate = jax.nn.sigmoid(_mm3(h, w_ple_gate[i]))
        h = h + _rms_norm(e * gate, ple_post_g[i])
    return h
```

```python
import functools
import math

import jax
import jax.numpy as jnp
from jax import lax
from jax.experimental import pallas as pl
from jax.experimental.pallas import tpu as pltpu

F32 = jnp.float32
BF16 = jnp.bfloat16
I32 = jnp.int32

V7X_VMEM_LIMIT_BYTES = 48 * 1024 * 1024
NEG = -1e30
LOG2E = 1.4426950408889634
INT_MIN = -2147483648

NORM_EPS = 1e-6
ROPE_THETA = 500000.0
ROPE_FRACTION = 4
A_DV = 128
A_DC = 64
DIFF_SUBLN_EPS = 1e-5
B_HEAD = 64
LNX_EPS = 64e-5
C_HEAD_DIM = 128
C_KV_HEADS = 4
IDX_HEADS = 16
IDX_DIM = 64
TOPK_MAX = 256
LN_EPS = 1e-6
WKV_CHUNK = 64


def _params(sem):
    return pltpu.CompilerParams(dimension_semantics=sem, vmem_limit_bytes=V7X_VMEM_LIMIT_BYTES)


def _pick(n, prefs):
    for t in prefs:
        if n % t == 0:
            return t
    return n


def _mm_body(a_ref, w_ref, o_ref, *scratch, nk, act):
    def epilogue(acc):
        if act == "relu2":
            r = jnp.maximum(acc, 0.0)
            acc = r * r
        return acc.astype(o_ref.dtype)

    if nk == 1:
        o_ref[...] = epilogue(jnp.dot(a_ref[...], w_ref[...], preferred_element_type=F32))
        return
    (acc_ref,) = scratch
    k = pl.program_id(2)

    @pl.when(k == 0)
    def _():
        acc_ref[...] = jnp.zeros_like(acc_ref)

    acc_ref[...] += jnp.dot(a_ref[...], w_ref[...], preferred_element_type=F32)

    @pl.when(k == nk - 1)
    def _():
        o_ref[...] = epilogue(acc_ref[...])


def matmul(a, w, out_dtype=F32, act=None):
    a = a.astype(BF16)
    w = w.astype(BF16)
    m, kdim = a.shape
    n = w.shape[1]
    tm = _pick(m, (1024, 512, 256, 128, 64, 32, 16, 8))
    tn = _pick(n, (1024, 512, 256, 128))
    tk = kdim if kdim <= 2048 else _pick(kdim, (2048, 1024, 512))
    nk = kdim // tk
    scratch = [pltpu.VMEM((tm, tn), F32)] if nk > 1 else []
    return pl.pallas_call(
        functools.partial(_mm_body, nk=nk, act=act),
        grid=(m // tm, n // tn, nk),
        in_specs=[pl.BlockSpec((tm, tk), lambda i, j, k: (i, k)),
                  pl.BlockSpec((tk, tn), lambda i, j, k: (k, j))],
        out_specs=pl.BlockSpec((tm, tn), lambda i, j, k: (i, j)),
        out_shape=jax.ShapeDtypeStruct((m, n), out_dtype),
        scratch_shapes=scratch,
        compiler_params=_params(("parallel", "parallel", "arbitrary")),
        name="dense_matmul",
    )(a, w)


def _softmax_steps(ss, vt, m_ref, l_ref, acc_ref):
    n = range(len(ss))
    m_old = [m_ref[i] for i in n]
    m_new = [jnp.maximum(m_old[i], jnp.max(ss[i], axis=0, keepdims=True)) for i in n]
    alpha = [jnp.exp2(m_old[i] - m_new[i]) for i in n]
    p = [jnp.exp2(ss[i] - m_new[i]) for i in n]
    for i in n:
        l_ref[i] = alpha[i] * l_ref[i] + jnp.sum(p[i], axis=0, keepdims=True)
        m_ref[i] = m_new[i]
    pv = [jnp.dot(vt, p[i].astype(BF16), preferred_element_type=F32) for i in n]
    for i in n:
        acc_ref[i] = alpha[i] * acc_ref[i] + pv[i]


def _diff_attn_body(lam_ref, k_ref, q_ref, vt_ref, o_ref, m_ref, l_ref, acc_ref, *, tq):
    i = pl.program_id(2)
    m_ref[...] = jnp.full(m_ref.shape, NEG, F32)
    l_ref[...] = jnp.zeros(l_ref.shape, F32)
    acc_ref[...] = jnp.zeros(acc_ref.shape, F32)

    def chunk(j, masked):
        kj = k_ref[pl.ds(pl.multiple_of(j * tq, tq), tq), :]
        vj = vt_ref[j]
        ss = [jnp.dot(kj, q_ref[c], preferred_element_type=F32) for c in range(2)]
        if masked:
            row = lax.broadcasted_iota(I32, (tq, tq), 0)
            col = lax.broadcasted_iota(I32, (tq, tq), 1)
            ss = [jnp.where(row <= col, s, NEG) for s in ss]
        _softmax_steps(ss, vj, m_ref, l_ref, acc_ref)

    def full_chunk(j, carry):
        chunk(j, False)
        return carry

    lax.fori_loop(0, i, full_chunk, 0)
    chunk(i, True)
    lam = lam_ref[0]
    o_ref[...] = acc_ref[0] / l_ref[0] - lam * (acc_ref[1] / l_ref[1])


def diff_attention_core(q, k, v, lam):
    b, t, h = q.shape[0], q.shape[1], q.shape[2]
    tq = _pick(t, (512, 256, 128))
    nq = t // tq
    qs = (q * (A_DC ** -0.5 * LOG2E)).astype(BF16)
    qt = jnp.transpose(qs.reshape(b, nq, tq, h, 2, A_DC), (0, 3, 1, 4, 5, 2))
    zeros = jnp.zeros_like(qt[:, :, :, 0])
    qt = jnp.stack([jnp.concatenate([qt[:, :, :, 0], zeros], axis=-2),
                    jnp.concatenate([zeros, qt[:, :, :, 1]], axis=-2)], axis=3)
    kk = k.reshape(b, t, h * 2 * A_DC).astype(BF16)
    vt = jnp.transpose(v.astype(BF16).reshape(b, nq, tq, h, A_DV), (0, 3, 1, 4, 2))
    out = pl.pallas_call(
        functools.partial(_diff_attn_body, tq=tq),
        grid=(b, h, nq),
        in_specs=[pl.BlockSpec(memory_space=pltpu.SMEM),
                  pl.BlockSpec((None, t, 2 * A_DC), lambda bi, hi, qi: (bi, 0, hi)),
                  pl.BlockSpec((None, None, None, 2, 2 * A_DC, tq), lambda bi, hi, qi: (bi, hi, qi, 0, 0, 0)),
                  pl.BlockSpec((None, None, nq, A_DV, tq), lambda bi, hi, qi: (bi, hi, 0, 0, 0))],
        out_specs=pl.BlockSpec((None, None, None, A_DV, tq), lambda bi, hi, qi: (bi, hi, qi, 0, 0)),
        out_shape=jax.ShapeDtypeStruct((b, h, nq, A_DV, tq), F32),
        scratch_shapes=[pltpu.VMEM((2, 1, tq), F32), pltpu.VMEM((2, 1, tq), F32),
                        pltpu.VMEM((2, A_DV, tq), F32)],
        compiler_params=_params(("parallel", "parallel", "arbitrary")),
        name="diff_attention",
    )(lam.reshape(1).astype(F32), kk, qt, vt)
    return jnp.transpose(out, (0, 2, 4, 1, 3)).reshape(b, t, h, A_DV)


def _dot(a, b, dims, exact):
    if exact:
        return lax.dot_general(a, b, (dims, ((), ())), precision=lax.Precision.HIGHEST,
                               preferred_element_type=F32)
    return lax.dot_general(a.astype(BF16), b.astype(BF16), (dims, ((), ())), preferred_element_type=F32)


_NN = ((1,), (0,))
_NT = ((1,), (1,))
_TN = ((0,), (0,))


def _wkv_body(r_ref, lw_ref, k_ref, v_ref, a_ref, b_ref, y_ref, s_ref, *, hb, c, exact):
    @pl.when(pl.program_id(2) == 0)
    def _():
        s_ref[...] = jnp.zeros(s_ref.shape, F32)

    row = lax.broadcasted_iota(I32, (c, c), 0)
    col = lax.broadcasted_iota(I32, (c, c), 1)
    incl = row >= col
    strict = row > col
    tri = jnp.where(incl, 1.0, 0.0).astype(F32)
    eye = jnp.where(row == col, 1.0, 0.0).astype(F32)
    nsq = int(math.log2(c)) - 1

    hs = range(hb)
    dot = functools.partial(_dot, exact=exact)
    lw = [lw_ref[h] for h in hs]
    cum = [_dot(tri, lw[h], _NN, True) for h in hs]
    tot = [cum[h][c - 1:c, :] for h in hs]
    pinv = [jnp.exp(-cum[h]) for h in hs]
    pend = [jnp.exp(tot[h] - cum[h]) for h in hs]
    at = [a_ref[h] * jnp.exp(cum[h] - lw[h]) for h in hs]
    rt = [r_ref[h] * jnp.exp(cum[h]) for h in hs]
    bt = [b_ref[h] * pinv[h] for h in hs]
    kt = [k_ref[h] * pinv[h] for h in hs]
    v = [v_ref[h] for h in hs]
    a_ab = [jnp.where(strict, dot(at[h], bt[h], _NT), 0.0) for h in hs]
    a_ak = [jnp.where(strict, dot(at[h], kt[h], _NT), 0.0) for h in hs]
    a_rb = [jnp.where(incl, dot(rt[h], bt[h], _NT), 0.0) for h in hs]
    a_rk = [jnp.where(incl, dot(rt[h], kt[h], _NT), 0.0) for h in hs]
    x = a_ab
    minv = [eye + x[h] for h in hs]
    for _ in range(nsq):
        x = [dot(x[h], x[h], _NN) for h in hs]
        minv = [minv[h] + dot(minv[h], x[h], _NN) for h in hs]
    s0 = [s_ref[h] for h in hs]
    rhs = [dot(at[h], s0[h], _NT) + dot(a_ak[h], v[h], _NN) for h in hs]
    u = [dot(minv[h], rhs[h], _NN) for h in hs]
    for h in hs:
        y_ref[h] = dot(rt[h], s0[h], _NT) + dot(a_rb[h], u[h], _NN) + dot(a_rk[h], v[h], _NN)
    for h in hs:
        s_ref[h] = (s0[h] * jnp.exp(tot[h]) + dot(u[h], b_ref[h] * pend[h], _TN)
                    + dot(v[h], k_ref[h] * pend[h], _TN))


def wkv7(r, lw, k, v, a, b, exact=False):
    bsz, h, t, n = r.shape
    c = min(WKV_CHUNK, t)
    hb = _pick(h, (16, 8, 4, 2, 1))
    spec = pl.BlockSpec((None, hb, c, n), lambda bi, hi, ci: (bi, hi, ci, 0))
    return pl.pallas_call(
        functools.partial(_wkv_body, hb=hb, c=c, exact=exact),
        grid=(bsz, h // hb, t // c),
        in_specs=[spec] * 6,
        out_specs=spec,
        out_shape=jax.ShapeDtypeStruct((bsz, h, t, n), F32),
        scratch_shapes=[pltpu.VMEM((hb, n, n), F32)],
        compiler_params=_params(("parallel", "parallel", "arbitrary")),
        name="wkv7_chunked",
    )(r, lw, k, v, a, b)


def _dsa_index_body(ki_ref, qit_ref, wit_ref, bias_ref, key_ref, cut_ref, *, tq, tkc, topk, nheads, t):
    i = pl.program_id(1)
    nch = ((i + 1) * tq + tkc - 1) // tkc
    qpos = i * tq + lax.broadcasted_iota(I32, (1, tq), 1)

    def kpos_of(ci):
        return ci * tkc + lax.broadcasted_iota(I32, (tkc, 1), 0)

    def rows(ci):
        return pl.ds(pl.multiple_of(ci * tkc, tkc), tkc)

    def score_chunk(ci, carry):
        kc = ki_ref[rows(ci), :]
        acc = jnp.zeros((tkc, tq), F32)
        for h in range(nheads):
            s = jnp.dot(kc, qit_ref[h], preferred_element_type=F32)
            acc = acc + jnp.maximum(s, 0.0) * wit_ref[h]
        acc = jnp.where(kpos_of(ci) <= qpos, acc, -jnp.inf)
        bits = lax.bitcast_convert_type(acc, I32)
        key_ref[rows(ci), :] = jnp.where(bits >= 0, bits, bits ^ 0x7FFFFFFF)
        return carry

    lax.fori_loop(0, nch, score_chunk, 0)

    def count(pred):
        def body(ci, cnt):
            ones = pred(key_ref[rows(ci), :], kpos_of(ci))
            return cnt + jnp.sum(ones, axis=0, keepdims=True)
        return lax.fori_loop(0, nch, body, jnp.zeros((1, tq), I32))

    def value_bit(step, prefix):
        cand = prefix | lax.shift_left(jnp.int32(1), 31 - step)
        cand_signed = cand ^ INT_MIN
        cnt = count(lambda key, kpos: jnp.where(key >= cand_signed, 1, 0))
        return jnp.where(cnt >= topk, cand, prefix)

    thr = lax.fori_loop(0, 32, value_bit, jnp.zeros((1, tq), I32)) ^ INT_MIN
    n_ge = count(lambda key, kpos: jnp.where(key >= thr, 1, 0))
    n_gt = count(lambda key, kpos: jnp.where(key > thr, 1, 0))
    need = topk - n_gt
    cut_ref[...] = jnp.full((1, tq), t, I32)

    @pl.when(jnp.max(n_ge) > topk)
    def _():
        nbits = max(1, (t - 1).bit_length())

        def index_bit(step, x):
            cand = x | lax.shift_left(jnp.int32(1), nbits - 1 - step)
            g = count(lambda key, kpos: jnp.where(key == thr, jnp.where(kpos < cand, 1, 0), 0))
            return jnp.where(g < need, cand, x)

        cut_ref[...] = lax.fori_loop(0, nbits, index_bit, jnp.zeros((1, tq), I32))

    cutoff = cut_ref[...]

    def write_chunk(ci, carry):
        key = key_ref[rows(ci), :]
        kpos = kpos_of(ci)
        tie = jnp.where(kpos <= cutoff, 0.0, NEG)
        sel = jnp.where(key > thr, 0.0, jnp.where(key == thr, tie, NEG))
        bias_ref[rows(ci), :] = jnp.where(kpos <= qpos, sel, NEG).astype(BF16)
        return carry

    lax.fori_loop(0, nch, write_chunk, 0)

    def fill_chunk(ci, carry):
        bias_ref[rows(ci), :] = jnp.full((tkc, tq), NEG, BF16)
        return carry

    lax.fori_loop(nch, t // tkc, fill_chunk, 0)


def dsa_select(ki, qi, wi, topk):
    b, t, nh, d = qi.shape
    tq = _pick(t, (256, 128))
    tkc = _pick(t, (512, 256, 128))
    nq = t // tq
    qit = jnp.transpose(qi.astype(BF16).reshape(b, nq, tq, nh, d), (0, 1, 3, 4, 2))
    wit = jnp.transpose(wi.astype(F32).reshape(b, nq, tq, nh), (0, 1, 3, 2))[:, :, :, None, :]
    return pl.pallas_call(
        functools.partial(_dsa_index_body, tq=tq, tkc=tkc, topk=topk, nheads=nh, t=t),
        grid=(b, nq),
        in_specs=[pl.BlockSpec((None, t, d), lambda bi, qi_: (bi, 0, 0)),
                  pl.BlockSpec((None, None, nh, d, tq), lambda bi, qi_: (bi, qi_, 0, 0, 0)),
                  pl.BlockSpec((None, None, nh, 1, tq), lambda bi, qi_: (bi, qi_, 0, 0, 0))],
        out_specs=pl.BlockSpec((None, t, tq), lambda bi, qi_: (bi, 0, qi_)),
        out_shape=jax.ShapeDtypeStruct((b, t, t), BF16),
        scratch_shapes=[pltpu.VMEM((t, tq), I32), pltpu.VMEM((1, tq), I32)],
        compiler_params=_params(("parallel", "arbitrary")),
        name="dsa_index_topk",
    )(ki.astype(BF16), qit, wit)


def _dsa_attn_body(k_ref, vt_ref, q_ref, bias_ref, o_ref, m_ref, l_ref, acc_ref, *, tq, tk, rep):
    i = pl.program_id(2)
    nch = ((i + 1) * tq + tk - 1) // tk
    m_ref[...] = jnp.full(m_ref.shape, NEG, F32)
    l_ref[...] = jnp.zeros(l_ref.shape, F32)
    acc_ref[...] = jnp.zeros(acc_ref.shape, F32)

    def chunk(j, carry):
        rows = pl.ds(pl.multiple_of(j * tk, tk), tk)
        kj = k_ref[rows, :]
        vj = vt_ref[j]
        bj = bias_ref[rows, :].astype(F32)
        ss = [jnp.dot(kj, q_ref[r], preferred_element_type=F32) + bj for r in range(rep)]
        _softmax_steps(ss, vj, m_ref, l_ref, acc_ref)
        return carry

    lax.fori_loop(0, nch, chunk, 0)
    for r in range(rep):
        o_ref[r] = acc_ref[r] / l_ref[r]


def dsa_attention(q, k, v, bias):
    b, t, hq, d = q.shape
    g = k.shape[2]
    rep = hq // g
    tq = _pick(t, (256, 128))
    tk = _pick(t, (512, 256, 128))
    nq, nk = t // tq, t // tk
    qs = (q * (d ** -0.5 * LOG2E)).astype(BF16)
    qt = jnp.transpose(qs.reshape(b, nq, tq, g, rep, d), (0, 3, 1, 4, 5, 2))
    kk = k.reshape(b, t, g * d).astype(BF16)
    vt = jnp.transpose(v.astype(BF16).reshape(b, nk, tk, g, d), (0, 3, 1, 4, 2))
    out = pl.pallas_call(
        functools.partial(_dsa_attn_body, tq=tq, tk=tk, rep=rep),
        grid=(b, g, nq),
        in_specs=[pl.BlockSpec((None, t, d), lambda bi, gi, qi: (bi, 0, gi)),
                  pl.BlockSpec((None, None, nk, d, tk), lambda bi, gi, qi: (bi, gi, 0, 0, 0)),
                  pl.BlockSpec((None, None, None, rep, d, tq), lambda bi, gi, qi: (bi, gi, qi, 0, 0, 0)),
                  pl.BlockSpec((None, t, tq), lambda bi, gi, qi: (bi, 0, qi))],
        out_specs=pl.BlockSpec((None, None, None, rep, d, tq), lambda bi, gi, qi: (bi, gi, qi, 0, 0, 0)),
        out_shape=jax.ShapeDtypeStruct((b, g, nq, rep, d, tq), F32),
        scratch_shapes=[pltpu.VMEM((rep, 1, tq), F32), pltpu.VMEM((rep, 1, tq), F32),
                        pltpu.VMEM((rep, d, tq), F32)],
        compiler_params=_params(("parallel", "parallel", "arbitrary")),
        name="dsa_attention",
    )(kk, vt, qt, bias)
    return jnp.transpose(out, (0, 2, 5, 1, 3, 4)).reshape(b, t, hq * d)


def _rms_norm(x, g, eps=NORM_EPS):
    xf = x.astype(F32)
    return xf * lax.rsqrt(jnp.mean(xf * xf, axis=-1, keepdims=True) + eps) * g.astype(F32)


def _layer_norm(x, g, b, eps=LN_EPS):
    mu = jnp.mean(x, axis=-1, keepdims=True)
    var = jnp.mean(jnp.square(x - mu), axis=-1, keepdims=True)
    return (x - mu) * lax.rsqrt(var + eps) * g + b


def _partial_rope(x, pos):
    dh = x.shape[-1]
    rot = dh // ROPE_FRACTION
    half = rot // 2
    inv_freq = ROPE_THETA ** (-(jnp.arange(half, dtype=F32) * 2.0 / rot))
    ang = pos.astype(F32)[..., None] * inv_freq
    cos = jnp.cos(ang)[:, :, None, :]
    sin = jnp.sin(ang)[:, :, None, :]
    x1 = x[..., :half]
    x2 = x[..., half:rot]
    return jnp.concatenate([x1 * cos - x2 * sin, x2 * cos + x1 * sin, x[..., rot:]], axis=-1)


def _pad_cols(w, n):
    return jnp.pad(w, ((0, 0), (0, n - w.shape[1])))


def _pad_rows(w, n):
    return jnp.pad(w, ((0, n - w.shape[0]), (0, 0)))


def _round_up(n, m):
    return (n + m - 1) // m * m


def _mm3(x, w, **kw):
    b, t, _ = x.shape
    return matmul(x.reshape(b * t, x.shape[-1]), w, **kw).reshape(b, t, w.shape[1])


def _lora(x, w2):
    r = _round_up(x.shape[-1], 128)
    xp = jnp.pad(x, ((0, 0), (0, 0), (0, r - x.shape[-1])))
    return _mm3(xp, _pad_rows(w2, r))


def _mixer_ab(u, pos, w_in, w_out, lq1, lk1, lq2, lk2, subln_g, mu, w0, w2, a0, a2, g2,
              k_k, k_a, r_k, lnx_g, lnx_b, lambda_init):
    b, t, d = u.shape
    a_width = d // 2
    b_width = d // 2
    a_heads = a_width // A_DV
    b_heads = b_width // B_HEAD
    a_in = 3 * a_width
    n_in = w_in.shape[1]
    z = _mm3(u, _pad_cols(w_in, _round_up(n_in, 512)))[..., :n_in]
    za, zb = z[..., :a_in], z[..., a_in:]

    qa, ka, va = jnp.split(za, 3, axis=-1)
    qa = _partial_rope(qa.reshape(b, t, 2 * a_heads, A_DC), pos).reshape(b, t, a_heads, 2, A_DC)
    ka = _partial_rope(ka.reshape(b, t, 2 * a_heads, A_DC), pos).reshape(b, t, a_heads, 2, A_DC)
    va = va.reshape(b, t, a_heads, A_DV)
    lam = jnp.exp(jnp.sum(lq1 * lk1)) - jnp.exp(jnp.sum(lq2 * lk2)) + lambda_init
    ya = diff_attention_core(qa, ka, va, lam)
    ya = (_rms_norm(ya, subln_g, DIFF_SUBLN_EPS) * (1.0 - lambda_init)).reshape(b, t, a_width)

    prev = jnp.pad(zb, ((0, 0), (1, 0), (0, 0)))[:, :t]
    zb = zb + (prev - zb) * mu
    dl = w2.shape[0]
    al = a2.shape[0]
    cuts = [b_width, 2 * b_width, 3 * b_width, 3 * b_width + dl, 3 * b_width + dl + al]
    r, k, v, xw, xa, xg = jnp.split(zb, cuts, axis=-1)
    w = -jax.nn.softplus(-(w0 + _lora(jnp.tanh(xw), w2))) - 0.5
    a = jax.nn.sigmoid(a0 + _lora(xa, a2))
    g = _lora(jax.nn.sigmoid(xg), g2)
    heads = lambda x: jnp.transpose(x.reshape(b, t, b_heads, B_HEAD), (0, 2, 1, 3))
    kk = heads(k * k_k)
    kk = kk / jnp.maximum(jnp.sqrt(jnp.sum(kk * kk, axis=-1, keepdims=True)), 1e-12)
    k = k * (1.0 + (a - 1.0) * k_a)
    log_decay = -jnp.exp(w)
    y = wkv7(heads(r), heads(log_decay), heads(k), heads(v), -kk, kk * heads(a))
    y = jnp.transpose(y, (0, 2, 1, 3))
    mu_y = jnp.mean(y, axis=-1, keepdims=True)
    var_y = jnp.mean(jnp.square(y - mu_y), axis=-1, keepdims=True)
    y = (y - mu_y) * lax.rsqrt(var_y + LNX_EPS)
    y = y * lnx_g.reshape(b_heads, B_HEAD) + lnx_b.reshape(b_heads, B_HEAD)
    hd = lambda x: x.reshape(b, t, b_heads, B_HEAD)
    bonus = jnp.sum(hd(r) * hd(k) * r_k, axis=-1, keepdims=True) * hd(v)
    yb = (y + bonus).reshape(b, t, b_width) * g
    return _mm3(jnp.concatenate([ya, yb], axis=-1), w_out)


def _mixer_c(u, pos, w_in, w_out, idx_k_g, idx_k_b, topk):
    b, t, d = u.shape
    c_heads = d // C_HEAD_DIM
    n_in = w_in.shape[1]
    z = _mm3(u, _pad_cols(w_in, _round_up(n_in, 512)))[..., :n_in]
    c1 = c_heads * C_HEAD_DIM
    c2 = c1 + C_KV_HEADS * C_HEAD_DIM
    c3 = c2 + C_KV_HEADS * C_HEAD_DIM
    c4 = c3 + IDX_HEADS * IDX_DIM
    c5 = c4 + IDX_DIM
    q, k, v, qi, ki, wi = jnp.split(z, [c1, c2, c3, c4, c5], axis=-1)
    q = _partial_rope(q.reshape(b, t, c_heads, C_HEAD_DIM), pos)
    k = _partial_rope(k.reshape(b, t, C_KV_HEADS, C_HEAD_DIM), pos)
    v = v.reshape(b, t, C_KV_HEADS, C_HEAD_DIM)
    qi = _partial_rope(qi.reshape(b, t, IDX_HEADS, IDX_DIM), pos)
    ki = _partial_rope(_layer_norm(ki, idx_k_g, idx_k_b)[:, :, None, :], pos)[:, :, 0]
    wi = wi * ((IDX_HEADS * IDX_DIM) ** -0.5)
    bias = dsa_select(ki, qi, wi, topk)
    o = dsa_attention(q, k, v, bias)
    return _mm3(o, w_out)


def kernel(x, p, positions, mix_pre_g, mix_post_g, mlp_pre_g, mlp_post_g, w_mlp_up, w_mlp_down, w_ple_proj, w_ple_gate, ple_post_g, ab_w_in, ab_w_out, diff_lq1, diff_lk1, diff_lq2, diff_lk2, diff_subln_g, rwkv_mu, rwkv_w0, rwkv_w2, rwkv_a0, rwkv_a2, rwkv_g2, rwkv_k_k, rwkv_k_a, rwkv_r_k, rwkv_lnx_g, rwkv_lnx_b, c_w_in, c_w_out, idx_k_g, idx_k_b):
    depth = mix_pre_g.shape[0]
    t = x.shape[1]
    topk = min(TOPK_MAX, t // 4)
    h = x
    for i in range(depth):
        j = i // 2
        u = _rms_norm(h, mix_pre_g[i])
        if i % 2 == 0:
            lambda_init = 0.8 - 0.6 * math.exp(-0.3 * i)
            m = _mixer_ab(u, positions, ab_w_in[j], ab_w_out[j], diff_lq1[j], diff_lk1[j], diff_lq2[j],
                          diff_lk2[j], diff_subln_g[j], rwkv_mu[j], rwkv_w0[j], rwkv_w2[j], rwkv_a0[j],
                          rwkv_a2[j], rwkv_g2[j], rwkv_k_k[j], rwkv_k_a[j], rwkv_r_k[j], rwkv_lnx_g[j],
                          rwkv_lnx_b[j], lambda_init)
        else:
            m = _mixer_c(u, positions, c_w_in[j], c_w_out[j], idx_k_g[j], idx_k_b[j], topk)
        h = h + _rms_norm(m, mix_post_g[i])
        u = _rms_norm(h, mlp_pre_g[i])
        up = _mm3(u, w_mlp_up[i], out_dtype=BF16, act="relu2")
        h = h + _rms_norm(_mm3(up, w_mlp_down[i]), mlp_post_g[i])
        e = _mm3(p[i], w_ple_proj[i])
        gate = jax.nn.sigmoid(_mm3(h, w_ple_gate[i]))
        h = h + _rms_norm(e * gate, ple_post_g[i])
    return h
```

```python
import functools
import math

import jax
import jax.numpy as jnp
from jax import lax
from jax.experimental import pallas as pl
from jax.experimental.pallas import tpu as pltpu

F32 = jnp.float32
BF16 = jnp.bfloat16
I32 = jnp.int32

V7X_VMEM_LIMIT_BYTES = 48 * 1024 * 1024
NEG = -1e30
LOG2E = 1.4426950408889634
INT_MIN = -2147483648

NORM_EPS = 1e-6
ROPE_THETA = 500000.0
ROPE_FRACTION = 4
A_DV = 128
A_DC = 64
DIFF_SUBLN_EPS = 1e-5
B_HEAD = 64
LNX_EPS = 64e-5
C_HEAD_DIM = 128
C_KV_HEADS = 4
IDX_HEADS = 16
IDX_DIM = 64
TOPK_MAX = 256
LN_EPS = 1e-6
WKV_CHUNK = 64


def _params(sem):
    return pltpu.CompilerParams(dimension_semantics=sem, vmem_limit_bytes=V7X_VMEM_LIMIT_BYTES)


def _pick(n, prefs):
    for t in prefs:
        if n % t == 0:
            return t
    return n


def _mm_body(a_ref, w_ref, o_ref, *scratch, nk, act):
    def epilogue(acc):
        if act == "relu2":
            r = jnp.maximum(acc, 0.0)
            acc = r * r
        return acc.astype(o_ref.dtype)

    if nk == 1:
        o_ref[...] = epilogue(jnp.dot(a_ref[...], w_ref[...], preferred_element_type=F32))
        return
    (acc_ref,) = scratch
    k = pl.program_id(2)

    @pl.when(k == 0)
    def _():
        acc_ref[...] = jnp.zeros_like(acc_ref)

    acc_ref[...] += jnp.dot(a_ref[...], w_ref[...], preferred_element_type=F32)

    @pl.when(k == nk - 1)
    def _():
        o_ref[...] = epilogue(acc_ref[...])


def matmul(a, w, out_dtype=F32, act=None):
    a = a.astype(BF16)
    w = w.astype(BF16)
    m, kdim = a.shape
    n = w.shape[1]
    tm = _pick(m, (1024, 512, 256, 128, 64, 32, 16, 8))
    tn = _pick(n, (1024, 512, 256, 128))
    tk = kdim if kdim <= 2048 else _pick(kdim, (2048, 1024, 512))
    nk = kdim // tk
    scratch = [pltpu.VMEM((tm, tn), F32)] if nk > 1 else []
    return pl.pallas_call(
        functools.partial(_mm_body, nk=nk, act=act),
        grid=(m // tm, n // tn, nk),
        in_specs=[pl.BlockSpec((tm, tk), lambda i, j, k: (i, k)),
                  pl.BlockSpec((tk, tn), lambda i, j, k: (k, j))],
        out_specs=pl.BlockSpec((tm, tn), lambda i, j, k: (i, j)),
        out_shape=jax.ShapeDtypeStruct((m, n), out_dtype),
        scratch_shapes=scratch,
        compiler_params=_params(("parallel", "parallel", "arbitrary")),
        name="dense_matmul",
    )(a, w)


def _rms_rows(x, g):
    return x * lax.rsqrt(jnp.mean(x * x, axis=-1, keepdims=True) + NORM_EPS) * g


def _residual_update(m, h_ref, gpost_ref, gnext_ref, ho_ref, uo_ref):
    h_new = h_ref[...] + _rms_rows(m, gpost_ref[...])
    ho_ref[...] = h_new
    if uo_ref is not None:
        uo_ref[...] = _rms_rows(h_new, gnext_ref[...]).astype(BF16)


def _proj_res_body(*refs, nk, has_next):
    if has_next:
        a_ref, w_ref, h_ref, gpost_ref, gnext_ref, ho_ref, uo_ref, acc_ref = refs
    else:
        a_ref, w_ref, h_ref, gpost_ref, ho_ref, acc_ref = refs
        gnext_ref = uo_ref = None
    k = pl.program_id(1)

    @pl.when(k == 0)
    def _():
        acc_ref[...] = jnp.zeros_like(acc_ref)

    acc_ref[...] += jnp.dot(a_ref[...], w_ref[...], preferred_element_type=F32)

    @pl.when(k == nk - 1)
    def _():
        _residual_update(acc_ref[...], h_ref, gpost_ref, gnext_ref, ho_ref, uo_ref)


def proj_residual(a, w, h, g_post, g_next):
    m, kdim = a.shape
    n = w.shape[1]
    tm = _pick(m, (512, 256, 128, 64, 32, 16, 8))
    tk = _pick(kdim, (1024, 512, 256, 128))
    nk = kdim // tk
    has_next = g_next is not None
    row = pl.BlockSpec((tm, n), lambda i, k: (i, 0))
    vec = pl.BlockSpec((1, n), lambda i, k: (0, 0))
    in_specs = [pl.BlockSpec((tm, tk), lambda i, k: (i, k)), pl.BlockSpec((tk, n), lambda i, k: (k, 0)), row, vec]
    args = [a.astype(BF16), w.astype(BF16), h, g_post.reshape(1, n).astype(F32)]
    out_shape = [jax.ShapeDtypeStruct((m, n), F32)]
    out_specs = [row]
    if has_next:
        in_specs.append(vec)
        args.append(g_next.reshape(1, n).astype(F32))
        out_shape.append(jax.ShapeDtypeStruct((m, n), BF16))
        out_specs.append(row)
    out = pl.pallas_call(
        functools.partial(_proj_res_body, nk=nk, has_next=has_next),
        grid=(m // tm, nk),
        in_specs=in_specs, out_specs=out_specs, out_shape=out_shape,
        scratch_shapes=[pltpu.VMEM((tm, n), F32)],
        compiler_params=_params(("parallel", "arbitrary")),
        name="proj_residual",
    )(*args)
    return (out[0], out[1]) if has_next else (out[0], None)


def _ple_body(*refs, has_next):
    if has_next:
        p_ref, wp_ref, wg_ref, h_ref, gpost_ref, gnext_ref, ho_ref, uo_ref = refs
    else:
        p_ref, wp_ref, wg_ref, h_ref, gpost_ref, ho_ref = refs
        gnext_ref = uo_ref = None
    e = jnp.dot(p_ref[...], wp_ref[...], preferred_element_type=F32)
    gate = jax.nn.sigmoid(jnp.dot(h_ref[...].astype(BF16), wg_ref[...], preferred_element_type=F32))
    _residual_update(e * gate, h_ref, gpost_ref, gnext_ref, ho_ref, uo_ref)


def ple_residual(p, w_proj, w_gate, h, g_post, g_next):
    m, n = h.shape
    pd = p.shape[1]
    tm = _pick(m, (256, 128, 64, 32, 16, 8))
    has_next = g_next is not None
    row = pl.BlockSpec((tm, n), lambda i: (i, 0))
    vec = pl.BlockSpec((1, n), lambda i: (0, 0))
    in_specs = [pl.BlockSpec((tm, pd), lambda i: (i, 0)), pl.BlockSpec((pd, n), lambda i: (0, 0)),
                pl.BlockSpec((n, n), lambda i: (0, 0)), row, vec]
    args = [p.astype(BF16), w_proj.astype(BF16), w_gate.astype(BF16), h, g_post.reshape(1, n).astype(F32)]
    out_shape = [jax.ShapeDtypeStruct((m, n), F32)]
    out_specs = [row]
    if has_next:
        in_specs.append(vec)
        args.append(g_next.reshape(1, n).astype(F32))
        out_shape.append(jax.ShapeDtypeStruct((m, n), BF16))
        out_specs.append(row)
    out = pl.pallas_call(
        functools.partial(_ple_body, has_next=has_next),
        grid=(m // tm,),
        in_specs=in_specs, out_specs=out_specs, out_shape=out_shape,
        compiler_params=_params(("parallel",)),
        name="ple_residual",
    )(*args)
    return (out[0], out[1]) if has_next else (out[0], None)


def _softmax_steps(ss, vt, m_ref, l_ref, acc_ref):
    n = range(len(ss))
    m_old = [m_ref[i] for i in n]
    m_new = [jnp.maximum(m_old[i], jnp.max(ss[i], axis=0, keepdims=True)) for i in n]
    alpha = [jnp.exp2(m_old[i] - m_new[i]) for i in n]
    p = [jnp.exp2(ss[i] - m_new[i]) for i in n]
    for i in n:
        l_ref[i] = alpha[i] * l_ref[i] + jnp.sum(p[i], axis=0, keepdims=True)
        m_ref[i] = m_new[i]
    pv = [jnp.dot(vt, p[i].astype(BF16), preferred_element_type=F32) for i in n]
    for i in n:
        acc_ref[i] = alpha[i] * acc_ref[i] + pv[i]


def _diff_attn_body(lam_ref, k_ref, q_ref, vt_ref, o_ref, m_ref, l_ref, acc_ref, *, tq):
    i = pl.program_id(2)
    m_ref[...] = jnp.full(m_ref.shape, NEG, F32)
    l_ref[...] = jnp.zeros(l_ref.shape, F32)
    acc_ref[...] = jnp.zeros(acc_ref.shape, F32)

    def chunk(j, masked):
        kj = k_ref[pl.ds(pl.multiple_of(j * tq, tq), tq), :]
        vj = vt_ref[j]
        ss = [jnp.dot(kj, q_ref[c], preferred_element_type=F32) for c in range(2)]
        if masked:
            row = lax.broadcasted_iota(I32, (tq, tq), 0)
            col = lax.broadcasted_iota(I32, (tq, tq), 1)
            ss = [jnp.where(row <= col, s, NEG) for s in ss]
        _softmax_steps(ss, vj, m_ref, l_ref, acc_ref)

    def full_chunk(j, carry):
        chunk(j, False)
        return carry

    lax.fori_loop(0, i, full_chunk, 0)
    chunk(i, True)
    lam = lam_ref[0]
    o_ref[...] = acc_ref[0] / l_ref[0] - lam * (acc_ref[1] / l_ref[1])


def diff_attention_core(q, k, v, lam):
    b, t, h = q.shape[0], q.shape[1], q.shape[2]
    tq = _pick(t, (512, 256, 128))
    nq = t // tq
    qs = (q * (A_DC ** -0.5 * LOG2E)).astype(BF16)
    qt = jnp.transpose(qs.reshape(b, nq, tq, h, 2, A_DC), (0, 3, 1, 4, 5, 2))
    zeros = jnp.zeros_like(qt[:, :, :, 0])
    qt = jnp.stack([jnp.concatenate([qt[:, :, :, 0], zeros], axis=-2),
                    jnp.concatenate([zeros, qt[:, :, :, 1]], axis=-2)], axis=3)
    kk = k.reshape(b, t, h * 2 * A_DC).astype(BF16)
    vt = jnp.transpose(v.astype(BF16).reshape(b, nq, tq, h, A_DV), (0, 3, 1, 4, 2))
    out = pl.pallas_call(
        functools.partial(_diff_attn_body, tq=tq),
        grid=(b, h, nq),
        in_specs=[pl.BlockSpec(memory_space=pltpu.SMEM),
                  pl.BlockSpec((None, t, 2 * A_DC), lambda bi, hi, qi: (bi, 0, hi)),
                  pl.BlockSpec((None, None, None, 2, 2 * A_DC, tq), lambda bi, hi, qi: (bi, hi, qi, 0, 0, 0)),
                  pl.BlockSpec((None, None, nq, A_DV, tq), lambda bi, hi, qi: (bi, hi, 0, 0, 0))],
        out_specs=pl.BlockSpec((None, None, None, A_DV, tq), lambda bi, hi, qi: (bi, hi, qi, 0, 0)),
        out_shape=jax.ShapeDtypeStruct((b, h, nq, A_DV, tq), F32),
        scratch_shapes=[pltpu.VMEM((2, 1, tq), F32), pltpu.VMEM((2, 1, tq), F32),
                        pltpu.VMEM((2, A_DV, tq), F32)],
        compiler_params=_params(("parallel", "parallel", "arbitrary")),
        name="diff_attention",
    )(lam.reshape(1).astype(F32), kk, qt, vt)
    return jnp.transpose(out, (0, 2, 4, 1, 3)).reshape(b, t, h, A_DV)


def _dot(a, b, dims, exact):
    if exact:
        return lax.dot_general(a, b, (dims, ((), ())), precision=lax.Precision.HIGHEST,
                               preferred_element_type=F32)
    return lax.dot_general(a.astype(BF16), b.astype(BF16), (dims, ((), ())), preferred_element_type=F32)


_NN = ((1,), (0,))
_NT = ((1,), (1,))
_TN = ((0,), (0,))


def _wkv_body(r_ref, lw_ref, k_ref, v_ref, a_ref, b_ref, y_ref, s_ref, *, nh, n, c, exact):
    @pl.when(pl.program_id(1) == 0)
    def _():
        s_ref[...] = jnp.zeros(s_ref.shape, F32)

    row = lax.broadcasted_iota(I32, (c, c), 0)
    col = lax.broadcasted_iota(I32, (c, c), 1)
    incl = row >= col
    strict = row > col
    tri = jnp.where(incl, 1.0, 0.0).astype(F32)
    eye = jnp.where(row == col, 1.0, 0.0).astype(F32)
    nsq = int(math.log2(c)) - 1

    hs = range(nh)
    dot = functools.partial(_dot, exact=exact)
    heads = lambda ref: [ref[:, h * n:(h + 1) * n] for h in hs]
    lw, r, k, v, a, b = (heads(ref) for ref in (lw_ref, r_ref, k_ref, v_ref, a_ref, b_ref))
    cum = [_dot(tri, lw[h], _NN, True) for h in hs]
    tot = [cum[h][c - 1:c, :] for h in hs]
    pinv = [jnp.exp(-cum[h]) for h in hs]
    pend = [jnp.exp(tot[h] - cum[h]) for h in hs]
    at = [a[h] * jnp.exp(cum[h] - lw[h]) for h in hs]
    rt = [r[h] * jnp.exp(cum[h]) for h in hs]
    bt = [b[h] * pinv[h] for h in hs]
    kt = [k[h] * pinv[h] for h in hs]
    a_ab = [jnp.where(strict, dot(at[h], bt[h], _NT), 0.0) for h in hs]
    a_ak = [jnp.where(strict, dot(at[h], kt[h], _NT), 0.0) for h in hs]
    a_rb = [jnp.where(incl, dot(rt[h], bt[h], _NT), 0.0) for h in hs]
    a_rk = [jnp.where(incl, dot(rt[h], kt[h], _NT), 0.0) for h in hs]
    x = a_ab
    minv = [eye + x[h] for h in hs]
    for _ in range(nsq):
        x = [dot(x[h], x[h], _NN) for h in hs]
        minv = [minv[h] + dot(minv[h], x[h], _NN) for h in hs]
    s0 = [s_ref[h] for h in hs]
    rhs = [dot(at[h], s0[h], _NT) + dot(a_ak[h], v[h], _NN) for h in hs]
    u = [dot(minv[h], rhs[h], _NN) for h in hs]
    for h in hs:
        y_ref[:, h * n:(h + 1) * n] = dot(rt[h], s0[h], _NT) + dot(a_rb[h], u[h], _NN) + dot(a_rk[h], v[h], _NN)
    for h in hs:
        s_ref[h] = (s0[h] * jnp.exp(tot[h]) + dot(u[h], b[h] * pend[h], _TN)
                    + dot(v[h], k[h] * pend[h], _TN))


def wkv7(r, lw, k, v, a, b, n=B_HEAD, exact=False):
    bsz, t, width = r.shape
    nh = width // n
    c = min(WKV_CHUNK, t)
    spec = pl.BlockSpec((None, c, width), lambda bi, ci: (bi, ci, 0))
    return pl.pallas_call(
        functools.partial(_wkv_body, nh=nh, n=n, c=c, exact=exact),
        grid=(bsz, t // c),
        in_specs=[spec] * 6,
        out_specs=spec,
        out_shape=jax.ShapeDtypeStruct((bsz, t, width), F32),
        scratch_shapes=[pltpu.VMEM((nh, n, n), F32)],
        compiler_params=_params(("parallel", "arbitrary")),
        name="wkv7_chunked",
    )(r, lw, k, v, a, b)


def _dsa_index_body(ki_ref, qit_ref, wit_ref, bias_ref, key_ref, cut_ref, *, tq, tkc, topk, nheads, t):
    i = pl.program_id(1)
    nch = ((i + 1) * tq + tkc - 1) // tkc
    qpos = i * tq + lax.broadcasted_iota(I32, (1, tq), 1)

    def kpos_of(ci):
        return ci * tkc + lax.broadcasted_iota(I32, (tkc, 1), 0)

    def rows(ci):
        return pl.ds(pl.multiple_of(ci * tkc, tkc), tkc)

    def score_chunk(ci, carry):
        kc = ki_ref[rows(ci), :]
        acc = jnp.zeros((tkc, tq), F32)
        for h in range(nheads):
            s = jnp.dot(kc, qit_ref[h], preferred_element_type=F32)
            acc = acc + jnp.maximum(s, 0.0) * wit_ref[h]
        acc = jnp.where(kpos_of(ci) <= qpos, acc, -jnp.inf)
        bits = lax.bitcast_convert_type(acc, I32)
        key_ref[rows(ci), :] = jnp.where(bits >= 0, bits, bits ^ 0x7FFFFFFF)
        return carry

    lax.fori_loop(0, nch, score_chunk, 0)

    def count(pred):
        def body(ci, cnt):
            ones = pred(key_ref[rows(ci), :], kpos_of(ci))
            return cnt + jnp.sum(ones, axis=0, keepdims=True)
        return lax.fori_loop(0, nch, body, jnp.zeros((1, tq), I32))

    def value_bit(step, prefix):
        cand = prefix | lax.shift_left(jnp.int32(1), 31 - step)
        cand_signed = cand ^ INT_MIN
        cnt = count(lambda key, kpos: jnp.where(key >= cand_signed, 1, 0))
        return jnp.where(cnt >= topk, cand, prefix)

    thr = lax.fori_loop(0, 32, value_bit, jnp.zeros((1, tq), I32)) ^ INT_MIN
    n_ge = count(lambda key, kpos: jnp.where(key >= thr, 1, 0))
    n_gt = count(lambda key, kpos: jnp.where(key > thr, 1, 0))
    need = topk - n_gt
    cut_ref[...] = jnp.full((1, tq), t, I32)

    @pl.when(jnp.max(n_ge) > topk)
    def _():
        nbits = max(1, (t - 1).bit_length())

        def index_bit(step, x):
            cand = x | lax.shift_left(jnp.int32(1), nbits - 1 - step)
            g = count(lambda key, kpos: jnp.where(key == thr, jnp.where(kpos < cand, 1, 0), 0))
            return jnp.where(g < need, cand, x)

        cut_ref[...] = lax.fori_loop(0, nbits, index_bit, jnp.zeros((1, tq), I32))

    cutoff = cut_ref[...]

    def write_chunk(ci, carry):
        key = key_ref[rows(ci), :]
        kpos = kpos_of(ci)
        tie = jnp.where(kpos <= cutoff, 0.0, NEG)
        sel = jnp.where(key > thr, 0.0, jnp.where(key == thr, tie, NEG))
        bias_ref[rows(ci), :] = jnp.where(kpos <= qpos, sel, NEG).astype(BF16)
        return carry

    lax.fori_loop(0, nch, write_chunk, 0)

    def fill_chunk(ci, carry):
        bias_ref[rows(ci), :] = jnp.full((tkc, tq), NEG, BF16)
        return carry

    lax.fori_loop(nch, t // tkc, fill_chunk, 0)


def dsa_select(ki, qi, wi, topk):
    b, t, nh, d = qi.shape
    tq = _pick(t, (256, 128))
    tkc = _pick(t, (512, 256, 128))
    nq = t // tq
    qit = jnp.transpose(qi.astype(BF16).reshape(b, nq, tq, nh, d), (0, 1, 3, 4, 2))
    wit = jnp.transpose(wi.astype(F32).reshape(b, nq, tq, nh), (0, 1, 3, 2))[:, :, :, None, :]
    return pl.pallas_call(
        functools.partial(_dsa_index_body, tq=tq, tkc=tkc, topk=topk, nheads=nh, t=t),
        grid=(b, nq),
        in_specs=[pl.BlockSpec((None, t, d), lambda bi, qi_: (bi, 0, 0)),
                  pl.BlockSpec((None, None, nh, d, tq), lambda bi, qi_: (bi, qi_, 0, 0, 0)),
                  pl.BlockSpec((None, None, nh, 1, tq), lambda bi, qi_: (bi, qi_, 0, 0, 0))],
        out_specs=pl.BlockSpec((None, t, tq), lambda bi, qi_: (bi, 0, qi_)),
        out_shape=jax.ShapeDtypeStruct((b, t, t), BF16),
        scratch_shapes=[pltpu.VMEM((t, tq), I32), pltpu.VMEM((1, tq), I32)],
        compiler_params=_params(("parallel", "arbitrary")),
        name="dsa_index_topk",
    )(ki.astype(BF16), qit, wit)


def _dsa_attn_body(k_ref, vt_ref, q_ref, bias_ref, o_ref, m_ref, l_ref, acc_ref, *, tq, tk, rep):
    i = pl.program_id(2)
    nch = ((i + 1) * tq + tk - 1) // tk
    m_ref[...] = jnp.full(m_ref.shape, NEG, F32)
    l_ref[...] = jnp.zeros(l_ref.shape, F32)
    acc_ref[...] = jnp.zeros(acc_ref.shape, F32)

    def chunk(j, carry):
        rows = pl.ds(pl.multiple_of(j * tk, tk), tk)
        kj = k_ref[rows, :]
        vj = vt_ref[j]
        bj = bias_ref[rows, :].astype(F32)
        ss = [jnp.dot(kj, q_ref[r], preferred_element_type=F32) + bj for r in range(rep)]
        _softmax_steps(ss, vj, m_ref, l_ref, acc_ref)
        return carry

    lax.fori_loop(0, nch, chunk, 0)
    for r in range(rep):
        o_ref[r] = acc_ref[r] / l_ref[r]


def dsa_attention(q, k, v, bias):
    b, t, hq, d = q.shape
    g = k.shape[2]
    rep = hq // g
    tq = _pick(t, (256, 128))
    tk = _pick(t, (512, 256, 128))
    nq, nk = t // tq, t // tk
    qs = (q * (d ** -0.5 * LOG2E)).astype(BF16)
    qt = jnp.transpose(qs.reshape(b, nq, tq, g, rep, d), (0, 3, 1, 4, 5, 2))
    kk = k.reshape(b, t, g * d).astype(BF16)
    vt = jnp.transpose(v.astype(BF16).reshape(b, nk, tk, g, d), (0, 3, 1, 4, 2))
    out = pl.pallas_call(
        functools.partial(_dsa_attn_body, tq=tq, tk=tk, rep=rep),
        grid=(b, g, nq),
        in_specs=[pl.BlockSpec((None, t, d), lambda bi, gi, qi: (bi, 0, gi)),
                  pl.BlockSpec((None, None, nk, d, tk), lambda bi, gi, qi: (bi, gi, 0, 0, 0)),
                  pl.BlockSpec((None, None, None, rep, d, tq), lambda bi, gi, qi: (bi, gi, qi, 0, 0, 0)),
                  pl.BlockSpec((None, t, tq), lambda bi, gi, qi: (bi, 0, qi))],
        out_specs=pl.BlockSpec((None, None, None, rep, d, tq), lambda bi, gi, qi: (bi, gi, qi, 0, 0, 0)),
        out_shape=jax.ShapeDtypeStruct((b, g, nq, rep, d, tq), F32),
        scratch_shapes=[pltpu.VMEM((rep, 1, tq), F32), pltpu.VMEM((rep, 1, tq), F32),
                        pltpu.VMEM((rep, d, tq), F32)],
        compiler_params=_params(("parallel", "parallel", "arbitrary")),
        name="dsa_attention",
    )(kk, vt, qt, bias)
    return jnp.transpose(out, (0, 2, 5, 1, 3, 4)).reshape(b, t, hq * d)


def _rms_norm(x, g, eps=NORM_EPS):
    xf = x.astype(F32)
    return xf * lax.rsqrt(jnp.mean(xf * xf, axis=-1, keepdims=True) + eps) * g.astype(F32)


def _layer_norm(x, g, b, eps=LN_EPS):
    mu = jnp.mean(x, axis=-1, keepdims=True)
    var = jnp.mean(jnp.square(x - mu), axis=-1, keepdims=True)
    return (x - mu) * lax.rsqrt(var + eps) * g + b


def _partial_rope(x, pos):
    dh = x.shape[-1]
    rot = dh // ROPE_FRACTION
    half = rot // 2
    inv_freq = ROPE_THETA ** (-(jnp.arange(half, dtype=F32) * 2.0 / rot))
    ang = pos.astype(F32)[..., None] * inv_freq
    cos = jnp.cos(ang)[:, :, None, :]
    sin = jnp.sin(ang)[:, :, None, :]
    x1 = x[..., :half]
    x2 = x[..., half:rot]
    return jnp.concatenate([x1 * cos - x2 * sin, x2 * cos + x1 * sin, x[..., rot:]], axis=-1)


def _pad_cols(w, n):
    return jnp.pad(w, ((0, 0), (0, n - w.shape[1])))


def _pad_rows(w, n):
    return jnp.pad(w, ((0, n - w.shape[0]), (0, 0)))


def _round_up(n, m):
    return (n + m - 1) // m * m


def _mm3(x, w, **kw):
    b, t, _ = x.shape
    return matmul(x.reshape(b * t, x.shape[-1]), w, **kw).reshape(b, t, w.shape[1])


def _lora(x, w2):
    r = _round_up(x.shape[-1], 128)
    xp = jnp.pad(x, ((0, 0), (0, 0), (0, r - x.shape[-1])))
    return _mm3(xp, _pad_rows(w2, r))


def _mixer_ab(u, pos, w_in, lq1, lk1, lq2, lk2, subln_g, mu, w0, w2, a0, a2, g2,
              k_k, k_a, r_k, lnx_g, lnx_b, lambda_init):
    b, t, d = u.shape
    a_width = d // 2
    b_width = d // 2
    a_heads = a_width // A_DV
    b_heads = b_width // B_HEAD
    a_in = 3 * a_width
    n_in = w_in.shape[1]
    z = _mm3(u, _pad_cols(w_in, _round_up(n_in, 512)))[..., :n_in]
    za, zb = z[..., :a_in], z[..., a_in:]

    qa, ka, va = jnp.split(za, 3, axis=-1)
    qa = _partial_rope(qa.reshape(b, t, 2 * a_heads, A_DC), pos).reshape(b, t, a_heads, 2, A_DC)
    ka = _partial_rope(ka.reshape(b, t, 2 * a_heads, A_DC), pos).reshape(b, t, a_heads, 2, A_DC)
    va = va.reshape(b, t, a_heads, A_DV)
    lam = jnp.exp(jnp.sum(lq1 * lk1)) - jnp.exp(jnp.sum(lq2 * lk2)) + lambda_init
    ya = diff_attention_core(qa, ka, va, lam)
    ya = (_rms_norm(ya, subln_g, DIFF_SUBLN_EPS) * (1.0 - lambda_init)).reshape(b, t, a_width)

    prev = jnp.pad(zb, ((0, 0), (1, 0), (0, 0)))[:, :t]
    zb = zb + (prev - zb) * mu
    dl = w2.shape[0]
    al = a2.shape[0]
    cuts = [b_width, 2 * b_width, 3 * b_width, 3 * b_width + dl, 3 * b_width + dl + al]
    r, k, v, xw, xa, xg = jnp.split(zb, cuts, axis=-1)
    w = -jax.nn.softplus(-(w0 + _lora(jnp.tanh(xw), w2))) - 0.5
    a = jax.nn.sigmoid(a0 + _lora(xa, a2))
    g = _lora(jax.nn.sigmoid(xg), g2)
    hd = lambda x: x.reshape(b, t, b_heads, B_HEAD)
    kk = hd(k * k_k)
    kk = kk / jnp.maximum(jnp.sqrt(jnp.sum(kk * kk, axis=-1, keepdims=True)), 1e-12)
    k = k * (1.0 + (a - 1.0) * k_a)
    log_decay = -jnp.exp(w)
    fl = lambda x: x.reshape(b, t, b_width)
    y = hd(wkv7(r, log_decay, k, v, fl(-kk), fl(kk * hd(a))))
    mu_y = jnp.mean(y, axis=-1, keepdims=True)
    var_y = jnp.mean(jnp.square(y - mu_y), axis=-1, keepdims=True)
    y = (y - mu_y) * lax.rsqrt(var_y + LNX_EPS)
    y = y * lnx_g.reshape(b_heads, B_HEAD) + lnx_b.reshape(b_heads, B_HEAD)
    bonus = jnp.sum(hd(r) * hd(k) * r_k, axis=-1, keepdims=True) * hd(v)
    yb = (y + bonus).reshape(b, t, b_width) * g
    return jnp.concatenate([ya.astype(BF16), yb.astype(BF16)], axis=-1)


def _mixer_c(u, pos, w_in, idx_k_g, idx_k_b, topk):
    b, t, d = u.shape
    c_heads = d // C_HEAD_DIM
    n_in = w_in.shape[1]
    z = _mm3(u, _pad_cols(w_in, _round_up(n_in, 512)))[..., :n_in]
    c1 = c_heads * C_HEAD_DIM
    c2 = c1 + C_KV_HEADS * C_HEAD_DIM
    c3 = c2 + C_KV_HEADS * C_HEAD_DIM
    c4 = c3 + IDX_HEADS * IDX_DIM
    c5 = c4 + IDX_DIM
    q, k, v, qi, ki, wi = jnp.split(z, [c1, c2, c3, c4, c5], axis=-1)
    q = _partial_rope(q.reshape(b, t, c_heads, C_HEAD_DIM), pos)
    k = _partial_rope(k.reshape(b, t, C_KV_HEADS, C_HEAD_DIM), pos)
    v = v.reshape(b, t, C_KV_HEADS, C_HEAD_DIM)
    qi = _partial_rope(qi.reshape(b, t, IDX_HEADS, IDX_DIM), pos)
    ki = _partial_rope(_layer_norm(ki, idx_k_g, idx_k_b)[:, :, None, :], pos)[:, :, 0]
    wi = wi * ((IDX_HEADS * IDX_DIM) ** -0.5)
    bias = dsa_select(ki, qi, wi, topk)
    return dsa_attention(q, k, v, bias).astype(BF16)


def kernel(x, p, positions, mix_pre_g, mix_post_g, mlp_pre_g, mlp_post_g, w_mlp_up, w_mlp_down, w_ple_proj, w_ple_gate, ple_post_g, ab_w_in, ab_w_out, diff_lq1, diff_lk1, diff_lq2, diff_lk2, diff_subln_g, rwkv_mu, rwkv_w0, rwkv_w2, rwkv_a0, rwkv_a2, rwkv_g2, rwkv_k_k, rwkv_k_a, rwkv_r_k, rwkv_lnx_g, rwkv_lnx_b, c_w_in, c_w_out, idx_k_g, idx_k_b):
    depth = mix_pre_g.shape[0]
    b, t, d = x.shape
    topk = min(TOPK_MAX, t // 4)
    flat = lambda z: z.reshape(b * t, z.shape[-1])
    h = flat(x)
    u = _rms_norm(h, mix_pre_g[0]).astype(BF16)
    for i in range(depth):
        j = i // 2
        u3 = u.reshape(b, t, d)
        if i % 2 == 0:
            lambda_init = 0.8 - 0.6 * math.exp(-0.3 * i)
            m = _mixer_ab(u3, positions, ab_w_in[j], diff_lq1[j], diff_lk1[j], diff_lq2[j],
                          diff_lk2[j], diff_subln_g[j], rwkv_mu[j], rwkv_w0[j], rwkv_w2[j], rwkv_a0[j],
                          rwkv_a2[j], rwkv_g2[j], rwkv_k_k[j], rwkv_k_a[j], rwkv_r_k[j], rwkv_lnx_g[j],
                          rwkv_lnx_b[j], lambda_init)
            w_out = ab_w_out[j]
        else:
            m = _mixer_c(u3, positions, c_w_in[j], idx_k_g[j], idx_k_b[j], topk)
            w_out = c_w_out[j]
        h, u = proj_residual(flat(m), w_out, h, mix_post_g[i], mlp_pre_g[i])
        up = matmul(u, w_mlp_up[i], out_dtype=BF16, act="relu2")
        h, _ = proj_residual(up, w_mlp_down[i], h, mlp_post_g[i], None)
        g_next = mix_pre_g[i + 1] if i + 1 < depth else None
        h, u = ple_residual(flat(p[i]), w_ple_proj[i], w_ple_gate[i], h, ple_post_g[i], g_next)
    return h.reshape(b, t, d)
```

```python
import functools
import math

import jax
import jax.numpy as jnp
from jax import lax
from jax.experimental import pallas as pl
from jax.experimental.pallas import tpu as pltpu

F32 = jnp.float32
BF16 = jnp.bfloat16
I32 = jnp.int32

V7X_VMEM_LIMIT_BYTES = 48 * 1024 * 1024
NEG = -1e30
LOG2E = 1.4426950408889634
INT_MIN = -2147483648

NORM_EPS = 1e-6
ROPE_THETA = 500000.0
ROPE_FRACTION = 4
A_DV = 128
A_DC = 64
DIFF_SUBLN_EPS = 1e-5
B_HEAD = 64
LNX_EPS = 64e-5
C_HEAD_DIM = 128
C_KV_HEADS = 4
IDX_HEADS = 16
IDX_DIM = 64
TOPK_MAX = 256
LN_EPS = 1e-6
WKV_CHUNK = 64


def _params(sem):
    return pltpu.CompilerParams(dimension_semantics=sem, vmem_limit_bytes=V7X_VMEM_LIMIT_BYTES)


def _pick(n, prefs):
    for t in prefs:
        if n % t == 0:
            return t
    return n


def _mm_body(a_ref, w_ref, o_ref, *scratch, nk, act):
    def epilogue(acc):
        if act == "relu2":
            r = jnp.maximum(acc, 0.0)
            acc = r * r
        return acc.astype(o_ref.dtype)

    if nk == 1:
        o_ref[...] = epilogue(jnp.dot(a_ref[...], w_ref[...], preferred_element_type=F32))
        return
    (acc_ref,) = scratch
    k = pl.program_id(2)

    @pl.when(k == 0)
    def _():
        acc_ref[...] = jnp.zeros_like(acc_ref)

    acc_ref[...] += jnp.dot(a_ref[...], w_ref[...], preferred_element_type=F32)

    @pl.when(k == nk - 1)
    def _():
        o_ref[...] = epilogue(acc_ref[...])


def matmul(a, w, out_dtype=F32, act=None):
    a = a.astype(BF16)
    w = w.astype(BF16)
    m, kdim = a.shape
    n = w.shape[1]
    tm = _pick(m, (1024, 512, 256, 128, 64, 32, 16, 8))
    tn = _pick(n, (1024, 512, 256, 128))
    tk = kdim if kdim <= 2048 else _pick(kdim, (2048, 1024, 512))
    nk = kdim // tk
    scratch = [pltpu.VMEM((tm, tn), F32)] if nk > 1 else []
    return pl.pallas_call(
        functools.partial(_mm_body, nk=nk, act=act),
        grid=(m // tm, n // tn, nk),
        in_specs=[pl.BlockSpec((tm, tk), lambda i, j, k: (i, k)),
                  pl.BlockSpec((tk, tn), lambda i, j, k: (k, j))],
        out_specs=pl.BlockSpec((tm, tn), lambda i, j, k: (i, j)),
        out_shape=jax.ShapeDtypeStruct((m, n), out_dtype),
        scratch_shapes=scratch,
        compiler_params=_params(("parallel", "parallel", "arbitrary")),
        name="dense_matmul",
    )(a, w)


def _rms_rows(x, g):
    return x * lax.rsqrt(jnp.mean(x * x, axis=-1, keepdims=True) + NORM_EPS) * g


def _residual_update(m, h_ref, gpost_ref, gnext_ref, ho_ref, uo_ref):
    h_new = h_ref[...] + _rms_rows(m, gpost_ref[...])
    ho_ref[...] = h_new
    if uo_ref is not None:
        uo_ref[...] = _rms_rows(h_new, gnext_ref[...]).astype(BF16)


def _proj_res_body(*refs, nk, has_next):
    if has_next:
        a_ref, w_ref, h_ref, gpost_ref, gnext_ref, ho_ref, uo_ref, acc_ref = refs
    else:
        a_ref, w_ref, h_ref, gpost_ref, ho_ref, acc_ref = refs
        gnext_ref = uo_ref = None
    k = pl.program_id(1)

    @pl.when(k == 0)
    def _():
        acc_ref[...] = jnp.zeros_like(acc_ref)

    acc_ref[...] += jnp.dot(a_ref[...], w_ref[...], preferred_element_type=F32)

    @pl.when(k == nk - 1)
    def _():
        _residual_update(acc_ref[...], h_ref, gpost_ref, gnext_ref, ho_ref, uo_ref)


def proj_residual(a, w, h, g_post, g_next):
    m, kdim = a.shape
    n = w.shape[1]
    tm = _pick(m, (512, 256, 128, 64, 32, 16, 8))
    tk = _pick(kdim, (1024, 512, 256, 128))
    nk = kdim // tk
    has_next = g_next is not None
    row = pl.BlockSpec((tm, n), lambda i, k: (i, 0))
    vec = pl.BlockSpec((1, n), lambda i, k: (0, 0))
    in_specs = [pl.BlockSpec((tm, tk), lambda i, k: (i, k)), pl.BlockSpec((tk, n), lambda i, k: (k, 0)), row, vec]
    args = [a.astype(BF16), w.astype(BF16), h, g_post.reshape(1, n).astype(F32)]
    out_shape = [jax.ShapeDtypeStruct((m, n), F32)]
    out_specs = [row]
    if has_next:
        in_specs.append(vec)
        args.append(g_next.reshape(1, n).astype(F32))
        out_shape.append(jax.ShapeDtypeStruct((m, n), BF16))
        out_specs.append(row)
    out = pl.pallas_call(
        functools.partial(_proj_res_body, nk=nk, has_next=has_next),
        grid=(m // tm, nk),
        in_specs=in_specs, out_specs=out_specs, out_shape=out_shape,
        scratch_shapes=[pltpu.VMEM((tm, n), F32)],
        compiler_params=_params(("parallel", "arbitrary")),
        name="proj_residual",
    )(*args)
    return (out[0], out[1]) if has_next else (out[0], None)


def _ple_body(*refs, has_next):
    if has_next:
        p_ref, wp_ref, wg_ref, h_ref, gpost_ref, gnext_ref, ho_ref, uo_ref = refs
    else:
        p_ref, wp_ref, wg_ref, h_ref, gpost_ref, ho_ref = refs
        gnext_ref = uo_ref = None
    e = jnp.dot(p_ref[...], wp_ref[...], preferred_element_type=F32)
    gate = jax.nn.sigmoid(jnp.dot(h_ref[...].astype(BF16), wg_ref[...], preferred_element_type=F32))
    _residual_update(e * gate, h_ref, gpost_ref, gnext_ref, ho_ref, uo_ref)


def ple_residual(p, w_proj, w_gate, h, g_post, g_next):
    m, n = h.shape
    pd = p.shape[1]
    tm = _pick(m, (256, 128, 64, 32, 16, 8))
    has_next = g_next is not None
    row = pl.BlockSpec((tm, n), lambda i: (i, 0))
    vec = pl.BlockSpec((1, n), lambda i: (0, 0))
    in_specs = [pl.BlockSpec((tm, pd), lambda i: (i, 0)), pl.BlockSpec((pd, n), lambda i: (0, 0)),
                pl.BlockSpec((n, n), lambda i: (0, 0)), row, vec]
    args = [p.astype(BF16), w_proj.astype(BF16), w_gate.astype(BF16), h, g_post.reshape(1, n).astype(F32)]
    out_shape = [jax.ShapeDtypeStruct((m, n), F32)]
    out_specs = [row]
    if has_next:
        in_specs.append(vec)
        args.append(g_next.reshape(1, n).astype(F32))
        out_shape.append(jax.ShapeDtypeStruct((m, n), BF16))
        out_specs.append(row)
    out = pl.pallas_call(
        functools.partial(_ple_body, has_next=has_next),
        grid=(m // tm,),
        in_specs=in_specs, out_specs=out_specs, out_shape=out_shape,
        compiler_params=_params(("parallel",)),
        name="ple_residual",
    )(*args)
    return (out[0], out[1]) if has_next else (out[0], None)


ONES_ROWS = 16


def _with_ones_rows(vt):
    pad = jnp.zeros(vt.shape[:-2] + (ONES_ROWS, vt.shape[-1]), vt.dtype).at[..., 0, :].set(1)
    return jnp.concatenate([vt, pad], axis=-2)


def _softmax_steps(score_fns, vt, m_ref, acc_ref):
    n = range(len(score_fns))
    ss = [fn() for fn in score_fns]
    m_old = [m_ref[i] for i in n]
    m_new = [jnp.maximum(m_old[i], jnp.max(ss[i], axis=0, keepdims=True)) for i in n]
    alpha = [jnp.exp2(m_old[i] - m_new[i]) for i in n]
    p = [jnp.exp2((ss[i] - m_new[i]).astype(BF16)) for i in n]
    for i in n:
        m_ref[i] = m_new[i]
    pv = [jnp.dot(vt, p[i], preferred_element_type=F32) for i in n]
    for i in n:
        acc_ref[i] = alpha[i] * acc_ref[i] + pv[i]


def _diff_attn_body(lam_ref, k_ref, q_ref, vt_ref, o_ref, m_ref, acc_ref, *, tq, nsp):
    i = pl.program_id(2)
    tl = tq // nsp
    m_ref[...] = jnp.full(m_ref.shape, NEG, F32)
    acc_ref[...] = jnp.zeros(acc_ref.shape, F32)

    def chunk(j, masked):
        kj = k_ref[pl.ds(pl.multiple_of(j * tq, tq), tq), :]
        vj = vt_ref[j]
        def score(c, sp):
            s = jnp.dot(kj, q_ref[c, :, sp * tl:(sp + 1) * tl], preferred_element_type=F32)
            if masked:
                row = lax.broadcasted_iota(I32, (tq, tl), 0)
                col = lax.broadcasted_iota(I32, (tq, tl), 1)
                s = jnp.where(row <= col + sp * tl, s, NEG)
            return s

        fns = [functools.partial(score, c, sp) for c in range(2) for sp in range(nsp)]
        _softmax_steps(fns, vj, m_ref, acc_ref)

    def full_chunk(j, carry):
        chunk(j, False)
        return carry

    lax.fori_loop(0, i, full_chunk, 0)
    chunk(i, True)
    lam = lam_ref[0]
    for sp in range(nsp):
        a0 = acc_ref[sp]
        a1 = acc_ref[nsp + sp]
        o_ref[:, sp * tl:(sp + 1) * tl] = (a0[:A_DV] / a0[A_DV:A_DV + 1]
                                           - lam * (a1[:A_DV] / a1[A_DV:A_DV + 1]))


def diff_attention_core(q, k, v, lam):
    b, t, h = q.shape[0], q.shape[1], q.shape[2]
    tq = _pick(t, (512, 256, 128))
    nq = t // tq
    qs = (q * (A_DC ** -0.5 * LOG2E)).astype(BF16)
    qt = jnp.transpose(qs.reshape(b, nq, tq, h, 2, A_DC), (0, 3, 1, 4, 5, 2))
    zeros = jnp.zeros_like(qt[:, :, :, 0])
    qt = jnp.stack([jnp.concatenate([qt[:, :, :, 0], zeros], axis=-2),
                    jnp.concatenate([zeros, qt[:, :, :, 1]], axis=-2)], axis=3)
    kk = k.reshape(b, t, h * 2 * A_DC).astype(BF16)
    vt = jnp.transpose(v.astype(BF16).reshape(b, nq, tq, h, A_DV), (0, 3, 1, 4, 2))
    vt = _with_ones_rows(vt)
    dve = A_DV + ONES_ROWS
    nsp = 2 if tq >= 256 else 1
    out = pl.pallas_call(
        functools.partial(_diff_attn_body, tq=tq, nsp=nsp),
        grid=(b, h, nq),
        in_specs=[pl.BlockSpec(memory_space=pltpu.SMEM),
                  pl.BlockSpec((None, t, 2 * A_DC), lambda bi, hi, qi: (bi, 0, hi)),
                  pl.BlockSpec((None, None, None, 2, 2 * A_DC, tq), lambda bi, hi, qi: (bi, hi, qi, 0, 0, 0)),
                  pl.BlockSpec((None, None, nq, dve, tq), lambda bi, hi, qi: (bi, hi, 0, 0, 0))],
        out_specs=pl.BlockSpec((None, None, None, A_DV, tq), lambda bi, hi, qi: (bi, hi, qi, 0, 0)),
        out_shape=jax.ShapeDtypeStruct((b, h, nq, A_DV, tq), F32),
        scratch_shapes=[pltpu.VMEM((2 * nsp, 1, tq // nsp), F32),
                        pltpu.VMEM((2 * nsp, dve, tq // nsp), F32)],
        compiler_params=_params(("parallel", "parallel", "arbitrary")),
        name="diff_attention",
    )(lam.reshape(1).astype(F32), kk, qt, vt)
    return jnp.transpose(out, (0, 2, 4, 1, 3)).reshape(b, t, h, A_DV)


def _dot(a, b, dims, exact):
    if exact:
        return lax.dot_general(a, b, (dims, ((), ())), precision=lax.Precision.HIGHEST,
                               preferred_element_type=F32)
    return lax.dot_general(a.astype(BF16), b.astype(BF16), (dims, ((), ())), preferred_element_type=F32)


_NN = ((1,), (0,))
_NT = ((1,), (1,))
_TN = ((0,), (0,))


def _wkv_body(r_ref, lw_ref, k_ref, v_ref, a_ref, b_ref, y_ref, s_ref, *, nh, n, c, exact):
    @pl.when(pl.program_id(1) == 0)
    def _():
        s_ref[...] = jnp.zeros(s_ref.shape, F32)

    row = lax.broadcasted_iota(I32, (c, c), 0)
    col = lax.broadcasted_iota(I32, (c, c), 1)
    incl = row >= col
    strict = row > col
    tri = jnp.where(incl, 1.0, 0.0).astype(F32)
    eye = jnp.where(row == col, 1.0, 0.0).astype(F32)
    nsq = int(math.log2(c)) - 1

    hs = range(nh)
    dot = functools.partial(_dot, exact=exact)
    heads = lambda ref: [ref[:, h * n:(h + 1) * n] for h in hs]
    lw, r, k, v, a, b = (heads(ref) for ref in (lw_ref, r_ref, k_ref, v_ref, a_ref, b_ref))
    cum = [_dot(tri, lw[h], _NN, True) for h in hs]
    tot = [cum[h][c - 1:c, :] for h in hs]
    pinv = [jnp.exp(-cum[h]) for h in hs]
    pend = [jnp.exp(tot[h] - cum[h]) for h in hs]
    at = [a[h] * jnp.exp(cum[h] - lw[h]) for h in hs]
    rt = [r[h] * jnp.exp(cum[h]) for h in hs]
    bt = [b[h] * pinv[h] for h in hs]
    kt = [k[h] * pinv[h] for h in hs]
    a_ab = [jnp.where(strict, dot(at[h], bt[h], _NT), 0.0) for h in hs]
    a_ak = [jnp.where(strict, dot(at[h], kt[h], _NT), 0.0) for h in hs]
    a_rb = [jnp.where(incl, dot(rt[h], bt[h], _NT), 0.0) for h in hs]
    a_rk = [jnp.where(incl, dot(rt[h], kt[h], _NT), 0.0) for h in hs]
    x = a_ab
    minv = [eye + x[h] for h in hs]
    for _ in range(nsq):
        x = [dot(x[h], x[h], _NN) for h in hs]
        minv = [minv[h] + dot(minv[h], x[h], _NN) for h in hs]
    s0 = [s_ref[h] for h in hs]
    rhs = [dot(at[h], s0[h], _NT) + dot(a_ak[h], v[h], _NN) for h in hs]
    u = [dot(minv[h], rhs[h], _NN) for h in hs]
    for h in hs:
        y_ref[:, h * n:(h + 1) * n] = dot(rt[h], s0[h], _NT) + dot(a_rb[h], u[h], _NN) + dot(a_rk[h], v[h], _NN)
    for h in hs:
        s_ref[h] = (s0[h] * jnp.exp(tot[h]) + dot(u[h], b[h] * pend[h], _TN)
                    + dot(v[h], k[h] * pend[h], _TN))


def wkv7(r, lw, k, v, a, b, n=B_HEAD, exact=False):
    bsz, t, width = r.shape
    nh = width // n
    c = min(WKV_CHUNK, t)
    spec = pl.BlockSpec((None, c, width), lambda bi, ci: (bi, ci, 0))
    return pl.pallas_call(
        functools.partial(_wkv_body, nh=nh, n=n, c=c, exact=exact),
        grid=(bsz, t // c),
        in_specs=[spec] * 6,
        out_specs=spec,
        out_shape=jax.ShapeDtypeStruct((bsz, t, width), F32),
        scratch_shapes=[pltpu.VMEM((nh, n, n), F32)],
        compiler_params=_params(("parallel", "arbitrary")),
        name="wkv7_chunked",
    )(r, lw, k, v, a, b)


def _dsa_index_body(ki_ref, qit_ref, wit_ref, bias_ref, key_ref, cut_ref, *, tq, tkc, topk, nheads, t):
    i = pl.program_id(1)
    nch = ((i + 1) * tq + tkc - 1) // tkc
    qpos = i * tq + lax.broadcasted_iota(I32, (1, tq), 1)

    def kpos_of(ci):
        return ci * tkc + lax.broadcasted_iota(I32, (tkc, 1), 0)

    def rows(ci):
        return pl.ds(pl.multiple_of(ci * tkc, tkc), tkc)

    def score_chunk(ci, carry):
        kc = ki_ref[rows(ci), :]
        acc = jnp.zeros((tkc, tq), F32)
        for h in range(nheads):
            s = jnp.dot(kc, qit_ref[h], preferred_element_type=F32)
            acc = acc + jnp.maximum(s, 0.0) * wit_ref[h]
        acc = jnp.where(kpos_of(ci) <= qpos, acc, -jnp.inf)
        bits = lax.bitcast_convert_type(acc, I32)
        key_ref[rows(ci), :] = jnp.where(bits >= 0, bits, bits ^ 0x7FFFFFFF)
        return carry

    lax.fori_loop(0, nch, score_chunk, 0)

    def count(pred):
        def body(ci, cnt):
            ones = pred(key_ref[rows(ci), :], kpos_of(ci))
            return cnt + jnp.sum(ones, axis=0, keepdims=True)
        return lax.fori_loop(0, nch, body, jnp.zeros((1, tq), I32))

    def value_bit(step, prefix):
        cand = prefix | lax.shift_left(jnp.int32(1), 31 - step)
        cand_signed = cand ^ INT_MIN
        cnt = count(lambda key, kpos: jnp.where(key >= cand_signed, 1, 0))
        return jnp.where(cnt >= topk, cand, prefix)

    thr = lax.fori_loop(0, 32, value_bit, jnp.zeros((1, tq), I32)) ^ INT_MIN
    n_ge = count(lambda key, kpos: jnp.where(key >= thr, 1, 0))
    n_gt = count(lambda key, kpos: jnp.where(key > thr, 1, 0))
    need = topk - n_gt
    cut_ref[...] = jnp.full((1, tq), t, I32)

    @pl.when(jnp.max(n_ge) > topk)
    def _():
        nbits = max(1, (t - 1).bit_length())

        def index_bit(step, x):
            cand = x | lax.shift_left(jnp.int32(1), nbits - 1 - step)
            g = count(lambda key, kpos: jnp.where(key == thr, jnp.where(kpos < cand, 1, 0), 0))
            return jnp.where(g < need, cand, x)

        cut_ref[...] = lax.fori_loop(0, nbits, index_bit, jnp.zeros((1, tq), I32))

    cutoff = cut_ref[...]

    def write_chunk(ci, carry):
        key = key_ref[rows(ci), :]
        kpos = kpos_of(ci)
        tie = jnp.where(kpos <= cutoff, 0.0, NEG)
        sel = jnp.where(key > thr, 0.0, jnp.where(key == thr, tie, NEG))
        bias_ref[rows(ci), :] = jnp.where(kpos <= qpos, sel, NEG).astype(BF16)
        return carry

    lax.fori_loop(0, nch, write_chunk, 0)

    def fill_chunk(ci, carry):
        bias_ref[rows(ci), :] = jnp.full((tkc, tq), NEG, BF16)
        return carry

    lax.fori_loop(nch, t // tkc, fill_chunk, 0)


def dsa_select(ki, qi, wi, topk):
    b, t, nh, d = qi.shape
    tq = _pick(t, (256, 128))
    tkc = _pick(t, (512, 256, 128))
    nq = t // tq
    qit = jnp.transpose(qi.astype(BF16).reshape(b, nq, tq, nh, d), (0, 1, 3, 4, 2))
    wit = jnp.transpose(wi.astype(F32).reshape(b, nq, tq, nh), (0, 1, 3, 2))[:, :, :, None, :]
    return pl.pallas_call(
        functools.partial(_dsa_index_body, tq=tq, tkc=tkc, topk=topk, nheads=nh, t=t),
        grid=(b, nq),
        in_specs=[pl.BlockSpec((None, t, d), lambda bi, qi_: (bi, 0, 0)),
                  pl.BlockSpec((None, None, nh, d, tq), lambda bi, qi_: (bi, qi_, 0, 0, 0)),
                  pl.BlockSpec((None, None, nh, 1, tq), lambda bi, qi_: (bi, qi_, 0, 0, 0))],
        out_specs=pl.BlockSpec((None, t, tq), lambda bi, qi_: (bi, 0, qi_)),
        out_shape=jax.ShapeDtypeStruct((b, t, t), BF16),
        scratch_shapes=[pltpu.VMEM((t, tq), I32), pltpu.VMEM((1, tq), I32)],
        compiler_params=_params(("parallel", "arbitrary")),
        name="dsa_index_topk",
    )(ki.astype(BF16), qit, wit)


def _dsa_attn_body(k_ref, vt_ref, q_ref, bias_ref, o_ref, m_ref, acc_ref, *, tq, tk, rep, d):
    i = pl.program_id(2)
    nch = ((i + 1) * tq + tk - 1) // tk
    m_ref[...] = jnp.full(m_ref.shape, NEG, F32)
    acc_ref[...] = jnp.zeros(acc_ref.shape, F32)

    def chunk(j, carry):
        rows = pl.ds(pl.multiple_of(j * tk, tk), tk)
        kj = k_ref[rows, :]
        vj = vt_ref[j]
        bj = bias_ref[rows, :].astype(F32)
        fns = [lambda r=r: jnp.dot(kj, q_ref[r], preferred_element_type=F32) + bj for r in range(rep)]
        _softmax_steps(fns, vj, m_ref, acc_ref)
        return carry

    lax.fori_loop(0, nch, chunk, 0)
    for r in range(rep):
        acc = acc_ref[r]
        o_ref[r] = acc[:d] / acc[d:d + 1]


def dsa_attention(q, k, v, bias):
    b, t, hq, d = q.shape
    g = k.shape[2]
    rep = hq // g
    tq = _pick(t, (256, 128))
    tk = _pick(t, (512, 256, 128))
    nq, nk = t // tq, t // tk
    qs = (q * (d ** -0.5 * LOG2E)).astype(BF16)
    qt = jnp.transpose(qs.reshape(b, nq, tq, g, rep, d), (0, 3, 1, 4, 5, 2))
    kk = k.reshape(b, t, g * d).astype(BF16)
    vt = jnp.transpose(v.astype(BF16).reshape(b, nk, tk, g, d), (0, 3, 1, 4, 2))
    vt = _with_ones_rows(vt)
    dve = d + ONES_ROWS
    out = pl.pallas_call(
        functools.partial(_dsa_attn_body, tq=tq, tk=tk, rep=rep, d=d),
        grid=(b, g, nq),
        in_specs=[pl.BlockSpec((None, t, d), lambda bi, gi, qi: (bi, 0, gi)),
                  pl.BlockSpec((None, None, nk, dve, tk), lambda bi, gi, qi: (bi, gi, 0, 0, 0)),
                  pl.BlockSpec((None, None, None, rep, d, tq), lambda bi, gi, qi: (bi, gi, qi, 0, 0, 0)),
                  pl.BlockSpec((None, t, tq), lambda bi, gi, qi: (bi, 0, qi))],
        out_specs=pl.BlockSpec((None, None, None, rep, d, tq), lambda bi, gi, qi: (bi, gi, qi, 0, 0, 0)),
        out_shape=jax.ShapeDtypeStruct((b, g, nq, rep, d, tq), F32),
        scratch_shapes=[pltpu.VMEM((rep, 1, tq), F32), pltpu.VMEM((rep, dve, tq), F32)],
        compiler_params=_params(("parallel", "parallel", "arbitrary")),
        name="dsa_attention",
    )(kk, vt, qt, bias)
    return jnp.transpose(out, (0, 2, 5, 1, 3, 4)).reshape(b, t, hq * d)


def _rms_norm(x, g, eps=NORM_EPS):
    xf = x.astype(F32)
    return xf * lax.rsqrt(jnp.mean(xf * xf, axis=-1, keepdims=True) + eps) * g.astype(F32)


def _layer_norm(x, g, b, eps=LN_EPS):
    mu = jnp.mean(x, axis=-1, keepdims=True)
    var = jnp.mean(jnp.square(x - mu), axis=-1, keepdims=True)
    return (x - mu) * lax.rsqrt(var + eps) * g + b


def _partial_rope(x, pos):
    dh = x.shape[-1]
    rot = dh // ROPE_FRACTION
    half = rot // 2
    inv_freq = ROPE_THETA ** (-(jnp.arange(half, dtype=F32) * 2.0 / rot))
    ang = pos.astype(F32)[..., None] * inv_freq
    cos = jnp.cos(ang)[:, :, None, :]
    sin = jnp.sin(ang)[:, :, None, :]
    x1 = x[..., :half]
    x2 = x[..., half:rot]
    return jnp.concatenate([x1 * cos - x2 * sin, x2 * cos + x1 * sin, x[..., rot:]], axis=-1)


def _pad_cols(w, n):
    return jnp.pad(w, ((0, 0), (0, n - w.shape[1])))


def _pad_rows(w, n):
    return jnp.pad(w, ((0, n - w.shape[0]), (0, 0)))


def _round_up(n, m):
    return (n + m - 1) // m * m


def _mm3(x, w, **kw):
    b, t, _ = x.shape
    return matmul(x.reshape(b * t, x.shape[-1]), w, **kw).reshape(b, t, w.shape[1])


def _lora(x, w2):
    r = _round_up(x.shape[-1], 128)
    xp = jnp.pad(x, ((0, 0), (0, 0), (0, r - x.shape[-1])))
    return _mm3(xp, _pad_rows(w2, r))


def _mixer_ab(u, pos, w_in, lq1, lk1, lq2, lk2, subln_g, mu, w0, w2, a0, a2, g2,
              k_k, k_a, r_k, lnx_g, lnx_b, lambda_init):
    b, t, d = u.shape
    a_width = d // 2
    b_width = d // 2
    a_heads = a_width // A_DV
    b_heads = b_width // B_HEAD
    a_in = 3 * a_width
    n_in = w_in.shape[1]
    w_bf = w_in.astype(BF16)
    qa = _mm3(u, w_bf[:, :a_width])
    ka = _mm3(u, w_bf[:, a_width:2 * a_width])
    va = _mm3(u, w_bf[:, 2 * a_width:a_in], out_dtype=BF16)

    qa = _partial_rope(qa.reshape(b, t, 2 * a_heads, A_DC), pos).reshape(b, t, a_heads, 2, A_DC)
    ka = _partial_rope(ka.reshape(b, t, 2 * a_heads, A_DC), pos).reshape(b, t, a_heads, 2, A_DC)
    va = va.reshape(b, t, a_heads, A_DV)
    lam = jnp.exp(jnp.sum(lq1 * lk1)) - jnp.exp(jnp.sum(lq2 * lk2)) + lambda_init
    ya = diff_attention_core(qa, ka, va, lam)
    ya = (_rms_norm(ya, subln_g, DIFF_SUBLN_EPS) * (1.0 - lambda_init)).reshape(b, t, a_width)

    def shifted(lo, hi):
        z = _mm3(u, _pad_cols(w_bf[:, a_in + lo:a_in + hi], _round_up(hi - lo, 512)))[..., :hi - lo]
        prev = jnp.pad(z, ((0, 0), (1, 0), (0, 0)))[:, :t]
        return z + (prev - z) * mu[lo:hi]

    dl = w2.shape[0]
    al = a2.shape[0]
    r = shifted(0, b_width)
    k = shifted(b_width, 2 * b_width)
    v = shifted(2 * b_width, 3 * b_width)
    xw, xa, xg = jnp.split(shifted(3 * b_width, n_in - a_in), [dl, dl + al], axis=-1)
    w = -jax.nn.softplus(-(w0 + _lora(jnp.tanh(xw), w2))) - 0.5
    a = jax.nn.sigmoid(a0 + _lora(xa, a2))
    g = _lora(jax.nn.sigmoid(xg), g2)
    hd = lambda x: x.reshape(b, t, b_heads, B_HEAD)
    kk = hd(k * k_k)
    kk = kk / jnp.maximum(jnp.sqrt(jnp.sum(kk * kk, axis=-1, keepdims=True)), 1e-12)
    k = k * (1.0 + (a - 1.0) * k_a)
    log_decay = -jnp.exp(w)
    fl = lambda x: x.reshape(b, t, b_width)
    y = hd(wkv7(r, log_decay, k, v, fl(-kk), fl(kk * hd(a))))
    mu_y = jnp.mean(y, axis=-1, keepdims=True)
    var_y = jnp.mean(jnp.square(y - mu_y), axis=-1, keepdims=True)
    y = (y - mu_y) * lax.rsqrt(var_y + LNX_EPS)
    y = y * lnx_g.reshape(b_heads, B_HEAD) + lnx_b.reshape(b_heads, B_HEAD)
    bonus = jnp.sum(hd(r) * hd(k) * r_k, axis=-1, keepdims=True) * hd(v)
    yb = (y + bonus).reshape(b, t, b_width) * g
    return jnp.concatenate([ya.astype(BF16), yb.astype(BF16)], axis=-1)


def _mixer_c(u, pos, w_in, idx_k_g, idx_k_b, topk):
    b, t, d = u.shape
    c_heads = d // C_HEAD_DIM
    n_in = w_in.shape[1]
    c1 = c_heads * C_HEAD_DIM
    c2 = c1 + C_KV_HEADS * C_HEAD_DIM
    c3 = c2 + C_KV_HEADS * C_HEAD_DIM
    c4 = c3 + IDX_HEADS * IDX_DIM
    c5 = c4 + IDX_DIM
    w_bf = w_in.astype(BF16)
    q = _mm3(u, w_bf[:, :c1])
    k = _mm3(u, w_bf[:, c1:c2])
    v = _mm3(u, w_bf[:, c2:c3], out_dtype=BF16)
    qi = _mm3(u, w_bf[:, c3:c4])
    kw = _mm3(u, _pad_cols(w_bf[:, c4:], _round_up(n_in - c4, 128)))
    ki, wi = kw[..., :c5 - c4], kw[..., c5 - c4:n_in - c4]
    q = _partial_rope(q.reshape(b, t, c_heads, C_HEAD_DIM), pos)
    k = _partial_rope(k.reshape(b, t, C_KV_HEADS, C_HEAD_DIM), pos)
    v = v.reshape(b, t, C_KV_HEADS, C_HEAD_DIM)
    qi = _partial_rope(qi.reshape(b, t, IDX_HEADS, IDX_DIM), pos)
    ki = _partial_rope(_layer_norm(ki, idx_k_g, idx_k_b)[:, :, None, :], pos)[:, :, 0]
    wi = wi * ((IDX_HEADS * IDX_DIM) ** -0.5)
    bias = dsa_select(ki, qi, wi, topk)
    return dsa_attention(q, k, v, bias).astype(BF16)


def kernel(x, p, positions, mix_pre_g, mix_post_g, mlp_pre_g, mlp_post_g, w_mlp_up, w_mlp_down, w_ple_proj, w_ple_gate, ple_post_g, ab_w_in, ab_w_out, diff_lq1, diff_lk1, diff_lq2, diff_lk2, diff_subln_g, rwkv_mu, rwkv_w0, rwkv_w2, rwkv_a0, rwkv_a2, rwkv_g2, rwkv_k_k, rwkv_k_a, rwkv_r_k, rwkv_lnx_g, rwkv_lnx_b, c_w_in, c_w_out, idx_k_g, idx_k_b):
    depth = mix_pre_g.shape[0]
    b, t, d = x.shape
    topk = min(TOPK_MAX, t // 4)
    flat = lambda z: z.reshape(b * t, z.shape[-1])
    h = flat(x)
    u = _rms_norm(h, mix_pre_g[0]).astype(BF16)
    for i in range(depth):
        j = i // 2
        u3 = u.reshape(b, t, d)
        if i % 2 == 0:
            lambda_init = 0.8 - 0.6 * math.exp(-0.3 * i)
            m = _mixer_ab(u3, positions, ab_w_in[j], diff_lq1[j], diff_lk1[j], diff_lq2[j],
                          diff_lk2[j], diff_subln_g[j], rwkv_mu[j], rwkv_w0[j], rwkv_w2[j], rwkv_a0[j],
                          rwkv_a2[j], rwkv_g2[j], rwkv_k_k[j], rwkv_k_a[j], rwkv_r_k[j], rwkv_lnx_g[j],
                          rwkv_lnx_b[j], lambda_init)
            w_out = ab_w_out[j]
        else:
            m = _mixer_c(u3, positions, c_w_in[j], idx_k_g[j], idx_k_b[j], topk)
            w_out = c_w_out[j]
        h, u = proj_residual(flat(m), w_out, h, mix_post_g[i], mlp_pre_g[i])
        up = matmul(u, w_mlp_up[i], out_dtype=BF16, act="relu2")
        h, _ = proj_residual(up, w_mlp_down[i], h, mlp_post_g[i], None)
        g_next = mix_pre_g[i + 1] if i + 1 < depth else None
        h, u = ple_residual(flat(p[i]), w_ple_proj[i], w_ple_gate[i], h, ple_post_g[i], g_next)
    return h.reshape(b, t, d)
```

```python
import functools
import math

import jax
import jax.numpy as jnp
from jax import lax
from jax.experimental import pallas as pl
from jax.experimental.pallas import tpu as pltpu

F32 = jnp.float32
BF16 = jnp.bfloat16
I32 = jnp.int32

V7X_VMEM_LIMIT_BYTES = 48 * 1024 * 1024
NEG = -1e30
LOG2E = 1.4426950408889634
INT_MIN = -2147483648

NORM_EPS = 1e-6
ROPE_THETA = 500000.0
ROPE_FRACTION = 4
A_DV = 128
A_DC = 64
DIFF_SUBLN_EPS = 1e-5
B_HEAD = 64
LNX_EPS = 64e-5
C_HEAD_DIM = 128
C_KV_HEADS = 4
IDX_HEADS = 16
IDX_DIM = 64
TOPK_MAX = 256
LN_EPS = 1e-6
WKV_CHUNK = 64


def _params(sem):
    return pltpu.CompilerParams(dimension_semantics=sem, vmem_limit_bytes=V7X_VMEM_LIMIT_BYTES)


def _pick(n, prefs):
    for t in prefs:
        if n % t == 0:
            return t
    return n


def _mm_body(a_ref, w_ref, o_ref, *scratch, nk, act):
    def epilogue(acc):
        if act == "relu2":
            r = jnp.maximum(acc, 0.0)
            acc = r * r
        return acc.astype(o_ref.dtype)

    if nk == 1:
        o_ref[...] = epilogue(jnp.dot(a_ref[...], w_ref[...], preferred_element_type=F32))
        return
    (acc_ref,) = scratch
    k = pl.program_id(2)

    @pl.when(k == 0)
    def _():
        acc_ref[...] = jnp.zeros_like(acc_ref)

    acc_ref[...] += jnp.dot(a_ref[...], w_ref[...], preferred_element_type=F32)

    @pl.when(k == nk - 1)
    def _():
        o_ref[...] = epilogue(acc_ref[...])


def matmul(a, w, out_dtype=F32, act=None):
    a = a.astype(BF16)
    w = w.astype(BF16)
    m, kdim = a.shape
    n = w.shape[1]
    tm = _pick(m, (1024, 512, 256, 128, 64, 32, 16, 8))
    tn = _pick(n, (1024, 512, 256, 128))
    tk = kdim if kdim <= 2048 else _pick(kdim, (2048, 1024, 512))
    nk = kdim // tk
    scratch = [pltpu.VMEM((tm, tn), F32)] if nk > 1 else []
    return pl.pallas_call(
        functools.partial(_mm_body, nk=nk, act=act),
        grid=(m // tm, n // tn, nk),
        in_specs=[pl.BlockSpec((tm, tk), lambda i, j, k: (i, k)),
                  pl.BlockSpec((tk, tn), lambda i, j, k: (k, j))],
        out_specs=pl.BlockSpec((tm, tn), lambda i, j, k: (i, j)),
        out_shape=jax.ShapeDtypeStruct((m, n), out_dtype),
        scratch_shapes=scratch,
        compiler_params=_params(("parallel", "parallel", "arbitrary")),
        name="dense_matmul",
    )(a, w)


def _rms_rows(x, g):
    return x * lax.rsqrt(jnp.mean(x * x, axis=-1, keepdims=True) + NORM_EPS) * g


def _residual_update(m, h_ref, gpost_ref, gnext_ref, ho_ref, uo_ref):
    h_new = h_ref[...] + _rms_rows(m, gpost_ref[...])
    ho_ref[...] = h_new
    if uo_ref is not None:
        uo_ref[...] = _rms_rows(h_new, gnext_ref[...]).astype(BF16)


def _proj_res_body(*refs, nk, has_next):
    if has_next:
        a_ref, w_ref, h_ref, gpost_ref, gnext_ref, ho_ref, uo_ref, acc_ref = refs
    else:
        a_ref, w_ref, h_ref, gpost_ref, ho_ref, acc_ref = refs
        gnext_ref = uo_ref = None
    k = pl.program_id(1)

    @pl.when(k == 0)
    def _():
        acc_ref[...] = jnp.zeros_like(acc_ref)

    acc_ref[...] += jnp.dot(a_ref[...], w_ref[...], preferred_element_type=F32)

    @pl.when(k == nk - 1)
    def _():
        _residual_update(acc_ref[...], h_ref, gpost_ref, gnext_ref, ho_ref, uo_ref)


def proj_residual(a, w, h, g_post, g_next):
    m, kdim = a.shape
    n = w.shape[1]
    tm = _pick(m, (512, 256, 128, 64, 32, 16, 8))
    tk = _pick(kdim, (1024, 512, 256, 128))
    nk = kdim // tk
    has_next = g_next is not None
    row = pl.BlockSpec((tm, n), lambda i, k: (i, 0))
    vec = pl.BlockSpec((1, n), lambda i, k: (0, 0))
    in_specs = [pl.BlockSpec((tm, tk), lambda i, k: (i, k)), pl.BlockSpec((tk, n), lambda i, k: (k, 0)), row, vec]
    args = [a.astype(BF16), w.astype(BF16), h, g_post.reshape(1, n).astype(F32)]
    out_shape = [jax.ShapeDtypeStruct((m, n), F32)]
    out_specs = [row]
    if has_next:
        in_specs.append(vec)
        args.append(g_next.reshape(1, n).astype(F32))
        out_shape.append(jax.ShapeDtypeStruct((m, n), BF16))
        out_specs.append(row)
    out = pl.pallas_call(
        functools.partial(_proj_res_body, nk=nk, has_next=has_next),
        grid=(m // tm, nk),
        in_specs=in_specs, out_specs=out_specs, out_shape=out_shape,
        scratch_shapes=[pltpu.VMEM((tm, n), F32)],
        compiler_params=_params(("parallel", "arbitrary")),
        name="proj_residual",
    )(*args)
    return (out[0], out[1]) if has_next else (out[0], None)


def _ple_body(*refs, has_next):
    if has_next:
        p_ref, wp_ref, wg_ref, h_ref, gpost_ref, gnext_ref, ho_ref, uo_ref = refs
    else:
        p_ref, wp_ref, wg_ref, h_ref, gpost_ref, ho_ref = refs
        gnext_ref = uo_ref = None
    e = jnp.dot(p_ref[...], wp_ref[...], preferred_element_type=F32)
    gate = jax.nn.sigmoid(jnp.dot(h_ref[...].astype(BF16), wg_ref[...], preferred_element_type=F32))
    _residual_update(e * gate, h_ref, gpost_ref, gnext_ref, ho_ref, uo_ref)


def ple_residual(p, w_proj, w_gate, h, g_post, g_next):
    m, n = h.shape
    pd = p.shape[1]
    tm = _pick(m, (256, 128, 64, 32, 16, 8))
    has_next = g_next is not None
    row = pl.BlockSpec((tm, n), lambda i: (i, 0))
    vec = pl.BlockSpec((1, n), lambda i: (0, 0))
    in_specs = [pl.BlockSpec((tm, pd), lambda i: (i, 0)), pl.BlockSpec((pd, n), lambda i: (0, 0)),
                pl.BlockSpec((n, n), lambda i: (0, 0)), row, vec]
    args = [p.astype(BF16), w_proj.astype(BF16), w_gate.astype(BF16), h, g_post.reshape(1, n).astype(F32)]
    out_shape = [jax.ShapeDtypeStruct((m, n), F32)]
    out_specs = [row]
    if has_next:
        in_specs.append(vec)
        args.append(g_next.reshape(1, n).astype(F32))
        out_shape.append(jax.ShapeDtypeStruct((m, n), BF16))
        out_specs.append(row)
    out = pl.pallas_call(
        functools.partial(_ple_body, has_next=has_next),
        grid=(m // tm,),
        in_specs=in_specs, out_specs=out_specs, out_shape=out_shape,
        compiler_params=_params(("parallel",)),
        name="ple_residual",
    )(*args)
    return (out[0], out[1]) if has_next else (out[0], None)


ONES_ROWS = 16


def _with_ones_rows(vt):
    pad = jnp.zeros(vt.shape[:-2] + (ONES_ROWS, vt.shape[-1]), vt.dtype).at[..., 0, :].set(1)
    return jnp.concatenate([vt, pad], axis=-2)


def _score_stage(score_fns, m_ref, p_ref, alpha_ref):
    n = range(len(score_fns))
    ss = [fn() for fn in score_fns]
    m_old = [m_ref[i] for i in n]
    m_new = [jnp.maximum(m_old[i], jnp.max(ss[i], axis=0, keepdims=True)) for i in n]
    for i in n:
        alpha_ref[i] = jnp.exp2(m_old[i] - m_new[i])
        m_ref[i] = m_new[i]
    for i in n:
        p_ref[i] = jnp.exp2((ss[i] - m_new[i]).astype(BF16))


def _value_stage(vt, acc_ref, p_ref, alpha_ref):
    n = range(p_ref.shape[0])
    pv = [jnp.dot(vt, p_ref[i], preferred_element_type=F32) for i in n]
    for i in n:
        acc_ref[i] = alpha_ref[i] * acc_ref[i] + pv[i]


def _diff_attn_body(lam_ref, k_ref, q_ref, vt_ref, o_ref, m_ref, acc_ref, pa_ref, aa_ref, pb_ref, ab_ref,
                    *, tq, nsp):
    i = pl.program_id(2)
    tl = tq // nsp
    m_ref[...] = jnp.full(m_ref.shape, NEG, F32)
    acc_ref[...] = jnp.zeros(acc_ref.shape, F32)

    buf_a = (pa_ref, aa_ref)
    buf_b = (pb_ref, ab_ref)

    def scores(j, buf, masked=False):
        kj = k_ref[pl.ds(pl.multiple_of(j * tq, tq), tq), :]

        def score(c, sp):
            s = jnp.dot(kj, q_ref[c, :, sp * tl:(sp + 1) * tl], preferred_element_type=F32)
            if masked:
                row = lax.broadcasted_iota(I32, (tq, tl), 0)
                col = lax.broadcasted_iota(I32, (tq, tl), 1)
                s = jnp.where(row <= col + sp * tl, s, NEG)
            return s

        _score_stage([functools.partial(score, c, sp) for c in range(2) for sp in range(nsp)], m_ref, *buf)

    def values(j, buf):
        _value_stage(vt_ref[j], acc_ref, *buf)

    @pl.when(i == 0)
    def _():
        scores(0, buf_a, masked=True)
        values(0, buf_a)

    @pl.when(i > 0)
    def _():
        scores(0, buf_a)

        def pair(t, carry):
            c = 2 * t + 1
            scores(c, buf_b)
            values(c - 1, buf_a)
            scores(c + 1, buf_a)
            values(c, buf_b)
            return carry

        lax.fori_loop(0, (i - 1) // 2, pair, 0)

        @pl.when((i - 1) % 2 == 1)
        def _():
            scores(i - 1, buf_b)
            values(i - 2, buf_a)
            scores(i, buf_a, masked=True)
            values(i - 1, buf_b)
            values(i, buf_a)

        @pl.when((i - 1) % 2 == 0)
        def _():
            scores(i, buf_b, masked=True)
            values(i - 1, buf_a)
            values(i, buf_b)

    lam = lam_ref[0]
    for sp in range(nsp):
        a0 = acc_ref[sp]
        a1 = acc_ref[nsp + sp]
        o_ref[:, sp * tl:(sp + 1) * tl] = (a0[:A_DV] / a0[A_DV:A_DV + 1]
                                           - lam * (a1[:A_DV] / a1[A_DV:A_DV + 1]))


def diff_attention_core(q, k, v, lam):
    b, t, h = q.shape[0], q.shape[1], q.shape[2]
    tq = _pick(t, (512, 256, 128))
    nq = t // tq
    qs = (q * (A_DC ** -0.5 * LOG2E)).astype(BF16)
    qt = jnp.transpose(qs.reshape(b, nq, tq, h, 2, A_DC), (0, 3, 1, 4, 5, 2))
    zeros = jnp.zeros_like(qt[:, :, :, 0])
    qt = jnp.stack([jnp.concatenate([qt[:, :, :, 0], zeros], axis=-2),
                    jnp.concatenate([zeros, qt[:, :, :, 1]], axis=-2)], axis=3)
    kk = k.reshape(b, t, h * 2 * A_DC).astype(BF16)
    vt = jnp.transpose(v.astype(BF16).reshape(b, nq, tq, h, A_DV), (0, 3, 1, 4, 2))
    vt = _with_ones_rows(vt)
    dve = A_DV + ONES_ROWS
    nsp = 2 if tq >= 256 else 1
    out = pl.pallas_call(
        functools.partial(_diff_attn_body, tq=tq, nsp=nsp),
        grid=(b, h, nq),
        in_specs=[pl.BlockSpec(memory_space=pltpu.SMEM),
                  pl.BlockSpec((None, t, 2 * A_DC), lambda bi, hi, qi: (bi, 0, hi)),
                  pl.BlockSpec((None, None, None, 2, 2 * A_DC, tq), lambda bi, hi, qi: (bi, hi, qi, 0, 0, 0)),
                  pl.BlockSpec((None, None, nq, dve, tq), lambda bi, hi, qi: (bi, hi, 0, 0, 0))],
        out_specs=pl.BlockSpec((None, None, None, A_DV, tq), lambda bi, hi, qi: (bi, hi, qi, 0, 0)),
        out_shape=jax.ShapeDtypeStruct((b, h, nq, A_DV, tq), F32),
        scratch_shapes=[pltpu.VMEM((2 * nsp, 1, tq // nsp), F32),
                        pltpu.VMEM((2 * nsp, dve, tq // nsp), F32)]
        + [pltpu.VMEM((2 * nsp, tq, tq // nsp), BF16), pltpu.VMEM((2 * nsp, 1, tq // nsp), F32)] * 2,
        compiler_params=_params(("parallel", "parallel", "arbitrary")),
        name="diff_attention",
    )(lam.reshape(1).astype(F32), kk, qt, vt)
    return jnp.transpose(out, (0, 2, 4, 1, 3)).reshape(b, t, h, A_DV)


def _dot(a, b, dims, exact):
    if exact:
        return lax.dot_general(a, b, (dims, ((), ())), precision=lax.Precision.HIGHEST,
                               preferred_element_type=F32)
    return lax.dot_general(a.astype(BF16), b.astype(BF16), (dims, ((), ())), preferred_element_type=F32)


_NN = ((1,), (0,))
_NT = ((1,), (1,))
_TN = ((0,), (0,))


def _wkv_body(r_ref, lw_ref, k_ref, v_ref, a_ref, b_ref, y_ref, s_ref, *, nh, n, c, exact):
    @pl.when(pl.program_id(1) == 0)
    def _():
        s_ref[...] = jnp.zeros(s_ref.shape, F32)

    row = lax.broadcasted_iota(I32, (c, c), 0)
    col = lax.broadcasted_iota(I32, (c, c), 1)
    incl = row >= col
    strict = row > col
    tri = jnp.where(incl, 1.0, 0.0).astype(F32)
    eye = jnp.where(row == col, 1.0, 0.0).astype(F32)
    nsq = int(math.log2(c)) - 1

    hs = range(nh)
    dot = functools.partial(_dot, exact=exact)
    heads = lambda ref: [ref[:, h * n:(h + 1) * n] for h in hs]
    lw, r, k, v, a, b = (heads(ref) for ref in (lw_ref, r_ref, k_ref, v_ref, a_ref, b_ref))
    cum = [_dot(tri, lw[h], _NN, True) for h in hs]
    tot = [cum[h][c - 1:c, :] for h in hs]
    pinv = [jnp.exp(-cum[h]) for h in hs]
    pend = [jnp.exp(tot[h] - cum[h]) for h in hs]
    at = [a[h] * jnp.exp(cum[h] - lw[h]) for h in hs]
    rt = [r[h] * jnp.exp(cum[h]) for h in hs]
    bt = [b[h] * pinv[h] for h in hs]
    kt = [k[h] * pinv[h] for h in hs]
    a_ab = [jnp.where(strict, dot(at[h], bt[h], _NT), 0.0) for h in hs]
    a_ak = [jnp.where(strict, dot(at[h], kt[h], _NT), 0.0) for h in hs]
    a_rb = [jnp.where(incl, dot(rt[h], bt[h], _NT), 0.0) for h in hs]
    a_rk = [jnp.where(incl, dot(rt[h], kt[h], _NT), 0.0) for h in hs]
    x = a_ab
    minv = [eye + x[h] for h in hs]
    for _ in range(nsq):
        x = [dot(x[h], x[h], _NN) for h in hs]
        minv = [minv[h] + dot(minv[h], x[h], _NN) for h in hs]
    s0 = [s_ref[h] for h in hs]
    rhs = [dot(at[h], s0[h], _NT) + dot(a_ak[h], v[h], _NN) for h in hs]
    u = [dot(minv[h], rhs[h], _NN) for h in hs]
    for h in hs:
        y_ref[:, h * n:(h + 1) * n] = dot(rt[h], s0[h], _NT) + dot(a_rb[h], u[h], _NN) + dot(a_rk[h], v[h], _NN)
    for h in hs:
        s_ref[h] = (s0[h] * jnp.exp(tot[h]) + dot(u[h], b[h] * pend[h], _TN)
                    + dot(v[h], k[h] * pend[h], _TN))


def wkv7(r, lw, k, v, a, b, n=B_HEAD, exact=False):
    bsz, t, width = r.shape
    nh = width // n
    c = min(WKV_CHUNK, t)
    spec = pl.BlockSpec((None, c, width), lambda bi, ci: (bi, ci, 0))
    return pl.pallas_call(
        functools.partial(_wkv_body, nh=nh, n=n, c=c, exact=exact),
        grid=(bsz, t // c),
        in_specs=[spec] * 6,
        out_specs=spec,
        out_shape=jax.ShapeDtypeStruct((bsz, t, width), F32),
        scratch_shapes=[pltpu.VMEM((nh, n, n), F32)],
        compiler_params=_params(("parallel", "arbitrary")),
        name="wkv7_chunked",
    )(r, lw, k, v, a, b)


def _dsa_index_body(ki_ref, qit_ref, wit_ref, bias_ref, key_ref, cut_ref, *, tq, tkc, topk, nheads, t):
    i = pl.program_id(1)
    nch = ((i + 1) * tq + tkc - 1) // tkc
    qpos = i * tq + lax.broadcasted_iota(I32, (1, tq), 1)

    def kpos_of(ci):
        return ci * tkc + lax.broadcasted_iota(I32, (tkc, 1), 0)

    def rows(ci):
        return pl.ds(pl.multiple_of(ci * tkc, tkc), tkc)

    def score_chunk(ci, carry):
        kc = ki_ref[rows(ci), :]
        acc = jnp.zeros((tkc, tq), F32)
        for h in range(nheads):
            s = jnp.dot(kc, qit_ref[h], preferred_element_type=F32)
            acc = acc + jnp.maximum(s, 0.0) * wit_ref[h]
        acc = jnp.where(kpos_of(ci) <= qpos, acc, -jnp.inf)
        bits = lax.bitcast_convert_type(acc, I32)
        key_ref[rows(ci), :] = jnp.where(bits >= 0, bits, bits ^ 0x7FFFFFFF)
        return carry

    lax.fori_loop(0, nch, score_chunk, 0)

    def count(pred):
        def body(ci, cnt):
            ones = pred(key_ref[rows(ci), :], kpos_of(ci))
            return cnt + jnp.sum(ones, axis=0, keepdims=True)
        return lax.fori_loop(0, nch, body, jnp.zeros((1, tq), I32))

    def value_bit(step, prefix):
        cand = prefix | lax.shift_left(jnp.int32(1), 31 - step)
        cand_signed = cand ^ INT_MIN
        cnt = count(lambda key, kpos: jnp.where(key >= cand_signed, 1, 0))
        return jnp.where(cnt >= topk, cand, prefix)

    thr = lax.fori_loop(0, 32, value_bit, jnp.zeros((1, tq), I32)) ^ INT_MIN
    n_ge = count(lambda key, kpos: jnp.where(key >= thr, 1, 0))
    n_gt = count(lambda key, kpos: jnp.where(key > thr, 1, 0))
    need = topk - n_gt
    cut_ref[...] = jnp.full((1, tq), t, I32)

    @pl.when(jnp.max(n_ge) > topk)
    def _():
        nbits = max(1, (t - 1).bit_length())

        def index_bit(step, x):
            cand = x | lax.shift_left(jnp.int32(1), nbits - 1 - step)
            g = count(lambda key, kpos: jnp.where(key == thr, jnp.where(kpos < cand, 1, 0), 0))
            return jnp.where(g < need, cand, x)

        cut_ref[...] = lax.fori_loop(0, nbits, index_bit, jnp.zeros((1, tq), I32))

    cutoff = cut_ref[...]

    def write_chunk(ci, carry):
        key = key_ref[rows(ci), :]
        kpos = kpos_of(ci)
        tie = jnp.where(kpos <= cutoff, 0.0, NEG)
        sel = jnp.where(key > thr, 0.0, jnp.where(key == thr, tie, NEG))
        bias_ref[rows(ci), :] = jnp.where(kpos <= qpos, sel, NEG).astype(BF16)
        return carry

    lax.fori_loop(0, nch, write_chunk, 0)

    def fill_chunk(ci, carry):
        bias_ref[rows(ci), :] = jnp.full((tkc, tq), NEG, BF16)
        return carry

    lax.fori_loop(nch, t // tkc, fill_chunk, 0)


def dsa_select(ki, qi, wi, topk):
    b, t, nh, d = qi.shape
    tq = _pick(t, (256, 128))
    tkc = _pick(t, (512, 256, 128))
    nq = t // tq
    qit = jnp.transpose(qi.astype(BF16).reshape(b, nq, tq, nh, d), (0, 1, 3, 4, 2))
    wit = jnp.transpose(wi.astype(F32).reshape(b, nq, tq, nh), (0, 1, 3, 2))[:, :, :, None, :]
    return pl.pallas_call(
        functools.partial(_dsa_index_body, tq=tq, tkc=tkc, topk=topk, nheads=nh, t=t),
        grid=(b, nq),
        in_specs=[pl.BlockSpec((None, t, d), lambda bi, qi_: (bi, 0, 0)),
                  pl.BlockSpec((None, None, nh, d, tq), lambda bi, qi_: (bi, qi_, 0, 0, 0)),
                  pl.BlockSpec((None, None, nh, 1, tq), lambda bi, qi_: (bi, qi_, 0, 0, 0))],
        out_specs=pl.BlockSpec((None, t, tq), lambda bi, qi_: (bi, 0, qi_)),
        out_shape=jax.ShapeDtypeStruct((b, t, t), BF16),
        scratch_shapes=[pltpu.VMEM((t, tq), I32), pltpu.VMEM((1, tq), I32)],
        compiler_params=_params(("parallel", "arbitrary")),
        name="dsa_index_topk",
    )(ki.astype(BF16), qit, wit)


def _dsa_attn_body(k_ref, vt_ref, q_ref, bias_ref, o_ref, m_ref, acc_ref, pa_ref, aa_ref, pb_ref, ab_ref,
                   *, tq, tk, rep, d):
    i = pl.program_id(2)
    nch = ((i + 1) * tq + tk - 1) // tk
    m_ref[...] = jnp.full(m_ref.shape, NEG, F32)
    acc_ref[...] = jnp.zeros(acc_ref.shape, F32)
    buf_a = (pa_ref, aa_ref)
    buf_b = (pb_ref, ab_ref)

    def scores(c, buf):
        rows = pl.ds(pl.multiple_of(c * tk, tk), tk)
        kj = k_ref[rows, :]
        bj = bias_ref[rows, :].astype(F32)
        fns = [lambda r=r: jnp.dot(kj, q_ref[r], preferred_element_type=F32) + bj for r in range(rep)]
        _score_stage(fns, m_ref, *buf)

    def values(c, buf):
        _value_stage(vt_ref[c], acc_ref, *buf)

    scores(0, buf_a)

    def pair(t, carry):
        c = 2 * t + 1
        scores(c, buf_b)
        values(c - 1, buf_a)
        scores(c + 1, buf_a)
        values(c, buf_b)
        return carry

    lax.fori_loop(0, (nch - 1) // 2, pair, 0)
    last = nch - 1

    @pl.when(last % 2 == 1)
    def _():
        scores(last, buf_b)
        values(last - 1, buf_a)
        values(last, buf_b)

    @pl.when(last % 2 == 0)
    def _():
        values(last, buf_a)

    for r in range(rep):
        acc = acc_ref[r]
        o_ref[r] = acc[:d] / acc[d:d + 1]


def dsa_attention(q, k, v, bias):
    b, t, hq, d = q.shape
    g = k.shape[2]
    rep = hq // g
    tq = _pick(t, (256, 128))
    tk = _pick(t, (512, 256, 128))
    nq, nk = t // tq, t // tk
    qs = (q * (d ** -0.5 * LOG2E)).astype(BF16)
    qt = jnp.transpose(qs.reshape(b, nq, tq, g, rep, d), (0, 3, 1, 4, 5, 2))
    kk = k.reshape(b, t, g * d).astype(BF16)
    vt = jnp.transpose(v.astype(BF16).reshape(b, nk, tk, g, d), (0, 3, 1, 4, 2))
    vt = _with_ones_rows(vt)
    dve = d + ONES_ROWS
    out = pl.pallas_call(
        functools.partial(_dsa_attn_body, tq=tq, tk=tk, rep=rep, d=d),
        grid=(b, g, nq),
        in_specs=[pl.BlockSpec((None, t, d), lambda bi, gi, qi: (bi, 0, gi)),
                  pl.BlockSpec((None, None, nk, dve, tk), lambda bi, gi, qi: (bi, gi, 0, 0, 0)),
                  pl.BlockSpec((None, None, None, rep, d, tq), lambda bi, gi, qi: (bi, gi, qi, 0, 0, 0)),
                  pl.BlockSpec((None, t, tq), lambda bi, gi, qi: (bi, 0, qi))],
        out_specs=pl.BlockSpec((None, None, None, rep, d, tq), lambda bi, gi, qi: (bi, gi, qi, 0, 0, 0)),
        out_shape=jax.ShapeDtypeStruct((b, g, nq, rep, d, tq), F32),
        scratch_shapes=[pltpu.VMEM((rep, 1, tq), F32), pltpu.VMEM((rep, dve, tq), F32)]
        + [pltpu.VMEM((rep, tk, tq), BF16), pltpu.VMEM((rep, 1, tq), F32)] * 2,
        compiler_params=_params(("parallel", "parallel", "arbitrary")),
        name="dsa_attention",
    )(kk, vt, qt, bias)
    return jnp.transpose(out, (0, 2, 5, 1, 3, 4)).reshape(b, t, hq * d)


def _rms_norm(x, g, eps=NORM_EPS):
    xf = x.astype(F32)
    return xf * lax.rsqrt(jnp.mean(xf * xf, axis=-1, keepdims=True) + eps) * g.astype(F32)


def _layer_norm(x, g, b, eps=LN_EPS):
    mu = jnp.mean(x, axis=-1, keepdims=True)
    var = jnp.mean(jnp.square(x - mu), axis=-1, keepdims=True)
    return (x - mu) * lax.rsqrt(var + eps) * g + b


def _partial_rope(x, pos):
    dh = x.shape[-1]
    rot = dh // ROPE_FRACTION
    half = rot // 2
    inv_freq = ROPE_THETA ** (-(jnp.arange(half, dtype=F32) * 2.0 / rot))
    ang = pos.astype(F32)[..., None] * inv_freq
    cos = jnp.cos(ang)[:, :, None, :]
    sin = jnp.sin(ang)[:, :, None, :]
    x1 = x[..., :half]
    x2 = x[..., half:rot]
    return jnp.concatenate([x1 * cos - x2 * sin, x2 * cos + x1 * sin, x[..., rot:]], axis=-1)


def _pad_cols(w, n):
    return jnp.pad(w, ((0, 0), (0, n - w.shape[1])))


def _pad_rows(w, n):
    return jnp.pad(w, ((0, n - w.shape[0]), (0, 0)))


def _round_up(n, m):
    return (n + m - 1) // m * m


def _mm3(x, w, **kw):
    b, t, _ = x.shape
    return matmul(x.reshape(b * t, x.shape[-1]), w, **kw).reshape(b, t, w.shape[1])


def _lora(x, w2):
    r = _round_up(x.shape[-1], 128)
    xp = jnp.pad(x, ((0, 0), (0, 0), (0, r - x.shape[-1])))
    return _mm3(xp, _pad_rows(w2, r))


def _mixer_ab(u, pos, w_in, lq1, lk1, lq2, lk2, subln_g, mu, w0, w2, a0, a2, g2,
              k_k, k_a, r_k, lnx_g, lnx_b, lambda_init):
    b, t, d = u.shape
    a_width = d // 2
    b_width = d // 2
    a_heads = a_width // A_DV
    b_heads = b_width // B_HEAD
    a_in = 3 * a_width
    n_in = w_in.shape[1]
    z = _mm3(u, _pad_cols(w_in, _round_up(n_in, 512)))[..., :n_in]
    za, zb = z[..., :a_in], z[..., a_in:]

    qa, ka, va = jnp.split(za, 3, axis=-1)
    qa = _partial_rope(qa.reshape(b, t, 2 * a_heads, A_DC), pos).reshape(b, t, a_heads, 2, A_DC)
    ka = _partial_rope(ka.reshape(b, t, 2 * a_heads, A_DC), pos).reshape(b, t, a_heads, 2, A_DC)
    va = va.reshape(b, t, a_heads, A_DV)
    lam = jnp.exp(jnp.sum(lq1 * lk1)) - jnp.exp(jnp.sum(lq2 * lk2)) + lambda_init
    ya = diff_attention_core(qa, ka, va, lam)
    ya = (_rms_norm(ya, subln_g, DIFF_SUBLN_EPS) * (1.0 - lambda_init)).reshape(b, t, a_width)

    prev = jnp.pad(zb, ((0, 0), (1, 0), (0, 0)))[:, :t]
    zb = zb + (prev - zb) * mu
    dl = w2.shape[0]
    al = a2.shape[0]
    cuts = [b_width, 2 * b_width, 3 * b_width, 3 * b_width + dl, 3 * b_width + dl + al]
    r, k, v, xw, xa, xg = jnp.split(zb, cuts, axis=-1)
    w = -jax.nn.softplus(-(w0 + _lora(jnp.tanh(xw), w2))) - 0.5
    a = jax.nn.sigmoid(a0 + _lora(xa, a2))
    g = _lora(jax.nn.sigmoid(xg), g2)
    hd = lambda x: x.reshape(b, t, b_heads, B_HEAD)
    kk = hd(k * k_k)
    kk = kk / jnp.maximum(jnp.sqrt(jnp.sum(kk * kk, axis=-1, keepdims=True)), 1e-12)
    k = k * (1.0 + (a - 1.0) * k_a)
    log_decay = -jnp.exp(w)
    fl = lambda x: x.reshape(b, t, b_width)
    y = hd(wkv7(r, log_decay, k, v, fl(-kk), fl(kk * hd(a))))
    mu_y = jnp.mean(y, axis=-1, keepdims=True)
    var_y = jnp.mean(jnp.square(y - mu_y), axis=-1, keepdims=True)
    y = (y - mu_y) * lax.rsqrt(var_y + LNX_EPS)
    y = y * lnx_g.reshape(b_heads, B_HEAD) + lnx_b.reshape(b_heads, B_HEAD)
    bonus = jnp.sum(hd(r) * hd(k) * r_k, axis=-1, keepdims=True) * hd(v)
    yb = (y + bonus).reshape(b, t, b_width) * g
    return jnp.concatenate([ya.astype(BF16), yb.astype(BF16)], axis=-1)


def _mixer_c(u, pos, w_in, idx_k_g, idx_k_b, topk):
    b, t, d = u.shape
    c_heads = d // C_HEAD_DIM
    n_in = w_in.shape[1]
    c1 = c_heads * C_HEAD_DIM
    c2 = c1 + C_KV_HEADS * C_HEAD_DIM
    c3 = c2 + C_KV_HEADS * C_HEAD_DIM
    c4 = c3 + IDX_HEADS * IDX_DIM
    c5 = c4 + IDX_DIM
    z = _mm3(u, _pad_cols(w_in, _round_up(n_in, 512)))[..., :n_in]
    q, k, v, qi, ki, wi = jnp.split(z, [c1, c2, c3, c4, c5], axis=-1)
    q = _partial_rope(q.reshape(b, t, c_heads, C_HEAD_DIM), pos)
    k = _partial_rope(k.reshape(b, t, C_KV_HEADS, C_HEAD_DIM), pos)
    v = v.reshape(b, t, C_KV_HEADS, C_HEAD_DIM)
    qi = _partial_rope(qi.reshape(b, t, IDX_HEADS, IDX_DIM), pos)
    ki = _partial_rope(_layer_norm(ki, idx_k_g, idx_k_b)[:, :, None, :], pos)[:, :, 0]
    wi = wi * ((IDX_HEADS * IDX_DIM) ** -0.5)
    bias = dsa_select(ki, qi, wi, topk)
    return dsa_attention(q, k, v, bias).astype(BF16)


def kernel(x, p, positions, mix_pre_g, mix_post_g, mlp_pre_g, mlp_post_g, w_mlp_up, w_mlp_down, w_ple_proj, w_ple_gate, ple_post_g, ab_w_in, ab_w_out, diff_lq1, diff_lk1, diff_lq2, diff_lk2, diff_subln_g, rwkv_mu, rwkv_w0, rwkv_w2, rwkv_a0, rwkv_a2, rwkv_g2, rwkv_k_k, rwkv_k_a, rwkv_r_k, rwkv_lnx_g, rwkv_lnx_b, c_w_in, c_w_out, idx_k_g, idx_k_b):
    depth = mix_pre_g.shape[0]
    b, t, d = x.shape
    topk = min(TOPK_MAX, t // 4)
    flat = lambda z: z.reshape(b * t, z.shape[-1])
    h = flat(x)
    u = _rms_norm(h, mix_pre_g[0]).astype(BF16)
    for i in range(depth):
        j = i // 2
        u3 = u.reshape(b, t, d)
        if i % 2 == 0:
            lambda_init = 0.8 - 0.6 * math.exp(-0.3 * i)
            m = _mixer_ab(u3, positions, ab_w_in[j], diff_lq1[j], diff_lk1[j], diff_lq2[j],
                          diff_lk2[j], diff_subln_g[j], rwkv_mu[j], rwkv_w0[j], rwkv_w2[j], rwkv_a0[j],
                          rwkv_a2[j], rwkv_g2[j], rwkv_k_k[j], rwkv_k_a[j], rwkv_r_k[j], rwkv_lnx_g[j],
                          rwkv_lnx_b[j], lambda_init)
            w_out = ab_w_out[j]
        else:
            m = _mixer_c(u3, positions, c_w_in[j], idx_k_g[j], idx_k_b[j], topk)
            w_out = c_w_out[j]
        h, u = proj_residual(flat(m), w_out, h, mix_post_g[i], mlp_pre_g[i])
        up = matmul(u, w_mlp_up[i], out_dtype=BF16, act="relu2")
        h, _ = proj_residual(up, w_mlp_down[i], h, mlp_post_g[i], None)
        g_next = mix_pre_g[i + 1] if i + 1 < depth else None
        h, u = ple_residual(flat(p[i]), w_ple_proj[i], w_ple_gate[i], h, ple_post_g[i], g_next)
    return h.reshape(b, t, d)
```

```python
import functools
import math

import jax
import jax.numpy as jnp
from jax import lax
from jax.experimental import pallas as pl
from jax.experimental.pallas import tpu as pltpu

F32 = jnp.float32
BF16 = jnp.bfloat16
I32 = jnp.int32

V7X_VMEM_LIMIT_BYTES = 48 * 1024 * 1024
NEG = -1e30
LOG2E = 1.4426950408889634
INT_MIN = -2147483648

NORM_EPS = 1e-6
ROPE_THETA = 500000.0
ROPE_FRACTION = 4
A_DV = 128
A_DC = 64
DIFF_SUBLN_EPS = 1e-5
B_HEAD = 64
LNX_EPS = 64e-5
C_HEAD_DIM = 128
C_KV_HEADS = 4
IDX_HEADS = 16
IDX_DIM = 64
TOPK_MAX = 256
LN_EPS = 1e-6
WKV_CHUNK = 64


def _params(sem):
    return pltpu.CompilerParams(dimension_semantics=sem, vmem_limit_bytes=V7X_VMEM_LIMIT_BYTES)


def _pick(n, prefs):
    for t in prefs:
        if n % t == 0:
            return t
    return n


def _mm_body(a_ref, w_ref, o_ref, *scratch, nk, act):
    def epilogue(acc):
        if act == "relu2":
            r = jnp.maximum(acc, 0.0)
            acc = r * r
        return acc.astype(o_ref.dtype)

    if nk == 1:
        o_ref[...] = epilogue(jnp.dot(a_ref[...], w_ref[...], preferred_element_type=F32))
        return
    (acc_ref,) = scratch
    k = pl.program_id(2)

    @pl.when(k == 0)
    def _():
        acc_ref[...] = jnp.zeros_like(acc_ref)

    acc_ref[...] += jnp.dot(a_ref[...], w_ref[...], preferred_element_type=F32)

    @pl.when(k == nk - 1)
    def _():
        o_ref[...] = epilogue(acc_ref[...])


def matmul(a, w, out_dtype=F32, act=None):
    a = a.astype(BF16)
    w = w.astype(BF16)
    m, kdim = a.shape
    n = w.shape[1]
    tm = _pick(m, (1024, 512, 256, 128, 64, 32, 16, 8))
    tn = _pick(n, (1024, 512, 256, 128))
    tk = kdim if kdim <= 2048 else _pick(kdim, (2048, 1024, 512))
    nk = kdim // tk
    scratch = [pltpu.VMEM((tm, tn), F32)] if nk > 1 else []
    return pl.pallas_call(
        functools.partial(_mm_body, nk=nk, act=act),
        grid=(m // tm, n // tn, nk),
        in_specs=[pl.BlockSpec((tm, tk), lambda i, j, k: (i, k)),
                  pl.BlockSpec((tk, tn), lambda i, j, k: (k, j))],
        out_specs=pl.BlockSpec((tm, tn), lambda i, j, k: (i, j)),
        out_shape=jax.ShapeDtypeStruct((m, n), out_dtype),
        scratch_shapes=scratch,
        compiler_params=_params(("parallel", "parallel", "arbitrary")),
        name="dense_matmul",
    )(a, w)


def _rms_rows(x, g):
    return x * lax.rsqrt(jnp.mean(x * x, axis=-1, keepdims=True) + NORM_EPS) * g


def _residual_update(m, h_ref, gpost_ref, gnext_ref, ho_ref, uo_ref):
    h_new = h_ref[...] + _rms_rows(m, gpost_ref[...])
    ho_ref[...] = h_new
    if uo_ref is not None:
        uo_ref[...] = _rms_rows(h_new, gnext_ref[...]).astype(BF16)


def _proj_res_body(*refs, nk, has_next):
    if has_next:
        a_ref, w_ref, h_ref, gpost_ref, gnext_ref, ho_ref, uo_ref, acc_ref = refs
    else:
        a_ref, w_ref, h_ref, gpost_ref, ho_ref, acc_ref = refs
        gnext_ref = uo_ref = None
    k = pl.program_id(1)

    @pl.when(k == 0)
    def _():
        acc_ref[...] = jnp.zeros_like(acc_ref)

    acc_ref[...] += jnp.dot(a_ref[...], w_ref[...], preferred_element_type=F32)

    @pl.when(k == nk - 1)
    def _():
        _residual_update(acc_ref[...], h_ref, gpost_ref, gnext_ref, ho_ref, uo_ref)


def proj_residual(a, w, h, g_post, g_next):
    m, kdim = a.shape
    n = w.shape[1]
    tm = _pick(m, (512, 256, 128, 64, 32, 16, 8))
    tk = _pick(kdim, (1024, 512, 256, 128))
    nk = kdim // tk
    has_next = g_next is not None
    row = pl.BlockSpec((tm, n), lambda i, k: (i, 0))
    vec = pl.BlockSpec((1, n), lambda i, k: (0, 0))
    in_specs = [pl.BlockSpec((tm, tk), lambda i, k: (i, k)), pl.BlockSpec((tk, n), lambda i, k: (k, 0)), row, vec]
    args = [a.astype(BF16), w.astype(BF16), h, g_post.reshape(1, n).astype(F32)]
    out_shape = [jax.ShapeDtypeStruct((m, n), F32)]
    out_specs = [row]
    if has_next:
        in_specs.append(vec)
        args.append(g_next.reshape(1, n).astype(F32))
        out_shape.append(jax.ShapeDtypeStruct((m, n), BF16))
        out_specs.append(row)
    out = pl.pallas_call(
        functools.partial(_proj_res_body, nk=nk, has_next=has_next),
        grid=(m // tm, nk),
        in_specs=in_specs, out_specs=out_specs, out_shape=out_shape,
        scratch_shapes=[pltpu.VMEM((tm, n), F32)],
        compiler_params=_params(("parallel", "arbitrary")),
        name="proj_residual",
    )(*args)
    return (out[0], out[1]) if has_next else (out[0], None)


def _ple_body(*refs, has_next):
    if has_next:
        p_ref, wp_ref, wg_ref, h_ref, gpost_ref, gnext_ref, ho_ref, uo_ref = refs
    else:
        p_ref, wp_ref, wg_ref, h_ref, gpost_ref, ho_ref = refs
        gnext_ref = uo_ref = None
    e = jnp.dot(p_ref[...], wp_ref[...], preferred_element_type=F32)
    gate = jax.nn.sigmoid(jnp.dot(h_ref[...].astype(BF16), wg_ref[...], preferred_element_type=F32))
    _residual_update(e * gate, h_ref, gpost_ref, gnext_ref, ho_ref, uo_ref)


def ple_residual(p, w_proj, w_gate, h, g_post, g_next):
    m, n = h.shape
    pd = p.shape[1]
    tm = _pick(m, (256, 128, 64, 32, 16, 8))
    has_next = g_next is not None
    row = pl.BlockSpec((tm, n), lambda i: (i, 0))
    vec = pl.BlockSpec((1, n), lambda i: (0, 0))
    in_specs = [pl.BlockSpec((tm, pd), lambda i: (i, 0)), pl.BlockSpec((pd, n), lambda i: (0, 0)),
                pl.BlockSpec((n, n), lambda i: (0, 0)), row, vec]
    args = [p.astype(BF16), w_proj.astype(BF16), w_gate.astype(BF16), h, g_post.reshape(1, n).astype(F32)]
    out_shape = [jax.ShapeDtypeStruct((m, n), F32)]
    out_specs = [row]
    if has_next:
        in_specs.append(vec)
        args.append(g_next.reshape(1, n).astype(F32))
        out_shape.append(jax.ShapeDtypeStruct((m, n), BF16))
        out_specs.append(row)
    out = pl.pallas_call(
        functools.partial(_ple_body, has_next=has_next),
        grid=(m // tm,),
        in_specs=in_specs, out_specs=out_specs, out_shape=out_shape,
        compiler_params=_params(("parallel",)),
        name="ple_residual",
    )(*args)
    return (out[0], out[1]) if has_next else (out[0], None)


ONES_ROWS = 16


def _with_ones_rows(vt):
    pad = jnp.zeros(vt.shape[:-2] + (ONES_ROWS, vt.shape[-1]), vt.dtype).at[..., 0, :].set(1)
    return jnp.concatenate([vt, pad], axis=-2)


def _score_stage(score_fns, m_ref, p_ref, alpha_ref):
    n = range(len(score_fns))
    ss = [fn() for fn in score_fns]
    m_old = [m_ref[i] for i in n]
    m_new = [jnp.maximum(m_old[i], jnp.max(ss[i], axis=0, keepdims=True)) for i in n]
    for i in n:
        alpha_ref[i] = jnp.exp2(m_old[i] - m_new[i])
        m_ref[i] = m_new[i]
    for i in n:
        p_ref[i] = jnp.exp2((ss[i] - m_new[i]).astype(BF16))


def _value_stage(vt, acc_ref, p_ref, alpha_ref):
    n = range(p_ref.shape[0])
    pv = [jnp.dot(vt, p_ref[i], preferred_element_type=F32) for i in n]
    for i in n:
        acc_ref[i] = alpha_ref[i] * acc_ref[i] + pv[i]


def _diff_attn_body(lam_ref, k_ref, q_ref, vt_ref, o_ref, m_ref, acc_ref, pa_ref, aa_ref, pb_ref, ab_ref,
                    *, tq, nsp):
    i = pl.program_id(2)
    tl = tq // nsp
    m_ref[...] = jnp.full(m_ref.shape, NEG, F32)
    acc_ref[...] = jnp.zeros(acc_ref.shape, F32)

    buf_a = (pa_ref, aa_ref)
    buf_b = (pb_ref, ab_ref)

    def scores(j, buf, masked=False):
        kj = k_ref[pl.ds(pl.multiple_of(j * tq, tq), tq), :]

        def score(c, sp):
            s = jnp.dot(kj, q_ref[c, :, sp * tl:(sp + 1) * tl], preferred_element_type=F32)
            if masked:
                row = lax.broadcasted_iota(I32, (tq, tl), 0)
                col = lax.broadcasted_iota(I32, (tq, tl), 1)
                s = jnp.where(row <= col + sp * tl, s, NEG)
            return s

        _score_stage([functools.partial(score, c, sp) for c in range(2) for sp in range(nsp)], m_ref, *buf)

    def values(j, buf):
        _value_stage(vt_ref[j], acc_ref, *buf)

    @pl.when(i == 0)
    def _():
        scores(0, buf_a, masked=True)
        values(0, buf_a)

    @pl.when(i > 0)
    def _():
        scores(0, buf_a)

        def pair(t, carry):
            c = 2 * t + 1
            scores(c, buf_b)
            values(c - 1, buf_a)
            scores(c + 1, buf_a)
            values(c, buf_b)
            return carry

        lax.fori_loop(0, (i - 1) // 2, pair, 0)

        @pl.when((i - 1) % 2 == 1)
        def _():
            scores(i - 1, buf_b)
            values(i - 2, buf_a)
            scores(i, buf_a, masked=True)
            values(i - 1, buf_b)
            values(i, buf_a)

        @pl.when((i - 1) % 2 == 0)
        def _():
            scores(i, buf_b, masked=True)
            values(i - 1, buf_a)
            values(i, buf_b)

    lam = lam_ref[0]
    for sp in range(nsp):
        a0 = acc_ref[sp]
        a1 = acc_ref[nsp + sp]
        o_ref[:, sp * tl:(sp + 1) * tl] = (a0[:A_DV] / a0[A_DV:A_DV + 1]
                                           - lam * (a1[:A_DV] / a1[A_DV:A_DV + 1]))


def diff_attention_core(q, k, v, lam):
    b, t, h = q.shape[0], q.shape[1], q.shape[2]
    tq = _pick(t, (512, 256, 128))
    nq = t // tq
    qs = (q * (A_DC ** -0.5 * LOG2E)).astype(BF16)
    qt = jnp.transpose(qs.reshape(b, nq, tq, h, 2, A_DC), (0, 3, 1, 4, 5, 2))
    zeros = jnp.zeros_like(qt[:, :, :, 0])
    qt = jnp.stack([jnp.concatenate([qt[:, :, :, 0], zeros], axis=-2),
                    jnp.concatenate([zeros, qt[:, :, :, 1]], axis=-2)], axis=3)
    kk = k.reshape(b, t, h * 2 * A_DC).astype(BF16)
    vt = jnp.transpose(v.astype(BF16).reshape(b, nq, tq, h, A_DV), (0, 3, 1, 4, 2))
    vt = _with_ones_rows(vt)
    dve = A_DV + ONES_ROWS
    nsp = 2 if tq >= 256 else 1
    out = pl.pallas_call(
        functools.partial(_diff_attn_body, tq=tq, nsp=nsp),
        grid=(b, h, nq),
        in_specs=[pl.BlockSpec(memory_space=pltpu.SMEM),
                  pl.BlockSpec((None, t, 2 * A_DC), lambda bi, hi, qi: (bi, 0, hi)),
                  pl.BlockSpec((None, None, None, 2, 2 * A_DC, tq), lambda bi, hi, qi: (bi, hi, qi, 0, 0, 0)),
                  pl.BlockSpec((None, None, nq, dve, tq), lambda bi, hi, qi: (bi, hi, 0, 0, 0))],
        out_specs=pl.BlockSpec((None, None, None, A_DV, tq), lambda bi, hi, qi: (bi, hi, qi, 0, 0)),
        out_shape=jax.ShapeDtypeStruct((b, h, nq, A_DV, tq), F32),
        scratch_shapes=[pltpu.VMEM((2 * nsp, 1, tq // nsp), F32),
                        pltpu.VMEM((2 * nsp, dve, tq // nsp), F32)]
        + [pltpu.VMEM((2 * nsp, tq, tq // nsp), BF16), pltpu.VMEM((2 * nsp, 1, tq // nsp), F32)] * 2,
        compiler_params=_params(("parallel", "parallel", "arbitrary")),
        name="diff_attention",
    )(lam.reshape(1).astype(F32), kk, qt, vt)
    return jnp.transpose(out, (0, 2, 4, 1, 3)).reshape(b, t, h, A_DV)


def _split_dot(x, w):
    hi = x.astype(BF16)
    lo = (x - hi.astype(F32)).astype(BF16)
    return jnp.dot(hi, w, preferred_element_type=F32) + jnp.dot(lo, w, preferred_element_type=F32)


def _rwkv_prep_body(zr_ref, zk_ref, zv_ref, zl_ref, pr_ref, pk_ref, pv_ref, pl_ref,
                    mur_ref, muk_ref, muv_ref, mul_ref, w0_ref, a0_ref, kk_ref, ka_ref,
                    w2_ref, a2_ref, g2_ref, hsum_ref,
                    r_ref, lw_ref, k_ref, v_ref, an_ref, bn_ref, g_ref, *, tm, seq):
    first = (pl.program_id(0) * tm) % seq == 0

    def shifted(z_ref, p_ref, mu_ref):
        z = z_ref[...]
        last = jnp.where(first, 0.0, p_ref[7:8, :])
        row = lax.broadcasted_iota(I32, z.shape, 0)
        prev = jnp.where(row == 0, last, pltpu.roll(z, 1, 0))
        return z + (prev - z) * mu_ref[...]

    r = shifted(zr_ref, pr_ref, mur_ref)
    k = shifted(zk_ref, pk_ref, muk_ref)
    v = shifted(zv_ref, pv_ref, muv_ref)
    lo = shifted(zl_ref, pl_ref, mul_ref)
    lw = jnp.dot(jnp.tanh(lo).astype(BF16), w2_ref[...], preferred_element_type=F32)
    la = jnp.dot(lo.astype(BF16), a2_ref[...], preferred_element_type=F32)
    g = jnp.dot(jax.nn.sigmoid(lo).astype(BF16), g2_ref[...], preferred_element_type=F32)
    w = -jax.nn.softplus(-(w0_ref[...] + lw)) - 0.5
    a = jax.nn.sigmoid(a0_ref[...] + la)
    kk = k * kk_ref[...]
    norm = jnp.sqrt(_split_dot(kk * kk, hsum_ref[...]))
    kk = kk / jnp.maximum(norm, 1e-12)
    r_ref[...] = r
    lw_ref[...] = -jnp.exp(w)
    k_ref[...] = k * (1.0 + (a - 1.0) * ka_ref[...])
    v_ref[...] = v
    an_ref[...] = -kk
    bn_ref[...] = kk * a
    g_ref[...] = g


def rwkv_prep(z, col0, width, n_lora, seq, mu, w0, w2, a0, a2, g2, k_k, k_a):
    m = z.shape[0]
    tm = _pick(seq, (256, 128, 64, 32, 16, 8))
    lw_ = _round_up(n_lora, 128)
    assert col0 % width == 0 and (col0 + 3 * width) % lw_ == 0 and col0 + 3 * width + lw_ <= z.shape[1]
    cb = col0 // width
    lb = (col0 + 3 * width) // lw_
    dl, al = w2.shape[0], a2.shape[0]

    def rows_of(w, start):
        return jnp.zeros((lw_, width), BF16).at[start:start + w.shape[0]].set(w.astype(BF16))

    hsum = jnp.kron(jnp.eye(width // B_HEAD, dtype=BF16), jnp.ones((B_HEAD, B_HEAD), BF16))
    vec = lambda x: x.reshape(1, -1).astype(F32)
    mul = jnp.pad(mu[3 * width:], (0, lw_ - n_lora))
    tile = lambda c, wd: pl.BlockSpec((tm, wd), lambda i: (i, c))
    before = lambda c, wd: pl.BlockSpec((8, wd), lambda i: (jnp.maximum(i * (tm // 8) - 1, 0), c))
    par = lambda wd: pl.BlockSpec((1, wd), lambda i: (0, 0))
    mat = lambda r_, c_: pl.BlockSpec((r_, c_), lambda i: (0, 0))
    out = pl.BlockSpec((tm, width), lambda i: (i, 0))
    return pl.pallas_call(
        functools.partial(_rwkv_prep_body, tm=tm, seq=seq),
        grid=(m // tm,),
        in_specs=[tile(cb, width), tile(cb + 1, width), tile(cb + 2, width), tile(lb, lw_),
                  before(cb, width), before(cb + 1, width), before(cb + 2, width), before(lb, lw_),
                  par(width), par(width), par(width), par(lw_), par(width), par(width), par(width), par(width),
                  mat(lw_, width), mat(lw_, width), mat(lw_, width), mat(width, width)],
        out_specs=[out] * 7,
        out_shape=[jax.ShapeDtypeStruct((m, width), F32)] * 7,
        compiler_params=_params(("parallel",)),
        name="rwkv_prep",
    )(z, z, z, z, z, z, z, z,
      vec(mu[:width]), vec(mu[width:2 * width]), vec(mu[2 * width:3 * width]), vec(mul),
      vec(w0), vec(a0), vec(k_k), vec(k_a),
      rows_of(w2, 0), rows_of(a2, dl), rows_of(g2, dl + al), hsum)


def _dot(a, b, dims, exact):
    if exact:
        return lax.dot_general(a, b, (dims, ((), ())), precision=lax.Precision.HIGHEST,
                               preferred_element_type=F32)
    return lax.dot_general(a.astype(BF16), b.astype(BF16), (dims, ((), ())), preferred_element_type=F32)


_NN = ((1,), (0,))
_NT = ((1,), (1,))
_TN = ((0,), (0,))


def _wkv_body(r_ref, lw_ref, k_ref, v_ref, a_ref, b_ref, g_ref, rk_ref, lng_ref, lnb_ref, y_ref, s_ref,
              *, nh, n, c, exact):
    @pl.when(pl.program_id(1) == 0)
    def _():
        s_ref[...] = jnp.zeros(s_ref.shape, F32)

    row = lax.broadcasted_iota(I32, (c, c), 0)
    col = lax.broadcasted_iota(I32, (c, c), 1)
    incl = row >= col
    strict = row > col
    tri = jnp.where(incl, 1.0, 0.0).astype(F32)
    eye = jnp.where(row == col, 1.0, 0.0).astype(F32)
    nsq = int(math.log2(c)) - 1

    hs = range(nh)
    dot = functools.partial(_dot, exact=exact)
    heads = lambda ref: [ref[:, h * n:(h + 1) * n] for h in hs]
    lw, r, k, v, a, b = (heads(ref) for ref in (lw_ref, r_ref, k_ref, v_ref, a_ref, b_ref))
    cum = [_dot(tri, lw[h], _NN, True) for h in hs]
    tot = [cum[h][c - 1:c, :] for h in hs]
    pinv = [jnp.exp(-cum[h]) for h in hs]
    pend = [jnp.exp(tot[h] - cum[h]) for h in hs]
    at = [a[h] * jnp.exp(cum[h] - lw[h]) for h in hs]
    rt = [r[h] * jnp.exp(cum[h]) for h in hs]
    bt = [b[h] * pinv[h] for h in hs]
    kt = [k[h] * pinv[h] for h in hs]
    a_ab = [jnp.where(strict, dot(at[h], bt[h], _NT), 0.0) for h in hs]
    a_ak = [jnp.where(strict, dot(at[h], kt[h], _NT), 0.0) for h in hs]
    a_rb = [jnp.where(incl, dot(rt[h], bt[h], _NT), 0.0) for h in hs]
    a_rk = [jnp.where(incl, dot(rt[h], kt[h], _NT), 0.0) for h in hs]
    x = a_ab
    minv = [eye + x[h] for h in hs]
    for _ in range(nsq):
        x = [dot(x[h], x[h], _NN) for h in hs]
        minv = [minv[h] + dot(minv[h], x[h], _NN) for h in hs]
    s0 = [s_ref[h] for h in hs]
    rhs = [dot(at[h], s0[h], _NT) + dot(a_ak[h], v[h], _NN) for h in hs]
    u = [dot(minv[h], rhs[h], _NN) for h in hs]
    y = [dot(rt[h], s0[h], _NT) + dot(a_rb[h], u[h], _NN) + dot(a_rk[h], v[h], _NN) for h in hs]
    for h in hs:
        s_ref[h] = (s0[h] * jnp.exp(tot[h]) + dot(u[h], b[h] * pend[h], _TN)
                    + dot(v[h], k[h] * pend[h], _TN))
    for h in hs:
        cols = slice(h * n, (h + 1) * n)
        d = y[h] - jnp.mean(y[h], axis=-1, keepdims=True)
        yn = d * lax.rsqrt(jnp.mean(d * d, axis=-1, keepdims=True) + LNX_EPS)
        bonus = jnp.sum(r[h] * k[h] * rk_ref[:, cols], axis=-1, keepdims=True) * v[h]
        y_ref[:, cols] = (yn * lng_ref[:, cols] + lnb_ref[:, cols] + bonus) * g_ref[:, cols]


def wkv7(r, lw, k, v, a, b, g, r_k, lnx_g, lnx_b, n=B_HEAD, exact=False):
    bsz, t, width = r.shape
    nh = width // n
    c = min(WKV_CHUNK, t)
    spec = pl.BlockSpec((None, c, width), lambda bi, ci: (bi, ci, 0))
    par = pl.BlockSpec((1, width), lambda bi, ci: (0, 0))
    vec = lambda x: x.reshape(1, width).astype(F32)
    return pl.pallas_call(
        functools.partial(_wkv_body, nh=nh, n=n, c=c, exact=exact),
        grid=(bsz, t // c),
        in_specs=[spec] * 7 + [par] * 3,
        out_specs=spec,
        out_shape=jax.ShapeDtypeStruct((bsz, t, width), F32),
        scratch_shapes=[pltpu.VMEM((nh, n, n), F32)],
        compiler_params=_params(("parallel", "arbitrary")),
        name="wkv7_chunked",
    )(r, lw, k, v, a, b, g, vec(r_k), vec(lnx_g), vec(lnx_b))


def _dsa_index_body(ki_ref, qit_ref, wit_ref, bias_ref, key_ref, cut_ref, *, tq, tkc, topk, nheads, t):
    i = pl.program_id(1)
    nch = ((i + 1) * tq + tkc - 1) // tkc
    qpos = i * tq + lax.broadcasted_iota(I32, (1, tq), 1)

    def kpos_of(ci):
        return ci * tkc + lax.broadcasted_iota(I32, (tkc, 1), 0)

    def rows(ci):
        return pl.ds(pl.multiple_of(ci * tkc, tkc), tkc)

    def score_chunk(ci, carry):
        kc = ki_ref[rows(ci), :]
        acc = jnp.zeros((tkc, tq), F32)
        for h in range(nheads):
            s = jnp.dot(kc, qit_ref[h], preferred_element_type=F32)
            acc = acc + jnp.maximum(s, 0.0) * wit_ref[h]
        acc = jnp.where(kpos_of(ci) <= qpos, acc, -jnp.inf)
        bits = lax.bitcast_convert_type(acc, I32)
        key_ref[rows(ci), :] = jnp.where(bits >= 0, bits, bits ^ 0x7FFFFFFF)
        return carry

    lax.fori_loop(0, nch, score_chunk, 0)

    def count(pred):
        def body(ci, cnt):
            ones = pred(key_ref[rows(ci), :], kpos_of(ci))
            return cnt + jnp.sum(ones, axis=0, keepdims=True)
        return lax.fori_loop(0, nch, body, jnp.zeros((1, tq), I32))

    def value_bit(step, prefix):
        cand = prefix | lax.shift_left(jnp.int32(1), 31 - step)
        cand_signed = cand ^ INT_MIN
        cnt = count(lambda key, kpos: jnp.where(key >= cand_signed, 1, 0))
        return jnp.where(cnt >= topk, cand, prefix)

    thr = lax.fori_loop(0, 32, value_bit, jnp.zeros((1, tq), I32)) ^ INT_MIN
    n_ge = count(lambda key, kpos: jnp.where(key >= thr, 1, 0))
    n_gt = count(lambda key, kpos: jnp.where(key > thr, 1, 0))
    need = topk - n_gt
    cut_ref[...] = jnp.full((1, tq), t, I32)

    @pl.when(jnp.max(n_ge) > topk)
    def _():
        nbits = max(1, (t - 1).bit_length())

        def index_bit(step, x):
            cand = x | lax.shift_left(jnp.int32(1), nbits - 1 - step)
            g = count(lambda key, kpos: jnp.where(key == thr, jnp.where(kpos < cand, 1, 0), 0))
            return jnp.where(g < need, cand, x)

        cut_ref[...] = lax.fori_loop(0, nbits, index_bit, jnp.zeros((1, tq), I32))

    cutoff = cut_ref[...]

    def write_chunk(ci, carry):
        key = key_ref[rows(ci), :]
        kpos = kpos_of(ci)
        tie = jnp.where(kpos <= cutoff, 0.0, NEG)
        sel = jnp.where(key > thr, 0.0, jnp.where(key == thr, tie, NEG))
        bias_ref[rows(ci), :] = jnp.where(kpos <= qpos, sel, NEG).astype(BF16)
        return carry

    lax.fori_loop(0, nch, write_chunk, 0)

    def fill_chunk(ci, carry):
        bias_ref[rows(ci), :] = jnp.full((tkc, tq), NEG, BF16)
        return carry

    lax.fori_loop(nch, t // tkc, fill_chunk, 0)


def dsa_select(ki, qi, wi, topk):
    b, t, nh, d = qi.shape
    tq = _pick(t, (256, 128))
    tkc = _pick(t, (512, 256, 128))
    nq = t // tq
    qit = jnp.transpose(qi.astype(BF16).reshape(b, nq, tq, nh, d), (0, 1, 3, 4, 2))
    wit = jnp.transpose(wi.astype(F32).reshape(b, nq, tq, nh), (0, 1, 3, 2))[:, :, :, None, :]
    return pl.pallas_call(
        functools.partial(_dsa_index_body, tq=tq, tkc=tkc, topk=topk, nheads=nh, t=t),
        grid=(b, nq),
        in_specs=[pl.BlockSpec((None, t, d), lambda bi, qi_: (bi, 0, 0)),
                  pl.BlockSpec((None, None, nh, d, tq), lambda bi, qi_: (bi, qi_, 0, 0, 0)),
                  pl.BlockSpec((None, None, nh, 1, tq), lambda bi, qi_: (bi, qi_, 0, 0, 0))],
        out_specs=pl.BlockSpec((None, t, tq), lambda bi, qi_: (bi, 0, qi_)),
        out_shape=jax.ShapeDtypeStruct((b, t, t), BF16),
        scratch_shapes=[pltpu.VMEM((t, tq), I32), pltpu.VMEM((1, tq), I32)],
        compiler_params=_params(("parallel", "arbitrary")),
        name="dsa_index_topk",
    )(ki.astype(BF16), qit, wit)


def _dsa_attn_body(k_ref, vt_ref, q_ref, bias_ref, o_ref, m_ref, acc_ref, pa_ref, aa_ref, pb_ref, ab_ref,
                   *, tq, tk, rep, d):
    i = pl.program_id(2)
    nch = ((i + 1) * tq + tk - 1) // tk
    m_ref[...] = jnp.full(m_ref.shape, NEG, F32)
    acc_ref[...] = jnp.zeros(acc_ref.shape, F32)
    buf_a = (pa_ref, aa_ref)
    buf_b = (pb_ref, ab_ref)

    def scores(c, buf):
        rows = pl.ds(pl.multiple_of(c * tk, tk), tk)
        kj = k_ref[rows, :]
        bj = bias_ref[rows, :].astype(F32)
        fns = [lambda r=r: jnp.dot(kj, q_ref[r], preferred_element_type=F32) + bj for r in range(rep)]
        _score_stage(fns, m_ref, *buf)

    def values(c, buf):
        _value_stage(vt_ref[c], acc_ref, *buf)

    scores(0, buf_a)

    def pair(t, carry):
        c = 2 * t + 1
        scores(c, buf_b)
        values(c - 1, buf_a)
        scores(c + 1, buf_a)
        values(c, buf_b)
        return carry

    lax.fori_loop(0, (nch - 1) // 2, pair, 0)
    last = nch - 1

    @pl.when(last % 2 == 1)
    def _():
        scores(last, buf_b)
        values(last - 1, buf_a)
        values(last, buf_b)

    @pl.when(last % 2 == 0)
    def _():
        values(last, buf_a)

    for r in range(rep):
        acc = acc_ref[r]
        o_ref[r] = acc[:d] / acc[d:d + 1]


def dsa_attention(q, k, v, bias):
    b, t, hq, d = q.shape
    g = k.shape[2]
    rep = hq // g
    tq = _pick(t, (256, 128))
    tk = _pick(t, (512, 256, 128))
    nq, nk = t // tq, t // tk
    qs = (q * (d ** -0.5 * LOG2E)).astype(BF16)
    qt = jnp.transpose(qs.reshape(b, nq, tq, g, rep, d), (0, 3, 1, 4, 5, 2))
    kk = k.reshape(b, t, g * d).astype(BF16)
    vt = jnp.transpose(v.astype(BF16).reshape(b, nk, tk, g, d), (0, 3, 1, 4, 2))
    vt = _with_ones_rows(vt)
    dve = d + ONES_ROWS
    out = pl.pallas_call(
        functools.partial(_dsa_attn_body, tq=tq, tk=tk, rep=rep, d=d),
        grid=(b, g, nq),
        in_specs=[pl.BlockSpec((None, t, d), lambda bi, gi, qi: (bi, 0, gi)),
                  pl.BlockSpec((None, None, nk, dve, tk), lambda bi, gi, qi: (bi, gi, 0, 0, 0)),
                  pl.BlockSpec((None, None, None, rep, d, tq), lambda bi, gi, qi: (bi, gi, qi, 0, 0, 0)),
                  pl.BlockSpec((None, t, tq), lambda bi, gi, qi: (bi, 0, qi))],
        out_specs=pl.BlockSpec((None, None, None, rep, d, tq), lambda bi, gi, qi: (bi, gi, qi, 0, 0, 0)),
        out_shape=jax.ShapeDtypeStruct((b, g, nq, rep, d, tq), F32),
        scratch_shapes=[pltpu.VMEM((rep, 1, tq), F32), pltpu.VMEM((rep, dve, tq), F32)]
        + [pltpu.VMEM((rep, tk, tq), BF16), pltpu.VMEM((rep, 1, tq), F32)] * 2,
        compiler_params=_params(("parallel", "parallel", "arbitrary")),
        name="dsa_attention",
    )(kk, vt, qt, bias)
    return jnp.transpose(out, (0, 2, 5, 1, 3, 4)).reshape(b, t, hq * d)


def _rms_norm(x, g, eps=NORM_EPS):
    xf = x.astype(F32)
    return xf * lax.rsqrt(jnp.mean(xf * xf, axis=-1, keepdims=True) + eps) * g.astype(F32)


def _layer_norm(x, g, b, eps=LN_EPS):
    mu = jnp.mean(x, axis=-1, keepdims=True)
    var = jnp.mean(jnp.square(x - mu), axis=-1, keepdims=True)
    return (x - mu) * lax.rsqrt(var + eps) * g + b


def _partial_rope(x, pos):
    dh = x.shape[-1]
    rot = dh // ROPE_FRACTION
    half = rot // 2
    inv_freq = ROPE_THETA ** (-(jnp.arange(half, dtype=F32) * 2.0 / rot))
    ang = pos.astype(F32)[..., None] * inv_freq
    cos = jnp.cos(ang)[:, :, None, :]
    sin = jnp.sin(ang)[:, :, None, :]
    x1 = x[..., :half]
    x2 = x[..., half:rot]
    return jnp.concatenate([x1 * cos - x2 * sin, x2 * cos + x1 * sin, x[..., rot:]], axis=-1)


def _pad_cols(w, n):
    return jnp.pad(w, ((0, 0), (0, n - w.shape[1])))


def _round_up(n, m):
    return (n + m - 1) // m * m


def _mm3(x, w, **kw):
    b, t, _ = x.shape
    return matmul(x.reshape(b * t, x.shape[-1]), w, **kw).reshape(b, t, w.shape[1])


def _mixer_ab(u, pos, w_in, lq1, lk1, lq2, lk2, subln_g, mu, w0, w2, a0, a2, g2,
              k_k, k_a, r_k, lnx_g, lnx_b, lambda_init):
    b, t, d = u.shape
    a_width = d // 2
    b_width = d // 2
    a_heads = a_width // A_DV
    a_in = 3 * a_width
    n_in = w_in.shape[1]
    z = matmul(u.reshape(b * t, d), _pad_cols(w_in, _round_up(n_in, 512)))

    qa, ka, va = (z[:, c * a_width:(c + 1) * a_width].reshape(b, t, a_width) for c in range(3))
    qa = _partial_rope(qa.reshape(b, t, 2 * a_heads, A_DC), pos).reshape(b, t, a_heads, 2, A_DC)
    ka = _partial_rope(ka.reshape(b, t, 2 * a_heads, A_DC), pos).reshape(b, t, a_heads, 2, A_DC)
    va = va.reshape(b, t, a_heads, A_DV)
    lam = jnp.exp(jnp.sum(lq1 * lk1)) - jnp.exp(jnp.sum(lq2 * lk2)) + lambda_init
    ya = diff_attention_core(qa, ka, va, lam)
    ya = (_rms_norm(ya, subln_g, DIFF_SUBLN_EPS) * (1.0 - lambda_init)).reshape(b, t, a_width)

    prep = rwkv_prep(z, a_in, b_width, n_in - a_in - 3 * b_width, t, mu, w0, w2, a0, a2, g2, k_k, k_a)
    yb = wkv7(*(x.reshape(b, t, b_width) for x in prep), r_k.reshape(-1), lnx_g, lnx_b)
    return jnp.concatenate([ya.astype(BF16), yb.astype(BF16)], axis=-1)


def _mixer_c(u, pos, w_in, idx_k_g, idx_k_b, topk):
    b, t, d = u.shape
    c_heads = d // C_HEAD_DIM
    n_in = w_in.shape[1]
    c1 = c_heads * C_HEAD_DIM
    c2 = c1 + C_KV_HEADS * C_HEAD_DIM
    c3 = c2 + C_KV_HEADS * C_HEAD_DIM
    c4 = c3 + IDX_HEADS * IDX_DIM
    c5 = c4 + IDX_DIM
    z = _mm3(u, _pad_cols(w_in, _round_up(n_in, 512)))[..., :n_in]
    q, k, v, qi, ki, wi = jnp.split(z, [c1, c2, c3, c4, c5], axis=-1)
    q = _partial_rope(q.reshape(b, t, c_heads, C_HEAD_DIM), pos)
    k = _partial_rope(k.reshape(b, t, C_KV_HEADS, C_HEAD_DIM), pos)
    v = v.reshape(b, t, C_KV_HEADS, C_HEAD_DIM)
    qi = _partial_rope(qi.reshape(b, t, IDX_HEADS, IDX_DIM), pos)
    ki = _partial_rope(_layer_norm(ki, idx_k_g, idx_k_b)[:, :, None, :], pos)[:, :, 0]
    wi = wi * ((IDX_HEADS * IDX_DIM) ** -0.5)
    bias = dsa_select(ki, qi, wi, topk)
    return dsa_attention(q, k, v, bias).astype(BF16)


def kernel(x, p, positions, mix_pre_g, mix_post_g, mlp_pre_g, mlp_post_g, w_mlp_up, w_mlp_down, w_ple_proj, w_ple_gate, ple_post_g, ab_w_in, ab_w_out, diff_lq1, diff_lk1, diff_lq2, diff_lk2, diff_subln_g, rwkv_mu, rwkv_w0, rwkv_w2, rwkv_a0, rwkv_a2, rwkv_g2, rwkv_k_k, rwkv_k_a, rwkv_r_k, rwkv_lnx_g, rwkv_lnx_b, c_w_in, c_w_out, idx_k_g, idx_k_b):
    depth = mix_pre_g.shape[0]
    b, t, d = x.shape
    topk = min(TOPK_MAX, t // 4)
    flat = lambda z: z.reshape(b * t, z.shape[-1])
    h = flat(x)
    u = _rms_norm(h, mix_pre_g[0]).astype(BF16)
    for i in range(depth):
        j = i // 2
        u3 = u.reshape(b, t, d)
        if i % 2 == 0:
            lambda_init = 0.8 - 0.6 * math.exp(-0.3 * i)
            m = _mixer_ab(u3, positions, ab_w_in[j], diff_lq1[j], diff_lk1[j], diff_lq2[j],
                          diff_lk2[j], diff_subln_g[j], rwkv_mu[j], rwkv_w0[j], rwkv_w2[j], rwkv_a0[j],
                          rwkv_a2[j], rwkv_g2[j], rwkv_k_k[j], rwkv_k_a[j], rwkv_r_k[j], rwkv_lnx_g[j],
                          rwkv_lnx_b[j], lambda_init)
            w_out = ab_w_out[j]
        else:
            m = _mixer_c(u3, positions, c_w_in[j], idx_k_g[j], idx_k_b[j], topk)
            w_out = c_w_out[j]
        h, u = proj_residual(flat(m), w_out, h, mix_post_g[i], mlp_pre_g[i])
        up = matmul(u, w_mlp_up[i], out_dtype=BF16, act="relu2")
        h, _ = proj_residual(up, w_mlp_down[i], h, mlp_post_g[i], None)
        g_next = mix_pre_g[i + 1] if i + 1 < depth else None
        h, u = ple_residual(flat(p[i]), w_ple_proj[i], w_ple_gate[i], h, ple_post_g[i], g_next)
    return h.reshape(b, t, d)
```

```python
import functools
import math

import jax
import jax.numpy as jnp
from jax import lax
from jax.experimental import pallas as pl
from jax.experimental.pallas import tpu as pltpu

F32 = jnp.float32
BF16 = jnp.bfloat16
I32 = jnp.int32

V7X_VMEM_LIMIT_BYTES = 48 * 1024 * 1024
NEG = -1e30
LOG2E = 1.4426950408889634
INT_MIN = -2147483648

NORM_EPS = 1e-6
ROPE_THETA = 500000.0
ROPE_FRACTION = 4
A_DV = 128
A_DC = 64
DIFF_SUBLN_EPS = 1e-5
B_HEAD = 64
LNX_EPS = 64e-5
C_HEAD_DIM = 128
C_KV_HEADS = 4
IDX_HEADS = 16
IDX_DIM = 64
TOPK_MAX = 256
LN_EPS = 1e-6
WKV_CHUNK = 64


def _params(sem):
    return pltpu.CompilerParams(dimension_semantics=sem, vmem_limit_bytes=V7X_VMEM_LIMIT_BYTES)


def _pick(n, prefs):
    for t in prefs:
        if n % t == 0:
            return t
    return n


def _mm_body(a_ref, w_ref, o_ref, *scratch, nk, act):
    def epilogue(acc):
        if act == "relu2":
            r = jnp.maximum(acc, 0.0)
            acc = r * r
        return acc.astype(o_ref.dtype)

    if nk == 1:
        o_ref[...] = epilogue(jnp.dot(a_ref[...], w_ref[...], preferred_element_type=F32))
        return
    (acc_ref,) = scratch
    k = pl.program_id(2)

    @pl.when(k == 0)
    def _():
        acc_ref[...] = jnp.zeros_like(acc_ref)

    acc_ref[...] += jnp.dot(a_ref[...], w_ref[...], preferred_element_type=F32)

    @pl.when(k == nk - 1)
    def _():
        o_ref[...] = epilogue(acc_ref[...])


def matmul(a, w, out_dtype=F32, act=None):
    a = a.astype(BF16)
    w = w.astype(BF16)
    m, kdim = a.shape
    n = w.shape[1]
    tm = _pick(m, (1024, 512, 256, 128, 64, 32, 16, 8))
    tn = _pick(n, (1024, 512, 256, 128))
    tk = kdim if kdim <= 2048 else _pick(kdim, (2048, 1024, 512))
    nk = kdim // tk
    scratch = [pltpu.VMEM((tm, tn), F32)] if nk > 1 else []
    return pl.pallas_call(
        functools.partial(_mm_body, nk=nk, act=act),
        grid=(m // tm, n // tn, nk),
        in_specs=[pl.BlockSpec((tm, tk), lambda i, j, k: (i, k)),
                  pl.BlockSpec((tk, tn), lambda i, j, k: (k, j))],
        out_specs=pl.BlockSpec((tm, tn), lambda i, j, k: (i, j)),
        out_shape=jax.ShapeDtypeStruct((m, n), out_dtype),
        scratch_shapes=scratch,
        compiler_params=_params(("parallel", "parallel", "arbitrary")),
        name="dense_matmul",
    )(a, w)


def _rms_rows(x, g):
    return x * lax.rsqrt(jnp.mean(x * x, axis=-1, keepdims=True) + NORM_EPS) * g


def _residual_update(m, h_ref, gpost_ref, gnext_ref, ho_ref, uo_ref):
    h_new = h_ref[...] + _rms_rows(m, gpost_ref[...])
    ho_ref[...] = h_new
    if uo_ref is not None:
        uo_ref[...] = _rms_rows(h_new, gnext_ref[...]).astype(BF16)


def _proj_res_body(*refs, nk, has_next):
    if has_next:
        a_ref, w_ref, h_ref, gpost_ref, gnext_ref, ho_ref, uo_ref, acc_ref = refs
    else:
        a_ref, w_ref, h_ref, gpost_ref, ho_ref, acc_ref = refs
        gnext_ref = uo_ref = None
    k = pl.program_id(1)

    @pl.when(k == 0)
    def _():
        acc_ref[...] = jnp.zeros_like(acc_ref)

    acc_ref[...] += jnp.dot(a_ref[...], w_ref[...], preferred_element_type=F32)

    @pl.when(k == nk - 1)
    def _():
        _residual_update(acc_ref[...], h_ref, gpost_ref, gnext_ref, ho_ref, uo_ref)


def proj_residual(a, w, h, g_post, g_next):
    m, kdim = a.shape
    n = w.shape[1]
    tm = _pick(m, (512, 256, 128, 64, 32, 16, 8))
    tk = _pick(kdim, (1024, 512, 256, 128))
    nk = kdim // tk
    has_next = g_next is not None
    row = pl.BlockSpec((tm, n), lambda i, k: (i, 0))
    vec = pl.BlockSpec((1, n), lambda i, k: (0, 0))
    in_specs = [pl.BlockSpec((tm, tk), lambda i, k: (i, k)), pl.BlockSpec((tk, n), lambda i, k: (k, 0)), row, vec]
    args = [a.astype(BF16), w.astype(BF16), h, g_post.reshape(1, n).astype(F32)]
    out_shape = [jax.ShapeDtypeStruct((m, n), F32)]
    out_specs = [row]
    if has_next:
        in_specs.append(vec)
        args.append(g_next.reshape(1, n).astype(F32))
        out_shape.append(jax.ShapeDtypeStruct((m, n), BF16))
        out_specs.append(row)
    out = pl.pallas_call(
        functools.partial(_proj_res_body, nk=nk, has_next=has_next),
        grid=(m // tm, nk),
        in_specs=in_specs, out_specs=out_specs, out_shape=out_shape,
        scratch_shapes=[pltpu.VMEM((tm, n), F32)],
        compiler_params=_params(("parallel", "arbitrary")),
        name="proj_residual",
    )(*args)
    return (out[0], out[1]) if has_next else (out[0], None)


def _ple_body(*refs, has_next):
    if has_next:
        p_ref, wp_ref, wg_ref, h_ref, gpost_ref, gnext_ref, ho_ref, uo_ref = refs
    else:
        p_ref, wp_ref, wg_ref, h_ref, gpost_ref, ho_ref = refs
        gnext_ref = uo_ref = None
    e = jnp.dot(p_ref[...], wp_ref[...], preferred_element_type=F32)
    gate = jax.nn.sigmoid(jnp.dot(h_ref[...].astype(BF16), wg_ref[...], preferred_element_type=F32))
    _residual_update(e * gate, h_ref, gpost_ref, gnext_ref, ho_ref, uo_ref)


def ple_residual(p, w_proj, w_gate, h, g_post, g_next):
    m, n = h.shape
    pd = p.shape[1]
    tm = _pick(m, (256, 128, 64, 32, 16, 8))
    has_next = g_next is not None
    row = pl.BlockSpec((tm, n), lambda i: (i, 0))
    vec = pl.BlockSpec((1, n), lambda i: (0, 0))
    in_specs = [pl.BlockSpec((tm, pd), lambda i: (i, 0)), pl.BlockSpec((pd, n), lambda i: (0, 0)),
                pl.BlockSpec((n, n), lambda i: (0, 0)), row, vec]
    args = [p.astype(BF16), w_proj.astype(BF16), w_gate.astype(BF16), h, g_post.reshape(1, n).astype(F32)]
    out_shape = [jax.ShapeDtypeStruct((m, n), F32)]
    out_specs = [row]
    if has_next:
        in_specs.append(vec)
        args.append(g_next.reshape(1, n).astype(F32))
        out_shape.append(jax.ShapeDtypeStruct((m, n), BF16))
        out_specs.append(row)
    out = pl.pallas_call(
        functools.partial(_ple_body, has_next=has_next),
        grid=(m // tm,),
        in_specs=in_specs, out_specs=out_specs, out_shape=out_shape,
        compiler_params=_params(("parallel",)),
        name="ple_residual",
    )(*args)
    return (out[0], out[1]) if has_next else (out[0], None)


ONES_ROWS = 16


def _with_ones_rows(vt):
    pad = jnp.zeros(vt.shape[:-2] + (ONES_ROWS, vt.shape[-1]), vt.dtype).at[..., 0, :].set(1)
    return jnp.concatenate([vt, pad], axis=-2)


def _score_stage(score_fns, m_ref, p_ref, alpha_ref):
    n = range(len(score_fns))
    ss = [fn() for fn in score_fns]
    m_old = [m_ref[i] for i in n]
    m_new = [jnp.maximum(m_old[i], jnp.max(ss[i], axis=0, keepdims=True)) for i in n]
    for i in n:
        alpha_ref[i] = jnp.exp2(m_old[i] - m_new[i])
        m_ref[i] = m_new[i]
    for i in n:
        p_ref[i] = jnp.exp2((ss[i] - m_new[i]).astype(BF16))


def _value_stage(vt, acc_ref, p_ref, alpha_ref):
    n = range(p_ref.shape[0])
    pv = [jnp.dot(vt, p_ref[i], preferred_element_type=F32) for i in n]
    for i in n:
        acc_ref[i] = alpha_ref[i] * acc_ref[i] + pv[i]


def _diff_attn_body(lam_ref, k_ref, q_ref, cmask_ref, vt_ref, g_ref, o_ref, m_ref, acc_ref, qz_ref,
                    pa_ref, aa_ref, pb_ref, ab_ref, *, tq, nsp):
    i = pl.program_id(2)
    tl = tq // nsp
    m_ref[...] = jnp.full(m_ref.shape, NEG, F32)
    acc_ref[...] = jnp.zeros(acc_ref.shape, F32)
    for c in range(2):
        qz_ref[c] = q_ref[...] * cmask_ref[c]

    buf_a = (pa_ref, aa_ref)
    buf_b = (pb_ref, ab_ref)

    def scores(j, buf, masked=False):
        kj = k_ref[pl.ds(pl.multiple_of(j * tq, tq), tq), :]

        def score(c, sp):
            s = lax.dot_general(kj, qz_ref[c, sp * tl:(sp + 1) * tl, :], (_NT, ((), ())),
                                preferred_element_type=F32)
            if masked:
                row = lax.broadcasted_iota(I32, (tq, tl), 0)
                col = lax.broadcasted_iota(I32, (tq, tl), 1)
                s = jnp.where(row <= col + sp * tl, s, NEG)
            return s

        _score_stage([functools.partial(score, c, sp) for c in range(2) for sp in range(nsp)], m_ref, *buf)

    def values(j, buf):
        _value_stage(vt_ref[j], acc_ref, *buf)

    @pl.when(i == 0)
    def _():
        scores(0, buf_a, masked=True)
        values(0, buf_a)

    @pl.when(i > 0)
    def _():
        scores(0, buf_a)

        def pair(t, carry):
            c = 2 * t + 1
            scores(c, buf_b)
            values(c - 1, buf_a)
            scores(c + 1, buf_a)
            values(c, buf_b)
            return carry

        lax.fori_loop(0, (i - 1) // 2, pair, 0)

        @pl.when((i - 1) % 2 == 1)
        def _():
            scores(i - 1, buf_b)
            values(i - 2, buf_a)
            scores(i, buf_a, masked=True)
            values(i - 1, buf_b)
            values(i, buf_a)

        @pl.when((i - 1) % 2 == 0)
        def _():
            scores(i, buf_b, masked=True)
            values(i - 1, buf_a)
            values(i, buf_b)

    lam = lam_ref[0]
    for sp in range(nsp):
        a0 = acc_ref[sp]
        a1 = acc_ref[nsp + sp]
        out = (a0[:A_DV] / a0[A_DV:A_DV + 1] - lam * (a1[:A_DV] / a1[A_DV:A_DV + 1])).T
        y = out * lax.rsqrt(jnp.mean(out * out, axis=-1, keepdims=True) + DIFF_SUBLN_EPS) * g_ref[...]
        o_ref[sp * tl:(sp + 1) * tl, :] = y.astype(o_ref.dtype)


def diff_attention(q, k, v, lam, gain):
    b, t, width = q.shape
    h = width // A_DV
    tq = _pick(t, (512, 256, 128))
    nq = t // tq
    qs = (q * (A_DC ** -0.5 * LOG2E)).astype(BF16)
    vt = jnp.transpose(v.astype(BF16).reshape(b, nq, tq, h, A_DV), (0, 3, 1, 4, 2))
    vt = _with_ones_rows(vt)
    dve = A_DV + ONES_ROWS
    nsp = 2 if tq >= 256 else 1
    lane = jnp.arange(A_DV)
    cmask = jnp.stack([lane < A_DC, lane >= A_DC]).astype(BF16).reshape(2, 1, A_DV)
    head = pl.BlockSpec((None, tq, A_DV), lambda bi, hi, qi: (bi, qi, hi))
    return pl.pallas_call(
        functools.partial(_diff_attn_body, tq=tq, nsp=nsp),
        grid=(b, h, nq),
        in_specs=[pl.BlockSpec(memory_space=pltpu.SMEM),
                  pl.BlockSpec((None, t, A_DV), lambda bi, hi, qi: (bi, 0, hi)),
                  head,
                  pl.BlockSpec((2, 1, A_DV), lambda bi, hi, qi: (0, 0, 0)),
                  pl.BlockSpec((None, None, nq, dve, tq), lambda bi, hi, qi: (bi, hi, 0, 0, 0)),
                  pl.BlockSpec((1, A_DV), lambda bi, hi, qi: (0, 0))],
        out_specs=head,
        out_shape=jax.ShapeDtypeStruct((b, t, width), BF16),
        scratch_shapes=[pltpu.VMEM((2 * nsp, 1, tq // nsp), F32),
                        pltpu.VMEM((2 * nsp, dve, tq // nsp), F32),
                        pltpu.VMEM((2, tq, A_DV), BF16)]
        + [pltpu.VMEM((2 * nsp, tq, tq // nsp), BF16), pltpu.VMEM((2 * nsp, 1, tq // nsp), F32)] * 2,
        compiler_params=_params(("parallel", "parallel", "arbitrary")),
        name="diff_attention",
    )(lam.reshape(1).astype(F32), k.astype(BF16), qs, cmask, vt, gain.reshape(1, A_DV).astype(F32))


def _split_dot(x, w):
    hi = x.astype(BF16)
    lo = (x - hi.astype(F32)).astype(BF16)
    return jnp.dot(hi, w, preferred_element_type=F32) + jnp.dot(lo, w, preferred_element_type=F32)


def _rwkv_prep_body(zr_ref, zk_ref, zv_ref, zl_ref, pr_ref, pk_ref, pv_ref, pl_ref,
                    mur_ref, muk_ref, muv_ref, mul_ref, w0_ref, a0_ref, kk_ref, ka_ref,
                    w2_ref, a2_ref, g2_ref, hsum_ref,
                    r_ref, lw_ref, k_ref, v_ref, an_ref, bn_ref, g_ref, *, tm, seq):
    first = (pl.program_id(0) * tm) % seq == 0

    def shifted(z_ref, p_ref, mu_ref):
        z = z_ref[...]
        last = jnp.where(first, 0.0, p_ref[7:8, :])
        row = lax.broadcasted_iota(I32, z.shape, 0)
        prev = jnp.where(row == 0, last, pltpu.roll(z, 1, 0))
        return z + (prev - z) * mu_ref[...]

    r = shifted(zr_ref, pr_ref, mur_ref)
    k = shifted(zk_ref, pk_ref, muk_ref)
    v = shifted(zv_ref, pv_ref, muv_ref)
    lo = shifted(zl_ref, pl_ref, mul_ref)
    lw = jnp.dot(jnp.tanh(lo).astype(BF16), w2_ref[...], preferred_element_type=F32)
    la = jnp.dot(lo.astype(BF16), a2_ref[...], preferred_element_type=F32)
    g = jnp.dot(jax.nn.sigmoid(lo).astype(BF16), g2_ref[...], preferred_element_type=F32)
    w = -jax.nn.softplus(-(w0_ref[...] + lw)) - 0.5
    a = jax.nn.sigmoid(a0_ref[...] + la)
    kk = k * kk_ref[...]
    norm = jnp.sqrt(_split_dot(kk * kk, hsum_ref[...]))
    kk = kk / jnp.maximum(norm, 1e-12)
    r_ref[...] = r
    lw_ref[...] = -jnp.exp(w)
    k_ref[...] = k * (1.0 + (a - 1.0) * ka_ref[...])
    v_ref[...] = v
    an_ref[...] = -kk
    bn_ref[...] = kk * a
    g_ref[...] = g


def rwkv_prep(z, col0, width, n_lora, seq, mu, w0, w2, a0, a2, g2, k_k, k_a):
    m = z.shape[0]
    tm = _pick(seq, (256, 128, 64, 32, 16, 8))
    lw_ = _round_up(n_lora, 128)
    assert col0 % width == 0 and (col0 + 3 * width) % lw_ == 0 and col0 + 3 * width + lw_ <= z.shape[1]
    cb = col0 // width
    lb = (col0 + 3 * width) // lw_
    dl, al = w2.shape[0], a2.shape[0]

    def rows_of(w, start):
        return jnp.zeros((lw_, width), BF16).at[start:start + w.shape[0]].set(w.astype(BF16))

    hsum = jnp.kron(jnp.eye(width // B_HEAD, dtype=BF16), jnp.ones((B_HEAD, B_HEAD), BF16))
    vec = lambda x: x.reshape(1, -1).astype(F32)
    mul = jnp.pad(mu[3 * width:], (0, lw_ - n_lora))
    tile = lambda c, wd: pl.BlockSpec((tm, wd), lambda i: (i, c))
    before = lambda c, wd: pl.BlockSpec((8, wd), lambda i: (jnp.maximum(i * (tm // 8) - 1, 0), c))
    par = lambda wd: pl.BlockSpec((1, wd), lambda i: (0, 0))
    mat = lambda r_, c_: pl.BlockSpec((r_, c_), lambda i: (0, 0))
    out = pl.BlockSpec((tm, width), lambda i: (i, 0))
    return pl.pallas_call(
        functools.partial(_rwkv_prep_body, tm=tm, seq=seq),
        grid=(m // tm,),
        in_specs=[tile(cb, width), tile(cb + 1, width), tile(cb + 2, width), tile(lb, lw_),
                  before(cb, width), before(cb + 1, width), before(cb + 2, width), before(lb, lw_),
                  par(width), par(width), par(width), par(lw_), par(width), par(width), par(width), par(width),
                  mat(lw_, width), mat(lw_, width), mat(lw_, width), mat(width, width)],
        out_specs=[out] * 7,
        out_shape=[jax.ShapeDtypeStruct((m, width), F32)] * 7,
        compiler_params=_params(("parallel",)),
        name="rwkv_prep",
    )(z, z, z, z, z, z, z, z,
      vec(mu[:width]), vec(mu[width:2 * width]), vec(mu[2 * width:3 * width]), vec(mul),
      vec(w0), vec(a0), vec(k_k), vec(k_a),
      rows_of(w2, 0), rows_of(a2, dl), rows_of(g2, dl + al), hsum)


def _dot(a, b, dims, exact):
    if exact:
        return lax.dot_general(a, b, (dims, ((), ())), precision=lax.Precision.HIGHEST,
                               preferred_element_type=F32)
    return lax.dot_general(a.astype(BF16), b.astype(BF16), (dims, ((), ())), preferred_element_type=F32)


_NN = ((1,), (0,))
_NT = ((1,), (1,))
_TN = ((0,), (0,))


def _wkv_body(r_ref, lw_ref, k_ref, v_ref, a_ref, b_ref, g_ref, rk_ref, lng_ref, lnb_ref, y_ref, s_ref,
              *, nh, n, c, exact):
    @pl.when(pl.program_id(1) == 0)
    def _():
        s_ref[...] = jnp.zeros(s_ref.shape, F32)

    row = lax.broadcasted_iota(I32, (c, c), 0)
    col = lax.broadcasted_iota(I32, (c, c), 1)
    incl = row >= col
    strict = row > col
    tri = jnp.where(incl, 1.0, 0.0).astype(F32)
    eye = jnp.where(row == col, 1.0, 0.0).astype(F32)
    nsq = int(math.log2(c)) - 1

    hs = range(nh)
    dot = functools.partial(_dot, exact=exact)
    heads = lambda ref: [ref[:, h * n:(h + 1) * n] for h in hs]
    lw, r, k, v, a, b = (heads(ref) for ref in (lw_ref, r_ref, k_ref, v_ref, a_ref, b_ref))
    cum = [_dot(tri, lw[h], _NN, True) for h in hs]
    tot = [cum[h][c - 1:c, :] for h in hs]
    pinv = [jnp.exp(-cum[h]) for h in hs]
    pend = [jnp.exp(tot[h] - cum[h]) for h in hs]
    at = [a[h] * jnp.exp(cum[h] - lw[h]) for h in hs]
    rt = [r[h] * jnp.exp(cum[h]) for h in hs]
    bt = [b[h] * pinv[h] for h in hs]
    kt = [k[h] * pinv[h] for h in hs]
    a_ab = [jnp.where(strict, dot(at[h], bt[h], _NT), 0.0) for h in hs]
    a_ak = [jnp.where(strict, dot(at[h], kt[h], _NT), 0.0) for h in hs]
    a_rb = [jnp.where(incl, dot(rt[h], bt[h], _NT), 0.0) for h in hs]
    a_rk = [jnp.where(incl, dot(rt[h], kt[h], _NT), 0.0) for h in hs]
    x = a_ab
    minv = [eye + x[h] for h in hs]
    for _ in range(nsq):
        x = [dot(x[h], x[h], _NN) for h in hs]
        minv = [minv[h] + dot(minv[h], x[h], _NN) for h in hs]
    s0 = [s_ref[h] for h in hs]
    rhs = [dot(at[h], s0[h], _NT) + dot(a_ak[h], v[h], _NN) for h in hs]
    u = [dot(minv[h], rhs[h], _NN) for h in hs]
    y = [dot(rt[h], s0[h], _NT) + dot(a_rb[h], u[h], _NN) + dot(a_rk[h], v[h], _NN) for h in hs]
    for h in hs:
        s_ref[h] = (s0[h] * jnp.exp(tot[h]) + dot(u[h], b[h] * pend[h], _TN)
                    + dot(v[h], k[h] * pend[h], _TN))
    for h in hs:
        cols = slice(h * n, (h + 1) * n)
        d = y[h] - jnp.mean(y[h], axis=-1, keepdims=True)
        yn = d * lax.rsqrt(jnp.mean(d * d, axis=-1, keepdims=True) + LNX_EPS)
        bonus = jnp.sum(r[h] * k[h] * rk_ref[:, cols], axis=-1, keepdims=True) * v[h]
        y_ref[:, cols] = (yn * lng_ref[:, cols] + lnb_ref[:, cols] + bonus) * g_ref[:, cols]


def wkv7(r, lw, k, v, a, b, g, r_k, lnx_g, lnx_b, n=B_HEAD, exact=False):
    bsz, t, width = r.shape
    nh = width // n
    c = min(WKV_CHUNK, t)
    spec = pl.BlockSpec((None, c, width), lambda bi, ci: (bi, ci, 0))
    par = pl.BlockSpec((1, width), lambda bi, ci: (0, 0))
    vec = lambda x: x.reshape(1, width).astype(F32)
    return pl.pallas_call(
        functools.partial(_wkv_body, nh=nh, n=n, c=c, exact=exact),
        grid=(bsz, t // c),
        in_specs=[spec] * 7 + [par] * 3,
        out_specs=spec,
        out_shape=jax.ShapeDtypeStruct((bsz, t, width), F32),
        scratch_shapes=[pltpu.VMEM((nh, n, n), F32)],
        compiler_params=_params(("parallel", "arbitrary")),
        name="wkv7_chunked",
    )(r, lw, k, v, a, b, g, vec(r_k), vec(lnx_g), vec(lnx_b))


def _dsa_index_body(ki_ref, qit_ref, wit_ref, bias_ref, key_ref, cut_ref, *, tq, tkc, topk, nheads, t):
    i = pl.program_id(1)
    nch = ((i + 1) * tq + tkc - 1) // tkc
    qpos = i * tq + lax.broadcasted_iota(I32, (1, tq), 1)

    def kpos_of(ci):
        return ci * tkc + lax.broadcasted_iota(I32, (tkc, 1), 0)

    def rows(ci):
        return pl.ds(pl.multiple_of(ci * tkc, tkc), tkc)

    def score_chunk(ci, carry):
        kc = ki_ref[rows(ci), :]
        acc = jnp.zeros((tkc, tq), F32)
        for h in range(nheads):
            s = jnp.dot(kc, qit_ref[h], preferred_element_type=F32)
            acc = acc + jnp.maximum(s, 0.0) * wit_ref[h]
        acc = jnp.where(kpos_of(ci) <= qpos, acc, -jnp.inf)
        bits = lax.bitcast_convert_type(acc, I32)
        key_ref[rows(ci), :] = jnp.where(bits >= 0, bits, bits ^ 0x7FFFFFFF)
        return carry

    lax.fori_loop(0, nch, score_chunk, 0)

    def count(pred):
        def body(ci, cnt):
            ones = pred(key_ref[rows(ci), :], kpos_of(ci))
            return cnt + jnp.sum(ones, axis=0, keepdims=True)
        return lax.fori_loop(0, nch, body, jnp.zeros((1, tq), I32))

    def value_bit(step, prefix):
        cand = prefix | lax.shift_left(jnp.int32(1), 31 - step)
        cand_signed = cand ^ INT_MIN
        cnt = count(lambda key, kpos: jnp.where(key >= cand_signed, 1, 0))
        return jnp.where(cnt >= topk, cand, prefix)

    thr = lax.fori_loop(0, 32, value_bit, jnp.zeros((1, tq), I32)) ^ INT_MIN
    n_ge = count(lambda key, kpos: jnp.where(key >= thr, 1, 0))
    n_gt = count(lambda key, kpos: jnp.where(key > thr, 1, 0))
    need = topk - n_gt
    cut_ref[...] = jnp.full((1, tq), t, I32)

    @pl.when(jnp.max(n_ge) > topk)
    def _():
        nbits = max(1, (t - 1).bit_length())

        def index_bit(step, x):
            cand = x | lax.shift_left(jnp.int32(1), nbits - 1 - step)
            g = count(lambda key, kpos: jnp.where(key == thr, jnp.where(kpos < cand, 1, 0), 0))
            return jnp.where(g < need, cand, x)

        cut_ref[...] = lax.fori_loop(0, nbits, index_bit, jnp.zeros((1, tq), I32))

    cutoff = cut_ref[...]

    def write_chunk(ci, carry):
        key = key_ref[rows(ci), :]
        kpos = kpos_of(ci)
        tie = jnp.where(kpos <= cutoff, 0.0, NEG)
        sel = jnp.where(key > thr, 0.0, jnp.where(key == thr, tie, NEG))
        bias_ref[rows(ci), :] = jnp.where(kpos <= qpos, sel, NEG).astype(BF16)
        return carry

    lax.fori_loop(0, nch, write_chunk, 0)

    def fill_chunk(ci, carry):
        bias_ref[rows(ci), :] = jnp.full((tkc, tq), NEG, BF16)
        return carry

    lax.fori_loop(nch, t // tkc, fill_chunk, 0)


def dsa_select(ki, qi, wi, topk):
    b, t, nh, d = qi.shape
    tq = _pick(t, (256, 128))
    tkc = _pick(t, (512, 256, 128))
    nq = t // tq
    qit = jnp.transpose(qi.astype(BF16).reshape(b, nq, tq, nh, d), (0, 1, 3, 4, 2))
    wit = jnp.transpose(wi.astype(F32).reshape(b, nq, tq, nh), (0, 1, 3, 2))[:, :, :, None, :]
    return pl.pallas_call(
        functools.partial(_dsa_index_body, tq=tq, tkc=tkc, topk=topk, nheads=nh, t=t),
        grid=(b, nq),
        in_specs=[pl.BlockSpec((None, t, d), lambda bi, qi_: (bi, 0, 0)),
                  pl.BlockSpec((None, None, nh, d, tq), lambda bi, qi_: (bi, qi_, 0, 0, 0)),
                  pl.BlockSpec((None, None, nh, 1, tq), lambda bi, qi_: (bi, qi_, 0, 0, 0))],
        out_specs=pl.BlockSpec((None, t, tq), lambda bi, qi_: (bi, 0, qi_)),
        out_shape=jax.ShapeDtypeStruct((b, t, t), BF16),
        scratch_shapes=[pltpu.VMEM((t, tq), I32), pltpu.VMEM((1, tq), I32)],
        compiler_params=_params(("parallel", "arbitrary")),
        name="dsa_index_topk",
    )(ki.astype(BF16), qit, wit)


def _dsa_attn_body(k_ref, vt_ref, q_ref, bias_ref, o_ref, m_ref, acc_ref, pa_ref, aa_ref, pb_ref, ab_ref,
                   *, tq, tk, rep, d):
    i = pl.program_id(2)
    nch = ((i + 1) * tq + tk - 1) // tk
    m_ref[...] = jnp.full(m_ref.shape, NEG, F32)
    acc_ref[...] = jnp.zeros(acc_ref.shape, F32)
    buf_a = (pa_ref, aa_ref)
    buf_b = (pb_ref, ab_ref)

    def scores(c, buf):
        rows = pl.ds(pl.multiple_of(c * tk, tk), tk)
        kj = k_ref[rows, :]
        bj = bias_ref[rows, :].astype(F32)
        fns = [lambda r=r: lax.dot_general(kj, q_ref[:, r * d:(r + 1) * d], (_NT, ((), ())),
                                           preferred_element_type=F32) + bj for r in range(rep)]
        _score_stage(fns, m_ref, *buf)

    def values(c, buf):
        _value_stage(vt_ref[c], acc_ref, *buf)

    scores(0, buf_a)

    def pair(t, carry):
        c = 2 * t + 1
        scores(c, buf_b)
        values(c - 1, buf_a)
        scores(c + 1, buf_a)
        values(c, buf_b)
        return carry

    lax.fori_loop(0, (nch - 1) // 2, pair, 0)
    last = nch - 1

    @pl.when(last % 2 == 1)
    def _():
        scores(last, buf_b)
        values(last - 1, buf_a)
        values(last, buf_b)

    @pl.when(last % 2 == 0)
    def _():
        values(last, buf_a)

    for r in range(rep):
        acc = acc_ref[r]
        o_ref[:, r * d:(r + 1) * d] = (acc[:d] / acc[d:d + 1]).T.astype(o_ref.dtype)


def dsa_attention(q, k, v, bias):
    b, t, qw = q.shape
    d = C_HEAD_DIM
    g = k.shape[2] // d
    rep = qw // (g * d)
    tq = _pick(t, (256, 128))
    tk = _pick(t, (512, 256, 128))
    nq, nk = t // tq, t // tk
    qs = (q * (d ** -0.5 * LOG2E)).astype(BF16)
    vt = jnp.transpose(v.astype(BF16).reshape(b, nk, tk, g, d), (0, 3, 1, 4, 2))
    vt = _with_ones_rows(vt)
    dve = d + ONES_ROWS
    group = pl.BlockSpec((None, tq, rep * d), lambda bi, gi, qi: (bi, qi, gi))
    return pl.pallas_call(
        functools.partial(_dsa_attn_body, tq=tq, tk=tk, rep=rep, d=d),
        grid=(b, g, nq),
        in_specs=[pl.BlockSpec((None, t, d), lambda bi, gi, qi: (bi, 0, gi)),
                  pl.BlockSpec((None, None, nk, dve, tk), lambda bi, gi, qi: (bi, gi, 0, 0, 0)),
                  group,
                  pl.BlockSpec((None, t, tq), lambda bi, gi, qi: (bi, 0, qi))],
        out_specs=group,
        out_shape=jax.ShapeDtypeStruct((b, t, qw), BF16),
        scratch_shapes=[pltpu.VMEM((rep, 1, tq), F32), pltpu.VMEM((rep, dve, tq), F32)]
        + [pltpu.VMEM((rep, tk, tq), BF16), pltpu.VMEM((rep, 1, tq), F32)] * 2,
        compiler_params=_params(("parallel", "parallel", "arbitrary")),
        name="dsa_attention",
    )(k.astype(BF16), vt, qs, bias)


def _rms_norm(x, g, eps=NORM_EPS):
    xf = x.astype(F32)
    return xf * lax.rsqrt(jnp.mean(xf * xf, axis=-1, keepdims=True) + eps) * g.astype(F32)


def _layer_norm(x, g, b, eps=LN_EPS):
    mu = jnp.mean(x, axis=-1, keepdims=True)
    var = jnp.mean(jnp.square(x - mu), axis=-1, keepdims=True)
    return (x - mu) * lax.rsqrt(var + eps) * g + b


def _partial_rope(x, pos):
    dh = x.shape[-1]
    rot = dh // ROPE_FRACTION
    half = rot // 2
    inv_freq = ROPE_THETA ** (-(jnp.arange(half, dtype=F32) * 2.0 / rot))
    ang = pos.astype(F32)[..., None] * inv_freq
    cos = jnp.cos(ang)[:, :, None, :]
    sin = jnp.sin(ang)[:, :, None, :]
    x1 = x[..., :half]
    x2 = x[..., half:rot]
    return jnp.concatenate([x1 * cos - x2 * sin, x2 * cos + x1 * sin, x[..., rot:]], axis=-1)


def _pad_cols(w, n):
    return jnp.pad(w, ((0, 0), (0, n - w.shape[1])))


def _round_up(n, m):
    return (n + m - 1) // m * m


def _mm3(x, w, **kw):
    b, t, _ = x.shape
    return matmul(x.reshape(b * t, x.shape[-1]), w, **kw).reshape(b, t, w.shape[1])


def _mixer_ab(u, pos, w_in, lq1, lk1, lq2, lk2, subln_g, mu, w0, w2, a0, a2, g2,
              k_k, k_a, r_k, lnx_g, lnx_b, lambda_init):
    b, t, d = u.shape
    a_width = d // 2
    b_width = d // 2
    a_heads = a_width // A_DV
    a_in = 3 * a_width
    n_in = w_in.shape[1]
    z = matmul(u.reshape(b * t, d), _pad_cols(w_in, _round_up(n_in, 512)))

    qa, ka, va = (z[:, c * a_width:(c + 1) * a_width].reshape(b, t, a_width) for c in range(3))
    rope = lambda x: _partial_rope(x.reshape(b, t, 2 * a_heads, A_DC), pos).reshape(b, t, a_width)
    lam = jnp.exp(jnp.sum(lq1 * lk1)) - jnp.exp(jnp.sum(lq2 * lk2)) + lambda_init
    ya = diff_attention(rope(qa), rope(ka), va, lam, subln_g * (1.0 - lambda_init))

    prep = rwkv_prep(z, a_in, b_width, n_in - a_in - 3 * b_width, t, mu, w0, w2, a0, a2, g2, k_k, k_a)
    yb = wkv7(*(x.reshape(b, t, b_width) for x in prep), r_k.reshape(-1), lnx_g, lnx_b)
    return jnp.concatenate([ya, yb.astype(BF16)], axis=-1)


def _mixer_c(u, pos, w_in, idx_k_g, idx_k_b, topk):
    b, t, d = u.shape
    c_heads = d // C_HEAD_DIM
    n_in = w_in.shape[1]
    c1 = c_heads * C_HEAD_DIM
    c2 = c1 + C_KV_HEADS * C_HEAD_DIM
    c3 = c2 + C_KV_HEADS * C_HEAD_DIM
    c4 = c3 + IDX_HEADS * IDX_DIM
    c5 = c4 + IDX_DIM
    z = _mm3(u, _pad_cols(w_in, _round_up(n_in, 512)))[..., :n_in]
    q, k, v, qi, ki, wi = jnp.split(z, [c1, c2, c3, c4, c5], axis=-1)
    q = _partial_rope(q.reshape(b, t, c_heads, C_HEAD_DIM), pos).reshape(b, t, c1)
    k = _partial_rope(k.reshape(b, t, C_KV_HEADS, C_HEAD_DIM), pos).reshape(b, t, c2 - c1)
    qi = _partial_rope(qi.reshape(b, t, IDX_HEADS, IDX_DIM), pos)
    ki = _partial_rope(_layer_norm(ki, idx_k_g, idx_k_b)[:, :, None, :], pos)[:, :, 0]
    wi = wi * ((IDX_HEADS * IDX_DIM) ** -0.5)
    bias = dsa_select(ki, qi, wi, topk)
    return dsa_attention(q, k, v, bias)


def kernel(x, p, positions, mix_pre_g, mix_post_g, mlp_pre_g, mlp_post_g, w_mlp_up, w_mlp_down, w_ple_proj, w_ple_gate, ple_post_g, ab_w_in, ab_w_out, diff_lq1, diff_lk1, diff_lq2, diff_lk2, diff_subln_g, rwkv_mu, rwkv_w0, rwkv_w2, rwkv_a0, rwkv_a2, rwkv_g2, rwkv_k_k, rwkv_k_a, rwkv_r_k, rwkv_lnx_g, rwkv_lnx_b, c_w_in, c_w_out, idx_k_g, idx_k_b):
    depth = mix_pre_g.shape[0]
    b, t, d = x.shape
    topk = min(TOPK_MAX, t // 4)
    flat = lambda z: z.reshape(b * t, z.shape[-1])
    h = flat(x)
    u = _rms_norm(h, mix_pre_g[0]).astype(BF16)
    for i in range(depth):
        j = i // 2
        u3 = u.reshape(b, t, d)
        if i % 2 == 0:
            lambda_init = 0.8 - 0.6 * math.exp(-0.3 * i)
            m = _mixer_ab(u3, positions, ab_w_in[j], diff_lq1[j], diff_lk1[j], diff_lq2[j],
                          diff_lk2[j], diff_subln_g[j], rwkv_mu[j], rwkv_w0[j], rwkv_w2[j], rwkv_a0[j],
                          rwkv_a2[j], rwkv_g2[j], rwkv_k_k[j], rwkv_k_a[j], rwkv_r_k[j], rwkv_lnx_g[j],
                          rwkv_lnx_b[j], lambda_init)
            w_out = ab_w_out[j]
        else:
            m = _mixer_c(u3, positions, c_w_in[j], idx_k_g[j], idx_k_b[j], topk)
            w_out = c_w_out[j]
        h, u = proj_residual(flat(m), w_out, h, mix_post_g[i], mlp_pre_g[i])
        up = matmul(u, w_mlp_up[i], out_dtype=BF16, act="relu2")
        h, _ = proj_residual(up, w_mlp_down[i], h, mlp_post_g[i], None)
        g_next = mix_pre_g[i + 1] if i + 1 < depth else None
        h, u = ple_residual(flat(p[i]), w_ple_proj[i], w_ple_gate[i], h, ple_post_g[i], g_next)
    return h.reshape(b, t, d)
```

```python
import functools
import math

import jax
import jax.numpy as jnp
from jax import lax
from jax.experimental import pallas as pl
from jax.experimental.pallas import tpu as pltpu

F32 = jnp.float32
BF16 = jnp.bfloat16
I32 = jnp.int32

V7X_VMEM_LIMIT_BYTES = 56 * 1024 * 1024
NEG = -1e30
LOG2E = 1.4426950408889634
INT_MIN = -2147483648

NORM_EPS = 1e-6
ROPE_THETA = 500000.0
ROPE_FRACTION = 4
A_DV = 128
A_DC = 64
DIFF_SUBLN_EPS = 1e-5
B_HEAD = 64
LNX_EPS = 64e-5
C_HEAD_DIM = 128
C_KV_HEADS = 4
IDX_HEADS = 16
IDX_DIM = 64
TOPK_MAX = 256
LN_EPS = 1e-6
WKV_CHUNK = 64


def _params(sem):
    return pltpu.CompilerParams(dimension_semantics=sem, vmem_limit_bytes=V7X_VMEM_LIMIT_BYTES)


def _pick(n, prefs):
    for t in prefs:
        if n % t == 0:
            return t
    return n


def _mm_body(a_ref, w_ref, o_ref, *scratch, nk, act):
    def epilogue(acc):
        if act == "relu2":
            r = jnp.maximum(acc, 0.0)
            acc = r * r
        return acc.astype(o_ref.dtype)

    if nk == 1:
        o_ref[...] = epilogue(jnp.dot(a_ref[...], w_ref[...], preferred_element_type=F32))
        return
    (acc_ref,) = scratch
    k = pl.program_id(2)

    @pl.when(k == 0)
    def _():
        acc_ref[...] = jnp.zeros_like(acc_ref)

    acc_ref[...] += jnp.dot(a_ref[...], w_ref[...], preferred_element_type=F32)

    @pl.when(k == nk - 1)
    def _():
        o_ref[...] = epilogue(acc_ref[...])


def matmul(a, w, out_dtype=F32, act=None):
    a = a.astype(BF16)
    w = w.astype(BF16)
    m, kdim = a.shape
    n = w.shape[1]
    tm = _pick(m, (1024, 512, 256, 128, 64, 32, 16, 8))
    tn = _pick(n, (1024, 512, 256, 128))
    tk = kdim if kdim <= 2048 else _pick(kdim, (2048, 1024, 512))
    nk = kdim // tk
    scratch = [pltpu.VMEM((tm, tn), F32)] if nk > 1 else []
    return pl.pallas_call(
        functools.partial(_mm_body, nk=nk, act=act),
        grid=(m // tm, n // tn, nk),
        in_specs=[pl.BlockSpec((tm, tk), lambda i, j, k: (i, k)),
                  pl.BlockSpec((tk, tn), lambda i, j, k: (k, j))],
        out_specs=pl.BlockSpec((tm, tn), lambda i, j, k: (i, j)),
        out_shape=jax.ShapeDtypeStruct((m, n), out_dtype),
        scratch_shapes=scratch,
        compiler_params=_params(("parallel", "parallel", "arbitrary")),
        name="dense_matmul",
    )(a, w)


def _rms_rows(x, g):
    return x * lax.rsqrt(jnp.mean(x * x, axis=-1, keepdims=True) + NORM_EPS) * g


def _residual_update(m, h_ref, gpost_ref, gnext_ref, ho_ref, uo_ref):
    h_new = h_ref[...] + _rms_rows(m, gpost_ref[...])
    ho_ref[...] = h_new
    if uo_ref is not None:
        uo_ref[...] = _rms_rows(h_new, gnext_ref[...]).astype(BF16)


def _proj_res_body(*refs, nk, has_next):
    refs = list(refs)
    acc_ref = refs.pop() if nk > 1 else None
    if has_next:
        a_ref, w_ref, h_ref, gpost_ref, gnext_ref, ho_ref, uo_ref = refs
    else:
        a_ref, w_ref, h_ref, gpost_ref, ho_ref = refs
        gnext_ref = uo_ref = None
    part = jnp.dot(a_ref[...], w_ref[...], preferred_element_type=F32)
    if nk == 1:
        _residual_update(part, h_ref, gpost_ref, gnext_ref, ho_ref, uo_ref)
        return
    k = pl.program_id(1)

    @pl.when(k == 0)
    def _():
        acc_ref[...] = part

    @pl.when(jnp.logical_and(k > 0, k < nk - 1))
    def _():
        acc_ref[...] += part

    @pl.when(k == nk - 1)
    def _():
        _residual_update(acc_ref[...] + part, h_ref, gpost_ref, gnext_ref, ho_ref, uo_ref)


def proj_residual(a, w, h, g_post, g_next):
    m, kdim = a.shape
    n = w.shape[1]
    tm = _pick(m, (512, 256, 128, 64, 32, 16, 8))
    tk = _pick(kdim, (2048, 1024, 512, 256, 128))
    nk = kdim // tk
    has_next = g_next is not None
    row = pl.BlockSpec((tm, n), lambda i, k: (i, 0))
    vec = pl.BlockSpec((1, n), lambda i, k: (0, 0))
    in_specs = [pl.BlockSpec((tm, tk), lambda i, k: (i, k)), pl.BlockSpec((tk, n), lambda i, k: (k, 0)), row, vec]
    args = [a.astype(BF16), w.astype(BF16), h, g_post.reshape(1, n).astype(F32)]
    out_shape = [jax.ShapeDtypeStruct((m, n), F32)]
    out_specs = [row]
    if has_next:
        in_specs.append(vec)
        args.append(g_next.reshape(1, n).astype(F32))
        out_shape.append(jax.ShapeDtypeStruct((m, n), BF16))
        out_specs.append(row)
    out = pl.pallas_call(
        functools.partial(_proj_res_body, nk=nk, has_next=has_next),
        grid=(m // tm, nk),
        in_specs=in_specs, out_specs=out_specs, out_shape=out_shape,
        scratch_shapes=[pltpu.VMEM((tm, n), F32)] if nk > 1 else [],
        compiler_params=_params(("parallel", "arbitrary")),
        name="proj_residual",
    )(*args)
    return (out[0], out[1]) if has_next else (out[0], None)


def _ple_body(*refs, has_next):
    if has_next:
        p_ref, wp_ref, wg_ref, h_ref, gpost_ref, gnext_ref, ho_ref, uo_ref = refs
    else:
        p_ref, wp_ref, wg_ref, h_ref, gpost_ref, ho_ref = refs
        gnext_ref = uo_ref = None
    e = jnp.dot(p_ref[...], wp_ref[...], preferred_element_type=F32)
    gate = jax.nn.sigmoid(jnp.dot(h_ref[...].astype(BF16), wg_ref[...], preferred_element_type=F32))
    _residual_update(e * gate, h_ref, gpost_ref, gnext_ref, ho_ref, uo_ref)


def ple_residual(p, w_proj, w_gate, h, g_post, g_next):
    m, n = h.shape
    pd = p.shape[1]
    tm = _pick(m, (256, 128, 64, 32, 16, 8))
    has_next = g_next is not None
    row = pl.BlockSpec((tm, n), lambda i: (i, 0))
    vec = pl.BlockSpec((1, n), lambda i: (0, 0))
    in_specs = [pl.BlockSpec((tm, pd), lambda i: (i, 0)), pl.BlockSpec((pd, n), lambda i: (0, 0)),
                pl.BlockSpec((n, n), lambda i: (0, 0)), row, vec]
    args = [p.astype(BF16), w_proj.astype(BF16), w_gate.astype(BF16), h, g_post.reshape(1, n).astype(F32)]
    out_shape = [jax.ShapeDtypeStruct((m, n), F32)]
    out_specs = [row]
    if has_next:
        in_specs.append(vec)
        args.append(g_next.reshape(1, n).astype(F32))
        out_shape.append(jax.ShapeDtypeStruct((m, n), BF16))
        out_specs.append(row)
    out = pl.pallas_call(
        functools.partial(_ple_body, has_next=has_next),
        grid=(m // tm,),
        in_specs=in_specs, out_specs=out_specs, out_shape=out_shape,
        compiler_params=_params(("parallel",)),
        name="ple_residual",
    )(*args)
    return (out[0], out[1]) if has_next else (out[0], None)


ONES_ROWS = 16


def _with_ones_rows(vt):
    pad = jnp.zeros(vt.shape[:-2] + (ONES_ROWS, vt.shape[-1]), vt.dtype).at[..., 0, :].set(1)
    return jnp.concatenate([vt, pad], axis=-2)


def _score_stage(score_fns, m_ref, p_ref, alpha_ref):
    n = range(len(score_fns))
    ss = [fn() for fn in score_fns]
    m_old = [m_ref[i] for i in n]
    m_new = [jnp.maximum(m_old[i], jnp.max(ss[i], axis=0, keepdims=True)) for i in n]
    for i in n:
        alpha_ref[i] = jnp.exp2(m_old[i] - m_new[i])
        m_ref[i] = m_new[i]
    for i in n:
        p_ref[i] = jnp.exp2((ss[i] - m_new[i]).astype(BF16))


def _value_stage(vt, acc_ref, p_ref, alpha_ref):
    n = range(p_ref.shape[0])
    pv = [jnp.dot(vt, p_ref[i], preferred_element_type=F32) for i in n]
    for i in n:
        acc_ref[i] = alpha_ref[i] * acc_ref[i] + pv[i]


def _diff_attn_body(lam_ref, k_ref, q_ref, cmask_ref, vt_ref, g_ref, o_ref, m_ref, acc_ref, qz_ref,
                    pa_ref, aa_ref, pb_ref, ab_ref, *, tq, nsp):
    i = pl.program_id(2)
    tl = tq // nsp
    m_ref[...] = jnp.full(m_ref.shape, NEG, F32)
    acc_ref[...] = jnp.zeros(acc_ref.shape, F32)
    for c in range(2):
        qz_ref[c] = q_ref[...] * cmask_ref[c]

    buf_a = (pa_ref, aa_ref)
    buf_b = (pb_ref, ab_ref)

    def scores(j, buf, masked=False):
        kj = k_ref[pl.ds(pl.multiple_of(j * tq, tq), tq), :]

        def score(c, sp):
            s = lax.dot_general(kj, qz_ref[c, sp * tl:(sp + 1) * tl, :], (_NT, ((), ())),
                                preferred_element_type=F32)
            if masked:
                row = lax.broadcasted_iota(I32, (tq, tl), 0)
                col = lax.broadcasted_iota(I32, (tq, tl), 1)
                s = jnp.where(row <= col + sp * tl, s, NEG)
            return s

        _score_stage([functools.partial(score, c, sp) for c in range(2) for sp in range(nsp)], m_ref, *buf)

    def values(j, buf):
        _value_stage(vt_ref[j], acc_ref, *buf)

    @pl.when(i == 0)
    def _():
        scores(0, buf_a, masked=True)
        values(0, buf_a)

    @pl.when(i > 0)
    def _():
        scores(0, buf_a)

        def pair(t, carry):
            c = 2 * t + 1
            scores(c, buf_b)
            values(c - 1, buf_a)
            scores(c + 1, buf_a)
            values(c, buf_b)
            return carry

        lax.fori_loop(0, (i - 1) // 2, pair, 0)

        @pl.when((i - 1) % 2 == 1)
        def _():
            scores(i - 1, buf_b)
            values(i - 2, buf_a)
            scores(i, buf_a, masked=True)
            values(i - 1, buf_b)
            values(i, buf_a)

        @pl.when((i - 1) % 2 == 0)
        def _():
            scores(i, buf_b, masked=True)
            values(i - 1, buf_a)
            values(i, buf_b)

    lam = lam_ref[0]
    for sp in range(nsp):
        a0 = acc_ref[sp]
        a1 = acc_ref[nsp + sp]
        out = (a0[:A_DV] / a0[A_DV:A_DV + 1] - lam * (a1[:A_DV] / a1[A_DV:A_DV + 1])).T
        y = out * lax.rsqrt(jnp.mean(out * out, axis=-1, keepdims=True) + DIFF_SUBLN_EPS) * g_ref[...]
        o_ref[sp * tl:(sp + 1) * tl, :] = y.astype(o_ref.dtype)


def diff_attention(q, k, v, lam, gain):
    b, t, width = q.shape
    h = width // A_DV
    tq = _pick(t, (512, 256, 128))
    nq = t // tq
    qs = (q * (A_DC ** -0.5 * LOG2E)).astype(BF16)
    vt = jnp.transpose(v.astype(BF16).reshape(b, nq, tq, h, A_DV), (0, 3, 1, 4, 2))
    vt = _with_ones_rows(vt)
    dve = A_DV + ONES_ROWS
    nsp = 2 if tq >= 256 else 1
    lane = jnp.arange(A_DV)
    cmask = jnp.stack([lane < A_DC, lane >= A_DC]).astype(BF16).reshape(2, 1, A_DV)
    head = pl.BlockSpec((None, tq, A_DV), lambda bi, hi, qi: (bi, qi, hi))
    return pl.pallas_call(
        functools.partial(_diff_attn_body, tq=tq, nsp=nsp),
        grid=(b, h, nq),
        in_specs=[pl.BlockSpec(memory_space=pltpu.SMEM),
                  pl.BlockSpec((None, t, A_DV), lambda bi, hi, qi: (bi, 0, hi)),
                  head,
                  pl.BlockSpec((2, 1, A_DV), lambda bi, hi, qi: (0, 0, 0)),
                  pl.BlockSpec((None, None, nq, dve, tq), lambda bi, hi, qi: (bi, hi, 0, 0, 0)),
                  pl.BlockSpec((1, A_DV), lambda bi, hi, qi: (0, 0))],
        out_specs=head,
        out_shape=jax.ShapeDtypeStruct((b, t, width), BF16),
        scratch_shapes=[pltpu.VMEM((2 * nsp, 1, tq // nsp), F32),
                        pltpu.VMEM((2 * nsp, dve, tq // nsp), F32),
                        pltpu.VMEM((2, tq, A_DV), BF16)]
        + [pltpu.VMEM((2 * nsp, tq, tq // nsp), BF16), pltpu.VMEM((2 * nsp, 1, tq // nsp), F32)] * 2,
        compiler_params=_params(("parallel", "parallel", "arbitrary")),
        name="diff_attention",
    )(lam.reshape(1).astype(F32), k.astype(BF16), qs, cmask, vt, gain.reshape(1, A_DV).astype(F32))


def _split_dot(x, w):
    hi = x.astype(BF16)
    lo = (x - hi.astype(F32)).astype(BF16)
    return jnp.dot(hi, w, preferred_element_type=F32) + jnp.dot(lo, w, preferred_element_type=F32)


def _rwkv_prep_body(zr_ref, zk_ref, zv_ref, zl_ref, pr_ref, pk_ref, pv_ref, pl_ref,
                    mur_ref, muk_ref, muv_ref, mul_ref, w0_ref, a0_ref, kk_ref, ka_ref,
                    w2_ref, a2_ref, g2_ref, hsum_ref,
                    r_ref, lw_ref, k_ref, v_ref, an_ref, bn_ref, g_ref, *, tm, seq):
    first = (pl.program_id(0) * tm) % seq == 0

    def shifted(z_ref, p_ref, mu_ref):
        z = z_ref[...]
        last = jnp.where(first, 0.0, p_ref[7:8, :])
        row = lax.broadcasted_iota(I32, z.shape, 0)
        prev = jnp.where(row == 0, last, pltpu.roll(z, 1, 0))
        return z + (prev - z) * mu_ref[...]

    r = shifted(zr_ref, pr_ref, mur_ref)
    k = shifted(zk_ref, pk_ref, muk_ref)
    v = shifted(zv_ref, pv_ref, muv_ref)
    lo = shifted(zl_ref, pl_ref, mul_ref)
    lw = jnp.dot(jnp.tanh(lo).astype(BF16), w2_ref[...], preferred_element_type=F32)
    la = jnp.dot(lo.astype(BF16), a2_ref[...], preferred_element_type=F32)
    g = jnp.dot(jax.nn.sigmoid(lo).astype(BF16), g2_ref[...], preferred_element_type=F32)
    w = -jax.nn.softplus(-(w0_ref[...] + lw)) - 0.5
    a = jax.nn.sigmoid(a0_ref[...] + la)
    kk = k * kk_ref[...]
    norm = jnp.sqrt(_split_dot(kk * kk, hsum_ref[...]))
    kk = kk / jnp.maximum(norm, 1e-12)
    r_ref[...] = r
    lw_ref[...] = -jnp.exp(w)
    k_ref[...] = k * (1.0 + (a - 1.0) * ka_ref[...])
    v_ref[...] = v
    an_ref[...] = -kk
    bn_ref[...] = kk * a
    g_ref[...] = g


def rwkv_prep(z, col0, width, n_lora, seq, mu, w0, w2, a0, a2, g2, k_k, k_a):
    m = z.shape[0]
    tm = _pick(seq, (256, 128, 64, 32, 16, 8))
    lw_ = _round_up(n_lora, 128)
    assert col0 % width == 0 and (col0 + 3 * width) % lw_ == 0 and col0 + 3 * width + lw_ <= z.shape[1]
    cb = col0 // width
    lb = (col0 + 3 * width) // lw_
    dl, al = w2.shape[0], a2.shape[0]

    def rows_of(w, start):
        return jnp.zeros((lw_, width), BF16).at[start:start + w.shape[0]].set(w.astype(BF16))

    hsum = jnp.kron(jnp.eye(width // B_HEAD, dtype=BF16), jnp.ones((B_HEAD, B_HEAD), BF16))
    vec = lambda x: x.reshape(1, -1).astype(F32)
    mul = jnp.pad(mu[3 * width:], (0, lw_ - n_lora))
    tile = lambda c, wd: pl.BlockSpec((tm, wd), lambda i: (i, c))
    before = lambda c, wd: pl.BlockSpec((8, wd), lambda i: (jnp.maximum(i * (tm // 8) - 1, 0), c))
    par = lambda wd: pl.BlockSpec((1, wd), lambda i: (0, 0))
    mat = lambda r_, c_: pl.BlockSpec((r_, c_), lambda i: (0, 0))
    out = pl.BlockSpec((tm, width), lambda i: (i, 0))
    return pl.pallas_call(
        functools.partial(_rwkv_prep_body, tm=tm, seq=seq),
        grid=(m // tm,),
        in_specs=[tile(cb, width), tile(cb + 1, width), tile(cb + 2, width), tile(lb, lw_),
                  before(cb, width), before(cb + 1, width), before(cb + 2, width), before(lb, lw_),
                  par(width), par(width), par(width), par(lw_), par(width), par(width), par(width), par(width),
                  mat(lw_, width), mat(lw_, width), mat(lw_, width), mat(width, width)],
        out_specs=[out] * 7,
        out_shape=[jax.ShapeDtypeStruct((m, width), F32)] * 7,
        compiler_params=_params(("parallel",)),
        name="rwkv_prep",
    )(z, z, z, z, z, z, z, z,
      vec(mu[:width]), vec(mu[width:2 * width]), vec(mu[2 * width:3 * width]), vec(mul),
      vec(w0), vec(a0), vec(k_k), vec(k_a),
      rows_of(w2, 0), rows_of(a2, dl), rows_of(g2, dl + al), hsum)


def _dot(a, b, dims, exact):
    if exact:
        return lax.dot_general(a, b, (dims, ((), ())), precision=lax.Precision.HIGHEST,
                               preferred_element_type=F32)
    return lax.dot_general(a.astype(BF16), b.astype(BF16), (dims, ((), ())), preferred_element_type=F32)


_NN = ((1,), (0,))
_NT = ((1,), (1,))
_TN = ((0,), (0,))


def _wkv_body(r_ref, lw_ref, k_ref, v_ref, a_ref, b_ref, g_ref, rk_ref, lng_ref, lnb_ref, y_ref, s_ref,
              *, nh, n, c, exact):
    @pl.when(pl.program_id(1) == 0)
    def _():
        s_ref[...] = jnp.zeros(s_ref.shape, F32)

    row = lax.broadcasted_iota(I32, (c, c), 0)
    col = lax.broadcasted_iota(I32, (c, c), 1)
    incl = row >= col
    strict = row > col
    tri = jnp.where(incl, 1.0, 0.0).astype(F32)
    eye = jnp.where(row == col, 1.0, 0.0).astype(F32)
    nsq = int(math.log2(c)) - 1

    hs = range(nh)
    dot = functools.partial(_dot, exact=exact)
    heads = lambda ref: [ref[:, h * n:(h + 1) * n] for h in hs]
    lw, r, k, v, a, b = (heads(ref) for ref in (lw_ref, r_ref, k_ref, v_ref, a_ref, b_ref))
    cum = [_dot(tri, lw[h], _NN, True) for h in hs]
    tot = [cum[h][c - 1:c, :] for h in hs]
    pinv = [jnp.exp(-cum[h]) for h in hs]
    pend = [jnp.exp(tot[h] - cum[h]) for h in hs]
    at = [a[h] * jnp.exp(cum[h] - lw[h]) for h in hs]
    rt = [r[h] * jnp.exp(cum[h]) for h in hs]
    bt = [b[h] * pinv[h] for h in hs]
    kt = [k[h] * pinv[h] for h in hs]
    a_ab = [jnp.where(strict, dot(at[h], bt[h], _NT), 0.0) for h in hs]
    a_ak = [jnp.where(strict, dot(at[h], kt[h], _NT), 0.0) for h in hs]
    a_rb = [jnp.where(incl, dot(rt[h], bt[h], _NT), 0.0) for h in hs]
    a_rk = [jnp.where(incl, dot(rt[h], kt[h], _NT), 0.0) for h in hs]
    x = a_ab
    minv = [eye + x[h] for h in hs]
    for _ in range(nsq):
        x = [dot(x[h], x[h], _NN) for h in hs]
        minv = [minv[h] + dot(minv[h], x[h], _NN) for h in hs]
    s0 = [s_ref[h] for h in hs]
    rhs = [dot(at[h], s0[h], _NT) + dot(a_ak[h], v[h], _NN) for h in hs]
    u = [dot(minv[h], rhs[h], _NN) for h in hs]
    y = [dot(rt[h], s0[h], _NT) + dot(a_rb[h], u[h], _NN) + dot(a_rk[h], v[h], _NN) for h in hs]
    for h in hs:
        s_ref[h] = (s0[h] * jnp.exp(tot[h]) + dot(u[h], b[h] * pend[h], _TN)
                    + dot(v[h], k[h] * pend[h], _TN))
    for h in hs:
        cols = slice(h * n, (h + 1) * n)
        d = y[h] - jnp.mean(y[h], axis=-1, keepdims=True)
        yn = d * lax.rsqrt(jnp.mean(d * d, axis=-1, keepdims=True) + LNX_EPS)
        bonus = jnp.sum(r[h] * k[h] * rk_ref[:, cols], axis=-1, keepdims=True) * v[h]
        y_ref[:, cols] = (yn * lng_ref[:, cols] + lnb_ref[:, cols] + bonus) * g_ref[:, cols]


def wkv7(r, lw, k, v, a, b, g, r_k, lnx_g, lnx_b, n=B_HEAD, exact=False):
    bsz, t, width = r.shape
    nh = width // n
    c = min(WKV_CHUNK, t)
    spec = pl.BlockSpec((None, c, width), lambda bi, ci: (bi, ci, 0))
    par = pl.BlockSpec((1, width), lambda bi, ci: (0, 0))
    vec = lambda x: x.reshape(1, width).astype(F32)
    return pl.pallas_call(
        functools.partial(_wkv_body, nh=nh, n=n, c=c, exact=exact),
        grid=(bsz, t // c),
        in_specs=[spec] * 7 + [par] * 3,
        out_specs=spec,
        out_shape=jax.ShapeDtypeStruct((bsz, t, width), F32),
        scratch_shapes=[pltpu.VMEM((nh, n, n), F32)],
        compiler_params=_params(("parallel", "arbitrary")),
        name="wkv7_chunked",
    )(r, lw, k, v, a, b, g, vec(r_k), vec(lnx_g), vec(lnx_b))


def _dsa_index_body(ki_ref, qit_ref, wit_ref, bias_ref, key_ref, cut_ref, *, tq, tkc, topk, nheads, t):
    i = pl.program_id(1)
    nch = ((i + 1) * tq + tkc - 1) // tkc
    qpos = i * tq + lax.broadcasted_iota(I32, (1, tq), 1)

    def kpos_of(ci):
        return ci * tkc + lax.broadcasted_iota(I32, (tkc, 1), 0)

    def rows(ci):
        return pl.ds(pl.multiple_of(ci * tkc, tkc), tkc)

    def score_chunk(ci, carry):
        kc = ki_ref[rows(ci), :]
        acc = jnp.zeros((tkc, tq), F32)
        for h in range(nheads):
            s = jnp.dot(kc, qit_ref[h], preferred_element_type=F32)
            acc = acc + jnp.maximum(s, 0.0) * wit_ref[h]
        acc = jnp.where(kpos_of(ci) <= qpos, acc, -jnp.inf)
        bits = lax.bitcast_convert_type(acc, I32)
        key_ref[rows(ci), :] = jnp.where(bits >= 0, bits, bits ^ 0x7FFFFFFF)
        return carry

    lax.fori_loop(0, nch, score_chunk, 0)

    def count(pred):
        def body(ci, cnt):
            ones = pred(key_ref[rows(ci), :], kpos_of(ci))
            return cnt + jnp.sum(ones, axis=0, keepdims=True)
        return lax.fori_loop(0, nch, body, jnp.zeros((1, tq), I32))

    def value_bit(step, prefix):
        cand = prefix | lax.shift_left(jnp.int32(1), 31 - step)
        cand_signed = cand ^ INT_MIN
        cnt = count(lambda key, kpos: jnp.where(key >= cand_signed, 1, 0))
        return jnp.where(cnt >= topk, cand, prefix)

    thr = lax.fori_loop(0, 32, value_bit, jnp.zeros((1, tq), I32)) ^ INT_MIN
    n_ge = count(lambda key, kpos: jnp.where(key >= thr, 1, 0))
    n_gt = count(lambda key, kpos: jnp.where(key > thr, 1, 0))
    need = topk - n_gt
    cut_ref[...] = jnp.full((1, tq), t, I32)

    @pl.when(jnp.max(n_ge) > topk)
    def _():
        nbits = max(1, (t - 1).bit_length())

        def index_bit(step, x):
            cand = x | lax.shift_left(jnp.int32(1), nbits - 1 - step)
            g = count(lambda key, kpos: jnp.where(key == thr, jnp.where(kpos < cand, 1, 0), 0))
            return jnp.where(g < need, cand, x)

        cut_ref[...] = lax.fori_loop(0, nbits, index_bit, jnp.zeros((1, tq), I32))

    cutoff = cut_ref[...]

    def write_chunk(ci, carry):
        key = key_ref[rows(ci), :]
        kpos = kpos_of(ci)
        tie = jnp.where(kpos <= cutoff, 0.0, NEG)
        sel = jnp.where(key > thr, 0.0, jnp.where(key == thr, tie, NEG))
        bias_ref[rows(ci), :] = jnp.where(kpos <= qpos, sel, NEG).astype(BF16)
        return carry

    lax.fori_loop(0, nch, write_chunk, 0)

    def fill_chunk(ci, carry):
        bias_ref[rows(ci), :] = jnp.full((tkc, tq), NEG, BF16)
        return carry

    lax.fori_loop(nch, t // tkc, fill_chunk, 0)


def dsa_select(ki, qi, wi, topk):
    b, t, nh, d = qi.shape
    tq = _pick(t, (256, 128))
    tkc = _pick(t, (512, 256, 128))
    nq = t // tq
    qit = jnp.transpose(qi.astype(BF16).reshape(b, nq, tq, nh, d), (0, 1, 3, 4, 2))
    wit = jnp.transpose(wi.astype(F32).reshape(b, nq, tq, nh), (0, 1, 3, 2))[:, :, :, None, :]
    return pl.pallas_call(
        functools.partial(_dsa_index_body, tq=tq, tkc=tkc, topk=topk, nheads=nh, t=t),
        grid=(b, nq),
        in_specs=[pl.BlockSpec((None, t, d), lambda bi, qi_: (bi, 0, 0)),
                  pl.BlockSpec((None, None, nh, d, tq), lambda bi, qi_: (bi, qi_, 0, 0, 0)),
                  pl.BlockSpec((None, None, nh, 1, tq), lambda bi, qi_: (bi, qi_, 0, 0, 0))],
        out_specs=pl.BlockSpec((None, t, tq), lambda bi, qi_: (bi, 0, qi_)),
        out_shape=jax.ShapeDtypeStruct((b, t, t), BF16),
        scratch_shapes=[pltpu.VMEM((t, tq), I32), pltpu.VMEM((1, tq), I32)],
        compiler_params=_params(("parallel", "arbitrary")),
        name="dsa_index_topk",
    )(ki.astype(BF16), qit, wit)


def _dsa_attn_body(k_ref, vt_ref, q_ref, bias_ref, o_ref, m_ref, acc_ref, pa_ref, aa_ref, pb_ref, ab_ref,
                   *, tq, tk, rep, d):
    i = pl.program_id(2)
    nch = ((i + 1) * tq + tk - 1) // tk
    m_ref[...] = jnp.full(m_ref.shape, NEG, F32)
    acc_ref[...] = jnp.zeros(acc_ref.shape, F32)
    buf_a = (pa_ref, aa_ref)
    buf_b = (pb_ref, ab_ref)

    def scores(c, buf):
        rows = pl.ds(pl.multiple_of(c * tk, tk), tk)
        kj = k_ref[rows, :]
        bj = bias_ref[rows, :].astype(F32)
        fns = [lambda r=r: lax.dot_general(kj, q_ref[:, r * d:(r + 1) * d], (_NT, ((), ())),
                                           preferred_element_type=F32) + bj for r in range(rep)]
        _score_stage(fns, m_ref, *buf)

    def values(c, buf):
        _value_stage(vt_ref[c], acc_ref, *buf)

    scores(0, buf_a)

    def pair(t, carry):
        c = 2 * t + 1
        scores(c, buf_b)
        values(c - 1, buf_a)
        scores(c + 1, buf_a)
        values(c, buf_b)
        return carry

    lax.fori_loop(0, (nch - 1) // 2, pair, 0)
    last = nch - 1

    @pl.when(last % 2 == 1)
    def _():
        scores(last, buf_b)
        values(last - 1, buf_a)
        values(last, buf_b)

    @pl.when(last % 2 == 0)
    def _():
        values(last, buf_a)

    for r in range(rep):
        acc = acc_ref[r]
        o_ref[:, r * d:(r + 1) * d] = (acc[:d] / acc[d:d + 1]).T.astype(o_ref.dtype)


def dsa_attention(q, k, v, bias):
    b, t, qw = q.shape
    d = C_HEAD_DIM
    g = k.shape[2] // d
    rep = qw // (g * d)
    tq = _pick(t, (256, 128))
    tk = _pick(t, (512, 256, 128))
    nq, nk = t // tq, t // tk
    qs = (q * (d ** -0.5 * LOG2E)).astype(BF16)
    vt = jnp.transpose(v.astype(BF16).reshape(b, nk, tk, g, d), (0, 3, 1, 4, 2))
    vt = _with_ones_rows(vt)
    dve = d + ONES_ROWS
    group = pl.BlockSpec((None, tq, rep * d), lambda bi, gi, qi: (bi, qi, gi))
    return pl.pallas_call(
        functools.partial(_dsa_attn_body, tq=tq, tk=tk, rep=rep, d=d),
        grid=(b, g, nq),
        in_specs=[pl.BlockSpec((None, t, d), lambda bi, gi, qi: (bi, 0, gi)),
                  pl.BlockSpec((None, None, nk, dve, tk), lambda bi, gi, qi: (bi, gi, 0, 0, 0)),
                  group,
                  pl.BlockSpec((None, t, tq), lambda bi, gi, qi: (bi, 0, qi))],
        out_specs=group,
        out_shape=jax.ShapeDtypeStruct((b, t, qw), BF16),
        scratch_shapes=[pltpu.VMEM((rep, 1, tq), F32), pltpu.VMEM((rep, dve, tq), F32)]
        + [pltpu.VMEM((rep, tk, tq), BF16), pltpu.VMEM((rep, 1, tq), F32)] * 2,
        compiler_params=_params(("parallel", "parallel", "arbitrary")),
        name="dsa_attention",
    )(k.astype(BF16), vt, qs, bias)


def _rms_norm(x, g, eps=NORM_EPS):
    xf = x.astype(F32)
    return xf * lax.rsqrt(jnp.mean(xf * xf, axis=-1, keepdims=True) + eps) * g.astype(F32)


def _layer_norm(x, g, b, eps=LN_EPS):
    mu = jnp.mean(x, axis=-1, keepdims=True)
    var = jnp.mean(jnp.square(x - mu), axis=-1, keepdims=True)
    return (x - mu) * lax.rsqrt(var + eps) * g + b


def _partial_rope(x, pos):
    dh = x.shape[-1]
    rot = dh // ROPE_FRACTION
    half = rot // 2
    inv_freq = ROPE_THETA ** (-(jnp.arange(half, dtype=F32) * 2.0 / rot))
    ang = pos.astype(F32)[..., None] * inv_freq
    cos = jnp.cos(ang)[:, :, None, :]
    sin = jnp.sin(ang)[:, :, None, :]
    x1 = x[..., :half]
    x2 = x[..., half:rot]
    return jnp.concatenate([x1 * cos - x2 * sin, x2 * cos + x1 * sin, x[..., rot:]], axis=-1)


def _pad_cols(w, n):
    return jnp.pad(w, ((0, 0), (0, n - w.shape[1])))


def _round_up(n, m):
    return (n + m - 1) // m * m


def _mm3(x, w, **kw):
    b, t, _ = x.shape
    return matmul(x.reshape(b * t, x.shape[-1]), w, **kw).reshape(b, t, w.shape[1])


def _mixer_ab(u, pos, w_in, lq1, lk1, lq2, lk2, subln_g, mu, w0, w2, a0, a2, g2,
              k_k, k_a, r_k, lnx_g, lnx_b, lambda_init):
    b, t, d = u.shape
    a_width = d // 2
    b_width = d // 2
    a_heads = a_width // A_DV
    a_in = 3 * a_width
    n_in = w_in.shape[1]
    z = matmul(u.reshape(b * t, d), _pad_cols(w_in, _round_up(n_in, 512)))

    qa, ka, va = (z[:, c * a_width:(c + 1) * a_width].reshape(b, t, a_width) for c in range(3))
    rope = lambda x: _partial_rope(x.reshape(b, t, 2 * a_heads, A_DC), pos).reshape(b, t, a_width)
    lam = jnp.exp(jnp.sum(lq1 * lk1)) - jnp.exp(jnp.sum(lq2 * lk2)) + lambda_init
    ya = diff_attention(rope(qa), rope(ka), va, lam, subln_g * (1.0 - lambda_init))

    prep = rwkv_prep(z, a_in, b_width, n_in - a_in - 3 * b_width, t, mu, w0, w2, a0, a2, g2, k_k, k_a)
    yb = wkv7(*(x.reshape(b, t, b_width) for x in prep), r_k.reshape(-1), lnx_g, lnx_b)
    return jnp.concatenate([ya, yb.astype(BF16)], axis=-1)


def _mixer_c(u, pos, w_in, idx_k_g, idx_k_b, topk):
    b, t, d = u.shape
    c_heads = d // C_HEAD_DIM
    n_in = w_in.shape[1]
    c1 = c_heads * C_HEAD_DIM
    c2 = c1 + C_KV_HEADS * C_HEAD_DIM
    c3 = c2 + C_KV_HEADS * C_HEAD_DIM
    c4 = c3 + IDX_HEADS * IDX_DIM
    c5 = c4 + IDX_DIM
    z = _mm3(u, _pad_cols(w_in, _round_up(n_in, 512)))[..., :n_in]
    q, k, v, qi, ki, wi = jnp.split(z, [c1, c2, c3, c4, c5], axis=-1)
    q = _partial_rope(q.reshape(b, t, c_heads, C_HEAD_DIM), pos).reshape(b, t, c1)
    k = _partial_rope(k.reshape(b, t, C_KV_HEADS, C_HEAD_DIM), pos).reshape(b, t, c2 - c1)
    qi = _partial_rope(qi.reshape(b, t, IDX_HEADS, IDX_DIM), pos)
    ki = _partial_rope(_layer_norm(ki, idx_k_g, idx_k_b)[:, :, None, :], pos)[:, :, 0]
    wi = wi * ((IDX_HEADS * IDX_DIM) ** -0.5)
    bias = dsa_select(ki, qi, wi, topk)
    return dsa_attention(q, k, v, bias)


def kernel(x, p, positions, mix_pre_g, mix_post_g, mlp_pre_g, mlp_post_g, w_mlp_up, w_mlp_down, w_ple_proj, w_ple_gate, ple_post_g, ab_w_in, ab_w_out, diff_lq1, diff_lk1, diff_lq2, diff_lk2, diff_subln_g, rwkv_mu, rwkv_w0, rwkv_w2, rwkv_a0, rwkv_a2, rwkv_g2, rwkv_k_k, rwkv_k_a, rwkv_r_k, rwkv_lnx_g, rwkv_lnx_b, c_w_in, c_w_out, idx_k_g, idx_k_b):
    depth = mix_pre_g.shape[0]
    b, t, d = x.shape
    topk = min(TOPK_MAX, t // 4)
    flat = lambda z: z.reshape(b * t, z.shape[-1])
    h = flat(x)
    u = _rms_norm(h, mix_pre_g[0]).astype(BF16)
    for i in range(depth):
        j = i // 2
        u3 = u.reshape(b, t, d)
        if i % 2 == 0:
            lambda_init = 0.8 - 0.6 * math.exp(-0.3 * i)
            m = _mixer_ab(u3, positions, ab_w_in[j], diff_lq1[j], diff_lk1[j], diff_lq2[j],
                          diff_lk2[j], diff_subln_g[j], rwkv_mu[j], rwkv_w0[j], rwkv_w2[j], rwkv_a0[j],
                          rwkv_a2[j], rwkv_g2[j], rwkv_k_k[j], rwkv_k_a[j], rwkv_r_k[j], rwkv_lnx_g[j],
                          rwkv_lnx_b[j], lambda_init)
            w_out = ab_w_out[j]
        else:
            m = _mixer_c(u3, positions, c_w_in[j], idx_k_g[j], idx_k_b[j], topk)
            w_out = c_w_out[j]
        h, u = proj_residual(flat(m), w_out, h, mix_post_g[i], mlp_pre_g[i])
        up = matmul(u, w_mlp_up[i], out_dtype=BF16, act="relu2")
        h, _ = proj_residual(up, w_mlp_down[i], h, mlp_post_g[i], None)
        g_next = mix_pre_g[i + 1] if i + 1 < depth else None
        h, u = ple_residual(flat(p[i]), w_ple_proj[i], w_ple_gate[i], h, ple_post_g[i], g_next)
    return h.reshape(b, t, d)
```

```python
import functools
import math

import jax
import jax.numpy as jnp
from jax import lax
from jax.experimental import pallas as pl
from jax.experimental.pallas import tpu as pltpu

F32 = jnp.float32
BF16 = jnp.bfloat16
I32 = jnp.int32

V7X_VMEM_LIMIT_BYTES = 56 * 1024 * 1024
NEG = -1e30
LOG2E = 1.4426950408889634
INT_MIN = -2147483648

NORM_EPS = 1e-6
ROPE_THETA = 500000.0
ROPE_FRACTION = 4
A_DV = 128
A_DC = 64
DIFF_SUBLN_EPS = 1e-5
B_HEAD = 64
LNX_EPS = 64e-5
C_HEAD_DIM = 128
C_KV_HEADS = 4
IDX_HEADS = 16
IDX_DIM = 64
TOPK_MAX = 256
LN_EPS = 1e-6
WKV_CHUNK = 64


def _params(sem):
    return pltpu.CompilerParams(dimension_semantics=sem, vmem_limit_bytes=V7X_VMEM_LIMIT_BYTES)


def _pick(n, prefs):
    for t in prefs:
        if n % t == 0:
            return t
    return n


def _mm_body(a_ref, w_ref, o_ref, *scratch, nk, act):
    def epilogue(acc):
        if act == "relu2":
            r = jnp.maximum(acc, 0.0)
            acc = r * r
        return acc.astype(o_ref.dtype)

    if nk == 1:
        o_ref[...] = epilogue(jnp.dot(a_ref[...], w_ref[...], preferred_element_type=F32))
        return
    (acc_ref,) = scratch
    k = pl.program_id(2)

    @pl.when(k == 0)
    def _():
        acc_ref[...] = jnp.zeros_like(acc_ref)

    acc_ref[...] += jnp.dot(a_ref[...], w_ref[...], preferred_element_type=F32)

    @pl.when(k == nk - 1)
    def _():
        o_ref[...] = epilogue(acc_ref[...])


def matmul(a, w, out_dtype=F32, act=None):
    a = a.astype(BF16)
    w = w.astype(BF16)
    m, kdim = a.shape
    n = w.shape[1]
    tm = _pick(m, (1024, 512, 256, 128, 64, 32, 16, 8))
    tn = _pick(n, (1024, 512, 256, 128))
    tk = kdim if kdim <= 2048 else _pick(kdim, (2048, 1024, 512))
    nk = kdim // tk
    scratch = [pltpu.VMEM((tm, tn), F32)] if nk > 1 else []
    return pl.pallas_call(
        functools.partial(_mm_body, nk=nk, act=act),
        grid=(m // tm, n // tn, nk),
        in_specs=[pl.BlockSpec((tm, tk), lambda i, j, k: (i, k)),
                  pl.BlockSpec((tk, tn), lambda i, j, k: (k, j))],
        out_specs=pl.BlockSpec((tm, tn), lambda i, j, k: (i, j)),
        out_shape=jax.ShapeDtypeStruct((m, n), out_dtype),
        scratch_shapes=scratch,
        compiler_params=_params(("parallel", "parallel", "arbitrary")),
        name="dense_matmul",
    )(a, w)


def _rms_rows(x, g):
    return x * lax.rsqrt(jnp.mean(x * x, axis=-1, keepdims=True) + NORM_EPS) * g


def _residual_update(m, h_ref, gpost_ref, gnext_ref, ho_ref, uo_ref):
    h_new = h_ref[...] + _rms_rows(m, gpost_ref[...])
    ho_ref[...] = h_new
    if uo_ref is not None:
        uo_ref[...] = _rms_rows(h_new, gnext_ref[...]).astype(BF16)


def _proj_res_body(*refs, nk, has_next):
    if has_next:
        a_ref, w_ref, h_ref, gpost_ref, gnext_ref, ho_ref, uo_ref = refs
    else:
        a_ref, w_ref, h_ref, gpost_ref, ho_ref = refs
        gnext_ref = uo_ref = None
    part = jnp.dot(a_ref[...], w_ref[...], preferred_element_type=F32)
    if nk == 1:
        _residual_update(part, h_ref, gpost_ref, gnext_ref, ho_ref, uo_ref)
        return
    k = pl.program_id(1)

    @pl.when(k == 0)
    def _():
        ho_ref[...] = part

    @pl.when(jnp.logical_and(k > 0, k < nk - 1))
    def _():
        ho_ref[...] += part

    @pl.when(k == nk - 1)
    def _():
        _residual_update(ho_ref[...] + part, h_ref, gpost_ref, gnext_ref, ho_ref, uo_ref)


def proj_residual(a, w, h, g_post, g_next):
    m, kdim = a.shape
    n = w.shape[1]
    tk = _pick(kdim, (2048, 1024, 512, 256, 128) if kdim <= 2048 else (1024, 512, 256, 128))
    nk = kdim // tk
    has_next = g_next is not None
    if nk > 1:
        tm = _pick(m, (1024, 512, 256, 128, 64, 32, 16, 8))
        row = pl.BlockSpec((tm, n), lambda i, k: (i, 0), pipeline_mode=pl.Buffered(1))
    else:
        tm = _pick(m, (512, 256, 128, 64, 32, 16, 8))
        row = pl.BlockSpec((tm, n), lambda i, k: (i, 0))
    vec = pl.BlockSpec((1, n), lambda i, k: (0, 0))
    in_specs = [pl.BlockSpec((tm, tk), lambda i, k: (i, k)), pl.BlockSpec((tk, n), lambda i, k: (k, 0)), row, vec]
    args = [a.astype(BF16), w.astype(BF16), h, g_post.reshape(1, n).astype(F32)]
    out_shape = [jax.ShapeDtypeStruct((m, n), F32)]
    out_specs = [row]
    if has_next:
        in_specs.append(vec)
        args.append(g_next.reshape(1, n).astype(F32))
        out_shape.append(jax.ShapeDtypeStruct((m, n), BF16))
        out_specs.append(row)
    out = pl.pallas_call(
        functools.partial(_proj_res_body, nk=nk, has_next=has_next),
        grid=(m // tm, nk),
        in_specs=in_specs, out_specs=out_specs, out_shape=out_shape,
        compiler_params=_params(("parallel", "arbitrary")),
        name="proj_residual",
    )(*args)
    return (out[0], out[1]) if has_next else (out[0], None)


def _ple_body(*refs, has_next):
    if has_next:
        p_ref, wp_ref, wg_ref, h_ref, gpost_ref, gnext_ref, ho_ref, uo_ref = refs
    else:
        p_ref, wp_ref, wg_ref, h_ref, gpost_ref, ho_ref = refs
        gnext_ref = uo_ref = None
    e = jnp.dot(p_ref[...], wp_ref[...], preferred_element_type=F32)
    gate = jax.nn.sigmoid(jnp.dot(h_ref[...].astype(BF16), wg_ref[...], preferred_element_type=F32))
    _residual_update(e * gate, h_ref, gpost_ref, gnext_ref, ho_ref, uo_ref)


def ple_residual(p, w_proj, w_gate, h, g_post, g_next):
    m, n = h.shape
    pd = p.shape[1]
    tm = _pick(m, (256, 128, 64, 32, 16, 8))
    has_next = g_next is not None
    row = pl.BlockSpec((tm, n), lambda i: (i, 0))
    vec = pl.BlockSpec((1, n), lambda i: (0, 0))
    in_specs = [pl.BlockSpec((tm, pd), lambda i: (i, 0)), pl.BlockSpec((pd, n), lambda i: (0, 0)),
                pl.BlockSpec((n, n), lambda i: (0, 0)), row, vec]
    args = [p.astype(BF16), w_proj.astype(BF16), w_gate.astype(BF16), h, g_post.reshape(1, n).astype(F32)]
    out_shape = [jax.ShapeDtypeStruct((m, n), F32)]
    out_specs = [row]
    if has_next:
        in_specs.append(vec)
        args.append(g_next.reshape(1, n).astype(F32))
        out_shape.append(jax.ShapeDtypeStruct((m, n), BF16))
        out_specs.append(row)
    out = pl.pallas_call(
        functools.partial(_ple_body, has_next=has_next),
        grid=(m // tm,),
        in_specs=in_specs, out_specs=out_specs, out_shape=out_shape,
        compiler_params=_params(("parallel",)),
        name="ple_residual",
    )(*args)
    return (out[0], out[1]) if has_next else (out[0], None)


ONES_ROWS = 16


def _with_ones_rows(vt):
    pad = jnp.zeros(vt.shape[:-2] + (ONES_ROWS, vt.shape[-1]), vt.dtype).at[..., 0, :].set(1)
    return jnp.concatenate([vt, pad], axis=-2)


def _score_stage(score_fns, m_ref, p_ref, alpha_ref):
    n = range(len(score_fns))
    ss = [fn() for fn in score_fns]
    m_old = [m_ref[i] for i in n]
    m_new = [jnp.maximum(m_old[i], jnp.max(ss[i], axis=0, keepdims=True).astype(F32)) for i in n]
    for i in n:
        alpha_ref[i] = jnp.exp2(m_old[i] - m_new[i])
        m_ref[i] = m_new[i]
    for i in n:
        p_ref[i] = jnp.exp2(ss[i] - m_new[i].astype(BF16))


def _value_stage(vt, acc_ref, p_ref, alpha_ref):
    n = range(p_ref.shape[0])
    pv = [jnp.dot(vt, p_ref[i], preferred_element_type=F32) for i in n]
    for i in n:
        acc_ref[i] = alpha_ref[i] * acc_ref[i] + pv[i]


def _diff_attn_body(lam_ref, k_ref, q_ref, cmask_ref, vt_ref, g_ref, o_ref, m_ref, acc_ref, qz_ref,
                    pa_ref, aa_ref, pb_ref, ab_ref, *, tq, nsp):
    i = pl.program_id(2)
    tl = tq // nsp
    m_ref[...] = jnp.full(m_ref.shape, NEG, F32)
    acc_ref[...] = jnp.zeros(acc_ref.shape, F32)
    for c in range(2):
        qz_ref[c] = q_ref[...] * cmask_ref[c]

    buf_a = (pa_ref, aa_ref)
    buf_b = (pb_ref, ab_ref)

    def scores(j, buf, masked=False):
        kj = k_ref[pl.ds(pl.multiple_of(j * tq, tq), tq), :]

        def score(c, sp):
            s = lax.dot_general(kj, qz_ref[c, sp * tl:(sp + 1) * tl, :], (_NT, ((), ())),
                                preferred_element_type=F32)
            if masked:
                row = lax.broadcasted_iota(I32, (tq, tl), 0)
                col = lax.broadcasted_iota(I32, (tq, tl), 1)
                s = jnp.where(row <= col + sp * tl, s, NEG)
            return s.astype(BF16)

        _score_stage([functools.partial(score, c, sp) for c in range(2) for sp in range(nsp)], m_ref, *buf)

    def values(j, buf):
        _value_stage(vt_ref[j], acc_ref, *buf)

    @pl.when(i == 0)
    def _():
        scores(0, buf_a, masked=True)
        values(0, buf_a)

    @pl.when(i > 0)
    def _():
        scores(0, buf_a)

        def pair(t, carry):
            c = 2 * t + 1
            scores(c, buf_b)
            values(c - 1, buf_a)
            scores(c + 1, buf_a)
            values(c, buf_b)
            return carry

        lax.fori_loop(0, (i - 1) // 2, pair, 0)

        @pl.when((i - 1) % 2 == 1)
        def _():
            scores(i - 1, buf_b)
            values(i - 2, buf_a)
            scores(i, buf_a, masked=True)
            values(i - 1, buf_b)
            values(i, buf_a)

        @pl.when((i - 1) % 2 == 0)
        def _():
            scores(i, buf_b, masked=True)
            values(i - 1, buf_a)
            values(i, buf_b)

    lam = lam_ref[0]
    for sp in range(nsp):
        a0 = acc_ref[sp]
        a1 = acc_ref[nsp + sp]
        out = (a0[:A_DV] / a0[A_DV:A_DV + 1] - lam * (a1[:A_DV] / a1[A_DV:A_DV + 1])).T
        y = out * lax.rsqrt(jnp.mean(out * out, axis=-1, keepdims=True) + DIFF_SUBLN_EPS) * g_ref[...]
        o_ref[sp * tl:(sp + 1) * tl, :] = y.astype(o_ref.dtype)


def diff_attention(q, k, v, lam, gain):
    b, t, width = q.shape
    h = width // A_DV
    tq = _pick(t, (512, 256, 128))
    nq = t // tq
    qs = (q * (A_DC ** -0.5 * LOG2E)).astype(BF16)
    vt = jnp.transpose(v.astype(BF16).reshape(b, nq, tq, h, A_DV), (0, 3, 1, 4, 2))
    vt = _with_ones_rows(vt)
    dve = A_DV + ONES_ROWS
    nsp = 2 if tq >= 256 else 1
    lane = jnp.arange(A_DV)
    cmask = jnp.stack([lane < A_DC, lane >= A_DC]).astype(BF16).reshape(2, 1, A_DV)
    head = pl.BlockSpec((None, tq, A_DV), lambda bi, hi, qi: (bi, qi, hi))
    return pl.pallas_call(
        functools.partial(_diff_attn_body, tq=tq, nsp=nsp),
        grid=(b, h, nq),
        in_specs=[pl.BlockSpec(memory_space=pltpu.SMEM),
                  pl.BlockSpec((None, t, A_DV), lambda bi, hi, qi: (bi, 0, hi)),
                  head,
                  pl.BlockSpec((2, 1, A_DV), lambda bi, hi, qi: (0, 0, 0)),
                  pl.BlockSpec((None, None, nq, dve, tq), lambda bi, hi, qi: (bi, hi, 0, 0, 0)),
                  pl.BlockSpec((1, A_DV), lambda bi, hi, qi: (0, 0))],
        out_specs=head,
        out_shape=jax.ShapeDtypeStruct((b, t, width), BF16),
        scratch_shapes=[pltpu.VMEM((2 * nsp, 1, tq // nsp), F32),
                        pltpu.VMEM((2 * nsp, dve, tq // nsp), F32),
                        pltpu.VMEM((2, tq, A_DV), BF16)]
        + [pltpu.VMEM((2 * nsp, tq, tq // nsp), BF16), pltpu.VMEM((2 * nsp, 1, tq // nsp), F32)] * 2,
        compiler_params=_params(("parallel", "parallel", "arbitrary")),
        name="diff_attention",
    )(lam.reshape(1).astype(F32), k.astype(BF16), qs, cmask, vt, gain.reshape(1, A_DV).astype(F32))


def _split_dot(x, w):
    hi = x.astype(BF16)
    lo = (x - hi.astype(F32)).astype(BF16)
    return jnp.dot(hi, w, preferred_element_type=F32) + jnp.dot(lo, w, preferred_element_type=F32)


def _rwkv_prep_body(zr_ref, zk_ref, zv_ref, zl_ref, pr_ref, pk_ref, pv_ref, pl_ref,
                    mur_ref, muk_ref, muv_ref, mul_ref, w0_ref, a0_ref, kk_ref, ka_ref,
                    w2_ref, a2_ref, g2_ref, hsum_ref,
                    r_ref, lw_ref, k_ref, v_ref, an_ref, bn_ref, g_ref, *, tm, seq):
    first = (pl.program_id(0) * tm) % seq == 0

    def shifted(z_ref, p_ref, mu_ref):
        z = z_ref[...]
        last = jnp.where(first, 0.0, p_ref[7:8, :])
        row = lax.broadcasted_iota(I32, z.shape, 0)
        prev = jnp.where(row == 0, last, pltpu.roll(z, 1, 0))
        return z + (prev - z) * mu_ref[...]

    r = shifted(zr_ref, pr_ref, mur_ref)
    k = shifted(zk_ref, pk_ref, muk_ref)
    v = shifted(zv_ref, pv_ref, muv_ref)
    lo = shifted(zl_ref, pl_ref, mul_ref)
    lw = jnp.dot(jnp.tanh(lo).astype(BF16), w2_ref[...], preferred_element_type=F32)
    la = jnp.dot(lo.astype(BF16), a2_ref[...], preferred_element_type=F32)
    g = jnp.dot(jax.nn.sigmoid(lo).astype(BF16), g2_ref[...], preferred_element_type=F32)
    w = -jax.nn.softplus(-(w0_ref[...] + lw)) - 0.5
    a = jax.nn.sigmoid(a0_ref[...] + la)
    kk = k * kk_ref[...]
    norm = jnp.sqrt(_split_dot(kk * kk, hsum_ref[...]))
    kk = kk / jnp.maximum(norm, 1e-12)
    r_ref[...] = r
    lw_ref[...] = -jnp.exp(w)
    k_ref[...] = k * (1.0 + (a - 1.0) * ka_ref[...])
    v_ref[...] = v
    an_ref[...] = -kk
    bn_ref[...] = kk * a
    g_ref[...] = g


def rwkv_prep(z, col0, width, n_lora, seq, mu, w0, w2, a0, a2, g2, k_k, k_a):
    m = z.shape[0]
    tm = _pick(seq, (256, 128, 64, 32, 16, 8))
    lw_ = _round_up(n_lora, 128)
    assert col0 % width == 0 and (col0 + 3 * width) % lw_ == 0 and col0 + 3 * width + lw_ <= z.shape[1]
    cb = col0 // width
    lb = (col0 + 3 * width) // lw_
    dl, al = w2.shape[0], a2.shape[0]

    def rows_of(w, start):
        return jnp.zeros((lw_, width), BF16).at[start:start + w.shape[0]].set(w.astype(BF16))

    hsum = jnp.kron(jnp.eye(width // B_HEAD, dtype=BF16), jnp.ones((B_HEAD, B_HEAD), BF16))
    vec = lambda x: x.reshape(1, -1).astype(F32)
    mul = jnp.pad(mu[3 * width:], (0, lw_ - n_lora))
    tile = lambda c, wd: pl.BlockSpec((tm, wd), lambda i: (i, c))
    before = lambda c, wd: pl.BlockSpec((8, wd), lambda i: (jnp.maximum(i * (tm // 8) - 1, 0), c))
    par = lambda wd: pl.BlockSpec((1, wd), lambda i: (0, 0))
    mat = lambda r_, c_: pl.BlockSpec((r_, c_), lambda i: (0, 0))
    out = pl.BlockSpec((tm, width), lambda i: (i, 0))
    return pl.pallas_call(
        functools.partial(_rwkv_prep_body, tm=tm, seq=seq),
        grid=(m // tm,),
        in_specs=[tile(cb, width), tile(cb + 1, width), tile(cb + 2, width), tile(lb, lw_),
                  before(cb, width), before(cb + 1, width), before(cb + 2, width), before(lb, lw_),
                  par(width), par(width), par(width), par(lw_), par(width), par(width), par(width), par(width),
                  mat(lw_, width), mat(lw_, width), mat(lw_, width), mat(width, width)],
        out_specs=[out] * 7,
        out_shape=[jax.ShapeDtypeStruct((m, width), F32)] * 7,
        compiler_params=_params(("parallel",)),
        name="rwkv_prep",
    )(z, z, z, z, z, z, z, z,
      vec(mu[:width]), vec(mu[width:2 * width]), vec(mu[2 * width:3 * width]), vec(mul),
      vec(w0), vec(a0), vec(k_k), vec(k_a),
      rows_of(w2, 0), rows_of(a2, dl), rows_of(g2, dl + al), hsum)


def _dot(a, b, dims, exact):
    if exact:
        return lax.dot_general(a, b, (dims, ((), ())), precision=lax.Precision.HIGHEST,
                               preferred_element_type=F32)
    return lax.dot_general(a.astype(BF16), b.astype(BF16), (dims, ((), ())), preferred_element_type=F32)


_NN = ((1,), (0,))
_NT = ((1,), (1,))
_TN = ((0,), (0,))


def _wkv_body(r_ref, lw_ref, k_ref, v_ref, a_ref, b_ref, g_ref, rk_ref, lng_ref, lnb_ref, y_ref, s_ref,
              *, nh, n, c, exact):
    @pl.when(pl.program_id(1) == 0)
    def _():
        s_ref[...] = jnp.zeros(s_ref.shape, F32)

    row = lax.broadcasted_iota(I32, (c, c), 0)
    col = lax.broadcasted_iota(I32, (c, c), 1)
    incl = row >= col
    strict = row > col
    tri = jnp.where(incl, 1.0, 0.0).astype(F32)
    eye = jnp.where(row == col, 1.0, 0.0).astype(F32)
    nsq = int(math.log2(c)) - 1

    hs = range(nh)
    dot = functools.partial(_dot, exact=exact)
    heads = lambda ref: [ref[:, h * n:(h + 1) * n] for h in hs]
    lw, r, k, v, a, b = (heads(ref) for ref in (lw_ref, r_ref, k_ref, v_ref, a_ref, b_ref))
    cum = [_dot(tri, lw[h], _NN, True) for h in hs]
    tot = [cum[h][c - 1:c, :] for h in hs]
    pinv = [jnp.exp(-cum[h]) for h in hs]
    pend = [jnp.exp(tot[h] - cum[h]) for h in hs]
    at = [a[h] * jnp.exp(cum[h] - lw[h]) for h in hs]
    rt = [r[h] * jnp.exp(cum[h]) for h in hs]
    bt = [b[h] * pinv[h] for h in hs]
    kt = [k[h] * pinv[h] for h in hs]
    a_ab = [jnp.where(strict, dot(at[h], bt[h], _NT), 0.0) for h in hs]
    a_ak = [jnp.where(strict, dot(at[h], kt[h], _NT), 0.0) for h in hs]
    a_rb = [jnp.where(incl, dot(rt[h], bt[h], _NT), 0.0) for h in hs]
    a_rk = [jnp.where(incl, dot(rt[h], kt[h], _NT), 0.0) for h in hs]
    x = a_ab
    minv = [eye + x[h] for h in hs]
    for _ in range(nsq):
        x = [dot(x[h], x[h], _NN) for h in hs]
        minv = [minv[h] + dot(minv[h], x[h], _NN) for h in hs]
    s0 = [s_ref[h] for h in hs]
    rhs = [dot(at[h], s0[h], _NT) + dot(a_ak[h], v[h], _NN) for h in hs]
    u = [dot(minv[h], rhs[h], _NN) for h in hs]
    y = [dot(rt[h], s0[h], _NT) + dot(a_rb[h], u[h], _NN) + dot(a_rk[h], v[h], _NN) for h in hs]
    for h in hs:
        s_ref[h] = (s0[h] * jnp.exp(tot[h]) + dot(u[h], b[h] * pend[h], _TN)
                    + dot(v[h], k[h] * pend[h], _TN))
    for h in hs:
        cols = slice(h * n, (h + 1) * n)
        d = y[h] - jnp.mean(y[h], axis=-1, keepdims=True)
        yn = d * lax.rsqrt(jnp.mean(d * d, axis=-1, keepdims=True) + LNX_EPS)
        bonus = jnp.sum(r[h] * k[h] * rk_ref[:, cols], axis=-1, keepdims=True) * v[h]
        y_ref[:, cols] = (yn * lng_ref[:, cols] + lnb_ref[:, cols] + bonus) * g_ref[:, cols]


def wkv7(r, lw, k, v, a, b, g, r_k, lnx_g, lnx_b, n=B_HEAD, exact=False):
    bsz, t, width = r.shape
    nh = width // n
    c = min(WKV_CHUNK, t)
    spec = pl.BlockSpec((None, c, width), lambda bi, ci: (bi, ci, 0))
    par = pl.BlockSpec((1, width), lambda bi, ci: (0, 0))
    vec = lambda x: x.reshape(1, width).astype(F32)
    return pl.pallas_call(
        functools.partial(_wkv_body, nh=nh, n=n, c=c, exact=exact),
        grid=(bsz, t // c),
        in_specs=[spec] * 7 + [par] * 3,
        out_specs=spec,
        out_shape=jax.ShapeDtypeStruct((bsz, t, width), F32),
        scratch_shapes=[pltpu.VMEM((nh, n, n), F32)],
        compiler_params=_params(("parallel", "arbitrary")),
        name="wkv7_chunked",
    )(r, lw, k, v, a, b, g, vec(r_k), vec(lnx_g), vec(lnx_b))


def _dsa_index_body(ki_ref, qit_ref, wit_ref, bias_ref, key_ref, cut_ref, *, tq, tkc, topk, nheads, t):
    i = pl.program_id(1)
    nch = ((i + 1) * tq + tkc - 1) // tkc
    qpos = i * tq + lax.broadcasted_iota(I32, (1, tq), 1)

    def kpos_of(ci):
        return ci * tkc + lax.broadcasted_iota(I32, (tkc, 1), 0)

    def rows(ci):
        return pl.ds(pl.multiple_of(ci * tkc, tkc), tkc)

    def score_chunk(ci, carry):
        kc = ki_ref[rows(ci), :]
        acc = jnp.zeros((tkc, tq), F32)
        for h in range(nheads):
            s = jnp.dot(kc, qit_ref[h], preferred_element_type=F32)
            acc = acc + jnp.maximum(s, 0.0) * wit_ref[h]
        acc = jnp.where(kpos_of(ci) <= qpos, acc, -jnp.inf)
        bits = lax.bitcast_convert_type(acc, I32)
        key_ref[rows(ci), :] = jnp.where(bits >= 0, bits, bits ^ 0x7FFFFFFF)
        return carry

    lax.fori_loop(0, nch, score_chunk, 0)

    def count(pred):
        def body(ci, cnt):
            ones = pred(key_ref[rows(ci), :], kpos_of(ci))
            return cnt + jnp.sum(ones, axis=0, keepdims=True)
        return lax.fori_loop(0, nch, body, jnp.zeros((1, tq), I32))

    def value_bit(step, prefix):
        cand = prefix | lax.shift_left(jnp.int32(1), 31 - step)
        cand_signed = cand ^ INT_MIN
        cnt = count(lambda key, kpos: jnp.where(key >= cand_signed, 1, 0))
        return jnp.where(cnt >= topk, cand, prefix)

    thr = lax.fori_loop(0, 32, value_bit, jnp.zeros((1, tq), I32)) ^ INT_MIN
    n_ge = count(lambda key, kpos: jnp.where(key >= thr, 1, 0))
    n_gt = count(lambda key, kpos: jnp.where(key > thr, 1, 0))
    need = topk - n_gt
    cut_ref[...] = jnp.full((1, tq), t, I32)

    @pl.when(jnp.max(n_ge) > topk)
    def _():
        nbits = max(1, (t - 1).bit_length())

        def index_bit(step, x):
            cand = x | lax.shift_left(jnp.int32(1), nbits - 1 - step)
            g = count(lambda key, kpos: jnp.where(key == thr, jnp.where(kpos < cand, 1, 0), 0))
            return jnp.where(g < need, cand, x)

        cut_ref[...] = lax.fori_loop(0, nbits, index_bit, jnp.zeros((1, tq), I32))

    cutoff = cut_ref[...]

    def write_chunk(ci, carry):
        key = key_ref[rows(ci), :]
        kpos = kpos_of(ci)
        tie = jnp.where(kpos <= cutoff, 0.0, NEG)
        sel = jnp.where(key > thr, 0.0, jnp.where(key == thr, tie, NEG))
        bias_ref[rows(ci), :] = jnp.where(kpos <= qpos, sel, NEG).astype(BF16)
        return carry

    lax.fori_loop(0, nch, write_chunk, 0)

    def fill_chunk(ci, carry):
        bias_ref[rows(ci), :] = jnp.full((tkc, tq), NEG, BF16)
        return carry

    lax.fori_loop(nch, t // tkc, fill_chunk, 0)


def dsa_select(ki, qi, wi, topk):
    b, t, nh, d = qi.shape
    tq = _pick(t, (256, 128))
    tkc = _pick(t, (512, 256, 128))
    nq = t // tq
    qit = jnp.transpose(qi.astype(BF16).reshape(b, nq, tq, nh, d), (0, 1, 3, 4, 2))
    wit = jnp.transpose(wi.astype(F32).reshape(b, nq, tq, nh), (0, 1, 3, 2))[:, :, :, None, :]
    return pl.pallas_call(
        functools.partial(_dsa_index_body, tq=tq, tkc=tkc, topk=topk, nheads=nh, t=t),
        grid=(b, nq),
        in_specs=[pl.BlockSpec((None, t, d), lambda bi, qi_: (bi, 0, 0)),
                  pl.BlockSpec((None, None, nh, d, tq), lambda bi, qi_: (bi, qi_, 0, 0, 0)),
                  pl.BlockSpec((None, None, nh, 1, tq), lambda bi, qi_: (bi, qi_, 0, 0, 0))],
        out_specs=pl.BlockSpec((None, t, tq), lambda bi, qi_: (bi, 0, qi_)),
        out_shape=jax.ShapeDtypeStruct((b, t, t), BF16),
        scratch_shapes=[pltpu.VMEM((t, tq), I32), pltpu.VMEM((1, tq), I32)],
        compiler_params=_params(("parallel", "arbitrary")),
        name="dsa_index_topk",
    )(ki.astype(BF16), qit, wit)


def _dsa_attn_body(k_ref, vt_ref, q_ref, bias_ref, o_ref, m_ref, acc_ref, pa_ref, aa_ref, pb_ref, ab_ref,
                   *, tq, tk, rep, d):
    i = pl.program_id(2)
    nch = ((i + 1) * tq + tk - 1) // tk
    m_ref[...] = jnp.full(m_ref.shape, NEG, F32)
    acc_ref[...] = jnp.zeros(acc_ref.shape, F32)
    buf_a = (pa_ref, aa_ref)
    buf_b = (pb_ref, ab_ref)

    def scores(c, buf):
        rows = pl.ds(pl.multiple_of(c * tk, tk), tk)
        kj = k_ref[rows, :]
        bj = bias_ref[rows, :]
        fns = [lambda r=r: lax.dot_general(kj, q_ref[:, r * d:(r + 1) * d], (_NT, ((), ())),
                                           preferred_element_type=F32).astype(BF16) + bj
               for r in range(rep)]
        _score_stage(fns, m_ref, *buf)

    def values(c, buf):
        _value_stage(vt_ref[c], acc_ref, *buf)

    scores(0, buf_a)

    def pair(t, carry):
        c = 2 * t + 1
        scores(c, buf_b)
        values(c - 1, buf_a)
        scores(c + 1, buf_a)
        values(c, buf_b)
        return carry

    lax.fori_loop(0, (nch - 1) // 2, pair, 0)
    last = nch - 1

    @pl.when(last % 2 == 1)
    def _():
        scores(last, buf_b)
        values(last - 1, buf_a)
        values(last, buf_b)

    @pl.when(last % 2 == 0)
    def _():
        values(last, buf_a)

    for r in range(rep):
        acc = acc_ref[r]
        o_ref[:, r * d:(r + 1) * d] = (acc[:d] / acc[d:d + 1]).T.astype(o_ref.dtype)


def dsa_attention(q, k, v, bias):
    b, t, qw = q.shape
    d = C_HEAD_DIM
    g = k.shape[2] // d
    rep = qw // (g * d)
    tq = _pick(t, (256, 128))
    tk = _pick(t, (512, 256, 128))
    nq, nk = t // tq, t // tk
    qs = (q * (d ** -0.5 * LOG2E)).astype(BF16)
    vt = jnp.transpose(v.astype(BF16).reshape(b, nk, tk, g, d), (0, 3, 1, 4, 2))
    vt = _with_ones_rows(vt)
    dve = d + ONES_ROWS
    group = pl.BlockSpec((None, tq, rep * d), lambda bi, gi, qi: (bi, qi, gi))
    return pl.pallas_call(
        functools.partial(_dsa_attn_body, tq=tq, tk=tk, rep=rep, d=d),
        grid=(b, g, nq),
        in_specs=[pl.BlockSpec((None, t, d), lambda bi, gi, qi: (bi, 0, gi)),
                  pl.BlockSpec((None, None, nk, dve, tk), lambda bi, gi, qi: (bi, gi, 0, 0, 0)),
                  group,
                  pl.BlockSpec((None, t, tq), lambda bi, gi, qi: (bi, 0, qi))],
        out_specs=group,
        out_shape=jax.ShapeDtypeStruct((b, t, qw), BF16),
        scratch_shapes=[pltpu.VMEM((rep, 1, tq), F32), pltpu.VMEM((rep, dve, tq), F32)]
        + [pltpu.VMEM((rep, tk, tq), BF16), pltpu.VMEM((rep, 1, tq), F32)] * 2,
        compiler_params=_params(("parallel", "parallel", "arbitrary")),
        name="dsa_attention",
    )(k.astype(BF16), vt, qs, bias)


def _rms_norm(x, g, eps=NORM_EPS):
    xf = x.astype(F32)
    return xf * lax.rsqrt(jnp.mean(xf * xf, axis=-1, keepdims=True) + eps) * g.astype(F32)


def _layer_norm(x, g, b, eps=LN_EPS):
    mu = jnp.mean(x, axis=-1, keepdims=True)
    var = jnp.mean(jnp.square(x - mu), axis=-1, keepdims=True)
    return (x - mu) * lax.rsqrt(var + eps) * g + b


def _partial_rope(x, pos):
    dh = x.shape[-1]
    rot = dh // ROPE_FRACTION
    half = rot // 2
    inv_freq = ROPE_THETA ** (-(jnp.arange(half, dtype=F32) * 2.0 / rot))
    ang = pos.astype(F32)[..., None] * inv_freq
    cos = jnp.cos(ang)[:, :, None, :]
    sin = jnp.sin(ang)[:, :, None, :]
    x1 = x[..., :half]
    x2 = x[..., half:rot]
    return jnp.concatenate([x1 * cos - x2 * sin, x2 * cos + x1 * sin, x[..., rot:]], axis=-1)


def _pad_cols(w, n):
    return jnp.pad(w, ((0, 0), (0, n - w.shape[1])))


def _round_up(n, m):
    return (n + m - 1) // m * m


def _mm3(x, w, **kw):
    b, t, _ = x.shape
    return matmul(x.reshape(b * t, x.shape[-1]), w, **kw).reshape(b, t, w.shape[1])


def _mixer_ab(u, pos, w_in, lq1, lk1, lq2, lk2, subln_g, mu, w0, w2, a0, a2, g2,
              k_k, k_a, r_k, lnx_g, lnx_b, lambda_init):
    b, t, d = u.shape
    a_width = d // 2
    b_width = d // 2
    a_heads = a_width // A_DV
    a_in = 3 * a_width
    n_in = w_in.shape[1]
    z = matmul(u.reshape(b * t, d), _pad_cols(w_in, _round_up(n_in, 512)))

    qa, ka, va = (z[:, c * a_width:(c + 1) * a_width].reshape(b, t, a_width) for c in range(3))
    rope = lambda x: _partial_rope(x.reshape(b, t, 2 * a_heads, A_DC), pos).reshape(b, t, a_width)
    lam = jnp.exp(jnp.sum(lq1 * lk1)) - jnp.exp(jnp.sum(lq2 * lk2)) + lambda_init
    ya = diff_attention(rope(qa), rope(ka), va, lam, subln_g * (1.0 - lambda_init))

    prep = rwkv_prep(z, a_in, b_width, n_in - a_in - 3 * b_width, t, mu, w0, w2, a0, a2, g2, k_k, k_a)
    yb = wkv7(*(x.reshape(b, t, b_width) for x in prep), r_k.reshape(-1), lnx_g, lnx_b)
    return jnp.concatenate([ya, yb.astype(BF16)], axis=-1)


def _mixer_c(u, pos, w_in, idx_k_g, idx_k_b, topk):
    b, t, d = u.shape
    c_heads = d // C_HEAD_DIM
    n_in = w_in.shape[1]
    c1 = c_heads * C_HEAD_DIM
    c2 = c1 + C_KV_HEADS * C_HEAD_DIM
    c3 = c2 + C_KV_HEADS * C_HEAD_DIM
    c4 = c3 + IDX_HEADS * IDX_DIM
    c5 = c4 + IDX_DIM
    z = _mm3(u, _pad_cols(w_in, _round_up(n_in, 512)))[..., :n_in]
    q, k, v, qi, ki, wi = jnp.split(z, [c1, c2, c3, c4, c5], axis=-1)
    q = _partial_rope(q.reshape(b, t, c_heads, C_HEAD_DIM), pos).reshape(b, t, c1)
    k = _partial_rope(k.reshape(b, t, C_KV_HEADS, C_HEAD_DIM), pos).reshape(b, t, c2 - c1)
    qi = _partial_rope(qi.reshape(b, t, IDX_HEADS, IDX_DIM), pos)
    ki = _partial_rope(_layer_norm(ki, idx_k_g, idx_k_b)[:, :, None, :], pos)[:, :, 0]
    wi = wi * ((IDX_HEADS * IDX_DIM) ** -0.5)
    bias = dsa_select(ki, qi, wi, topk)
    return dsa_attention(q, k, v, bias)


def kernel(x, p, positions, mix_pre_g, mix_post_g, mlp_pre_g, mlp_post_g, w_mlp_up, w_mlp_down, w_ple_proj, w_ple_gate, ple_post_g, ab_w_in, ab_w_out, diff_lq1, diff_lk1, diff_lq2, diff_lk2, diff_subln_g, rwkv_mu, rwkv_w0, rwkv_w2, rwkv_a0, rwkv_a2, rwkv_g2, rwkv_k_k, rwkv_k_a, rwkv_r_k, rwkv_lnx_g, rwkv_lnx_b, c_w_in, c_w_out, idx_k_g, idx_k_b):
    depth = mix_pre_g.shape[0]
    b, t, d = x.shape
    topk = min(TOPK_MAX, t // 4)
    flat = lambda z: z.reshape(b * t, z.shape[-1])
    h = flat(x)
    u = _rms_norm(h, mix_pre_g[0]).astype(BF16)
    for i in range(depth):
        j = i // 2
        u3 = u.reshape(b, t, d)
        if i % 2 == 0:
            lambda_init = 0.8 - 0.6 * math.exp(-0.3 * i)
            m = _mixer_ab(u3, positions, ab_w_in[j], diff_lq1[j], diff_lk1[j], diff_lq2[j],
                          diff_lk2[j], diff_subln_g[j], rwkv_mu[j], rwkv_w0[j], rwkv_w2[j], rwkv_a0[j],
                          rwkv_a2[j], rwkv_g2[j], rwkv_k_k[j], rwkv_k_a[j], rwkv_r_k[j], rwkv_lnx_g[j],
                          rwkv_lnx_b[j], lambda_init)
            w_out = ab_w_out[j]
        else:
            m = _mixer_c(u3, positions, c_w_in[j], idx_k_g[j], idx_k_b[j], topk)
            w_out = c_w_out[j]
        h, u = proj_residual(flat(m), w_out, h, mix_post_g[i], mlp_pre_g[i])
        up = matmul(u, w_mlp_up[i], out_dtype=BF16, act="relu2")
        h, _ = proj_residual(up, w_mlp_down[i], h, mlp_post_g[i], None)
        g_next = mix_pre_g[i + 1] if i + 1 < depth else None
        h, u = ple_residual(flat(p[i]), w_ple_proj[i], w_ple_gate[i], h, ple_post_g[i], g_next)
    return h.reshape(b, t, d)
```

```python
import functools
import math

import jax
import jax.numpy as jnp
from jax import lax
from jax.experimental import pallas as pl
from jax.experimental.pallas import tpu as pltpu

F32 = jnp.float32
BF16 = jnp.bfloat16
I32 = jnp.int32

V7X_VMEM_LIMIT_BYTES = 56 * 1024 * 1024
NEG = -1e30
LOG2E = 1.4426950408889634
INT_MIN = -2147483648

NORM_EPS = 1e-6
ROPE_THETA = 500000.0
ROPE_FRACTION = 4
A_DV = 128
A_DC = 64
DIFF_SUBLN_EPS = 1e-5
B_HEAD = 64
LNX_EPS = 64e-5
C_HEAD_DIM = 128
C_KV_HEADS = 4
IDX_HEADS = 16
IDX_DIM = 64
TOPK_MAX = 256
LN_EPS = 1e-6
WKV_CHUNK = 64


def _params(sem):
    return pltpu.CompilerParams(dimension_semantics=sem, vmem_limit_bytes=V7X_VMEM_LIMIT_BYTES)


def _pick(n, prefs):
    for t in prefs:
        if n % t == 0:
            return t
    return n


def _mm_body(a_ref, w_ref, o_ref, *scratch, nk, act):
    def epilogue(acc):
        if act == "relu2":
            r = jnp.maximum(acc, 0.0)
            acc = r * r
        return acc.astype(o_ref.dtype)

    if nk == 1:
        o_ref[...] = epilogue(jnp.dot(a_ref[...], w_ref[...], preferred_element_type=F32))
        return
    (acc_ref,) = scratch
    k = pl.program_id(2)

    @pl.when(k == 0)
    def _():
        acc_ref[...] = jnp.zeros_like(acc_ref)

    acc_ref[...] += jnp.dot(a_ref[...], w_ref[...], preferred_element_type=F32)

    @pl.when(k == nk - 1)
    def _():
        o_ref[...] = epilogue(acc_ref[...])


def matmul(a, w, out_dtype=F32, act=None):
    a = a.astype(BF16)
    w = w.astype(BF16)
    m, kdim = a.shape
    n = w.shape[1]
    tm = _pick(m, (1024, 512, 256, 128, 64, 32, 16, 8))
    tn = _pick(n, (1024, 512, 256, 128))
    tk = kdim if kdim <= 2048 else _pick(kdim, (2048, 1024, 512))
    nk = kdim // tk
    scratch = [pltpu.VMEM((tm, tn), F32)] if nk > 1 else []
    return pl.pallas_call(
        functools.partial(_mm_body, nk=nk, act=act),
        grid=(m // tm, n // tn, nk),
        in_specs=[pl.BlockSpec((tm, tk), lambda i, j, k: (i, k)),
                  pl.BlockSpec((tk, tn), lambda i, j, k: (k, j))],
        out_specs=pl.BlockSpec((tm, tn), lambda i, j, k: (i, j)),
        out_shape=jax.ShapeDtypeStruct((m, n), out_dtype),
        scratch_shapes=scratch,
        compiler_params=_params(("parallel", "parallel", "arbitrary")),
        name="dense_matmul",
    )(a, w)


def _rms_rows(x, g):
    return x * lax.rsqrt(jnp.mean(x * x, axis=-1, keepdims=True) + NORM_EPS) * g


def _residual_update(m, h_ref, gpost_ref, gnext_ref, ho_ref, uo_ref):
    h_new = h_ref[...] + _rms_rows(m, gpost_ref[...])
    ho_ref[...] = h_new
    if uo_ref is not None:
        uo_ref[...] = _rms_rows(h_new, gnext_ref[...]).astype(BF16)


def _proj_res_body(*refs, nk, has_next):
    if has_next:
        a_ref, w_ref, h_ref, gpost_ref, gnext_ref, ho_ref, uo_ref = refs
    else:
        a_ref, w_ref, h_ref, gpost_ref, ho_ref = refs
        gnext_ref = uo_ref = None
    part = jnp.dot(a_ref[...], w_ref[...], preferred_element_type=F32)
    if nk == 1:
        _residual_update(part, h_ref, gpost_ref, gnext_ref, ho_ref, uo_ref)
        return
    k = pl.program_id(1)

    @pl.when(k == 0)
    def _():
        ho_ref[...] = part

    @pl.when(jnp.logical_and(k > 0, k < nk - 1))
    def _():
        ho_ref[...] += part

    @pl.when(k == nk - 1)
    def _():
        _residual_update(ho_ref[...] + part, h_ref, gpost_ref, gnext_ref, ho_ref, uo_ref)


def proj_residual(a, w, h, g_post, g_next):
    m, kdim = a.shape
    n = w.shape[1]
    tm = _pick(m, (512, 256, 128, 64, 32, 16, 8))
    tk = _pick(kdim, (2048, 1024, 512, 256, 128))
    nk = kdim // tk
    has_next = g_next is not None
    row = pl.BlockSpec((tm, n), lambda i, k: (i, 0))
    vec = pl.BlockSpec((1, n), lambda i, k: (0, 0))
    in_specs = [pl.BlockSpec((tm, tk), lambda i, k: (i, k)), pl.BlockSpec((tk, n), lambda i, k: (k, 0)), row, vec]
    args = [a.astype(BF16), w.astype(BF16), h, g_post.reshape(1, n).astype(F32)]
    out_shape = [jax.ShapeDtypeStruct((m, n), F32)]
    out_specs = [row]
    if has_next:
        in_specs.append(vec)
        args.append(g_next.reshape(1, n).astype(F32))
        out_shape.append(jax.ShapeDtypeStruct((m, n), BF16))
        out_specs.append(row)
    out = pl.pallas_call(
        functools.partial(_proj_res_body, nk=nk, has_next=has_next),
        grid=(m // tm, nk),
        in_specs=in_specs, out_specs=out_specs, out_shape=out_shape,
        compiler_params=_params(("parallel", "arbitrary")),
        name="proj_residual",
    )(*args)
    return (out[0], out[1]) if has_next else (out[0], None)


def _ple_body(*refs, has_next):
    if has_next:
        p_ref, wp_ref, wg_ref, h_ref, gpost_ref, gnext_ref, ho_ref, uo_ref = refs
    else:
        p_ref, wp_ref, wg_ref, h_ref, gpost_ref, ho_ref = refs
        gnext_ref = uo_ref = None
    e = jnp.dot(p_ref[...], wp_ref[...], preferred_element_type=F32)
    gate = jax.nn.sigmoid(jnp.dot(h_ref[...].astype(BF16), wg_ref[...], preferred_element_type=F32))
    _residual_update(e * gate, h_ref, gpost_ref, gnext_ref, ho_ref, uo_ref)


def ple_residual(p, w_proj, w_gate, h, g_post, g_next):
    m, n = h.shape
    pd = p.shape[1]
    tm = _pick(m, (256, 128, 64, 32, 16, 8))
    has_next = g_next is not None
    row = pl.BlockSpec((tm, n), lambda i: (i, 0))
    vec = pl.BlockSpec((1, n), lambda i: (0, 0))
    in_specs = [pl.BlockSpec((tm, pd), lambda i: (i, 0)), pl.BlockSpec((pd, n), lambda i: (0, 0)),
                pl.BlockSpec((n, n), lambda i: (0, 0)), row, vec]
    args = [p.astype(BF16), w_proj.astype(BF16), w_gate.astype(BF16), h, g_post.reshape(1, n).astype(F32)]
    out_shape = [jax.ShapeDtypeStruct((m, n), F32)]
    out_specs = [row]
    if has_next:
        in_specs.append(vec)
        args.append(g_next.reshape(1, n).astype(F32))
        out_shape.append(jax.ShapeDtypeStruct((m, n), BF16))
        out_specs.append(row)
    out = pl.pallas_call(
        functools.partial(_ple_body, has_next=has_next),
        grid=(m // tm,),
        in_specs=in_specs, out_specs=out_specs, out_shape=out_shape,
        compiler_params=_params(("parallel",)),
        name="ple_residual",
    )(*args)
    return (out[0], out[1]) if has_next else (out[0], None)


ONES_ROWS = 16


def _with_ones_rows(vt):
    pad = jnp.zeros(vt.shape[:-2] + (ONES_ROWS, vt.shape[-1]), vt.dtype).at[..., 0, :].set(1)
    return jnp.concatenate([vt, pad], axis=-2)


def _score_stage(score_fns, m_ref, p_ref, alpha_ref):
    n = range(len(score_fns))
    ss = [fn() for fn in score_fns]
    m_old = [m_ref[i] for i in n]
    m_new = [jnp.maximum(m_old[i], jnp.max(ss[i], axis=0, keepdims=True).astype(F32)) for i in n]
    for i in n:
        alpha_ref[i] = jnp.exp2(m_old[i] - m_new[i])
        m_ref[i] = m_new[i]
    for i in n:
        p_ref[i] = jnp.exp2(ss[i] - m_new[i].astype(BF16))


def _value_stage(vt, acc_ref, p_ref, alpha_ref):
    n = range(p_ref.shape[0])
    pv = [jnp.dot(vt, p_ref[i], preferred_element_type=F32) for i in n]
    for i in n:
        acc_ref[i] = alpha_ref[i] * acc_ref[i] + pv[i]


def _diff_attn_body(lam_ref, k_ref, q_ref, cmask_ref, vt_ref, g_ref, o_ref, m_ref, acc_ref, qz_ref,
                    pa_ref, aa_ref, pb_ref, ab_ref, *, tq, nsp):
    i = pl.program_id(2)
    tl = tq // nsp
    m_ref[...] = jnp.full(m_ref.shape, NEG, F32)
    acc_ref[...] = jnp.zeros(acc_ref.shape, F32)
    for c in range(2):
        qz_ref[c] = q_ref[...] * cmask_ref[c]

    buf_a = (pa_ref, aa_ref)
    buf_b = (pb_ref, ab_ref)

    def scores(j, buf, masked=False):
        kj = k_ref[pl.ds(pl.multiple_of(j * tq, tq), tq), :]

        def score(c, sp):
            s = lax.dot_general(kj, qz_ref[c, sp * tl:(sp + 1) * tl, :], (_NT, ((), ())),
                                preferred_element_type=F32)
            if masked:
                row = lax.broadcasted_iota(I32, (tq, tl), 0)
                col = lax.broadcasted_iota(I32, (tq, tl), 1)
                s = jnp.where(row <= col + sp * tl, s, NEG)
            return s.astype(BF16)

        _score_stage([functools.partial(score, c, sp) for c in range(2) for sp in range(nsp)], m_ref, *buf)

    def values(j, buf):
        _value_stage(vt_ref[j], acc_ref, *buf)

    @pl.when(i == 0)
    def _():
        scores(0, buf_a, masked=True)
        values(0, buf_a)

    @pl.when(i > 0)
    def _():
        scores(0, buf_a)

        def pair(t, carry):
            c = 2 * t + 1
            scores(c, buf_b)
            values(c - 1, buf_a)
            scores(c + 1, buf_a)
            values(c, buf_b)
            return carry

        lax.fori_loop(0, (i - 1) // 2, pair, 0)

        @pl.when((i - 1) % 2 == 1)
        def _():
            scores(i - 1, buf_b)
            values(i - 2, buf_a)
            scores(i, buf_a, masked=True)
            values(i - 1, buf_b)
            values(i, buf_a)

        @pl.when((i - 1) % 2 == 0)
        def _():
            scores(i, buf_b, masked=True)
            values(i - 1, buf_a)
            values(i, buf_b)

    lam = lam_ref[0]
    for sp in range(nsp):
        a0 = acc_ref[sp]
        a1 = acc_ref[nsp + sp]
        out = (a0[:A_DV] / a0[A_DV:A_DV + 1] - lam * (a1[:A_DV] / a1[A_DV:A_DV + 1])).T
        y = out * lax.rsqrt(jnp.mean(out * out, axis=-1, keepdims=True) + DIFF_SUBLN_EPS) * g_ref[...]
        o_ref[sp * tl:(sp + 1) * tl, :] = y.astype(o_ref.dtype)


def diff_attention(q, k, v, lam, gain):
    b, t, width = q.shape
    h = width // A_DV
    tq = _pick(t, (512, 256, 128))
    nq = t // tq
    qs = (q * (A_DC ** -0.5 * LOG2E)).astype(BF16)
    vt = jnp.transpose(v.astype(BF16).reshape(b, nq, tq, h, A_DV), (0, 3, 1, 4, 2))
    vt = _with_ones_rows(vt)
    dve = A_DV + ONES_ROWS
    nsp = 2 if tq >= 256 else 1
    lane = jnp.arange(A_DV)
    cmask = jnp.stack([lane < A_DC, lane >= A_DC]).astype(BF16).reshape(2, 1, A_DV)
    head = pl.BlockSpec((None, tq, A_DV), lambda bi, hi, qi: (bi, qi, hi))
    return pl.pallas_call(
        functools.partial(_diff_attn_body, tq=tq, nsp=nsp),
        grid=(b, h, nq),
        in_specs=[pl.BlockSpec(memory_space=pltpu.SMEM),
                  pl.BlockSpec((None, t, A_DV), lambda bi, hi, qi: (bi, 0, hi)),
                  head,
                  pl.BlockSpec((2, 1, A_DV), lambda bi, hi, qi: (0, 0, 0)),
                  pl.BlockSpec((None, None, nq, dve, tq), lambda bi, hi, qi: (bi, hi, 0, 0, 0)),
                  pl.BlockSpec((1, A_DV), lambda bi, hi, qi: (0, 0))],
        out_specs=head,
        out_shape=jax.ShapeDtypeStruct((b, t, width), BF16),
        scratch_shapes=[pltpu.VMEM((2 * nsp, 1, tq // nsp), F32),
                        pltpu.VMEM((2 * nsp, dve, tq // nsp), F32),
                        pltpu.VMEM((2, tq, A_DV), BF16)]
        + [pltpu.VMEM((2 * nsp, tq, tq // nsp), BF16), pltpu.VMEM((2 * nsp, 1, tq // nsp), F32)] * 2,
        compiler_params=_params(("parallel", "parallel", "arbitrary")),
        name="diff_attention",
    )(lam.reshape(1).astype(F32), k.astype(BF16), qs, cmask, vt, gain.reshape(1, A_DV).astype(F32))


def _split_dot(x, w):
    hi = x.astype(BF16)
    lo = (x - hi.astype(F32)).astype(BF16)
    return jnp.dot(hi, w, preferred_element_type=F32) + jnp.dot(lo, w, preferred_element_type=F32)


def _rwkv_prep_body(zr_ref, zk_ref, zv_ref, zl_ref, pr_ref, pk_ref, pv_ref, pl_ref,
                    mur_ref, muk_ref, muv_ref, mul_ref, w0_ref, a0_ref, kk_ref, ka_ref,
                    w2_ref, a2_ref, g2_ref, hsum_ref,
                    r_ref, lw_ref, k_ref, v_ref, an_ref, bn_ref, g_ref, *, tm, seq):
    first = (pl.program_id(0) * tm) % seq == 0

    def shifted(z_ref, p_ref, mu_ref):
        z = z_ref[...]
        last = jnp.where(first, 0.0, p_ref[7:8, :])
        row = lax.broadcasted_iota(I32, z.shape, 0)
        prev = jnp.where(row == 0, last, pltpu.roll(z, 1, 0))
        return z + (prev - z) * mu_ref[...]

    r = shifted(zr_ref, pr_ref, mur_ref)
    k = shifted(zk_ref, pk_ref, muk_ref)
    v = shifted(zv_ref, pv_ref, muv_ref)
    lo = shifted(zl_ref, pl_ref, mul_ref)
    lw = jnp.dot(jnp.tanh(lo).astype(BF16), w2_ref[...], preferred_element_type=F32)
    la = jnp.dot(lo.astype(BF16), a2_ref[...], preferred_element_type=F32)
    g = jnp.dot(jax.nn.sigmoid(lo).astype(BF16), g2_ref[...], preferred_element_type=F32)
    w = -jax.nn.softplus(-(w0_ref[...] + lw)) - 0.5
    a = jax.nn.sigmoid(a0_ref[...] + la)
    kk = k * kk_ref[...]
    norm = jnp.sqrt(_split_dot(kk * kk, hsum_ref[...]))
    kk = kk / jnp.maximum(norm, 1e-12)
    r_ref[...] = r
    lw_ref[...] = -jnp.exp(w)
    k_ref[...] = k * (1.0 + (a - 1.0) * ka_ref[...])
    v_ref[...] = v
    an_ref[...] = -kk
    bn_ref[...] = kk * a
    g_ref[...] = g


def rwkv_prep(z, col0, width, n_lora, seq, mu, w0, w2, a0, a2, g2, k_k, k_a):
    m = z.shape[0]
    tm = _pick(seq, (256, 128, 64, 32, 16, 8))
    lw_ = _round_up(n_lora, 128)
    assert col0 % width == 0 and (col0 + 3 * width) % lw_ == 0 and col0 + 3 * width + lw_ <= z.shape[1]
    cb = col0 // width
    lb = (col0 + 3 * width) // lw_
    dl, al = w2.shape[0], a2.shape[0]

    def rows_of(w, start):
        return jnp.zeros((lw_, width), BF16).at[start:start + w.shape[0]].set(w.astype(BF16))

    hsum = jnp.kron(jnp.eye(width // B_HEAD, dtype=BF16), jnp.ones((B_HEAD, B_HEAD), BF16))
    vec = lambda x: x.reshape(1, -1).astype(F32)
    mul = jnp.pad(mu[3 * width:], (0, lw_ - n_lora))
    tile = lambda c, wd: pl.BlockSpec((tm, wd), lambda i: (i, c))
    before = lambda c, wd: pl.BlockSpec((8, wd), lambda i: (jnp.maximum(i * (tm // 8) - 1, 0), c))
    par = lambda wd: pl.BlockSpec((1, wd), lambda i: (0, 0))
    mat = lambda r_, c_: pl.BlockSpec((r_, c_), lambda i: (0, 0))
    out = pl.BlockSpec((tm, width), lambda i: (i, 0))
    return pl.pallas_call(
        functools.partial(_rwkv_prep_body, tm=tm, seq=seq),
        grid=(m // tm,),
        in_specs=[tile(cb, width), tile(cb + 1, width), tile(cb + 2, width), tile(lb, lw_),
                  before(cb, width), before(cb + 1, width), before(cb + 2, width), before(lb, lw_),
                  par(width), par(width), par(width), par(lw_), par(width), par(width), par(width), par(width),
                  mat(lw_, width), mat(lw_, width), mat(lw_, width), mat(width, width)],
        out_specs=[out] * 7,
        out_shape=[jax.ShapeDtypeStruct((m, width), F32)] * 7,
        compiler_params=_params(("parallel",)),
        name="rwkv_prep",
    )(z, z, z, z, z, z, z, z,
      vec(mu[:width]), vec(mu[width:2 * width]), vec(mu[2 * width:3 * width]), vec(mul),
      vec(w0), vec(a0), vec(k_k), vec(k_a),
      rows_of(w2, 0), rows_of(a2, dl), rows_of(g2, dl + al), hsum)


def _dot(a, b, dims, exact):
    if exact:
        return lax.dot_general(a, b, (dims, ((), ())), precision=lax.Precision.HIGHEST,
                               preferred_element_type=F32)
    return lax.dot_general(a.astype(BF16), b.astype(BF16), (dims, ((), ())), preferred_element_type=F32)


_NN = ((1,), (0,))
_NT = ((1,), (1,))
_TN = ((0,), (0,))


def _wkv_body(r_ref, lw_ref, k_ref, v_ref, a_ref, b_ref, g_ref, rk_ref, lng_ref, lnb_ref, y_ref, s_ref,
              *, nh, n, c, exact):
    @pl.when(pl.program_id(1) == 0)
    def _():
        s_ref[...] = jnp.zeros(s_ref.shape, F32)

    row = lax.broadcasted_iota(I32, (c, c), 0)
    col = lax.broadcasted_iota(I32, (c, c), 1)
    incl = row >= col
    strict = row > col
    tri = jnp.where(incl, 1.0, 0.0).astype(F32)
    eye = jnp.where(row == col, 1.0, 0.0).astype(F32)
    nsq = int(math.log2(c)) - 1

    hs = range(nh)
    dot = functools.partial(_dot, exact=exact)
    heads = lambda ref: [ref[:, h * n:(h + 1) * n] for h in hs]
    lw, r, k, v, a, b = (heads(ref) for ref in (lw_ref, r_ref, k_ref, v_ref, a_ref, b_ref))
    cum = [_dot(tri, lw[h], _NN, True) for h in hs]
    tot = [cum[h][c - 1:c, :] for h in hs]
    pinv = [jnp.exp(-cum[h]) for h in hs]
    pend = [jnp.exp(tot[h] - cum[h]) for h in hs]
    at = [a[h] * jnp.exp(cum[h] - lw[h]) for h in hs]
    rt = [r[h] * jnp.exp(cum[h]) for h in hs]
    bt = [b[h] * pinv[h] for h in hs]
    kt = [k[h] * pinv[h] for h in hs]
    a_ab = [jnp.where(strict, dot(at[h], bt[h], _NT), 0.0) for h in hs]
    a_ak = [jnp.where(strict, dot(at[h], kt[h], _NT), 0.0) for h in hs]
    a_rb = [jnp.where(incl, dot(rt[h], bt[h], _NT), 0.0) for h in hs]
    a_rk = [jnp.where(incl, dot(rt[h], kt[h], _NT), 0.0) for h in hs]
    x = a_ab
    minv = [eye + x[h] for h in hs]
    for _ in range(nsq):
        x = [dot(x[h], x[h], _NN) for h in hs]
        minv = [minv[h] + dot(minv[h], x[h], _NN) for h in hs]
    s0 = [s_ref[h] for h in hs]
    rhs = [dot(at[h], s0[h], _NT) + dot(a_ak[h], v[h], _NN) for h in hs]
    u = [dot(minv[h], rhs[h], _NN) for h in hs]
    y = [dot(rt[h], s0[h], _NT) + dot(a_rb[h], u[h], _NN) + dot(a_rk[h], v[h], _NN) for h in hs]
    for h in hs:
        s_ref[h] = (s0[h] * jnp.exp(tot[h]) + dot(u[h], b[h] * pend[h], _TN)
                    + dot(v[h], k[h] * pend[h], _TN))
    for h in hs:
        cols = slice(h * n, (h + 1) * n)
        d = y[h] - jnp.mean(y[h], axis=-1, keepdims=True)
        yn = d * lax.rsqrt(jnp.mean(d * d, axis=-1, keepdims=True) + LNX_EPS)
        bonus = jnp.sum(r[h] * k[h] * rk_ref[:, cols], axis=-1, keepdims=True) * v[h]
        y_ref[:, cols] = (yn * lng_ref[:, cols] + lnb_ref[:, cols] + bonus) * g_ref[:, cols]


def wkv7(r, lw, k, v, a, b, g, r_k, lnx_g, lnx_b, n=B_HEAD, exact=False):
    bsz, t, width = r.shape
    nh = width // n
    c = min(WKV_CHUNK, t)
    spec = pl.BlockSpec((None, c, width), lambda bi, ci: (bi, ci, 0))
    par = pl.BlockSpec((1, width), lambda bi, ci: (0, 0))
    vec = lambda x: x.reshape(1, width).astype(F32)
    return pl.pallas_call(
        functools.partial(_wkv_body, nh=nh, n=n, c=c, exact=exact),
        grid=(bsz, t // c),
        in_specs=[spec] * 7 + [par] * 3,
        out_specs=spec,
        out_shape=jax.ShapeDtypeStruct((bsz, t, width), F32),
        scratch_shapes=[pltpu.VMEM((nh, n, n), F32)],
        compiler_params=_params(("parallel", "arbitrary")),
        name="wkv7_chunked",
    )(r, lw, k, v, a, b, g, vec(r_k), vec(lnx_g), vec(lnx_b))


def _dsa_index_body(ki_ref, qit_ref, wit_ref, bias_ref, key_ref, dig_ref, cut_ref, *, tq, tkc, topk, nheads, t):
    i = pl.program_id(1)
    nch = ((i + 1) * tq + tkc - 1) // tkc
    qpos = i * tq + lax.broadcasted_iota(I32, (1, tq), 1)

    def kpos_of(ci):
        return ci * tkc + lax.broadcasted_iota(I32, (tkc, 1), 0)

    def rows(ci):
        return pl.ds(pl.multiple_of(ci * tkc, tkc), tkc)

    def score_chunk(ci, carry):
        kc = ki_ref[rows(ci), :]
        acc = jnp.zeros((tkc, tq), F32)
        for h in range(nheads):
            s = jnp.dot(kc, qit_ref[h], preferred_element_type=F32)
            acc = acc + jnp.maximum(s, 0.0) * wit_ref[h]
        acc = jnp.where(kpos_of(ci) <= qpos, acc, -jnp.inf)
        bits = lax.bitcast_convert_type(acc, I32)
        key_ref[rows(ci), :] = jnp.where(bits >= 0, bits, bits ^ 0x7FFFFFFF)
        return carry

    lax.fori_loop(0, nch, score_chunk, 0)

    def count(pred):
        def body(ci, cnt):
            ones = pred(key_ref[rows(ci), :], kpos_of(ci))
            return cnt + jnp.sum(ones, axis=0, keepdims=True)
        return lax.fori_loop(0, nch, body, jnp.zeros((1, tq), I32))

    fold = 64
    assert (t // tkc) * (tkc // fold) <= 256

    def count_digit_ge(cand):
        cand_bf = cand.astype(F32).astype(BF16)

        def body(ci, acc):
            hit = jnp.where(dig_ref[rows(ci), :] >= cand_bf, jnp.ones((), BF16), jnp.zeros((), BF16))
            for s in range(tkc // fold):
                acc = acc + hit[s * fold:(s + 1) * fold]
            return acc

        acc = lax.fori_loop(0, nch, body, jnp.zeros((fold, tq), BF16))
        return jnp.sum(acc.astype(F32), axis=0, keepdims=True).astype(I32)

    prefix = jnp.zeros((1, tq), I32)
    above = jnp.zeros((1, tq), I32)
    n_ge = above
    for level in (3, 2, 1, 0):
        shift = 8 * level

        def build(ci, carry, shift=shift, level=level, prefix=prefix):
            u = key_ref[rows(ci), :] ^ INT_MIN
            digit = lax.shift_right_logical(u, jnp.int32(shift)) & 255
            if level < 3:
                same = lax.shift_right_logical(u, jnp.int32(shift + 8)) == lax.shift_right_logical(
                    prefix, jnp.int32(shift + 8))
                digit = jnp.where(same, digit, -1)
            dig_ref[rows(ci), :] = digit.astype(F32).astype(BF16)
            return carry

        lax.fori_loop(0, nch, build, 0)

        def digit_bit(step, d, above=above):
            cand = d | lax.shift_left(jnp.int32(1), 7 - step)
            return jnp.where(above + count_digit_ge(cand) >= topk, cand, d)

        d = lax.fori_loop(0, 8, digit_bit, jnp.zeros((1, tq), I32))
        if level == 0:
            n_ge = above + count_digit_ge(d)
        above = above + count_digit_ge(d + 1)
        prefix = prefix | lax.shift_left(d, jnp.int32(shift))

    thr = prefix ^ INT_MIN
    n_gt = above
    need = topk - n_gt
    cut_ref[...] = jnp.full((1, tq), t, I32)

    @pl.when(jnp.max(n_ge) > topk)
    def _():
        nbits = max(1, (t - 1).bit_length())

        def index_bit(step, x):
            cand = x | lax.shift_left(jnp.int32(1), nbits - 1 - step)
            g = count(lambda key, kpos: jnp.where(key == thr, jnp.where(kpos < cand, 1, 0), 0))
            return jnp.where(g < need, cand, x)

        cut_ref[...] = lax.fori_loop(0, nbits, index_bit, jnp.zeros((1, tq), I32))

    cutoff = cut_ref[...]

    def write_chunk(ci, carry):
        key = key_ref[rows(ci), :]
        kpos = kpos_of(ci)
        tie = jnp.where(kpos <= cutoff, 0.0, NEG)
        sel = jnp.where(key > thr, 0.0, jnp.where(key == thr, tie, NEG))
        bias_ref[rows(ci), :] = jnp.where(kpos <= qpos, sel, NEG).astype(BF16)
        return carry

    lax.fori_loop(0, nch, write_chunk, 0)

    def fill_chunk(ci, carry):
        bias_ref[rows(ci), :] = jnp.full((tkc, tq), NEG, BF16)
        return carry

    lax.fori_loop(nch, t // tkc, fill_chunk, 0)


def dsa_select(ki, qi, wi, topk):
    b, t, nh, d = qi.shape
    tq = _pick(t, (256, 128))
    tkc = _pick(t, (512, 256, 128))
    nq = t // tq
    qit = jnp.transpose(qi.astype(BF16).reshape(b, nq, tq, nh, d), (0, 1, 3, 4, 2))
    wit = jnp.transpose(wi.astype(F32).reshape(b, nq, tq, nh), (0, 1, 3, 2))[:, :, :, None, :]
    return pl.pallas_call(
        functools.partial(_dsa_index_body, tq=tq, tkc=tkc, topk=topk, nheads=nh, t=t),
        grid=(b, nq),
        in_specs=[pl.BlockSpec((None, t, d), lambda bi, qi_: (bi, 0, 0)),
                  pl.BlockSpec((None, None, nh, d, tq), lambda bi, qi_: (bi, qi_, 0, 0, 0)),
                  pl.BlockSpec((None, None, nh, 1, tq), lambda bi, qi_: (bi, qi_, 0, 0, 0))],
        out_specs=pl.BlockSpec((None, t, tq), lambda bi, qi_: (bi, 0, qi_)),
        out_shape=jax.ShapeDtypeStruct((b, t, t), BF16),
        scratch_shapes=[pltpu.VMEM((t, tq), I32), pltpu.VMEM((t, tq), BF16), pltpu.VMEM((1, tq), I32)],
        compiler_params=_params(("parallel", "arbitrary")),
        name="dsa_index_topk",
    )(ki.astype(BF16), qit, wit)


def _dsa_attn_body(k_ref, vt_ref, q_ref, bias_ref, o_ref, m_ref, acc_ref, pa_ref, aa_ref, pb_ref, ab_ref,
                   *, tq, tk, rep, d):
    i = pl.program_id(2)
    nch = ((i + 1) * tq + tk - 1) // tk
    m_ref[...] = jnp.full(m_ref.shape, NEG, F32)
    acc_ref[...] = jnp.zeros(acc_ref.shape, F32)
    buf_a = (pa_ref, aa_ref)
    buf_b = (pb_ref, ab_ref)

    def scores(c, buf):
        rows = pl.ds(pl.multiple_of(c * tk, tk), tk)
        kj = k_ref[rows, :]
        bj = bias_ref[rows, :]
        fns = [lambda r=r: lax.dot_general(kj, q_ref[:, r * d:(r + 1) * d], (_NT, ((), ())),
                                           preferred_element_type=F32).astype(BF16) + bj
               for r in range(rep)]
        _score_stage(fns, m_ref, *buf)

    def values(c, buf):
        _value_stage(vt_ref[c], acc_ref, *buf)

    scores(0, buf_a)

    def pair(t, carry):
        c = 2 * t + 1
        scores(c, buf_b)
        values(c - 1, buf_a)
        scores(c + 1, buf_a)
        values(c, buf_b)
        return carry

    lax.fori_loop(0, (nch - 1) // 2, pair, 0)
    last = nch - 1

    @pl.when(last % 2 == 1)
    def _():
        scores(last, buf_b)
        values(last - 1, buf_a)
        values(last, buf_b)

    @pl.when(last % 2 == 0)
    def _():
        values(last, buf_a)

    for r in range(rep):
        acc = acc_ref[r]
        o_ref[:, r * d:(r + 1) * d] = (acc[:d] / acc[d:d + 1]).T.astype(o_ref.dtype)


def dsa_attention(q, k, v, bias):
    b, t, qw = q.shape
    d = C_HEAD_DIM
    g = k.shape[2] // d
    rep = qw // (g * d)
    tq = _pick(t, (256, 128))
    tk = _pick(t, (512, 256, 128))
    nq, nk = t // tq, t // tk
    qs = (q * (d ** -0.5 * LOG2E)).astype(BF16)
    vt = jnp.transpose(v.astype(BF16).reshape(b, nk, tk, g, d), (0, 3, 1, 4, 2))
    vt = _with_ones_rows(vt)
    dve = d + ONES_ROWS
    group = pl.BlockSpec((None, tq, rep * d), lambda bi, gi, qi: (bi, qi, gi))
    return pl.pallas_call(
        functools.partial(_dsa_attn_body, tq=tq, tk=tk, rep=rep, d=d),
        grid=(b, g, nq),
        in_specs=[pl.BlockSpec((None, t, d), lambda bi, gi, qi: (bi, 0, gi)),
                  pl.BlockSpec((None, None, nk, dve, tk), lambda bi, gi, qi: (bi, gi, 0, 0, 0)),
                  group,
                  pl.BlockSpec((None, t, tq), lambda bi, gi, qi: (bi, 0, qi))],
        out_specs=group,
        out_shape=jax.ShapeDtypeStruct((b, t, qw), BF16),
        scratch_shapes=[pltpu.VMEM((rep, 1, tq), F32), pltpu.VMEM((rep, dve, tq), F32)]
        + [pltpu.VMEM((rep, tk, tq), BF16), pltpu.VMEM((rep, 1, tq), F32)] * 2,
        compiler_params=_params(("parallel", "parallel", "arbitrary")),
        name="dsa_attention",
    )(k.astype(BF16), vt, qs, bias)


def _rms_norm(x, g, eps=NORM_EPS):
    xf = x.astype(F32)
    return xf * lax.rsqrt(jnp.mean(xf * xf, axis=-1, keepdims=True) + eps) * g.astype(F32)


def _layer_norm(x, g, b, eps=LN_EPS):
    mu = jnp.mean(x, axis=-1, keepdims=True)
    var = jnp.mean(jnp.square(x - mu), axis=-1, keepdims=True)
    return (x - mu) * lax.rsqrt(var + eps) * g + b


def _partial_rope(x, pos):
    dh = x.shape[-1]
    rot = dh // ROPE_FRACTION
    half = rot // 2
    inv_freq = ROPE_THETA ** (-(jnp.arange(half, dtype=F32) * 2.0 / rot))
    ang = pos.astype(F32)[..., None] * inv_freq
    cos = jnp.cos(ang)[:, :, None, :]
    sin = jnp.sin(ang)[:, :, None, :]
    x1 = x[..., :half]
    x2 = x[..., half:rot]
    return jnp.concatenate([x1 * cos - x2 * sin, x2 * cos + x1 * sin, x[..., rot:]], axis=-1)


def _pad_cols(w, n):
    return jnp.pad(w, ((0, 0), (0, n - w.shape[1])))


def _round_up(n, m):
    return (n + m - 1) // m * m


def _mm3(x, w, **kw):
    b, t, _ = x.shape
    return matmul(x.reshape(b * t, x.shape[-1]), w, **kw).reshape(b, t, w.shape[1])


def _mixer_ab(u, pos, w_in, lq1, lk1, lq2, lk2, subln_g, mu, w0, w2, a0, a2, g2,
              k_k, k_a, r_k, lnx_g, lnx_b, lambda_init):
    b, t, d = u.shape
    a_width = d // 2
    b_width = d // 2
    a_heads = a_width // A_DV
    a_in = 3 * a_width
    n_in = w_in.shape[1]
    z = matmul(u.reshape(b * t, d), _pad_cols(w_in, _round_up(n_in, 512)))

    qa, ka, va = (z[:, c * a_width:(c + 1) * a_width].reshape(b, t, a_width) for c in range(3))
    rope = lambda x: _partial_rope(x.reshape(b, t, 2 * a_heads, A_DC), pos).reshape(b, t, a_width)
    lam = jnp.exp(jnp.sum(lq1 * lk1)) - jnp.exp(jnp.sum(lq2 * lk2)) + lambda_init
    ya = diff_attention(rope(qa), rope(ka), va, lam, subln_g * (1.0 - lambda_init))

    prep = rwkv_prep(z, a_in, b_width, n_in - a_in - 3 * b_width, t, mu, w0, w2, a0, a2, g2, k_k, k_a)
    yb = wkv7(*(x.reshape(b, t, b_width) for x in prep), r_k.reshape(-1), lnx_g, lnx_b)
    return jnp.concatenate([ya, yb.astype(BF16)], axis=-1)


def _mixer_c(u, pos, w_in, idx_k_g, idx_k_b, topk):
    b, t, d = u.shape
    c_heads = d // C_HEAD_DIM
    n_in = w_in.shape[1]
    c1 = c_heads * C_HEAD_DIM
    c2 = c1 + C_KV_HEADS * C_HEAD_DIM
    c3 = c2 + C_KV_HEADS * C_HEAD_DIM
    c4 = c3 + IDX_HEADS * IDX_DIM
    c5 = c4 + IDX_DIM
    z = _mm3(u, _pad_cols(w_in, _round_up(n_in, 512)))[..., :n_in]
    q, k, v, qi, ki, wi = jnp.split(z, [c1, c2, c3, c4, c5], axis=-1)
    q = _partial_rope(q.reshape(b, t, c_heads, C_HEAD_DIM), pos).reshape(b, t, c1)
    k = _partial_rope(k.reshape(b, t, C_KV_HEADS, C_HEAD_DIM), pos).reshape(b, t, c2 - c1)
    qi = _partial_rope(qi.reshape(b, t, IDX_HEADS, IDX_DIM), pos)
    ki = _partial_rope(_layer_norm(ki, idx_k_g, idx_k_b)[:, :, None, :], pos)[:, :, 0]
    wi = wi * ((IDX_HEADS * IDX_DIM) ** -0.5)
    bias = dsa_select(ki, qi, wi, topk)
    return dsa_attention(q, k, v, bias)


def kernel(x, p, positions, mix_pre_g, mix_post_g, mlp_pre_g, mlp_post_g, w_mlp_up, w_mlp_down, w_ple_proj, w_ple_gate, ple_post_g, ab_w_in, ab_w_out, diff_lq1, diff_lk1, diff_lq2, diff_lk2, diff_subln_g, rwkv_mu, rwkv_w0, rwkv_w2, rwkv_a0, rwkv_a2, rwkv_g2, rwkv_k_k, rwkv_k_a, rwkv_r_k, rwkv_lnx_g, rwkv_lnx_b, c_w_in, c_w_out, idx_k_g, idx_k_b):
    depth = mix_pre_g.shape[0]
    b, t, d = x.shape
    topk = min(TOPK_MAX, t // 4)
    flat = lambda z: z.reshape(b * t, z.shape[-1])
    h = flat(x)
    u = _rms_norm(h, mix_pre_g[0]).astype(BF16)
    for i in range(depth):
        j = i // 2
        u3 = u.reshape(b, t, d)
        if i % 2 == 0:
            lambda_init = 0.8 - 0.6 * math.exp(-0.3 * i)
            m = _mixer_ab(u3, positions, ab_w_in[j], diff_lq1[j], diff_lk1[j], diff_lq2[j],
                          diff_lk2[j], diff_subln_g[j], rwkv_mu[j], rwkv_w0[j], rwkv_w2[j], rwkv_a0[j],
                          rwkv_a2[j], rwkv_g2[j], rwkv_k_k[j], rwkv_k_a[j], rwkv_r_k[j], rwkv_lnx_g[j],
                          rwkv_lnx_b[j], lambda_init)
            w_out = ab_w_out[j]
        else:
            m = _mixer_c(u3, positions, c_w_in[j], idx_k_g[j], idx_k_b[j], topk)
            w_out = c_w_out[j]
        h, u = proj_residual(flat(m), w_out, h, mix_post_g[i], mlp_pre_g[i])
        up = matmul(u, w_mlp_up[i], out_dtype=BF16, act="relu2")
        h, _ = proj_residual(up, w_mlp_down[i], h, mlp_post_g[i], None)
        g_next = mix_pre_g[i + 1] if i + 1 < depth else None
        h, u = ple_residual(flat(p[i]), w_ple_proj[i], w_ple_gate[i], h, ple_post_g[i], g_next)
    return h.reshape(b, t, d)
```

```python
import functools
import math

import jax
import jax.numpy as jnp
from jax import lax
from jax.experimental import pallas as pl
from jax.experimental.pallas import tpu as pltpu

F32 = jnp.float32
BF16 = jnp.bfloat16
I32 = jnp.int32

V7X_VMEM_LIMIT_BYTES = 56 * 1024 * 1024
NEG = -1e30
LOG2E = 1.4426950408889634
INT_MIN = -2147483648

NORM_EPS = 1e-6
ROPE_THETA = 500000.0
ROPE_FRACTION = 4
A_DV = 128
A_DC = 64
DIFF_SUBLN_EPS = 1e-5
B_HEAD = 64
LNX_EPS = 64e-5
C_HEAD_DIM = 128
C_KV_HEADS = 4
IDX_HEADS = 16
IDX_DIM = 64
TOPK_MAX = 256
LN_EPS = 1e-6
WKV_CHUNK = 64


def _params(sem):
    return pltpu.CompilerParams(dimension_semantics=sem, vmem_limit_bytes=V7X_VMEM_LIMIT_BYTES)


def _pick(n, prefs):
    for t in prefs:
        if n % t == 0:
            return t
    return n


def _mm_body(a_ref, w_ref, o_ref, *scratch, nk, act):
    def epilogue(acc):
        if act == "relu2":
            r = jnp.maximum(acc, 0.0)
            acc = r * r
        return acc.astype(o_ref.dtype)

    if nk == 1:
        o_ref[...] = epilogue(jnp.dot(a_ref[...], w_ref[...], preferred_element_type=F32))
        return
    (acc_ref,) = scratch
    k = pl.program_id(2)

    @pl.when(k == 0)
    def _():
        acc_ref[...] = jnp.zeros_like(acc_ref)

    acc_ref[...] += jnp.dot(a_ref[...], w_ref[...], preferred_element_type=F32)

    @pl.when(k == nk - 1)
    def _():
        o_ref[...] = epilogue(acc_ref[...])


def matmul(a, w, out_dtype=F32, act=None):
    a = a.astype(BF16)
    w = w.astype(BF16)
    m, kdim = a.shape
    n = w.shape[1]
    tm = _pick(m, (1024, 512, 256, 128, 64, 32, 16, 8))
    tn = _pick(n, (1024, 512, 256, 128))
    tk = kdim if kdim <= 2048 else _pick(kdim, (2048, 1024, 512))
    nk = kdim // tk
    scratch = [pltpu.VMEM((tm, tn), F32)] if nk > 1 else []
    return pl.pallas_call(
        functools.partial(_mm_body, nk=nk, act=act),
        grid=(m // tm, n // tn, nk),
        in_specs=[pl.BlockSpec((tm, tk), lambda i, j, k: (i, k)),
                  pl.BlockSpec((tk, tn), lambda i, j, k: (k, j))],
        out_specs=pl.BlockSpec((tm, tn), lambda i, j, k: (i, j)),
        out_shape=jax.ShapeDtypeStruct((m, n), out_dtype),
        scratch_shapes=scratch,
        compiler_params=_params(("parallel", "parallel", "arbitrary")),
        name="dense_matmul",
    )(a, w)


def _rms_rows(x, g):
    return x * lax.rsqrt(jnp.mean(x * x, axis=-1, keepdims=True) + NORM_EPS) * g


def _residual_update(m, h_ref, gpost_ref, gnext_ref, ho_ref, uo_ref):
    h_new = h_ref[...] + _rms_rows(m, gpost_ref[...])
    ho_ref[...] = h_new
    if uo_ref is not None:
        uo_ref[...] = _rms_rows(h_new, gnext_ref[...]).astype(BF16)


def _proj_res_body(*refs, nk, has_next):
    if has_next:
        a_ref, w_ref, h_ref, gpost_ref, gnext_ref, ho_ref, uo_ref = refs
    else:
        a_ref, w_ref, h_ref, gpost_ref, ho_ref = refs
        gnext_ref = uo_ref = None
    part = jnp.dot(a_ref[...], w_ref[...], preferred_element_type=F32)
    if nk == 1:
        _residual_update(part, h_ref, gpost_ref, gnext_ref, ho_ref, uo_ref)
        return
    k = pl.program_id(1)

    @pl.when(k == 0)
    def _():
        ho_ref[...] = part

    @pl.when(jnp.logical_and(k > 0, k < nk - 1))
    def _():
        ho_ref[...] += part

    @pl.when(k == nk - 1)
    def _():
        _residual_update(ho_ref[...] + part, h_ref, gpost_ref, gnext_ref, ho_ref, uo_ref)


def proj_residual(a, w, h, g_post, g_next):
    m, kdim = a.shape
    n = w.shape[1]
    tm = _pick(m, (512, 256, 128, 64, 32, 16, 8))
    tk = _pick(kdim, (2048, 1024, 512, 256, 128))
    nk = kdim // tk
    has_next = g_next is not None
    row = pl.BlockSpec((tm, n), lambda i, k: (i, 0))
    vec = pl.BlockSpec((1, n), lambda i, k: (0, 0))
    in_specs = [pl.BlockSpec((tm, tk), lambda i, k: (i, k)), pl.BlockSpec((tk, n), lambda i, k: (k, 0)), row, vec]
    args = [a.astype(BF16), w.astype(BF16), h, g_post.reshape(1, n).astype(F32)]
    out_shape = [jax.ShapeDtypeStruct((m, n), F32)]
    out_specs = [row]
    if has_next:
        in_specs.append(vec)
        args.append(g_next.reshape(1, n).astype(F32))
        out_shape.append(jax.ShapeDtypeStruct((m, n), BF16))
        out_specs.append(row)
    out = pl.pallas_call(
        functools.partial(_proj_res_body, nk=nk, has_next=has_next),
        grid=(m // tm, nk),
        in_specs=in_specs, out_specs=out_specs, out_shape=out_shape,
        compiler_params=_params(("parallel", "arbitrary")),
        name="proj_residual",
    )(*args)
    return (out[0], out[1]) if has_next else (out[0], None)


def _ple_body(*refs, has_next):
    if has_next:
        p_ref, wp_ref, wg_ref, h_ref, gpost_ref, gnext_ref, ho_ref, uo_ref = refs
    else:
        p_ref, wp_ref, wg_ref, h_ref, gpost_ref, ho_ref = refs
        gnext_ref = uo_ref = None
    e = jnp.dot(p_ref[...], wp_ref[...], preferred_element_type=F32)
    gate = jax.nn.sigmoid(jnp.dot(h_ref[...].astype(BF16), wg_ref[...], preferred_element_type=F32))
    _residual_update(e * gate, h_ref, gpost_ref, gnext_ref, ho_ref, uo_ref)


def ple_residual(p, w_proj, w_gate, h, g_post, g_next):
    m, n = h.shape
    pd = p.shape[1]
    tm = _pick(m, (256, 128, 64, 32, 16, 8))
    has_next = g_next is not None
    row = pl.BlockSpec((tm, n), lambda i: (i, 0))
    vec = pl.BlockSpec((1, n), lambda i: (0, 0))
    in_specs = [pl.BlockSpec((tm, pd), lambda i: (i, 0)), pl.BlockSpec((pd, n), lambda i: (0, 0)),
                pl.BlockSpec((n, n), lambda i: (0, 0)), row, vec]
    args = [p.astype(BF16), w_proj.astype(BF16), w_gate.astype(BF16), h, g_post.reshape(1, n).astype(F32)]
    out_shape = [jax.ShapeDtypeStruct((m, n), F32)]
    out_specs = [row]
    if has_next:
        in_specs.append(vec)
        args.append(g_next.reshape(1, n).astype(F32))
        out_shape.append(jax.ShapeDtypeStruct((m, n), BF16))
        out_specs.append(row)
    out = pl.pallas_call(
        functools.partial(_ple_body, has_next=has_next),
        grid=(m // tm,),
        in_specs=in_specs, out_specs=out_specs, out_shape=out_shape,
        compiler_params=_params(("parallel",)),
        name="ple_residual",
    )(*args)
    return (out[0], out[1]) if has_next else (out[0], None)


ONES_ROWS = 16


def _with_ones_rows(vt):
    pad = jnp.zeros(vt.shape[:-2] + (ONES_ROWS, vt.shape[-1]), vt.dtype).at[..., 0, :].set(1)
    return jnp.concatenate([vt, pad], axis=-2)


def _score_stage(score_fns, m_ref, p_ref, alpha_ref):
    n = range(len(score_fns))
    ss = [fn() for fn in score_fns]
    m_old = [m_ref[i] for i in n]
    m_new = [jnp.maximum(m_old[i], jnp.max(ss[i], axis=0, keepdims=True).astype(F32)) for i in n]
    for i in n:
        alpha_ref[i] = jnp.exp2(m_old[i] - m_new[i])
        m_ref[i] = m_new[i]
    for i in n:
        p_ref[i] = jnp.exp2(ss[i] - m_new[i].astype(BF16))


def _value_stage(vt, acc_ref, p_ref, alpha_ref):
    n = range(p_ref.shape[0])
    pv = [jnp.dot(vt, p_ref[i], preferred_element_type=F32) for i in n]
    for i in n:
        acc_ref[i] = alpha_ref[i] * acc_ref[i] + pv[i]


def _diff_attn_body(lam_ref, k_ref, q_ref, cmask_ref, vt_ref, g_ref, o_ref, m_ref, acc_ref, qz_ref,
                    pa_ref, aa_ref, pb_ref, ab_ref, *, tq, nsp):
    i = pl.program_id(2)
    tl = tq // nsp
    m_ref[...] = jnp.full(m_ref.shape, NEG, F32)
    acc_ref[...] = jnp.zeros(acc_ref.shape, F32)
    for c in range(2):
        qz_ref[c] = q_ref[...] * cmask_ref[c]

    buf_a = (pa_ref, aa_ref)
    buf_b = (pb_ref, ab_ref)

    def scores(j, buf, masked=False):
        kj = k_ref[pl.ds(pl.multiple_of(j * tq, tq), tq), :]

        def score(c, sp):
            s = lax.dot_general(kj, qz_ref[c, sp * tl:(sp + 1) * tl, :], (_NT, ((), ())),
                                preferred_element_type=F32)
            if masked:
                row = lax.broadcasted_iota(I32, (tq, tl), 0)
                col = lax.broadcasted_iota(I32, (tq, tl), 1)
                s = jnp.where(row <= col + sp * tl, s, NEG)
            return s.astype(BF16)

        _score_stage([functools.partial(score, c, sp) for c in range(2) for sp in range(nsp)], m_ref, *buf)

    def values(j, buf):
        _value_stage(vt_ref[j], acc_ref, *buf)

    @pl.when(i == 0)
    def _():
        scores(0, buf_a, masked=True)
        values(0, buf_a)

    @pl.when(i > 0)
    def _():
        scores(0, buf_a)

        def pair(t, carry):
            c = 2 * t + 1
            scores(c, buf_b)
            values(c - 1, buf_a)
            scores(c + 1, buf_a)
            values(c, buf_b)
            return carry

        lax.fori_loop(0, (i - 1) // 2, pair, 0)

        @pl.when((i - 1) % 2 == 1)
        def _():
            scores(i - 1, buf_b)
            values(i - 2, buf_a)
            scores(i, buf_a, masked=True)
            values(i - 1, buf_b)
            values(i, buf_a)

        @pl.when((i - 1) % 2 == 0)
        def _():
            scores(i, buf_b, masked=True)
            values(i - 1, buf_a)
            values(i, buf_b)

    lam = lam_ref[0]
    for sp in range(nsp):
        a0 = acc_ref[sp]
        a1 = acc_ref[nsp + sp]
        out = (a0[:A_DV] / a0[A_DV:A_DV + 1] - lam * (a1[:A_DV] / a1[A_DV:A_DV + 1])).T
        y = out * lax.rsqrt(jnp.mean(out * out, axis=-1, keepdims=True) + DIFF_SUBLN_EPS) * g_ref[...]
        o_ref[sp * tl:(sp + 1) * tl, :] = y.astype(o_ref.dtype)


def diff_attention(q, k, v, lam, gain):
    b, t, width = q.shape
    h = width // A_DV
    tq = _pick(t, (512, 256, 128))
    nq = t // tq
    vt = jnp.transpose(v.astype(BF16).reshape(b, nq, tq, h, A_DV), (0, 3, 1, 4, 2))
    vt = _with_ones_rows(vt)
    dve = A_DV + ONES_ROWS
    nsp = 2 if tq >= 256 else 1
    lane = jnp.arange(A_DV)
    cmask = jnp.stack([lane < A_DC, lane >= A_DC]).astype(BF16).reshape(2, 1, A_DV)
    head = pl.BlockSpec((None, tq, A_DV), lambda bi, hi, qi: (bi, qi, hi))
    return pl.pallas_call(
        functools.partial(_diff_attn_body, tq=tq, nsp=nsp),
        grid=(b, h, nq),
        in_specs=[pl.BlockSpec(memory_space=pltpu.SMEM),
                  pl.BlockSpec((None, t, A_DV), lambda bi, hi, qi: (bi, 0, hi)),
                  head,
                  pl.BlockSpec((2, 1, A_DV), lambda bi, hi, qi: (0, 0, 0)),
                  pl.BlockSpec((None, None, nq, dve, tq), lambda bi, hi, qi: (bi, hi, 0, 0, 0)),
                  pl.BlockSpec((1, A_DV), lambda bi, hi, qi: (0, 0))],
        out_specs=head,
        out_shape=jax.ShapeDtypeStruct((b, t, width), BF16),
        scratch_shapes=[pltpu.VMEM((2 * nsp, 1, tq // nsp), F32),
                        pltpu.VMEM((2 * nsp, dve, tq // nsp), F32),
                        pltpu.VMEM((2, tq, A_DV), BF16)]
        + [pltpu.VMEM((2 * nsp, tq, tq // nsp), BF16), pltpu.VMEM((2 * nsp, 1, tq // nsp), F32)] * 2,
        compiler_params=_params(("parallel", "parallel", "arbitrary")),
        name="diff_attention",
    )(lam.reshape(1).astype(F32), k.astype(BF16), q.astype(BF16), cmask, vt, gain.reshape(1, A_DV).astype(F32))


def _split_dot(x, w):
    hi = x.astype(BF16)
    lo = (x - hi.astype(F32)).astype(BF16)
    return jnp.dot(hi, w, preferred_element_type=F32) + jnp.dot(lo, w, preferred_element_type=F32)


def _rwkv_prep_body(zr_ref, zk_ref, zv_ref, zl_ref, pr_ref, pk_ref, pv_ref, pl_ref,
                    mur_ref, muk_ref, muv_ref, mul_ref, w0_ref, a0_ref, kk_ref, ka_ref,
                    w2_ref, a2_ref, g2_ref, hsum_ref,
                    r_ref, lw_ref, k_ref, v_ref, an_ref, bn_ref, g_ref, *, tm, seq):
    first = (pl.program_id(0) * tm) % seq == 0

    def shifted(z_ref, p_ref, mu_ref):
        z = z_ref[...]
        last = jnp.where(first, 0.0, p_ref[7:8, :])
        row = lax.broadcasted_iota(I32, z.shape, 0)
        prev = jnp.where(row == 0, last, pltpu.roll(z, 1, 0))
        return z + (prev - z) * mu_ref[...]

    r = shifted(zr_ref, pr_ref, mur_ref)
    k = shifted(zk_ref, pk_ref, muk_ref)
    v = shifted(zv_ref, pv_ref, muv_ref)
    lo = shifted(zl_ref, pl_ref, mul_ref)
    lw = jnp.dot(jnp.tanh(lo).astype(BF16), w2_ref[...], preferred_element_type=F32)
    la = jnp.dot(lo.astype(BF16), a2_ref[...], preferred_element_type=F32)
    g = jnp.dot(jax.nn.sigmoid(lo).astype(BF16), g2_ref[...], preferred_element_type=F32)
    w = -jax.nn.softplus(-(w0_ref[...] + lw)) - 0.5
    a = jax.nn.sigmoid(a0_ref[...] + la)
    kk = k * kk_ref[...]
    norm = jnp.sqrt(_split_dot(kk * kk, hsum_ref[...]))
    kk = kk / jnp.maximum(norm, 1e-12)
    r_ref[...] = r
    lw_ref[...] = -jnp.exp(w)
    k_ref[...] = k * (1.0 + (a - 1.0) * ka_ref[...])
    v_ref[...] = v
    an_ref[...] = -kk
    bn_ref[...] = kk * a
    g_ref[...] = g


def rwkv_prep(z, col0, width, n_lora, seq, mu, w0, w2, a0, a2, g2, k_k, k_a):
    m = z.shape[0]
    tm = _pick(seq, (256, 128, 64, 32, 16, 8))
    lw_ = _round_up(n_lora, 128)
    assert col0 % width == 0 and (col0 + 3 * width) % lw_ == 0 and col0 + 3 * width + lw_ <= z.shape[1]
    cb = col0 // width
    lb = (col0 + 3 * width) // lw_
    dl, al = w2.shape[0], a2.shape[0]

    def rows_of(w, start):
        return jnp.zeros((lw_, width), BF16).at[start:start + w.shape[0]].set(w.astype(BF16))

    hsum = jnp.kron(jnp.eye(width // B_HEAD, dtype=BF16), jnp.ones((B_HEAD, B_HEAD), BF16))
    vec = lambda x: x.reshape(1, -1).astype(F32)
    mul = jnp.pad(mu[3 * width:], (0, lw_ - n_lora))
    tile = lambda c, wd: pl.BlockSpec((tm, wd), lambda i: (i, c))
    before = lambda c, wd: pl.BlockSpec((8, wd), lambda i: (jnp.maximum(i * (tm // 8) - 1, 0), c))
    par = lambda wd: pl.BlockSpec((1, wd), lambda i: (0, 0))
    mat = lambda r_, c_: pl.BlockSpec((r_, c_), lambda i: (0, 0))
    out = pl.BlockSpec((tm, width), lambda i: (i, 0))
    return pl.pallas_call(
        functools.partial(_rwkv_prep_body, tm=tm, seq=seq),
        grid=(m // tm,),
        in_specs=[tile(cb, width), tile(cb + 1, width), tile(cb + 2, width), tile(lb, lw_),
                  before(cb, width), before(cb + 1, width), before(cb + 2, width), before(lb, lw_),
                  par(width), par(width), par(width), par(lw_), par(width), par(width), par(width), par(width),
                  mat(lw_, width), mat(lw_, width), mat(lw_, width), mat(width, width)],
        out_specs=[out] * 7,
        out_shape=[jax.ShapeDtypeStruct((m, width), F32)] * 7,
        compiler_params=_params(("parallel",)),
        name="rwkv_prep",
    )(z, z, z, z, z, z, z, z,
      vec(mu[:width]), vec(mu[width:2 * width]), vec(mu[2 * width:3 * width]), vec(mul),
      vec(w0), vec(a0), vec(k_k), vec(k_a),
      rows_of(w2, 0), rows_of(a2, dl), rows_of(g2, dl + al), hsum)


def _dot(a, b, dims, exact):
    if exact:
        return lax.dot_general(a, b, (dims, ((), ())), precision=lax.Precision.HIGHEST,
                               preferred_element_type=F32)
    return lax.dot_general(a.astype(BF16), b.astype(BF16), (dims, ((), ())), preferred_element_type=F32)


_NN = ((1,), (0,))
_NT = ((1,), (1,))
_TN = ((0,), (0,))


def _wkv_body(r_ref, lw_ref, k_ref, v_ref, a_ref, b_ref, g_ref, rk_ref, lng_ref, lnb_ref, y_ref, s_ref,
              *, nh, n, c, exact):
    @pl.when(pl.program_id(1) == 0)
    def _():
        s_ref[...] = jnp.zeros(s_ref.shape, F32)

    row = lax.broadcasted_iota(I32, (c, c), 0)
    col = lax.broadcasted_iota(I32, (c, c), 1)
    incl = row >= col
    strict = row > col
    tri = jnp.where(incl, 1.0, 0.0).astype(F32)
    eye = jnp.where(row == col, 1.0, 0.0).astype(F32)
    nsq = int(math.log2(c)) - 1

    hs = range(nh)
    dot = functools.partial(_dot, exact=exact)
    heads = lambda ref: [ref[:, h * n:(h + 1) * n] for h in hs]
    lw, r, k, v, a, b = (heads(ref) for ref in (lw_ref, r_ref, k_ref, v_ref, a_ref, b_ref))
    cum = [_dot(tri, lw[h], _NN, True) for h in hs]
    tot = [cum[h][c - 1:c, :] for h in hs]
    pinv = [jnp.exp(-cum[h]) for h in hs]
    pend = [jnp.exp(tot[h] - cum[h]) for h in hs]
    at = [a[h] * jnp.exp(cum[h] - lw[h]) for h in hs]
    rt = [r[h] * jnp.exp(cum[h]) for h in hs]
    bt = [b[h] * pinv[h] for h in hs]
    kt = [k[h] * pinv[h] for h in hs]
    a_ab = [jnp.where(strict, dot(at[h], bt[h], _NT), 0.0) for h in hs]
    a_ak = [jnp.where(strict, dot(at[h], kt[h], _NT), 0.0) for h in hs]
    a_rb = [jnp.where(incl, dot(rt[h], bt[h], _NT), 0.0) for h in hs]
    a_rk = [jnp.where(incl, dot(rt[h], kt[h], _NT), 0.0) for h in hs]
    x = a_ab
    minv = [eye + x[h] for h in hs]
    for _ in range(nsq):
        x = [dot(x[h], x[h], _NN) for h in hs]
        minv = [minv[h] + dot(minv[h], x[h], _NN) for h in hs]
    s0 = [s_ref[h] for h in hs]
    rhs = [dot(at[h], s0[h], _NT) + dot(a_ak[h], v[h], _NN) for h in hs]
    u = [dot(minv[h], rhs[h], _NN) for h in hs]
    y = [dot(rt[h], s0[h], _NT) + dot(a_rb[h], u[h], _NN) + dot(a_rk[h], v[h], _NN) for h in hs]
    for h in hs:
        s_ref[h] = (s0[h] * jnp.exp(tot[h]) + dot(u[h], b[h] * pend[h], _TN)
                    + dot(v[h], k[h] * pend[h], _TN))
    for h in hs:
        cols = slice(h * n, (h + 1) * n)
        d = y[h] - jnp.mean(y[h], axis=-1, keepdims=True)
        yn = d * lax.rsqrt(jnp.mean(d * d, axis=-1, keepdims=True) + LNX_EPS)
        bonus = jnp.sum(r[h] * k[h] * rk_ref[:, cols], axis=-1, keepdims=True) * v[h]
        y_ref[:, cols] = (yn * lng_ref[:, cols] + lnb_ref[:, cols] + bonus) * g_ref[:, cols]


def wkv7(r, lw, k, v, a, b, g, r_k, lnx_g, lnx_b, n=B_HEAD, exact=False):
    bsz, t, width = r.shape
    nh = width // n
    c = min(WKV_CHUNK, t)
    spec = pl.BlockSpec((None, c, width), lambda bi, ci: (bi, ci, 0))
    par = pl.BlockSpec((1, width), lambda bi, ci: (0, 0))
    vec = lambda x: x.reshape(1, width).astype(F32)
    return pl.pallas_call(
        functools.partial(_wkv_body, nh=nh, n=n, c=c, exact=exact),
        grid=(bsz, t // c),
        in_specs=[spec] * 7 + [par] * 3,
        out_specs=spec,
        out_shape=jax.ShapeDtypeStruct((bsz, t, width), F32),
        scratch_shapes=[pltpu.VMEM((nh, n, n), F32)],
        compiler_params=_params(("parallel", "arbitrary")),
        name="wkv7_chunked",
    )(r, lw, k, v, a, b, g, vec(r_k), vec(lnx_g), vec(lnx_b))


def _dsa_index_body(ki_ref, qit_ref, wit_ref, bias_ref, key_ref, dig_ref, cut_ref, *, tq, tkc, topk, nheads, t):
    i = pl.program_id(1)
    nch = ((i + 1) * tq + tkc - 1) // tkc
    qpos = i * tq + lax.broadcasted_iota(I32, (1, tq), 1)

    def kpos_of(ci):
        return ci * tkc + lax.broadcasted_iota(I32, (tkc, 1), 0)

    def rows(ci):
        return pl.ds(pl.multiple_of(ci * tkc, tkc), tkc)

    def score_chunk(ci, carry):
        kc = ki_ref[rows(ci), :]
        acc = jnp.zeros((tkc, tq), F32)
        for h in range(nheads):
            s = jnp.dot(kc, qit_ref[h], preferred_element_type=F32)
            acc = acc + jnp.maximum(s, 0.0) * wit_ref[h]
        acc = jnp.where(kpos_of(ci) <= qpos, acc, -jnp.inf)
        bits = lax.bitcast_convert_type(acc, I32)
        key_ref[rows(ci), :] = jnp.where(bits >= 0, bits, bits ^ 0x7FFFFFFF)
        return carry

    lax.fori_loop(0, nch, score_chunk, 0)

    def count(pred):
        def body(ci, cnt):
            ones = pred(key_ref[rows(ci), :], kpos_of(ci))
            return cnt + jnp.sum(ones, axis=0, keepdims=True)
        return lax.fori_loop(0, nch, body, jnp.zeros((1, tq), I32))

    fold = 64
    assert (t // tkc) * (tkc // fold) <= 256

    def count_digit_ge(cand):
        cand_bf = cand.astype(F32).astype(BF16)

        def body(ci, acc):
            hit = jnp.where(dig_ref[rows(ci), :] >= cand_bf, jnp.ones((), BF16), jnp.zeros((), BF16))
            for s in range(tkc // fold):
                acc = acc + hit[s * fold:(s + 1) * fold]
            return acc

        acc = lax.fori_loop(0, nch, body, jnp.zeros((fold, tq), BF16))
        return jnp.sum(acc.astype(F32), axis=0, keepdims=True).astype(I32)

    prefix = jnp.zeros((1, tq), I32)
    above = jnp.zeros((1, tq), I32)
    n_ge = above
    for level in (3, 2, 1, 0):
        shift = 8 * level

        def build(ci, carry, shift=shift, level=level, prefix=prefix):
            u = key_ref[rows(ci), :] ^ INT_MIN
            digit = lax.shift_right_logical(u, jnp.int32(shift)) & 255
            if level < 3:
                same = lax.shift_right_logical(u, jnp.int32(shift + 8)) == lax.shift_right_logical(
                    prefix, jnp.int32(shift + 8))
                digit = jnp.where(same, digit, -1)
            dig_ref[rows(ci), :] = digit.astype(F32).astype(BF16)
            return carry

        lax.fori_loop(0, nch, build, 0)

        def digit_bit(step, d, above=above):
            cand = d | lax.shift_left(jnp.int32(1), 7 - step)
            return jnp.where(above + count_digit_ge(cand) >= topk, cand, d)

        d = lax.fori_loop(0, 8, digit_bit, jnp.zeros((1, tq), I32))
        if level == 0:
            n_ge = above + count_digit_ge(d)
        above = above + count_digit_ge(d + 1)
        prefix = prefix | lax.shift_left(d, jnp.int32(shift))

    thr = prefix ^ INT_MIN
    n_gt = above
    need = topk - n_gt
    cut_ref[...] = jnp.full((1, tq), t, I32)

    @pl.when(jnp.max(n_ge) > topk)
    def _():
        nbits = max(1, (t - 1).bit_length())

        def index_bit(step, x):
            cand = x | lax.shift_left(jnp.int32(1), nbits - 1 - step)
            g = count(lambda key, kpos: jnp.where(key == thr, jnp.where(kpos < cand, 1, 0), 0))
            return jnp.where(g < need, cand, x)

        cut_ref[...] = lax.fori_loop(0, nbits, index_bit, jnp.zeros((1, tq), I32))

    cutoff = cut_ref[...]

    def write_chunk(ci, carry):
        key = key_ref[rows(ci), :]
        kpos = kpos_of(ci)
        tie = jnp.where(kpos <= cutoff, 0.0, NEG)
        sel = jnp.where(key > thr, 0.0, jnp.where(key == thr, tie, NEG))
        bias_ref[rows(ci), :] = jnp.where(kpos <= qpos, sel, NEG).astype(BF16)
        return carry

    lax.fori_loop(0, nch, write_chunk, 0)

    def fill_chunk(ci, carry):
        bias_ref[rows(ci), :] = jnp.full((tkc, tq), NEG, BF16)
        return carry

    lax.fori_loop(nch, t // tkc, fill_chunk, 0)


def dsa_select(ki, qi, wi, topk):
    b, t, nh, d = qi.shape
    tq = _pick(t, (256, 128))
    tkc = _pick(t, (512, 256, 128))
    nq = t // tq
    qit = jnp.transpose(qi.astype(BF16).reshape(b, nq, tq, nh, d), (0, 1, 3, 4, 2))
    wit = jnp.transpose(wi.astype(F32).reshape(b, nq, tq, nh), (0, 1, 3, 2))[:, :, :, None, :]
    return pl.pallas_call(
        functools.partial(_dsa_index_body, tq=tq, tkc=tkc, topk=topk, nheads=nh, t=t),
        grid=(b, nq),
        in_specs=[pl.BlockSpec((None, t, d), lambda bi, qi_: (bi, 0, 0)),
                  pl.BlockSpec((None, None, nh, d, tq), lambda bi, qi_: (bi, qi_, 0, 0, 0)),
                  pl.BlockSpec((None, None, nh, 1, tq), lambda bi, qi_: (bi, qi_, 0, 0, 0))],
        out_specs=pl.BlockSpec((None, t, tq), lambda bi, qi_: (bi, 0, qi_)),
        out_shape=jax.ShapeDtypeStruct((b, t, t), BF16),
        scratch_shapes=[pltpu.VMEM((t, tq), I32), pltpu.VMEM((t, tq), BF16), pltpu.VMEM((1, tq), I32)],
        compiler_params=_params(("parallel", "arbitrary")),
        name="dsa_index_topk",
    )(ki.astype(BF16), qit, wit)


def _dsa_attn_body(k_ref, vt_ref, q_ref, bias_ref, o_ref, m_ref, acc_ref, pa_ref, aa_ref, pb_ref, ab_ref,
                   *, tq, tk, rep, d):
    i = pl.program_id(2)
    nch = ((i + 1) * tq + tk - 1) // tk
    m_ref[...] = jnp.full(m_ref.shape, NEG, F32)
    acc_ref[...] = jnp.zeros(acc_ref.shape, F32)
    buf_a = (pa_ref, aa_ref)
    buf_b = (pb_ref, ab_ref)

    def scores(c, buf):
        rows = pl.ds(pl.multiple_of(c * tk, tk), tk)
        kj = k_ref[rows, :]
        bj = bias_ref[rows, :]
        fns = [lambda r=r: lax.dot_general(kj, q_ref[:, r * d:(r + 1) * d], (_NT, ((), ())),
                                           preferred_element_type=F32).astype(BF16) + bj
               for r in range(rep)]
        _score_stage(fns, m_ref, *buf)

    def values(c, buf):
        _value_stage(vt_ref[c], acc_ref, *buf)

    scores(0, buf_a)

    def pair(t, carry):
        c = 2 * t + 1
        scores(c, buf_b)
        values(c - 1, buf_a)
        scores(c + 1, buf_a)
        values(c, buf_b)
        return carry

    lax.fori_loop(0, (nch - 1) // 2, pair, 0)
    last = nch - 1

    @pl.when(last % 2 == 1)
    def _():
        scores(last, buf_b)
        values(last - 1, buf_a)
        values(last, buf_b)

    @pl.when(last % 2 == 0)
    def _():
        values(last, buf_a)

    for r in range(rep):
        acc = acc_ref[r]
        o_ref[:, r * d:(r + 1) * d] = (acc[:d] / acc[d:d + 1]).T.astype(o_ref.dtype)


def dsa_attention(q, k, v, bias):
    b, t, qw = q.shape
    d = C_HEAD_DIM
    g = k.shape[2] // d
    rep = qw // (g * d)
    tq = _pick(t, (256, 128))
    tk = _pick(t, (512, 256, 128))
    nq, nk = t // tq, t // tk
    vt = jnp.transpose(v.astype(BF16).reshape(b, nk, tk, g, d), (0, 3, 1, 4, 2))
    vt = _with_ones_rows(vt)
    dve = d + ONES_ROWS
    group = pl.BlockSpec((None, tq, rep * d), lambda bi, gi, qi: (bi, qi, gi))
    return pl.pallas_call(
        functools.partial(_dsa_attn_body, tq=tq, tk=tk, rep=rep, d=d),
        grid=(b, g, nq),
        in_specs=[pl.BlockSpec((None, t, d), lambda bi, gi, qi: (bi, 0, gi)),
                  pl.BlockSpec((None, None, nk, dve, tk), lambda bi, gi, qi: (bi, gi, 0, 0, 0)),
                  group,
                  pl.BlockSpec((None, t, tq), lambda bi, gi, qi: (bi, 0, qi))],
        out_specs=group,
        out_shape=jax.ShapeDtypeStruct((b, t, qw), BF16),
        scratch_shapes=[pltpu.VMEM((rep, 1, tq), F32), pltpu.VMEM((rep, dve, tq), F32)]
        + [pltpu.VMEM((rep, tk, tq), BF16), pltpu.VMEM((rep, 1, tq), F32)] * 2,
        compiler_params=_params(("parallel", "parallel", "arbitrary")),
        name="dsa_attention",
    )(k.astype(BF16), vt, q.astype(BF16), bias)


LANES = 128


def _rope_tables(pos, head_dim, scale=1.0):
    rot = head_dim // ROPE_FRACTION
    half = rot // 2
    inv_freq = ROPE_THETA ** (-(jnp.arange(half, dtype=F32) * 2.0 / rot))
    ang = pos.astype(F32).reshape(-1, 1) * inv_freq
    cos, sin = jnp.cos(ang), jnp.sin(ang)
    zero = jnp.zeros_like(sin)
    rest = jnp.zeros((ang.shape[0], head_dim - rot), F32)
    tabs = (jnp.concatenate([cos, cos, rest + 1.0], axis=-1),
            jnp.concatenate([-sin, zero, rest], axis=-1),
            jnp.concatenate([zero, sin, rest], axis=-1))
    return tuple(jnp.tile(x * scale, (1, LANES // head_dim)) for x in tabs)


def _rope_body(z_ref, c_ref, up_ref, down_ref, o_ref, *, half, ncols):
    c, up, down = c_ref[...], up_ref[...], down_ref[...]
    for cb in range(ncols // LANES):
        cols = slice(cb * LANES, (cb + 1) * LANES)
        x = z_ref[:, cols]
        y = x * c + pltpu.roll(x, LANES - half, 1) * up + pltpu.roll(x, half, 1) * down
        o_ref[:, cols] = y.astype(o_ref.dtype)


def rope_cast(z, col0, ncols, head_dim, tables):
    m = z.shape[0]
    assert col0 % ncols == 0 and ncols % LANES == 0 and LANES % head_dim == 0
    tm = _pick(m, (512, 256, 128, 64, 32, 16, 8))
    tab = pl.BlockSpec((tm, LANES), lambda i: (i, 0))
    return pl.pallas_call(
        functools.partial(_rope_body, half=head_dim // ROPE_FRACTION // 2, ncols=ncols),
        grid=(m // tm,),
        in_specs=[pl.BlockSpec((tm, ncols), lambda i: (i, col0 // ncols)), tab, tab, tab],
        out_specs=pl.BlockSpec((tm, ncols), lambda i: (i, 0)),
        out_shape=jax.ShapeDtypeStruct((m, ncols), BF16),
        compiler_params=_params(("parallel",)),
        name="rope_cast",
    )(z, *tables)


def _rms_norm(x, g, eps=NORM_EPS):
    xf = x.astype(F32)
    return xf * lax.rsqrt(jnp.mean(xf * xf, axis=-1, keepdims=True) + eps) * g.astype(F32)


def _layer_norm(x, g, b, eps=LN_EPS):
    mu = jnp.mean(x, axis=-1, keepdims=True)
    var = jnp.mean(jnp.square(x - mu), axis=-1, keepdims=True)
    return (x - mu) * lax.rsqrt(var + eps) * g + b


def _partial_rope(x, pos):
    dh = x.shape[-1]
    rot = dh // ROPE_FRACTION
    half = rot // 2
    inv_freq = ROPE_THETA ** (-(jnp.arange(half, dtype=F32) * 2.0 / rot))
    ang = pos.astype(F32)[..., None] * inv_freq
    cos = jnp.cos(ang)[:, :, None, :]
    sin = jnp.sin(ang)[:, :, None, :]
    x1 = x[..., :half]
    x2 = x[..., half:rot]
    return jnp.concatenate([x1 * cos - x2 * sin, x2 * cos + x1 * sin, x[..., rot:]], axis=-1)


def _pad_cols(w, n):
    return jnp.pad(w, ((0, 0), (0, n - w.shape[1])))


def _round_up(n, m):
    return (n + m - 1) // m * m


def _mixer_ab(u, pos, w_in, lq1, lk1, lq2, lk2, subln_g, mu, w0, w2, a0, a2, g2,
              k_k, k_a, r_k, lnx_g, lnx_b, lambda_init):
    b, t, d = u.shape
    a_width = d // 2
    b_width = d // 2
    a_in = 3 * a_width
    n_in = w_in.shape[1]
    z = matmul(u.reshape(b * t, d), _pad_cols(w_in, _round_up(n_in, 512)))

    seq = lambda x: x.reshape(b, t, a_width)
    qa = rope_cast(z, 0, a_width, A_DC, _rope_tables(pos, A_DC, A_DC ** -0.5 * LOG2E))
    ka = rope_cast(z, a_width, a_width, A_DC, _rope_tables(pos, A_DC))
    va = z[:, 2 * a_width:a_in].astype(BF16)
    lam = jnp.exp(jnp.sum(lq1 * lk1)) - jnp.exp(jnp.sum(lq2 * lk2)) + lambda_init
    ya = diff_attention(seq(qa), seq(ka), seq(va), lam, subln_g * (1.0 - lambda_init))

    prep = rwkv_prep(z, a_in, b_width, n_in - a_in - 3 * b_width, t, mu, w0, w2, a0, a2, g2, k_k, k_a)
    yb = wkv7(*(x.reshape(b, t, b_width) for x in prep), r_k.reshape(-1), lnx_g, lnx_b)
    return jnp.concatenate([ya, yb.astype(BF16)], axis=-1)


def _mixer_c(u, pos, w_in, idx_k_g, idx_k_b, topk):
    b, t, d = u.shape
    c_heads = d // C_HEAD_DIM
    n_in = w_in.shape[1]
    c1 = c_heads * C_HEAD_DIM
    c2 = c1 + C_KV_HEADS * C_HEAD_DIM
    c3 = c2 + C_KV_HEADS * C_HEAD_DIM
    c4 = c3 + IDX_HEADS * IDX_DIM
    c5 = c4 + IDX_DIM
    z = matmul(u.reshape(b * t, d), _pad_cols(w_in, _round_up(n_in, 512)))
    seq = lambda x: x.reshape(b, t, x.shape[-1])
    tabs = _rope_tables(pos, C_HEAD_DIM)
    q = rope_cast(z, 0, c1, C_HEAD_DIM, _rope_tables(pos, C_HEAD_DIM, C_HEAD_DIM ** -0.5 * LOG2E))
    k = rope_cast(z, c1, c2 - c1, C_HEAD_DIM, tabs)
    v = z[:, c2:c3].astype(BF16)
    qi = rope_cast(z, c3, c4 - c3, IDX_DIM, _rope_tables(pos, IDX_DIM))
    ki = _partial_rope(_layer_norm(seq(z[:, c4:c5]), idx_k_g, idx_k_b)[:, :, None, :], pos)[:, :, 0]
    wi = seq(z[:, c5:n_in]) * ((IDX_HEADS * IDX_DIM) ** -0.5)
    bias = dsa_select(ki, qi.reshape(b, t, IDX_HEADS, IDX_DIM), wi, topk)
    return dsa_attention(seq(q), seq(k), seq(v), bias)


def kernel(x, p, positions, mix_pre_g, mix_post_g, mlp_pre_g, mlp_post_g, w_mlp_up, w_mlp_down, w_ple_proj, w_ple_gate, ple_post_g, ab_w_in, ab_w_out, diff_lq1, diff_lk1, diff_lq2, diff_lk2, diff_subln_g, rwkv_mu, rwkv_w0, rwkv_w2, rwkv_a0, rwkv_a2, rwkv_g2, rwkv_k_k, rwkv_k_a, rwkv_r_k, rwkv_lnx_g, rwkv_lnx_b, c_w_in, c_w_out, idx_k_g, idx_k_b):
    depth = mix_pre_g.shape[0]
    b, t, d = x.shape
    topk = min(TOPK_MAX, t // 4)
    flat = lambda z: z.reshape(b * t, z.shape[-1])
    h = flat(x)
    u = _rms_norm(h, mix_pre_g[0]).astype(BF16)
    for i in range(depth):
        j = i // 2
        u3 = u.reshape(b, t, d)
        if i % 2 == 0:
            lambda_init = 0.8 - 0.6 * math.exp(-0.3 * i)
            m = _mixer_ab(u3, positions, ab_w_in[j], diff_lq1[j], diff_lk1[j], diff_lq2[j],
                          diff_lk2[j], diff_subln_g[j], rwkv_mu[j], rwkv_w0[j], rwkv_w2[j], rwkv_a0[j],
                          rwkv_a2[j], rwkv_g2[j], rwkv_k_k[j], rwkv_k_a[j], rwkv_r_k[j], rwkv_lnx_g[j],
                          rwkv_lnx_b[j], lambda_init)
            w_out = ab_w_out[j]
        else:
            m = _mixer_c(u3, positions, c_w_in[j], idx_k_g[j], idx_k_b[j], topk)
            w_out = c_w_out[j]
        h, u = proj_residual(flat(m), w_out, h, mix_post_g[i], mlp_pre_g[i])
        up = matmul(u, w_mlp_up[i], out_dtype=BF16, act="relu2")
        h, _ = proj_residual(up, w_mlp_down[i], h, mlp_post_g[i], None)
        g_next = mix_pre_g[i + 1] if i + 1 < depth else None
        h, u = ple_residual(flat(p[i]), w_ple_proj[i], w_ple_gate[i], h, ple_post_g[i], g_next)
    return h.reshape(b, t, d)
```

```python
import functools
import math

import jax
import jax.numpy as jnp
from jax import lax
from jax.experimental import pallas as pl
from jax.experimental.pallas import tpu as pltpu

F32 = jnp.float32
BF16 = jnp.bfloat16
I32 = jnp.int32

V7X_VMEM_LIMIT_BYTES = 56 * 1024 * 1024
NEG = -1e30
LOG2E = 1.4426950408889634
INT_MIN = -2147483648

NORM_EPS = 1e-6
ROPE_THETA = 500000.0
ROPE_FRACTION = 4
A_DV = 128
A_DC = 64
DIFF_SUBLN_EPS = 1e-5
B_HEAD = 64
LNX_EPS = 64e-5
C_HEAD_DIM = 128
C_KV_HEADS = 4
IDX_HEADS = 16
IDX_DIM = 64
TOPK_MAX = 256
LN_EPS = 1e-6
WKV_CHUNK = 64


def _params(sem):
    return pltpu.CompilerParams(dimension_semantics=sem, vmem_limit_bytes=V7X_VMEM_LIMIT_BYTES)


def _pick(n, prefs):
    for t in prefs:
        if n % t == 0:
            return t
    return n


def _mm_body(a_ref, w_ref, o_ref, *scratch, nk, act):
    def epilogue(acc):
        if act == "relu2":
            r = jnp.maximum(acc, 0.0)
            acc = r * r
        return acc.astype(o_ref.dtype)

    if nk == 1:
        o_ref[...] = epilogue(jnp.dot(a_ref[...], w_ref[...], preferred_element_type=F32))
        return
    (acc_ref,) = scratch
    k = pl.program_id(2)

    @pl.when(k == 0)
    def _():
        acc_ref[...] = jnp.zeros_like(acc_ref)

    acc_ref[...] += jnp.dot(a_ref[...], w_ref[...], preferred_element_type=F32)

    @pl.when(k == nk - 1)
    def _():
        o_ref[...] = epilogue(acc_ref[...])


def matmul(a, w, out_dtype=F32, act=None):
    a = a.astype(BF16)
    w = w.astype(BF16)
    m, kdim = a.shape
    n = w.shape[1]
    tn = _pick(n, (2048, 1024, 512, 256, 128))
    tile_elems = (4 << 20) // jnp.dtype(out_dtype).itemsize
    tm = _pick(m, tuple(r for r in (2048, 1024, 512, 256, 128, 64, 32, 16, 8) if r * tn <= tile_elems))
    tk = kdim if kdim <= 2048 else _pick(kdim, (2048, 1024, 512))
    nk = kdim // tk
    scratch = [pltpu.VMEM((tm, tn), F32)] if nk > 1 else []
    return pl.pallas_call(
        functools.partial(_mm_body, nk=nk, act=act),
        grid=(m // tm, n // tn, nk),
        in_specs=[pl.BlockSpec((tm, tk), lambda i, j, k: (i, k)),
                  pl.BlockSpec((tk, tn), lambda i, j, k: (k, j))],
        out_specs=pl.BlockSpec((tm, tn), lambda i, j, k: (i, j)),
        out_shape=jax.ShapeDtypeStruct((m, n), out_dtype),
        scratch_shapes=scratch,
        compiler_params=_params(("parallel", "parallel", "arbitrary")),
        name="dense_matmul",
    )(a, w)


def _rms_rows(x, g):
    return x * lax.rsqrt(jnp.mean(x * x, axis=-1, keepdims=True) + NORM_EPS) * g


def _residual_update(m, h_ref, gpost_ref, gnext_ref, ho_ref, uo_ref):
    h_new = h_ref[...] + _rms_rows(m, gpost_ref[...])
    ho_ref[...] = h_new
    if uo_ref is not None:
        uo_ref[...] = _rms_rows(h_new, gnext_ref[...]).astype(BF16)


def _proj_res_body(*refs, nk, has_next):
    if has_next:
        a_ref, w_ref, h_ref, gpost_ref, gnext_ref, ho_ref, uo_ref = refs
    else:
        a_ref, w_ref, h_ref, gpost_ref, ho_ref = refs
        gnext_ref = uo_ref = None
    part = jnp.dot(a_ref[...], w_ref[...], preferred_element_type=F32)
    if nk == 1:
        _residual_update(part, h_ref, gpost_ref, gnext_ref, ho_ref, uo_ref)
        return
    k = pl.program_id(1)

    @pl.when(k == 0)
    def _():
        ho_ref[...] = part

    @pl.when(jnp.logical_and(k > 0, k < nk - 1))
    def _():
        ho_ref[...] += part

    @pl.when(k == nk - 1)
    def _():
        _residual_update(ho_ref[...] + part, h_ref, gpost_ref, gnext_ref, ho_ref, uo_ref)


def proj_residual(a, w, h, g_post, g_next):
    m, kdim = a.shape
    n = w.shape[1]
    tm = _pick(m, (512, 256, 128, 64, 32, 16, 8))
    tk = _pick(kdim, (2048, 1024, 512, 256, 128))
    nk = kdim // tk
    has_next = g_next is not None
    row = pl.BlockSpec((tm, n), lambda i, k: (i, 0))
    vec = pl.BlockSpec((1, n), lambda i, k: (0, 0))
    in_specs = [pl.BlockSpec((tm, tk), lambda i, k: (i, k)), pl.BlockSpec((tk, n), lambda i, k: (k, 0)), row, vec]
    args = [a.astype(BF16), w.astype(BF16), h, g_post.reshape(1, n).astype(F32)]
    out_shape = [jax.ShapeDtypeStruct((m, n), F32)]
    out_specs = [row]
    if has_next:
        in_specs.append(vec)
        args.append(g_next.reshape(1, n).astype(F32))
        out_shape.append(jax.ShapeDtypeStruct((m, n), BF16))
        out_specs.append(row)
    out = pl.pallas_call(
        functools.partial(_proj_res_body, nk=nk, has_next=has_next),
        grid=(m // tm, nk),
        in_specs=in_specs, out_specs=out_specs, out_shape=out_shape,
        compiler_params=_params(("parallel", "arbitrary")),
        name="proj_residual",
    )(*args)
    return (out[0], out[1]) if has_next else (out[0], None)


def _ple_body(*refs, has_next):
    if has_next:
        p_ref, wp_ref, wg_ref, h_ref, gpost_ref, gnext_ref, ho_ref, uo_ref = refs
    else:
        p_ref, wp_ref, wg_ref, h_ref, gpost_ref, ho_ref = refs
        gnext_ref = uo_ref = None
    e = jnp.dot(p_ref[...], wp_ref[...], preferred_element_type=F32)
    gate = jax.nn.sigmoid(jnp.dot(h_ref[...].astype(BF16), wg_ref[...], preferred_element_type=F32))
    _residual_update(e * gate, h_ref, gpost_ref, gnext_ref, ho_ref, uo_ref)


def ple_residual(p, w_proj, w_gate, h, g_post, g_next):
    m, n = h.shape
    pd = p.shape[1]
    tm = _pick(m, (256, 128, 64, 32, 16, 8))
    has_next = g_next is not None
    row = pl.BlockSpec((tm, n), lambda i: (i, 0))
    vec = pl.BlockSpec((1, n), lambda i: (0, 0))
    in_specs = [pl.BlockSpec((tm, pd), lambda i: (i, 0)), pl.BlockSpec((pd, n), lambda i: (0, 0)),
                pl.BlockSpec((n, n), lambda i: (0, 0)), row, vec]
    args = [p.astype(BF16), w_proj.astype(BF16), w_gate.astype(BF16), h, g_post.reshape(1, n).astype(F32)]
    out_shape = [jax.ShapeDtypeStruct((m, n), F32)]
    out_specs = [row]
    if has_next:
        in_specs.append(vec)
        args.append(g_next.reshape(1, n).astype(F32))
        out_shape.append(jax.ShapeDtypeStruct((m, n), BF16))
        out_specs.append(row)
    out = pl.pallas_call(
        functools.partial(_ple_body, has_next=has_next),
        grid=(m // tm,),
        in_specs=in_specs, out_specs=out_specs, out_shape=out_shape,
        compiler_params=_params(("parallel",)),
        name="ple_residual",
    )(*args)
    return (out[0], out[1]) if has_next else (out[0], None)


ONES_ROWS = 16


def _with_ones_rows(vt):
    pad = jnp.zeros(vt.shape[:-2] + (ONES_ROWS, vt.shape[-1]), vt.dtype).at[..., 0, :].set(1)
    return jnp.concatenate([vt, pad], axis=-2)


def _score_stage(score_fns, m_ref, p_ref, alpha_ref):
    n = range(len(score_fns))
    ss = [fn() for fn in score_fns]
    m_old = [m_ref[i] for i in n]
    m_new = [jnp.maximum(m_old[i], jnp.max(ss[i], axis=0, keepdims=True).astype(F32)) for i in n]
    for i in n:
        alpha_ref[i] = jnp.exp2(m_old[i] - m_new[i])
        m_ref[i] = m_new[i]
    for i in n:
        p_ref[i] = jnp.exp2(ss[i] - m_new[i].astype(BF16))


def _value_stage(vt, acc_ref, p_ref, alpha_ref):
    n = range(p_ref.shape[0])
    pv = [jnp.dot(vt, p_ref[i], preferred_element_type=F32) for i in n]
    for i in n:
        acc_ref[i] = alpha_ref[i] * acc_ref[i] + pv[i]


def _diff_attn_body(lam_ref, k_ref, q_ref, cmask_ref, vt_ref, g_ref, o_ref, m_ref, acc_ref, qz_ref,
                    pa_ref, aa_ref, pb_ref, ab_ref, *, tq, nsp):
    i = pl.program_id(2)
    tl = tq // nsp
    m_ref[...] = jnp.full(m_ref.shape, NEG, F32)
    acc_ref[...] = jnp.zeros(acc_ref.shape, F32)
    for c in range(2):
        qz_ref[c] = q_ref[...] * cmask_ref[c]

    buf_a = (pa_ref, aa_ref)
    buf_b = (pb_ref, ab_ref)

    def scores(j, buf, masked=False):
        kj = k_ref[pl.ds(pl.multiple_of(j * tq, tq), tq), :]

        def score(c, sp):
            s = lax.dot_general(kj, qz_ref[c, sp * tl:(sp + 1) * tl, :], (_NT, ((), ())),
                                preferred_element_type=F32)
            if masked:
                row = lax.broadcasted_iota(I32, (tq, tl), 0)
                col = lax.broadcasted_iota(I32, (tq, tl), 1)
                s = jnp.where(row <= col + sp * tl, s, NEG)
            return s.astype(BF16)

        _score_stage([functools.partial(score, c, sp) for c in range(2) for sp in range(nsp)], m_ref, *buf)

    def values(j, buf):
        _value_stage(vt_ref[j], acc_ref, *buf)

    @pl.when(i == 0)
    def _():
        scores(0, buf_a, masked=True)
        values(0, buf_a)

    @pl.when(i > 0)
    def _():
        scores(0, buf_a)

        def pair(t, carry):
            c = 2 * t + 1
            scores(c, buf_b)
            values(c - 1, buf_a)
            scores(c + 1, buf_a)
            values(c, buf_b)
            return carry

        lax.fori_loop(0, (i - 1) // 2, pair, 0)

        @pl.when((i - 1) % 2 == 1)
        def _():
            scores(i - 1, buf_b)
            values(i - 2, buf_a)
            scores(i, buf_a, masked=True)
            values(i - 1, buf_b)
            values(i, buf_a)

        @pl.when((i - 1) % 2 == 0)
        def _():
            scores(i, buf_b, masked=True)
            values(i - 1, buf_a)
            values(i, buf_b)

    lam = lam_ref[0]
    for sp in range(nsp):
        a0 = acc_ref[sp]
        a1 = acc_ref[nsp + sp]
        out = (a0[:A_DV] / a0[A_DV:A_DV + 1] - lam * (a1[:A_DV] / a1[A_DV:A_DV + 1])).T
        y = out * lax.rsqrt(jnp.mean(out * out, axis=-1, keepdims=True) + DIFF_SUBLN_EPS) * g_ref[...]
        o_ref[sp * tl:(sp + 1) * tl, :] = y.astype(o_ref.dtype)


def diff_attention(q, k, v, lam, gain):
    b, t, width = q.shape
    h = width // A_DV
    tq = _pick(t, (512, 256, 128))
    nq = t // tq
    vt = jnp.transpose(v.astype(BF16).reshape(b, nq, tq, h, A_DV), (0, 3, 1, 4, 2))
    vt = _with_ones_rows(vt)
    dve = A_DV + ONES_ROWS
    nsp = 2 if tq >= 256 else 1
    lane = jnp.arange(A_DV)
    cmask = jnp.stack([lane < A_DC, lane >= A_DC]).astype(BF16).reshape(2, 1, A_DV)
    head = pl.BlockSpec((None, tq, A_DV), lambda bi, hi, qi: (bi, qi, hi))
    return pl.pallas_call(
        functools.partial(_diff_attn_body, tq=tq, nsp=nsp),
        grid=(b, h, nq),
        in_specs=[pl.BlockSpec(memory_space=pltpu.SMEM),
                  pl.BlockSpec((None, t, A_DV), lambda bi, hi, qi: (bi, 0, hi)),
                  head,
                  pl.BlockSpec((2, 1, A_DV), lambda bi, hi, qi: (0, 0, 0)),
                  pl.BlockSpec((None, None, nq, dve, tq), lambda bi, hi, qi: (bi, hi, 0, 0, 0)),
                  pl.BlockSpec((1, A_DV), lambda bi, hi, qi: (0, 0))],
        out_specs=head,
        out_shape=jax.ShapeDtypeStruct((b, t, width), BF16),
        scratch_shapes=[pltpu.VMEM((2 * nsp, 1, tq // nsp), F32),
                        pltpu.VMEM((2 * nsp, dve, tq // nsp), F32),
                        pltpu.VMEM((2, tq, A_DV), BF16)]
        + [pltpu.VMEM((2 * nsp, tq, tq // nsp), BF16), pltpu.VMEM((2 * nsp, 1, tq // nsp), F32)] * 2,
        compiler_params=_params(("parallel", "parallel", "arbitrary")),
        name="diff_attention",
    )(lam.reshape(1).astype(F32), k.astype(BF16), q.astype(BF16), cmask, vt, gain.reshape(1, A_DV).astype(F32))


def _split_dot(x, w):
    hi = x.astype(BF16)
    lo = (x - hi.astype(F32)).astype(BF16)
    return jnp.dot(hi, w, preferred_element_type=F32) + jnp.dot(lo, w, preferred_element_type=F32)


def _rwkv_prep_body(zr_ref, zk_ref, zv_ref, zl_ref, pr_ref, pk_ref, pv_ref, pl_ref,
                    mur_ref, muk_ref, muv_ref, mul_ref, w0_ref, a0_ref, kk_ref, ka_ref,
                    w2_ref, a2_ref, g2_ref, hsum_ref,
                    r_ref, lw_ref, k_ref, v_ref, an_ref, bn_ref, g_ref, *, tm, seq):
    first = (pl.program_id(0) * tm) % seq == 0

    def shifted(z_ref, p_ref, mu_ref):
        z = z_ref[...]
        last = jnp.where(first, 0.0, p_ref[7:8, :])
        row = lax.broadcasted_iota(I32, z.shape, 0)
        prev = jnp.where(row == 0, last, pltpu.roll(z, 1, 0))
        return z + (prev - z) * mu_ref[...]

    r = shifted(zr_ref, pr_ref, mur_ref)
    k = shifted(zk_ref, pk_ref, muk_ref)
    v = shifted(zv_ref, pv_ref, muv_ref)
    lo = shifted(zl_ref, pl_ref, mul_ref)
    lw = jnp.dot(jnp.tanh(lo).astype(BF16), w2_ref[...], preferred_element_type=F32)
    la = jnp.dot(lo.astype(BF16), a2_ref[...], preferred_element_type=F32)
    g = jnp.dot(jax.nn.sigmoid(lo).astype(BF16), g2_ref[...], preferred_element_type=F32)
    w = -jax.nn.softplus(-(w0_ref[...] + lw)) - 0.5
    a = jax.nn.sigmoid(a0_ref[...] + la)
    kk = k * kk_ref[...]
    norm = jnp.sqrt(_split_dot(kk * kk, hsum_ref[...]))
    kk = kk / jnp.maximum(norm, 1e-12)
    r_ref[...] = r
    lw_ref[...] = -jnp.exp(w)
    k_ref[...] = k * (1.0 + (a - 1.0) * ka_ref[...])
    v_ref[...] = v
    an_ref[...] = -kk
    bn_ref[...] = kk * a
    g_ref[...] = g


def rwkv_prep(z, col0, width, n_lora, seq, mu, w0, w2, a0, a2, g2, k_k, k_a):
    m = z.shape[0]
    tm = _pick(seq, (256, 128, 64, 32, 16, 8))
    lw_ = _round_up(n_lora, 128)
    assert col0 % width == 0 and (col0 + 3 * width) % lw_ == 0 and col0 + 3 * width + lw_ <= z.shape[1]
    cb = col0 // width
    lb = (col0 + 3 * width) // lw_
    dl, al = w2.shape[0], a2.shape[0]

    def rows_of(w, start):
        return jnp.zeros((lw_, width), BF16).at[start:start + w.shape[0]].set(w.astype(BF16))

    hsum = jnp.kron(jnp.eye(width // B_HEAD, dtype=BF16), jnp.ones((B_HEAD, B_HEAD), BF16))
    vec = lambda x: x.reshape(1, -1).astype(F32)
    mul = jnp.pad(mu[3 * width:], (0, lw_ - n_lora))
    tile = lambda c, wd: pl.BlockSpec((tm, wd), lambda i: (i, c))
    before = lambda c, wd: pl.BlockSpec((8, wd), lambda i: (jnp.maximum(i * (tm // 8) - 1, 0), c))
    par = lambda wd: pl.BlockSpec((1, wd), lambda i: (0, 0))
    mat = lambda r_, c_: pl.BlockSpec((r_, c_), lambda i: (0, 0))
    out = pl.BlockSpec((tm, width), lambda i: (i, 0))
    return pl.pallas_call(
        functools.partial(_rwkv_prep_body, tm=tm, seq=seq),
        grid=(m // tm,),
        in_specs=[tile(cb, width), tile(cb + 1, width), tile(cb + 2, width), tile(lb, lw_),
                  before(cb, width), before(cb + 1, width), before(cb + 2, width), before(lb, lw_),
                  par(width), par(width), par(width), par(lw_), par(width), par(width), par(width), par(width),
                  mat(lw_, width), mat(lw_, width), mat(lw_, width), mat(width, width)],
        out_specs=[out] * 7,
        out_shape=[jax.ShapeDtypeStruct((m, width), F32)] * 7,
        compiler_params=_params(("parallel",)),
        name="rwkv_prep",
    )(z, z, z, z, z, z, z, z,
      vec(mu[:width]), vec(mu[width:2 * width]), vec(mu[2 * width:3 * width]), vec(mul),
      vec(w0), vec(a0), vec(k_k), vec(k_a),
      rows_of(w2, 0), rows_of(a2, dl), rows_of(g2, dl + al), hsum)


def _dot(a, b, dims, exact):
    if exact:
        return lax.dot_general(a, b, (dims, ((), ())), precision=lax.Precision.HIGHEST,
                               preferred_element_type=F32)
    return lax.dot_general(a.astype(BF16), b.astype(BF16), (dims, ((), ())), preferred_element_type=F32)


def _prefix_sums(tri, x):
    hi = x.astype(BF16)
    r1 = x - hi.astype(F32)
    mid = r1.astype(BF16)
    lo = (r1 - mid.astype(F32)).astype(BF16)
    return (jnp.dot(tri, hi, preferred_element_type=F32) + jnp.dot(tri, mid, preferred_element_type=F32)
            + jnp.dot(tri, lo, preferred_element_type=F32))


_NN = ((1,), (0,))
_NT = ((1,), (1,))
_TN = ((0,), (0,))


def _wkv_body(r_ref, lw_ref, k_ref, v_ref, a_ref, b_ref, g_ref, rk_ref, lng_ref, lnb_ref, y_ref, s_ref,
              *, nh, n, c, exact):
    @pl.when(pl.program_id(1) == 0)
    def _():
        s_ref[...] = jnp.zeros(s_ref.shape, F32)

    row = lax.broadcasted_iota(I32, (c, c), 0)
    col = lax.broadcasted_iota(I32, (c, c), 1)
    incl = row >= col
    strict = row > col
    tri_bf = jnp.where(incl, 1.0, 0.0).astype(BF16)
    eye = jnp.where(row == col, 1.0, 0.0).astype(F32)
    nsq = int(math.log2(c)) - 1

    hs = range(nh)
    dot = functools.partial(_dot, exact=exact)
    heads = lambda ref: [ref[:, h * n:(h + 1) * n] for h in hs]
    lw, r, k, v, a, b = (heads(ref) for ref in (lw_ref, r_ref, k_ref, v_ref, a_ref, b_ref))
    cum = [_prefix_sums(tri_bf, lw[h]) for h in hs]
    tot = [cum[h][c - 1:c, :] for h in hs]
    pinv = [jnp.exp(-cum[h]) for h in hs]
    pend = [jnp.exp(tot[h] - cum[h]) for h in hs]
    at = [a[h] * jnp.exp(cum[h] - lw[h]) for h in hs]
    rt = [r[h] * jnp.exp(cum[h]) for h in hs]
    bt = [b[h] * pinv[h] for h in hs]
    kt = [k[h] * pinv[h] for h in hs]
    a_ab = [jnp.where(strict, dot(at[h], bt[h], _NT), 0.0) for h in hs]
    a_ak = [jnp.where(strict, dot(at[h], kt[h], _NT), 0.0) for h in hs]
    a_rb = [jnp.where(incl, dot(rt[h], bt[h], _NT), 0.0) for h in hs]
    a_rk = [jnp.where(incl, dot(rt[h], kt[h], _NT), 0.0) for h in hs]
    x = a_ab
    minv = [eye + x[h] for h in hs]
    for _ in range(nsq):
        x = [dot(x[h], x[h], _NN) for h in hs]
        minv = [minv[h] + dot(minv[h], x[h], _NN) for h in hs]
    s0 = [s_ref[h] for h in hs]
    rhs = [dot(at[h], s0[h], _NT) + dot(a_ak[h], v[h], _NN) for h in hs]
    u = [dot(minv[h], rhs[h], _NN) for h in hs]
    y = [dot(rt[h], s0[h], _NT) + dot(a_rb[h], u[h], _NN) + dot(a_rk[h], v[h], _NN) for h in hs]
    for h in hs:
        s_ref[h] = (s0[h] * jnp.exp(tot[h]) + dot(u[h], b[h] * pend[h], _TN)
                    + dot(v[h], k[h] * pend[h], _TN))
    ones = jnp.ones((n, n), BF16)
    cols = [slice(h * n, (h + 1) * n) for h in hs]
    mean = [_split_dot(y[h], ones) * (1.0 / n) for h in hs]
    d = [y[h] - mean[h] for h in hs]
    var = [_split_dot(d[h] * d[h], ones) * (1.0 / n) for h in hs]
    rk = [_split_dot(r[h] * k[h] * rk_ref[:, cols[h]], ones) for h in hs]
    for h in hs:
        yn = d[h] * lax.rsqrt(var[h] + LNX_EPS)
        y_ref[:, cols[h]] = (yn * lng_ref[:, cols[h]] + lnb_ref[:, cols[h]] + rk[h] * v[h]) * g_ref[:, cols[h]]


def wkv7(r, lw, k, v, a, b, g, r_k, lnx_g, lnx_b, n=B_HEAD, exact=False):
    bsz, t, width = r.shape
    nh = width // n
    c = min(WKV_CHUNK, t)
    spec = pl.BlockSpec((None, c, width), lambda bi, ci: (bi, ci, 0))
    par = pl.BlockSpec((1, width), lambda bi, ci: (0, 0))
    vec = lambda x: x.reshape(1, width).astype(F32)
    return pl.pallas_call(
        functools.partial(_wkv_body, nh=nh, n=n, c=c, exact=exact),
        grid=(bsz, t // c),
        in_specs=[spec] * 7 + [par] * 3,
        out_specs=spec,
        out_shape=jax.ShapeDtypeStruct((bsz, t, width), F32),
        scratch_shapes=[pltpu.VMEM((nh, n, n), F32)],
        compiler_params=_params(("parallel", "arbitrary")),
        name="wkv7_chunked",
    )(r, lw, k, v, a, b, g, vec(r_k), vec(lnx_g), vec(lnx_b))


def _dsa_index_body(ki_ref, qit_ref, wit_ref, bias_ref, key_ref, dig_ref, cut_ref, *, tq, tkc, topk, nheads, t):
    i = pl.program_id(1)
    nch = ((i + 1) * tq + tkc - 1) // tkc
    qpos = i * tq + lax.broadcasted_iota(I32, (1, tq), 1)

    def kpos_of(ci):
        return ci * tkc + lax.broadcasted_iota(I32, (tkc, 1), 0)

    def rows(ci):
        return pl.ds(pl.multiple_of(ci * tkc, tkc), tkc)

    def score_chunk(ci, carry):
        kc = ki_ref[rows(ci), :]
        acc = jnp.zeros((tkc, tq), F32)
        for h in range(nheads):
            s = jnp.dot(kc, qit_ref[h], preferred_element_type=F32)
            acc = acc + jnp.maximum(s, 0.0) * wit_ref[h]
        acc = jnp.where(kpos_of(ci) <= qpos, acc, -jnp.inf)
        bits = lax.bitcast_convert_type(acc, I32)
        key_ref[rows(ci), :] = jnp.where(bits >= 0, bits, bits ^ 0x7FFFFFFF)
        return carry

    lax.fori_loop(0, nch, score_chunk, 0)

    def count(pred):
        def body(ci, cnt):
            ones = pred(key_ref[rows(ci), :], kpos_of(ci))
            return cnt + jnp.sum(ones, axis=0, keepdims=True)
        return lax.fori_loop(0, nch, body, jnp.zeros((1, tq), I32))

    fold = 64
    assert (t // tkc) * (tkc // fold) <= 256

    def count_digit_ge(cand):
        cand_bf = cand.astype(F32).astype(BF16)

        def body(ci, acc):
            hit = jnp.where(dig_ref[rows(ci), :] >= cand_bf, jnp.ones((), BF16), jnp.zeros((), BF16))
            for s in range(tkc // fold):
                acc = acc + hit[s * fold:(s + 1) * fold]
            return acc

        acc = lax.fori_loop(0, nch, body, jnp.zeros((fold, tq), BF16))
        return jnp.sum(acc.astype(F32), axis=0, keepdims=True).astype(I32)

    prefix = jnp.zeros((1, tq), I32)
    above = jnp.zeros((1, tq), I32)
    n_ge = above
    for level in (3, 2, 1, 0):
        shift = 8 * level

        def build(ci, carry, shift=shift, level=level, prefix=prefix):
            u = key_ref[rows(ci), :] ^ INT_MIN
            digit = lax.shift_right_logical(u, jnp.int32(shift)) & 255
            if level < 3:
                same = lax.shift_right_logical(u, jnp.int32(shift + 8)) == lax.shift_right_logical(
                    prefix, jnp.int32(shift + 8))
                digit = jnp.where(same, digit, -1)
            dig_ref[rows(ci), :] = digit.astype(F32).astype(BF16)
            return carry

        lax.fori_loop(0, nch, build, 0)

        def digit_bit(step, d, above=above):
            cand = d | lax.shift_left(jnp.int32(1), 7 - step)
            return jnp.where(above + count_digit_ge(cand) >= topk, cand, d)

        d = lax.fori_loop(0, 8, digit_bit, jnp.zeros((1, tq), I32))
        if level == 0:
            n_ge = above + count_digit_ge(d)
        above = above + count_digit_ge(d + 1)
        prefix = prefix | lax.shift_left(d, jnp.int32(shift))

    thr = prefix ^ INT_MIN
    n_gt = above
    need = topk - n_gt
    cut_ref[...] = jnp.full((1, tq), t, I32)

    @pl.when(jnp.max(n_ge) > topk)
    def _():
        nbits = max(1, (t - 1).bit_length())

        def index_bit(step, x):
            cand = x | lax.shift_left(jnp.int32(1), nbits - 1 - step)
            g = count(lambda key, kpos: jnp.where(key == thr, jnp.where(kpos < cand, 1, 0), 0))
            return jnp.where(g < need, cand, x)

        cut_ref[...] = lax.fori_loop(0, nbits, index_bit, jnp.zeros((1, tq), I32))

    cutoff = cut_ref[...]

    def write_chunk(ci, carry):
        key = key_ref[rows(ci), :]
        kpos = kpos_of(ci)
        tie = jnp.where(kpos <= cutoff, 0.0, NEG)
        sel = jnp.where(key > thr, 0.0, jnp.where(key == thr, tie, NEG))
        bias_ref[rows(ci), :] = jnp.where(kpos <= qpos, sel, NEG).astype(BF16)
        return carry

    lax.fori_loop(0, nch, write_chunk, 0)

    def fill_chunk(ci, carry):
        bias_ref[rows(ci), :] = jnp.full((tkc, tq), NEG, BF16)
        return carry

    lax.fori_loop(nch, t // tkc, fill_chunk, 0)


def dsa_select(ki, qi, wi, topk):
    b, t, nh, d = qi.shape
    tq = _pick(t, (256, 128))
    tkc = _pick(t, (512, 256, 128))
    nq = t // tq
    qit = jnp.transpose(qi.astype(BF16).reshape(b, nq, tq, nh, d), (0, 1, 3, 4, 2))
    wit = jnp.transpose(wi.astype(F32).reshape(b, nq, tq, nh), (0, 1, 3, 2))[:, :, :, None, :]
    return pl.pallas_call(
        functools.partial(_dsa_index_body, tq=tq, tkc=tkc, topk=topk, nheads=nh, t=t),
        grid=(b, nq),
        in_specs=[pl.BlockSpec((None, t, d), lambda bi, qi_: (bi, 0, 0)),
                  pl.BlockSpec((None, None, nh, d, tq), lambda bi, qi_: (bi, qi_, 0, 0, 0)),
                  pl.BlockSpec((None, None, nh, 1, tq), lambda bi, qi_: (bi, qi_, 0, 0, 0))],
        out_specs=pl.BlockSpec((None, t, tq), lambda bi, qi_: (bi, 0, qi_)),
        out_shape=jax.ShapeDtypeStruct((b, t, t), BF16),
        scratch_shapes=[pltpu.VMEM((t, tq), I32), pltpu.VMEM((t, tq), BF16), pltpu.VMEM((1, tq), I32)],
        compiler_params=_params(("parallel", "arbitrary")),
        name="dsa_index_topk",
    )(ki.astype(BF16), qit, wit)


def _dsa_attn_body(k_ref, vt_ref, q_ref, bias_ref, o_ref, m_ref, acc_ref, pa_ref, aa_ref, pb_ref, ab_ref,
                   *, tq, tk, rep, d):
    i = pl.program_id(2)
    nch = ((i + 1) * tq + tk - 1) // tk
    m_ref[...] = jnp.full(m_ref.shape, NEG, F32)
    acc_ref[...] = jnp.zeros(acc_ref.shape, F32)
    buf_a = (pa_ref, aa_ref)
    buf_b = (pb_ref, ab_ref)

    def scores(c, buf):
        rows = pl.ds(pl.multiple_of(c * tk, tk), tk)
        kj = k_ref[rows, :]
        bj = bias_ref[rows, :]
        fns = [lambda r=r: lax.dot_general(kj, q_ref[:, r * d:(r + 1) * d], (_NT, ((), ())),
                                           preferred_element_type=F32).astype(BF16) + bj
               for r in range(rep)]
        _score_stage(fns, m_ref, *buf)

    def values(c, buf):
        _value_stage(vt_ref[c], acc_ref, *buf)

    scores(0, buf_a)

    def pair(t, carry):
        c = 2 * t + 1
        scores(c, buf_b)
        values(c - 1, buf_a)
        scores(c + 1, buf_a)
        values(c, buf_b)
        return carry

    lax.fori_loop(0, (nch - 1) // 2, pair, 0)
    last = nch - 1

    @pl.when(last % 2 == 1)
    def _():
        scores(last, buf_b)
        values(last - 1, buf_a)
        values(last, buf_b)

    @pl.when(last % 2 == 0)
    def _():
        values(last, buf_a)

    for r in range(rep):
        acc = acc_ref[r]
        o_ref[:, r * d:(r + 1) * d] = (acc[:d] / acc[d:d + 1]).T.astype(o_ref.dtype)


def dsa_attention(q, k, v, bias):
    b, t, qw = q.shape
    d = C_HEAD_DIM
    g = k.shape[2] // d
    rep = qw // (g * d)
    tq = _pick(t, (256, 128))
    tk = _pick(t, (512, 256, 128))
    nq, nk = t // tq, t // tk
    vt = jnp.transpose(v.astype(BF16).reshape(b, nk, tk, g, d), (0, 3, 1, 4, 2))
    vt = _with_ones_rows(vt)
    dve = d + ONES_ROWS
    group = pl.BlockSpec((None, tq, rep * d), lambda bi, gi, qi: (bi, qi, gi))
    return pl.pallas_call(
        functools.partial(_dsa_attn_body, tq=tq, tk=tk, rep=rep, d=d),
        grid=(b, g, nq),
        in_specs=[pl.BlockSpec((None, t, d), lambda bi, gi, qi: (bi, 0, gi)),
                  pl.BlockSpec((None, None, nk, dve, tk), lambda bi, gi, qi: (bi, gi, 0, 0, 0)),
                  group,
                  pl.BlockSpec((None, t, tq), lambda bi, gi, qi: (bi, 0, qi))],
        out_specs=group,
        out_shape=jax.ShapeDtypeStruct((b, t, qw), BF16),
        scratch_shapes=[pltpu.VMEM((rep, 1, tq), F32), pltpu.VMEM((rep, dve, tq), F32)]
        + [pltpu.VMEM((rep, tk, tq), BF16), pltpu.VMEM((rep, 1, tq), F32)] * 2,
        compiler_params=_params(("parallel", "parallel", "arbitrary")),
        name="dsa_attention",
    )(k.astype(BF16), vt, q.astype(BF16), bias)


LANES = 128


def _rope_tables(pos, head_dim, scale=1.0):
    rot = head_dim // ROPE_FRACTION
    half = rot // 2
    inv_freq = ROPE_THETA ** (-(jnp.arange(half, dtype=F32) * 2.0 / rot))
    ang = pos.astype(F32).reshape(-1, 1) * inv_freq
    cos, sin = jnp.cos(ang), jnp.sin(ang)
    zero = jnp.zeros_like(sin)
    rest = jnp.zeros((ang.shape[0], head_dim - rot), F32)
    tabs = (jnp.concatenate([cos, cos, rest + 1.0], axis=-1),
            jnp.concatenate([-sin, zero, rest], axis=-1),
            jnp.concatenate([zero, sin, rest], axis=-1))
    return tuple(jnp.tile(x * scale, (1, LANES // head_dim)) for x in tabs)


def _rope_body(z_ref, c_ref, up_ref, down_ref, o_ref, *, half, ncols):
    c, up, down = c_ref[...], up_ref[...], down_ref[...]
    for cb in range(ncols // LANES):
        cols = slice(cb * LANES, (cb + 1) * LANES)
        x = z_ref[:, cols]
        y = x * c + pltpu.roll(x, LANES - half, 1) * up + pltpu.roll(x, half, 1) * down
        o_ref[:, cols] = y.astype(o_ref.dtype)


def rope_cast(z, col0, ncols, head_dim, tables):
    m = z.shape[0]
    assert col0 % ncols == 0 and ncols % LANES == 0 and LANES % head_dim == 0
    tm = _pick(m, (512, 256, 128, 64, 32, 16, 8))
    tab = pl.BlockSpec((tm, LANES), lambda i: (i, 0))
    return pl.pallas_call(
        functools.partial(_rope_body, half=head_dim // ROPE_FRACTION // 2, ncols=ncols),
        grid=(m // tm,),
        in_specs=[pl.BlockSpec((tm, ncols), lambda i: (i, col0 // ncols)), tab, tab, tab],
        out_specs=pl.BlockSpec((tm, ncols), lambda i: (i, 0)),
        out_shape=jax.ShapeDtypeStruct((m, ncols), BF16),
        compiler_params=_params(("parallel",)),
        name="rope_cast",
    )(z, *tables)


def _rms_norm(x, g, eps=NORM_EPS):
    xf = x.astype(F32)
    return xf * lax.rsqrt(jnp.mean(xf * xf, axis=-1, keepdims=True) + eps) * g.astype(F32)


def _layer_norm(x, g, b, eps=LN_EPS):
    mu = jnp.mean(x, axis=-1, keepdims=True)
    var = jnp.mean(jnp.square(x - mu), axis=-1, keepdims=True)
    return (x - mu) * lax.rsqrt(var + eps) * g + b


def _partial_rope(x, pos):
    dh = x.shape[-1]
    rot = dh // ROPE_FRACTION
    half = rot // 2
    inv_freq = ROPE_THETA ** (-(jnp.arange(half, dtype=F32) * 2.0 / rot))
    ang = pos.astype(F32)[..., None] * inv_freq
    cos = jnp.cos(ang)[:, :, None, :]
    sin = jnp.sin(ang)[:, :, None, :]
    x1 = x[..., :half]
    x2 = x[..., half:rot]
    return jnp.concatenate([x1 * cos - x2 * sin, x2 * cos + x1 * sin, x[..., rot:]], axis=-1)


def _pad_cols(w, n):
    return jnp.pad(w, ((0, 0), (0, n - w.shape[1])))


def _round_up(n, m):
    return (n + m - 1) // m * m


def _mixer_ab(u, pos, w_in, lq1, lk1, lq2, lk2, subln_g, mu, w0, w2, a0, a2, g2,
              k_k, k_a, r_k, lnx_g, lnx_b, lambda_init):
    b, t, d = u.shape
    a_width = d // 2
    b_width = d // 2
    a_in = 3 * a_width
    n_in = w_in.shape[1]
    z = matmul(u.reshape(b * t, d), _pad_cols(w_in, _round_up(n_in, 512)))

    seq = lambda x: x.reshape(b, t, a_width)
    qa = rope_cast(z, 0, a_width, A_DC, _rope_tables(pos, A_DC, A_DC ** -0.5 * LOG2E))
    ka = rope_cast(z, a_width, a_width, A_DC, _rope_tables(pos, A_DC))
    va = z[:, 2 * a_width:a_in].astype(BF16)
    lam = jnp.exp(jnp.sum(lq1 * lk1)) - jnp.exp(jnp.sum(lq2 * lk2)) + lambda_init
    ya = diff_attention(seq(qa), seq(ka), seq(va), lam, subln_g * (1.0 - lambda_init))

    prep = rwkv_prep(z, a_in, b_width, n_in - a_in - 3 * b_width, t, mu, w0, w2, a0, a2, g2, k_k, k_a)
    yb = wkv7(*(x.reshape(b, t, b_width) for x in prep), r_k.reshape(-1), lnx_g, lnx_b)
    return jnp.concatenate([ya, yb.astype(BF16)], axis=-1)


def _mixer_c(u, pos, w_in, idx_k_g, idx_k_b, topk):
    b, t, d = u.shape
    c_heads = d // C_HEAD_DIM
    n_in = w_in.shape[1]
    c1 = c_heads * C_HEAD_DIM
    c2 = c1 + C_KV_HEADS * C_HEAD_DIM
    c3 = c2 + C_KV_HEADS * C_HEAD_DIM
    c4 = c3 + IDX_HEADS * IDX_DIM
    c5 = c4 + IDX_DIM
    z = matmul(u.reshape(b * t, d), _pad_cols(w_in, _round_up(n_in, 512)))
    seq = lambda x: x.reshape(b, t, x.shape[-1])
    tabs = _rope_tables(pos, C_HEAD_DIM)
    q = rope_cast(z, 0, c1, C_HEAD_DIM, _rope_tables(pos, C_HEAD_DIM, C_HEAD_DIM ** -0.5 * LOG2E))
    k = rope_cast(z, c1, c2 - c1, C_HEAD_DIM, tabs)
    v = z[:, c2:c3].astype(BF16)
    qi = rope_cast(z, c3, c4 - c3, IDX_DIM, _rope_tables(pos, IDX_DIM))
    ki = _partial_rope(_layer_norm(seq(z[:, c4:c5]), idx_k_g, idx_k_b)[:, :, None, :], pos)[:, :, 0]
    wi = seq(z[:, c5:n_in]) * ((IDX_HEADS * IDX_DIM) ** -0.5)
    bias = dsa_select(ki, qi.reshape(b, t, IDX_HEADS, IDX_DIM), wi, topk)
    return dsa_attention(seq(q), seq(k), seq(v), bias)


def kernel(x, p, positions, mix_pre_g, mix_post_g, mlp_pre_g, mlp_post_g, w_mlp_up, w_mlp_down, w_ple_proj, w_ple_gate, ple_post_g, ab_w_in, ab_w_out, diff_lq1, diff_lk1, diff_lq2, diff_lk2, diff_subln_g, rwkv_mu, rwkv_w0, rwkv_w2, rwkv_a0, rwkv_a2, rwkv_g2, rwkv_k_k, rwkv_k_a, rwkv_r_k, rwkv_lnx_g, rwkv_lnx_b, c_w_in, c_w_out, idx_k_g, idx_k_b):
    depth = mix_pre_g.shape[0]
    b, t, d = x.shape
    topk = min(TOPK_MAX, t // 4)
    flat = lambda z: z.reshape(b * t, z.shape[-1])
    h = flat(x)
    u = _rms_norm(h, mix_pre_g[0]).astype(BF16)
    for i in range(depth):
        j = i // 2
        u3 = u.reshape(b, t, d)
        if i % 2 == 0:
            lambda_init = 0.8 - 0.6 * math.exp(-0.3 * i)
            m = _mixer_ab(u3, positions, ab_w_in[j], diff_lq1[j], diff_lk1[j], diff_lq2[j],
                          diff_lk2[j], diff_subln_g[j], rwkv_mu[j], rwkv_w0[j], rwkv_w2[j], rwkv_a0[j],
                          rwkv_a2[j], rwkv_g2[j], rwkv_k_k[j], rwkv_k_a[j], rwkv_r_k[j], rwkv_lnx_g[j],
                          rwkv_lnx_b[j], lambda_init)
            w_out = ab_w_out[j]
        else:
            m = _mixer_c(u3, positions, c_w_in[j], idx_k_g[j], idx_k_b[j], topk)
            w_out = c_w_out[j]
        h, u = proj_residual(flat(m), w_out, h, mix_post_g[i], mlp_pre_g[i])
        up = matmul(u, w_mlp_up[i], out_dtype=BF16, act="relu2")
        h, _ = proj_residual(up, w_mlp_down[i], h, mlp_post_g[i], None)
        g_next = mix_pre_g[i + 1] if i + 1 < depth else None
        h, u = ple_residual(flat(p[i]), w_ple_proj[i], w_ple_gate[i], h, ple_post_g[i], g_next)
    return h.reshape(b, t, d)
```

```python
import functools
import math

import jax
import jax.numpy as jnp
from jax import lax
from jax.experimental import pallas as pl
from jax.experimental.pallas import tpu as pltpu

F32 = jnp.float32
BF16 = jnp.bfloat16
I32 = jnp.int32

V7X_VMEM_LIMIT_BYTES = 56 * 1024 * 1024
NEG = -1e30
LOG2E = 1.4426950408889634
SUBLANES = 8
INT_MIN = -2147483648

NORM_EPS = 1e-6
ROPE_THETA = 500000.0
ROPE_FRACTION = 4
A_DV = 128
A_DC = 64
DIFF_SUBLN_EPS = 1e-5
B_HEAD = 64
LNX_EPS = 64e-5
C_HEAD_DIM = 128
C_KV_HEADS = 4
IDX_HEADS = 16
IDX_DIM = 64
TOPK_MAX = 256
LN_EPS = 1e-6
WKV_CHUNK = 64


def _params(sem):
    return pltpu.CompilerParams(dimension_semantics=sem, vmem_limit_bytes=V7X_VMEM_LIMIT_BYTES)


def _pick(n, prefs):
    for t in prefs:
        if n % t == 0:
            return t
    return n


def _mm_body(a_ref, w_ref, o_ref, *scratch, nk, act):
    def epilogue(acc):
        if act == "relu2":
            r = jnp.maximum(acc, 0.0)
            acc = r * r
        return acc.astype(o_ref.dtype)

    if nk == 1:
        o_ref[...] = epilogue(jnp.dot(a_ref[...], w_ref[...], preferred_element_type=F32))
        return
    (acc_ref,) = scratch
    k = pl.program_id(2)

    @pl.when(k == 0)
    def _():
        acc_ref[...] = jnp.zeros_like(acc_ref)

    acc_ref[...] += jnp.dot(a_ref[...], w_ref[...], preferred_element_type=F32)

    @pl.when(k == nk - 1)
    def _():
        o_ref[...] = epilogue(acc_ref[...])


def matmul(a, w, out_dtype=F32, act=None):
    a = a.astype(BF16)
    w = w.astype(BF16)
    m, kdim = a.shape
    n = w.shape[1]
    tn = _pick(n, (2048, 1024, 512, 256, 128))
    tile_elems = (4 << 20) // jnp.dtype(out_dtype).itemsize
    tm = _pick(m, tuple(r for r in (2048, 1024, 512, 256, 128, 64, 32, 16, 8) if r * tn <= tile_elems))
    tk = kdim if kdim <= 2048 else _pick(kdim, (2048, 1024, 512))
    nk = kdim // tk
    scratch = [pltpu.VMEM((tm, tn), F32)] if nk > 1 else []
    return pl.pallas_call(
        functools.partial(_mm_body, nk=nk, act=act),
        grid=(m // tm, n // tn, nk),
        in_specs=[pl.BlockSpec((tm, tk), lambda i, j, k: (i, k)),
                  pl.BlockSpec((tk, tn), lambda i, j, k: (k, j))],
        out_specs=pl.BlockSpec((tm, tn), lambda i, j, k: (i, j)),
        out_shape=jax.ShapeDtypeStruct((m, n), out_dtype),
        scratch_shapes=scratch,
        compiler_params=_params(("parallel", "parallel", "arbitrary")),
        name="dense_matmul",
    )(a, w)


def _rms_rows(x, g):
    return x * lax.rsqrt(jnp.mean(x * x, axis=-1, keepdims=True) + NORM_EPS) * g


def _residual_update(m, h_ref, gpost_ref, gnext_ref, ho_ref, uo_ref):
    h_new = h_ref[...] + _rms_rows(m, gpost_ref[...])
    ho_ref[...] = h_new
    if uo_ref is not None:
        uo_ref[...] = _rms_rows(h_new, gnext_ref[...]).astype(BF16)


def _proj_res_body(*refs, nk, has_next):
    if has_next:
        a_ref, w_ref, h_ref, gpost_ref, gnext_ref, ho_ref, uo_ref = refs
    else:
        a_ref, w_ref, h_ref, gpost_ref, ho_ref = refs
        gnext_ref = uo_ref = None
    part = jnp.dot(a_ref[...], w_ref[...], preferred_element_type=F32)
    if nk == 1:
        _residual_update(part, h_ref, gpost_ref, gnext_ref, ho_ref, uo_ref)
        return
    k = pl.program_id(1)

    @pl.when(k == 0)
    def _():
        ho_ref[...] = part

    @pl.when(jnp.logical_and(k > 0, k < nk - 1))
    def _():
        ho_ref[...] += part

    @pl.when(k == nk - 1)
    def _():
        _residual_update(ho_ref[...] + part, h_ref, gpost_ref, gnext_ref, ho_ref, uo_ref)


def proj_residual(a, w, h, g_post, g_next):
    m, kdim = a.shape
    n = w.shape[1]
    tm = _pick(m, (512, 256, 128, 64, 32, 16, 8))
    tk = _pick(kdim, (2048, 1024, 512, 256, 128))
    nk = kdim // tk
    has_next = g_next is not None
    row = pl.BlockSpec((tm, n), lambda i, k: (i, 0))
    vec = pl.BlockSpec((1, n), lambda i, k: (0, 0))
    in_specs = [pl.BlockSpec((tm, tk), lambda i, k: (i, k)), pl.BlockSpec((tk, n), lambda i, k: (k, 0)), row, vec]
    args = [a.astype(BF16), w.astype(BF16), h, g_post.reshape(1, n).astype(F32)]
    out_shape = [jax.ShapeDtypeStruct((m, n), F32)]
    out_specs = [row]
    if has_next:
        in_specs.append(vec)
        args.append(g_next.reshape(1, n).astype(F32))
        out_shape.append(jax.ShapeDtypeStruct((m, n), BF16))
        out_specs.append(row)
    out = pl.pallas_call(
        functools.partial(_proj_res_body, nk=nk, has_next=has_next),
        grid=(m // tm, nk),
        in_specs=in_specs, out_specs=out_specs, out_shape=out_shape,
        compiler_params=_params(("parallel", "arbitrary")),
        name="proj_residual",
    )(*args)
    return (out[0], out[1]) if has_next else (out[0], None)


def _ple_body(*refs, has_next):
    if has_next:
        p_ref, wp_ref, wg_ref, h_ref, gpost_ref, gnext_ref, ho_ref, uo_ref = refs
    else:
        p_ref, wp_ref, wg_ref, h_ref, gpost_ref, ho_ref = refs
        gnext_ref = uo_ref = None
    e = jnp.dot(p_ref[...], wp_ref[...], preferred_element_type=F32)
    gate = jax.nn.sigmoid(jnp.dot(h_ref[...].astype(BF16), wg_ref[...], preferred_element_type=F32))
    _residual_update(e * gate, h_ref, gpost_ref, gnext_ref, ho_ref, uo_ref)


def ple_residual(p, w_proj, w_gate, h, g_post, g_next):
    m, n = h.shape
    pd = p.shape[1]
    tm = _pick(m, (512, 256, 128, 64, 32, 16, 8))
    has_next = g_next is not None
    row = pl.BlockSpec((tm, n), lambda i: (i, 0))
    vec = pl.BlockSpec((1, n), lambda i: (0, 0))
    once = pl.Buffered(1)
    in_specs = [pl.BlockSpec((tm, pd), lambda i: (i, 0)),
                pl.BlockSpec((pd, n), lambda i: (0, 0), pipeline_mode=once),
                pl.BlockSpec((n, n), lambda i: (0, 0), pipeline_mode=once), row, vec]
    args = [p.astype(BF16), w_proj.astype(BF16), w_gate.astype(BF16), h, g_post.reshape(1, n).astype(F32)]
    out_shape = [jax.ShapeDtypeStruct((m, n), F32)]
    out_specs = [row]
    if has_next:
        in_specs.append(vec)
        args.append(g_next.reshape(1, n).astype(F32))
        out_shape.append(jax.ShapeDtypeStruct((m, n), BF16))
        out_specs.append(row)
    out = pl.pallas_call(
        functools.partial(_ple_body, has_next=has_next),
        grid=(m // tm,),
        in_specs=in_specs, out_specs=out_specs, out_shape=out_shape,
        compiler_params=_params(("parallel",)),
        name="ple_residual",
    )(*args)
    return (out[0], out[1]) if has_next else (out[0], None)


ONES_ROWS = 16


def _with_ones_rows(vt):
    pad = jnp.zeros(vt.shape[:-2] + (ONES_ROWS, vt.shape[-1]), vt.dtype).at[..., 0, :].set(1)
    return jnp.concatenate([vt, pad], axis=-2)


def _score_stage(score_fns, m_ref, p_ref, alpha_ref):
    n = range(len(score_fns))
    ss = [fn() for fn in score_fns]
    m_old = [m_ref[i] for i in n]
    m_new = [jnp.maximum(m_old[i], jnp.max(ss[i], axis=0, keepdims=True).astype(F32)) for i in n]
    for i in n:
        alpha_ref[i] = jnp.exp2(m_old[i] - m_new[i])
        m_ref[i] = m_new[i]
    for i in n:
        p_ref[i] = jnp.exp2(ss[i] - m_new[i].astype(BF16))


def _value_stage(vt, acc_ref, p_ref, alpha_ref):
    n = range(p_ref.shape[0])
    pv = [jnp.dot(vt, p_ref[i], preferred_element_type=F32) for i in n]
    for i in n:
        acc_ref[i] = alpha_ref[i] * acc_ref[i] + pv[i]


def _diff_attn_body(lam_ref, k_ref, q_ref, cmask_ref, vt_ref, g_ref, o_ref, m_ref, acc_ref, qz_ref,
                    pa_ref, aa_ref, pb_ref, ab_ref, *, tq, nsp):
    i = pl.program_id(2)
    tl = tq // nsp
    m_ref[...] = jnp.full(m_ref.shape, NEG, F32)
    acc_ref[...] = jnp.zeros(acc_ref.shape, F32)
    for c in range(2):
        qz_ref[c] = q_ref[...] * cmask_ref[c]

    buf_a = (pa_ref, aa_ref)
    buf_b = (pb_ref, ab_ref)

    def scores(j, buf, masked=False):
        kj = k_ref[pl.ds(pl.multiple_of(j * tq, tq), tq), :]

        def score(c, sp):
            s = lax.dot_general(kj, qz_ref[c, sp * tl:(sp + 1) * tl, :], (_NT, ((), ())),
                                preferred_element_type=F32)
            if masked:
                row = lax.broadcasted_iota(I32, (tq, tl), 0)
                col = lax.broadcasted_iota(I32, (tq, tl), 1)
                s = jnp.where(row <= col + sp * tl, s, NEG)
            return s.astype(BF16)

        _score_stage([functools.partial(score, c, sp) for c in range(2) for sp in range(nsp)], m_ref, *buf)

    def values(j, buf):
        _value_stage(vt_ref[j], acc_ref, *buf)

    @pl.when(i == 0)
    def _():
        scores(0, buf_a, masked=True)
        values(0, buf_a)

    @pl.when(i > 0)
    def _():
        scores(0, buf_a)

        def pair(t, carry):
            c = 2 * t + 1
            scores(c, buf_b)
            values(c - 1, buf_a)
            scores(c + 1, buf_a)
            values(c, buf_b)
            return carry

        lax.fori_loop(0, (i - 1) // 2, pair, 0)

        @pl.when((i - 1) % 2 == 1)
        def _():
            scores(i - 1, buf_b)
            values(i - 2, buf_a)
            scores(i, buf_a, masked=True)
            values(i - 1, buf_b)
            values(i, buf_a)

        @pl.when((i - 1) % 2 == 0)
        def _():
            scores(i, buf_b, masked=True)
            values(i - 1, buf_a)
            values(i, buf_b)

    lam = lam_ref[0]
    for sp in range(nsp):
        a0 = acc_ref[sp]
        a1 = acc_ref[nsp + sp]
        out = (a0[:A_DV] / a0[A_DV:A_DV + 1] - lam * (a1[:A_DV] / a1[A_DV:A_DV + 1])).T
        y = out * lax.rsqrt(jnp.mean(out * out, axis=-1, keepdims=True) + DIFF_SUBLN_EPS) * g_ref[...]
        o_ref[sp * tl:(sp + 1) * tl, :] = y.astype(o_ref.dtype)


def diff_attention(q, k, v, lam, gain):
    b, t, width = q.shape
    h = width // A_DV
    tq = _pick(t, (512, 256, 128))
    nq = t // tq
    vt = jnp.transpose(v.astype(BF16).reshape(b, nq, tq, h, A_DV), (0, 3, 1, 4, 2))
    vt = _with_ones_rows(vt)
    dve = A_DV + ONES_ROWS
    nsp = 2 if tq >= 256 else 1
    lane = jnp.arange(A_DV)
    cmask = jnp.stack([lane < A_DC, lane >= A_DC]).astype(BF16).reshape(2, 1, A_DV)
    head = pl.BlockSpec((None, tq, A_DV), lambda bi, hi, qi: (bi, qi, hi))
    return pl.pallas_call(
        functools.partial(_diff_attn_body, tq=tq, nsp=nsp),
        grid=(b, h, nq),
        in_specs=[pl.BlockSpec(memory_space=pltpu.SMEM),
                  pl.BlockSpec((None, t, A_DV), lambda bi, hi, qi: (bi, 0, hi)),
                  head,
                  pl.BlockSpec((2, 1, A_DV), lambda bi, hi, qi: (0, 0, 0)),
                  pl.BlockSpec((None, None, nq, dve, tq), lambda bi, hi, qi: (bi, hi, 0, 0, 0)),
                  pl.BlockSpec((1, A_DV), lambda bi, hi, qi: (0, 0))],
        out_specs=head,
        out_shape=jax.ShapeDtypeStruct((b, t, width), BF16),
        scratch_shapes=[pltpu.VMEM((2 * nsp, 1, tq // nsp), F32),
                        pltpu.VMEM((2 * nsp, dve, tq // nsp), F32),
                        pltpu.VMEM((2, tq, A_DV), BF16)]
        + [pltpu.VMEM((2 * nsp, tq, tq // nsp), BF16), pltpu.VMEM((2 * nsp, 1, tq // nsp), F32)] * 2,
        compiler_params=_params(("parallel", "parallel", "arbitrary")),
        name="diff_attention",
    )(lam.reshape(1).astype(F32), k.astype(BF16), q.astype(BF16), cmask, vt, gain.reshape(1, A_DV).astype(F32))


def _split_dot(x, w):
    hi = x.astype(BF16)
    lo = (x - hi.astype(F32)).astype(BF16)
    return jnp.dot(hi, w, preferred_element_type=F32) + jnp.dot(lo, w, preferred_element_type=F32)


def _rwkv_prep_body(zr_ref, zk_ref, zv_ref, zl_ref, pr_ref, pk_ref, pv_ref, pl_ref,
                    mur_ref, muk_ref, muv_ref, mul_ref, w0_ref, a0_ref, kk_ref, ka_ref,
                    w2_ref, a2_ref, g2_ref, hsum_ref,
                    r_ref, lw_ref, k_ref, v_ref, an_ref, bn_ref, g_ref, *, tm, seq):
    first = (pl.program_id(0) * tm) % seq == 0

    def shifted(z_ref, p_ref, mu_ref):
        z = z_ref[...]
        last = jnp.where(first, 0.0, p_ref[SUBLANES - 1:SUBLANES, :])
        row = lax.broadcasted_iota(I32, z.shape, 0)
        prev = jnp.where(row == 0, last, pltpu.roll(z, 1, 0))
        return z + (prev - z) * mu_ref[...]

    r = shifted(zr_ref, pr_ref, mur_ref)
    k = shifted(zk_ref, pk_ref, muk_ref)
    v = shifted(zv_ref, pv_ref, muv_ref)
    lo = shifted(zl_ref, pl_ref, mul_ref)
    lw = jnp.dot(jnp.tanh(lo).astype(BF16), w2_ref[...], preferred_element_type=F32)
    la = jnp.dot(lo.astype(BF16), a2_ref[...], preferred_element_type=F32)
    g = jnp.dot(jax.nn.sigmoid(lo).astype(BF16), g2_ref[...], preferred_element_type=F32)
    w = -jax.nn.softplus(-(w0_ref[...] + lw)) - 0.5
    a = jax.nn.sigmoid(a0_ref[...] + la)
    kk = k * kk_ref[...]
    norm = jnp.sqrt(_split_dot(kk * kk, hsum_ref[...]))
    kk = kk / jnp.maximum(norm, 1e-12)
    r_ref[...] = r
    lw_ref[...] = -jnp.exp(w)
    k_ref[...] = k * (1.0 + (a - 1.0) * ka_ref[...])
    v_ref[...] = v
    an_ref[...] = -kk
    bn_ref[...] = kk * a
    g_ref[...] = g


def rwkv_prep(z, col0, width, n_lora, seq, mu, w0, w2, a0, a2, g2, k_k, k_a):
    m = z.shape[0]
    tm = _pick(seq, (256, 128, 64, 32, 16, 8))
    lw_ = _round_up(n_lora, 128)
    assert col0 % width == 0 and (col0 + 3 * width) % lw_ == 0 and col0 + 3 * width + lw_ <= z.shape[1]
    cb = col0 // width
    lb = (col0 + 3 * width) // lw_
    dl, al = w2.shape[0], a2.shape[0]

    def rows_of(w, start):
        return jnp.zeros((lw_, width), BF16).at[start:start + w.shape[0]].set(w.astype(BF16))

    hsum = jnp.kron(jnp.eye(width // B_HEAD, dtype=BF16), jnp.ones((B_HEAD, B_HEAD), BF16))
    vec = lambda x: x.reshape(1, -1).astype(F32)
    mul = jnp.pad(mu[3 * width:], (0, lw_ - n_lora))
    tile = lambda c, wd: pl.BlockSpec((tm, wd), lambda i: (i, c))
    before = lambda c, wd: pl.BlockSpec((SUBLANES, wd), lambda i: (jnp.maximum(i * (tm // SUBLANES) - 1, 0), c))
    par = lambda wd: pl.BlockSpec((1, wd), lambda i: (0, 0))
    mat = lambda r_, c_: pl.BlockSpec((r_, c_), lambda i: (0, 0))
    out = pl.BlockSpec((tm, width), lambda i: (i, 0))
    return pl.pallas_call(
        functools.partial(_rwkv_prep_body, tm=tm, seq=seq),
        grid=(m // tm,),
        in_specs=[tile(cb, width), tile(cb + 1, width), tile(cb + 2, width), tile(lb, lw_),
                  before(cb, width), before(cb + 1, width), before(cb + 2, width), before(lb, lw_),
                  par(width), par(width), par(width), par(lw_), par(width), par(width), par(width), par(width),
                  mat(lw_, width), mat(lw_, width), mat(lw_, width), mat(width, width)],
        out_specs=[out] * 7,
        out_shape=[jax.ShapeDtypeStruct((m, width), F32)] * 7,
        compiler_params=_params(("parallel",)),
        name="rwkv_prep",
    )(z, z, z, z, z, z, z, z,
      vec(mu[:width]), vec(mu[width:2 * width]), vec(mu[2 * width:3 * width]), vec(mul),
      vec(w0), vec(a0), vec(k_k), vec(k_a),
      rows_of(w2, 0), rows_of(a2, dl), rows_of(g2, dl + al), hsum)


def _dot(a, b, dims):
    return lax.dot_general(a.astype(BF16), b.astype(BF16), (dims, ((), ())), preferred_element_type=F32)


def _prefix_sums(tri, x):
    hi = x.astype(BF16)
    r1 = x - hi.astype(F32)
    mid = r1.astype(BF16)
    lo = (r1 - mid.astype(F32)).astype(BF16)
    return (jnp.dot(tri, hi, preferred_element_type=F32) + jnp.dot(tri, mid, preferred_element_type=F32)
            + jnp.dot(tri, lo, preferred_element_type=F32))


_NN = ((1,), (0,))
_NT = ((1,), (1,))
_TN = ((0,), (0,))


def _wkv_body(r_ref, lw_ref, k_ref, v_ref, a_ref, b_ref, g_ref, rk_ref, lng_ref, lnb_ref, y_ref, s_ref,
              *, nh, n, c):
    @pl.when(pl.program_id(1) == 0)
    def _():
        s_ref[...] = jnp.zeros(s_ref.shape, F32)

    row = lax.broadcasted_iota(I32, (c, c), 0)
    col = lax.broadcasted_iota(I32, (c, c), 1)
    incl = row >= col
    strict = row > col
    tri_bf = jnp.where(incl, 1.0, 0.0).astype(BF16)
    eye = jnp.where(row == col, 1.0, 0.0).astype(F32)
    nsq = int(math.log2(c)) - 1

    hs = range(nh)
    dot = _dot
    heads = lambda ref: [ref[:, h * n:(h + 1) * n] for h in hs]
    lw, r, k, v, a, b = (heads(ref) for ref in (lw_ref, r_ref, k_ref, v_ref, a_ref, b_ref))
    cum = [_prefix_sums(tri_bf, lw[h]) for h in hs]
    tot = [cum[h][c - 1:c, :] for h in hs]
    pinv = [jnp.exp(-cum[h]) for h in hs]
    pend = [jnp.exp(tot[h] - cum[h]) for h in hs]
    at = [a[h] * jnp.exp(cum[h] - lw[h]) for h in hs]
    rt = [r[h] * jnp.exp(cum[h]) for h in hs]
    bt = [b[h] * pinv[h] for h in hs]
    kt = [k[h] * pinv[h] for h in hs]
    a_ab = [jnp.where(strict, dot(at[h], bt[h], _NT), 0.0) for h in hs]
    a_ak = [jnp.where(strict, dot(at[h], kt[h], _NT), 0.0) for h in hs]
    a_rb = [jnp.where(incl, dot(rt[h], bt[h], _NT), 0.0) for h in hs]
    a_rk = [jnp.where(incl, dot(rt[h], kt[h], _NT), 0.0) for h in hs]
    x = a_ab
    minv = [eye + x[h] for h in hs]
    for _ in range(nsq):
        x = [dot(x[h], x[h], _NN) for h in hs]
        minv = [minv[h] + dot(minv[h], x[h], _NN) for h in hs]
    s0 = [s_ref[h] for h in hs]
    rhs = [dot(at[h], s0[h], _NT) + dot(a_ak[h], v[h], _NN) for h in hs]
    u = [dot(minv[h], rhs[h], _NN) for h in hs]
    y = [dot(rt[h], s0[h], _NT) + dot(a_rb[h], u[h], _NN) + dot(a_rk[h], v[h], _NN) for h in hs]
    for h in hs:
        s_ref[h] = (s0[h] * jnp.exp(tot[h]) + dot(u[h], b[h] * pend[h], _TN)
                    + dot(v[h], k[h] * pend[h], _TN))
    ones = jnp.ones((n, n), BF16)
    cols = [slice(h * n, (h + 1) * n) for h in hs]
    mean = [_split_dot(y[h], ones) * (1.0 / n) for h in hs]
    d = [y[h] - mean[h] for h in hs]
    var = [_split_dot(d[h] * d[h], ones) * (1.0 / n) for h in hs]
    rk = [_split_dot(r[h] * k[h] * rk_ref[:, cols[h]], ones) for h in hs]
    for h in hs:
        yn = d[h] * lax.rsqrt(var[h] + LNX_EPS)
        y_ref[:, cols[h]] = (yn * lng_ref[:, cols[h]] + lnb_ref[:, cols[h]] + rk[h] * v[h]) * g_ref[:, cols[h]]


def wkv7(r, lw, k, v, a, b, g, r_k, lnx_g, lnx_b, n=B_HEAD):
    bsz, t, width = r.shape
    nh = width // n
    c = min(WKV_CHUNK, t)
    spec = pl.BlockSpec((None, c, width), lambda bi, ci: (bi, ci, 0))
    par = pl.BlockSpec((1, width), lambda bi, ci: (0, 0))
    vec = lambda x: x.reshape(1, width).astype(F32)
    return pl.pallas_call(
        functools.partial(_wkv_body, nh=nh, n=n, c=c),
        grid=(bsz, t // c),
        in_specs=[spec] * 7 + [par] * 3,
        out_specs=spec,
        out_shape=jax.ShapeDtypeStruct((bsz, t, width), F32),
        scratch_shapes=[pltpu.VMEM((nh, n, n), F32)],
        compiler_params=_params(("parallel", "arbitrary")),
        name="wkv7_chunked",
    )(r, lw, k, v, a, b, g, vec(r_k), vec(lnx_g), vec(lnx_b))


def _dsa_index_body(ki_ref, qit_ref, wit_ref, bias_ref, key_ref, dig_ref, cut_ref, *, tq, tkc, topk, nheads, t):
    i = pl.program_id(1)
    nch = ((i + 1) * tq + tkc - 1) // tkc
    qpos = i * tq + lax.broadcasted_iota(I32, (1, tq), 1)

    def kpos_of(ci):
        return ci * tkc + lax.broadcasted_iota(I32, (tkc, 1), 0)

    def rows(ci):
        return pl.ds(pl.multiple_of(ci * tkc, tkc), tkc)

    def score_chunk(ci, carry):
        kc = ki_ref[rows(ci), :]
        acc = jnp.zeros((tkc, tq), F32)
        for h in range(nheads):
            s = jnp.dot(kc, qit_ref[h], preferred_element_type=F32)
            acc = acc + jnp.maximum(s, 0.0) * wit_ref[h]
        acc = jnp.where(kpos_of(ci) <= qpos, acc, -jnp.inf)
        bits = lax.bitcast_convert_type(acc, I32)
        key_ref[rows(ci), :] = jnp.where(bits >= 0, bits, bits ^ 0x7FFFFFFF)
        return carry

    lax.fori_loop(0, nch, score_chunk, 0)

    def count(pred):
        def body(ci, cnt):
            ones = pred(key_ref[rows(ci), :], kpos_of(ci))
            return cnt + jnp.sum(ones, axis=0, keepdims=True)
        return lax.fori_loop(0, nch, body, jnp.zeros((1, tq), I32))

    fold = 64
    assert (t // tkc) * (tkc // fold) <= 256

    def count_digit_ge(cand):
        cand_bf = cand.astype(F32).astype(BF16)

        def body(ci, acc):
            hit = jnp.where(dig_ref[rows(ci), :] >= cand_bf, jnp.ones((), BF16), jnp.zeros((), BF16))
            for s in range(tkc // fold):
                acc = acc + hit[s * fold:(s + 1) * fold]
            return acc

        acc = lax.fori_loop(0, nch, body, jnp.zeros((fold, tq), BF16))
        return jnp.sum(acc.astype(F32), axis=0, keepdims=True).astype(I32)

    prefix = jnp.zeros((1, tq), I32)
    above = jnp.zeros((1, tq), I32)
    n_ge = above
    for level in (3, 2, 1, 0):
        shift = 8 * level

        def build(ci, carry, shift=shift, level=level, prefix=prefix):
            u = key_ref[rows(ci), :] ^ INT_MIN
            digit = lax.shift_right_logical(u, jnp.int32(shift)) & 255
            if level < 3:
                same = lax.shift_right_logical(u, jnp.int32(shift + 8)) == lax.shift_right_logical(
                    prefix, jnp.int32(shift + 8))
                digit = jnp.where(same, digit, -1)
            dig_ref[rows(ci), :] = digit.astype(F32).astype(BF16)
            return carry

        lax.fori_loop(0, nch, build, 0)

        def digit_bit(step, d, above=above):
            cand = d | lax.shift_left(jnp.int32(1), 7 - step)
            return jnp.where(above + count_digit_ge(cand) >= topk, cand, d)

        d = lax.fori_loop(0, 8, digit_bit, jnp.zeros((1, tq), I32))
        if level == 0:
            n_ge = above + count_digit_ge(d)
        above = above + count_digit_ge(d + 1)
        prefix = prefix | lax.shift_left(d, jnp.int32(shift))

    thr = prefix ^ INT_MIN
    n_gt = above
    need = topk - n_gt
    cut_ref[...] = jnp.full((1, tq), t, I32)

    @pl.when(jnp.max(n_ge) > topk)
    def _():
        nbits = max(1, (t - 1).bit_length())

        def index_bit(step, x):
            cand = x | lax.shift_left(jnp.int32(1), nbits - 1 - step)
            g = count(lambda key, kpos: jnp.where(key == thr, jnp.where(kpos < cand, 1, 0), 0))
            return jnp.where(g < need, cand, x)

        cut_ref[...] = lax.fori_loop(0, nbits, index_bit, jnp.zeros((1, tq), I32))

    cutoff = cut_ref[...]

    def write_chunk(ci, carry):
        key = key_ref[rows(ci), :]
        kpos = kpos_of(ci)
        tie = jnp.where(kpos <= cutoff, 0.0, NEG)
        sel = jnp.where(key > thr, 0.0, jnp.where(key == thr, tie, NEG))
        bias_ref[rows(ci), :] = jnp.where(kpos <= qpos, sel, NEG).astype(BF16)
        return carry

    lax.fori_loop(0, nch, write_chunk, 0)

    def fill_chunk(ci, carry):
        bias_ref[rows(ci), :] = jnp.full((tkc, tq), NEG, BF16)
        return carry

    lax.fori_loop(nch, t // tkc, fill_chunk, 0)


def dsa_select(ki, qi, wi, topk):
    b, t, nh, d = qi.shape
    tq = _pick(t, (256, 128))
    tkc = _pick(t, (512, 256, 128))
    nq = t // tq
    qit = jnp.transpose(qi.astype(BF16).reshape(b, nq, tq, nh, d), (0, 1, 3, 4, 2))
    wit = jnp.transpose(wi.astype(F32).reshape(b, nq, tq, nh), (0, 1, 3, 2))[:, :, :, None, :]
    return pl.pallas_call(
        functools.partial(_dsa_index_body, tq=tq, tkc=tkc, topk=topk, nheads=nh, t=t),
        grid=(b, nq),
        in_specs=[pl.BlockSpec((None, t, d), lambda bi, qi_: (bi, 0, 0)),
                  pl.BlockSpec((None, None, nh, d, tq), lambda bi, qi_: (bi, qi_, 0, 0, 0)),
                  pl.BlockSpec((None, None, nh, 1, tq), lambda bi, qi_: (bi, qi_, 0, 0, 0))],
        out_specs=pl.BlockSpec((None, t, tq), lambda bi, qi_: (bi, 0, qi_)),
        out_shape=jax.ShapeDtypeStruct((b, t, t), BF16),
        scratch_shapes=[pltpu.VMEM((t, tq), I32), pltpu.VMEM((t, tq), BF16), pltpu.VMEM((1, tq), I32)],
        compiler_params=_params(("parallel", "arbitrary")),
        name="dsa_index_topk",
    )(ki.astype(BF16), qit, wit)


def _dsa_attn_body(k_ref, vt_ref, q_ref, bias_ref, o_ref, m_ref, acc_ref, pa_ref, aa_ref, pb_ref, ab_ref,
                   *, tq, tk, rep, d):
    i = pl.program_id(2)
    nch = ((i + 1) * tq + tk - 1) // tk
    m_ref[...] = jnp.full(m_ref.shape, NEG, F32)
    acc_ref[...] = jnp.zeros(acc_ref.shape, F32)
    buf_a = (pa_ref, aa_ref)
    buf_b = (pb_ref, ab_ref)

    def scores(c, buf):
        rows = pl.ds(pl.multiple_of(c * tk, tk), tk)
        kj = k_ref[rows, :]
        bj = bias_ref[rows, :]
        fns = [lambda r=r: lax.dot_general(kj, q_ref[:, r * d:(r + 1) * d], (_NT, ((), ())),
                                           preferred_element_type=F32).astype(BF16) + bj
               for r in range(rep)]
        _score_stage(fns, m_ref, *buf)

    def values(c, buf):
        _value_stage(vt_ref[c], acc_ref, *buf)

    scores(0, buf_a)

    def pair(t, carry):
        c = 2 * t + 1
        scores(c, buf_b)
        values(c - 1, buf_a)
        scores(c + 1, buf_a)
        values(c, buf_b)
        return carry

    lax.fori_loop(0, (nch - 1) // 2, pair, 0)
    last = nch - 1

    @pl.when(last % 2 == 1)
    def _():
        scores(last, buf_b)
        values(last - 1, buf_a)
        values(last, buf_b)

    @pl.when(last % 2 == 0)
    def _():
        values(last, buf_a)

    for r in range(rep):
        acc = acc_ref[r]
        o_ref[:, r * d:(r + 1) * d] = (acc[:d] / acc[d:d + 1]).T.astype(o_ref.dtype)


def dsa_attention(q, k, v, bias):
    b, t, qw = q.shape
    d = C_HEAD_DIM
    g = k.shape[2] // d
    rep = qw // (g * d)
    tq = _pick(t, (256, 128))
    tk = _pick(t, (512, 256, 128))
    nq, nk = t // tq, t // tk
    vt = jnp.transpose(v.astype(BF16).reshape(b, nk, tk, g, d), (0, 3, 1, 4, 2))
    vt = _with_ones_rows(vt)
    dve = d + ONES_ROWS
    group = pl.BlockSpec((None, tq, rep * d), lambda bi, gi, qi: (bi, qi, gi))
    return pl.pallas_call(
        functools.partial(_dsa_attn_body, tq=tq, tk=tk, rep=rep, d=d),
        grid=(b, g, nq),
        in_specs=[pl.BlockSpec((None, t, d), lambda bi, gi, qi: (bi, 0, gi)),
                  pl.BlockSpec((None, None, nk, dve, tk), lambda bi, gi, qi: (bi, gi, 0, 0, 0)),
                  group,
                  pl.BlockSpec((None, t, tq), lambda bi, gi, qi: (bi, 0, qi))],
        out_specs=group,
        out_shape=jax.ShapeDtypeStruct((b, t, qw), BF16),
        scratch_shapes=[pltpu.VMEM((rep, 1, tq), F32), pltpu.VMEM((rep, dve, tq), F32)]
        + [pltpu.VMEM((rep, tk, tq), BF16), pltpu.VMEM((rep, 1, tq), F32)] * 2,
        compiler_params=_params(("parallel", "parallel", "arbitrary")),
        name="dsa_attention",
    )(k.astype(BF16), vt, q.astype(BF16), bias)


LANES = 128


def _rope_tables(pos, head_dim, scale=1.0):
    rot = head_dim // ROPE_FRACTION
    half = rot // 2
    inv_freq = ROPE_THETA ** (-(jnp.arange(half, dtype=F32) * 2.0 / rot))
    ang = pos.astype(F32).reshape(-1, 1) * inv_freq
    cos, sin = jnp.cos(ang), jnp.sin(ang)
    zero = jnp.zeros_like(sin)
    rest = jnp.zeros((ang.shape[0], head_dim - rot), F32)
    tabs = (jnp.concatenate([cos, cos, rest + 1.0], axis=-1),
            jnp.concatenate([-sin, zero, rest], axis=-1),
            jnp.concatenate([zero, sin, rest], axis=-1))
    return tuple(jnp.tile(x * scale, (1, LANES // head_dim)) for x in tabs)


def _rope_body(z_ref, c_ref, up_ref, down_ref, o_ref, *, half, ncols):
    c, up, down = c_ref[...], up_ref[...], down_ref[...]
    for cb in range(ncols // LANES):
        cols = slice(cb * LANES, (cb + 1) * LANES)
        x = z_ref[:, cols]
        y = x * c + pltpu.roll(x, LANES - half, 1) * up + pltpu.roll(x, half, 1) * down
        o_ref[:, cols] = y.astype(o_ref.dtype)


def rope_cast(z, col0, ncols, head_dim, tables):
    m = z.shape[0]
    assert col0 % ncols == 0 and ncols % LANES == 0 and LANES % head_dim == 0
    tm = _pick(m, (512, 256, 128, 64, 32, 16, 8))
    tab = pl.BlockSpec((tm, LANES), lambda i: (i, 0))
    return pl.pallas_call(
        functools.partial(_rope_body, half=head_dim // ROPE_FRACTION // 2, ncols=ncols),
        grid=(m // tm,),
        in_specs=[pl.BlockSpec((tm, ncols), lambda i: (i, col0 // ncols)), tab, tab, tab],
        out_specs=pl.BlockSpec((tm, ncols), lambda i: (i, 0)),
        out_shape=jax.ShapeDtypeStruct((m, ncols), BF16),
        compiler_params=_params(("parallel",)),
        name="rope_cast",
    )(z, *tables)


def _rms_norm(x, g, eps=NORM_EPS):
    xf = x.astype(F32)
    return xf * lax.rsqrt(jnp.mean(xf * xf, axis=-1, keepdims=True) + eps) * g.astype(F32)


def _layer_norm(x, g, b, eps=LN_EPS):
    mu = jnp.mean(x, axis=-1, keepdims=True)
    var = jnp.mean(jnp.square(x - mu), axis=-1, keepdims=True)
    return (x - mu) * lax.rsqrt(var + eps) * g + b


def _partial_rope(x, pos):
    dh = x.shape[-1]
    rot = dh // ROPE_FRACTION
    half = rot // 2
    inv_freq = ROPE_THETA ** (-(jnp.arange(half, dtype=F32) * 2.0 / rot))
    ang = pos.astype(F32)[..., None] * inv_freq
    cos = jnp.cos(ang)[:, :, None, :]
    sin = jnp.sin(ang)[:, :, None, :]
    x1 = x[..., :half]
    x2 = x[..., half:rot]
    return jnp.concatenate([x1 * cos - x2 * sin, x2 * cos + x1 * sin, x[..., rot:]], axis=-1)


def _pad_cols(w, n):
    return jnp.pad(w, ((0, 0), (0, n - w.shape[1])))


def _round_up(n, m):
    return (n + m - 1) // m * m


def _mixer_ab(u, pos, w_in, lq1, lk1, lq2, lk2, subln_g, mu, w0, w2, a0, a2, g2,
              k_k, k_a, r_k, lnx_g, lnx_b, lambda_init):
    b, t, d = u.shape
    a_width = d // 2
    b_width = d // 2
    a_in = 3 * a_width
    n_in = w_in.shape[1]
    z = matmul(u.reshape(b * t, d), _pad_cols(w_in, _round_up(n_in, 512)))

    seq = lambda x: x.reshape(b, t, a_width)
    qa = rope_cast(z, 0, a_width, A_DC, _rope_tables(pos, A_DC, A_DC ** -0.5 * LOG2E))
    ka = rope_cast(z, a_width, a_width, A_DC, _rope_tables(pos, A_DC))
    va = z[:, 2 * a_width:a_in].astype(BF16)
    lam = jnp.exp(jnp.sum(lq1 * lk1)) - jnp.exp(jnp.sum(lq2 * lk2)) + lambda_init
    ya = diff_attention(seq(qa), seq(ka), seq(va), lam, subln_g * (1.0 - lambda_init))

    prep = rwkv_prep(z, a_in, b_width, n_in - a_in - 3 * b_width, t, mu, w0, w2, a0, a2, g2, k_k, k_a)
    yb = wkv7(*(x.reshape(b, t, b_width) for x in prep), r_k.reshape(-1), lnx_g, lnx_b)
    return jnp.concatenate([ya, yb.astype(BF16)], axis=-1)


def _mixer_c(u, pos, w_in, idx_k_g, idx_k_b, topk):
    b, t, d = u.shape
    c_heads = d // C_HEAD_DIM
    n_in = w_in.shape[1]
    c1 = c_heads * C_HEAD_DIM
    c2 = c1 + C_KV_HEADS * C_HEAD_DIM
    c3 = c2 + C_KV_HEADS * C_HEAD_DIM
    c4 = c3 + IDX_HEADS * IDX_DIM
    c5 = c4 + IDX_DIM
    z = matmul(u.reshape(b * t, d), _pad_cols(w_in, _round_up(n_in, 512)))
    seq = lambda x: x.reshape(b, t, x.shape[-1])
    tabs = _rope_tables(pos, C_HEAD_DIM)
    q = rope_cast(z, 0, c1, C_HEAD_DIM, _rope_tables(pos, C_HEAD_DIM, C_HEAD_DIM ** -0.5 * LOG2E))
    k = rope_cast(z, c1, c2 - c1, C_HEAD_DIM, tabs)
    v = z[:, c2:c3].astype(BF16)
    qi = rope_cast(z, c3, c4 - c3, IDX_DIM, _rope_tables(pos, IDX_DIM))
    ki = _partial_rope(_layer_norm(seq(z[:, c4:c5]), idx_k_g, idx_k_b)[:, :, None, :], pos)[:, :, 0]
    wi = seq(z[:, c5:n_in]) * ((IDX_HEADS * IDX_DIM) ** -0.5)
    bias = dsa_select(ki, qi.reshape(b, t, IDX_HEADS, IDX_DIM), wi, topk)
    return dsa_attention(seq(q), seq(k), seq(v), bias)


def kernel(x, p, positions, mix_pre_g, mix_post_g, mlp_pre_g, mlp_post_g, w_mlp_up, w_mlp_down, w_ple_proj, w_ple_gate, ple_post_g, ab_w_in, ab_w_out, diff_lq1, diff_lk1, diff_lq2, diff_lk2, diff_subln_g, rwkv_mu, rwkv_w0, rwkv_w2, rwkv_a0, rwkv_a2, rwkv_g2, rwkv_k_k, rwkv_k_a, rwkv_r_k, rwkv_lnx_g, rwkv_lnx_b, c_w_in, c_w_out, idx_k_g, idx_k_b):
    depth = mix_pre_g.shape[0]
    b, t, d = x.shape
    topk = min(TOPK_MAX, t // 4)
    flat = lambda z: z.reshape(b * t, z.shape[-1])
    h = flat(x)
    u = _rms_norm(h, mix_pre_g[0]).astype(BF16)
    for i in range(depth):
        j = i // 2
        u3 = u.reshape(b, t, d)
        if i % 2 == 0:
            lambda_init = 0.8 - 0.6 * math.exp(-0.3 * i)
            m = _mixer_ab(u3, positions, ab_w_in[j], diff_lq1[j], diff_lk1[j], diff_lq2[j],
                          diff_lk2[j], diff_subln_g[j], rwkv_mu[j], rwkv_w0[j], rwkv_w2[j], rwkv_a0[j],
                          rwkv_a2[j], rwkv_g2[j], rwkv_k_k[j], rwkv_k_a[j], rwkv_r_k[j], rwkv_lnx_g[j],
                          rwkv_lnx_b[j], lambda_init)
            w_out = ab_w_out[j]
        else:
            m = _mixer_c(u3, positions, c_w_in[j], idx_k_g[j], idx_k_b[j], topk)
            w_out = c_w_out[j]
        h, u = proj_residual(flat(m), w_out, h, mix_post_g[i], mlp_pre_g[i])
        up = matmul(u, w_mlp_up[i], out_dtype=BF16, act="relu2")
        h, _ = proj_residual(up, w_mlp_down[i], h, mlp_post_g[i], None)
        g_next = mix_pre_g[i + 1] if i + 1 < depth else None
        h, u = ple_residual(flat(p[i]), w_ple_proj[i], w_ple_gate[i], h, ple_post_g[i], g_next)
    return h.reshape(b, t, d)
```

```python
import functools
import math

import jax
import jax.numpy as jnp
from jax import lax
from jax.experimental import pallas as pl
from jax.experimental.pallas import tpu as pltpu

F32 = jnp.float32
BF16 = jnp.bfloat16
I32 = jnp.int32

V7X_VMEM_LIMIT_BYTES = 56 * 1024 * 1024
NEG = -1e30
LOG2E = 1.4426950408889634
SUBLANES = 8
INT_MIN = -2147483648

NORM_EPS = 1e-6
ROPE_THETA = 500000.0
ROPE_FRACTION = 4
A_DV = 128
A_DC = 64
DIFF_SUBLN_EPS = 1e-5
B_HEAD = 64
LNX_EPS = 64e-5
C_HEAD_DIM = 128
C_KV_HEADS = 4
IDX_HEADS = 16
IDX_DIM = 64
TOPK_MAX = 256
LN_EPS = 1e-6
WKV_CHUNK = 64


def _params(sem):
    return pltpu.CompilerParams(dimension_semantics=sem, vmem_limit_bytes=V7X_VMEM_LIMIT_BYTES)


def _pick(n, prefs):
    for t in prefs:
        if n % t == 0:
            return t
    return n


def _mm_body(a_ref, w_ref, o_ref, *scratch, nk, act):
    def epilogue(acc):
        if act == "relu2":
            r = jnp.maximum(acc, 0.0)
            acc = r * r
        return acc.astype(o_ref.dtype)

    if nk == 1:
        o_ref[...] = epilogue(jnp.dot(a_ref[...], w_ref[...], preferred_element_type=F32))
        return
    (acc_ref,) = scratch
    k = pl.program_id(2)

    @pl.when(k == 0)
    def _():
        acc_ref[...] = jnp.zeros_like(acc_ref)

    acc_ref[...] += jnp.dot(a_ref[...], w_ref[...], preferred_element_type=F32)

    @pl.when(k == nk - 1)
    def _():
        o_ref[...] = epilogue(acc_ref[...])


def matmul(a, w, out_dtype=F32, act=None):
    a = a.astype(BF16)
    w = w.astype(BF16)
    m, kdim = a.shape
    n = w.shape[1]
    tn = _pick(n, (2048, 1024, 512, 256, 128))
    tile_elems = (4 << 20) // jnp.dtype(out_dtype).itemsize
    tm = _pick(m, tuple(r for r in (2048, 1024, 512, 256, 128, 64, 32, 16, 8) if r * tn <= tile_elems))
    tk = kdim if kdim <= 2048 else _pick(kdim, (2048, 1024, 512))
    nk = kdim // tk
    scratch = [pltpu.VMEM((tm, tn), F32)] if nk > 1 else []
    return pl.pallas_call(
        functools.partial(_mm_body, nk=nk, act=act),
        grid=(m // tm, n // tn, nk),
        in_specs=[pl.BlockSpec((tm, tk), lambda i, j, k: (i, k)),
                  pl.BlockSpec((tk, tn), lambda i, j, k: (k, j))],
        out_specs=pl.BlockSpec((tm, tn), lambda i, j, k: (i, j)),
        out_shape=jax.ShapeDtypeStruct((m, n), out_dtype),
        scratch_shapes=scratch,
        compiler_params=_params(("parallel", "parallel", "arbitrary")),
        name="dense_matmul",
    )(a, w)


def _rms_rows(x, g):
    return x * lax.rsqrt(jnp.mean(x * x, axis=-1, keepdims=True) + NORM_EPS) * g


def _residual_update(m, h_ref, gpost_ref, gnext_ref, ho_ref, uo_ref):
    h_new = h_ref[...] + _rms_rows(m, gpost_ref[...])
    ho_ref[...] = h_new
    if uo_ref is not None:
        uo_ref[...] = _rms_rows(h_new, gnext_ref[...]).astype(BF16)


def _proj_res_body(*refs, nk, has_next, widths):
    a_refs, refs = refs[:len(widths)], refs[len(widths):]
    if has_next:
        w_ref, h_ref, gpost_ref, gnext_ref, ho_ref, uo_ref = refs
    else:
        w_ref, h_ref, gpost_ref, ho_ref = refs
        gnext_ref = uo_ref = None
    part, lo = None, 0
    for a_ref, width in zip(a_refs, widths):
        rows = slice(None) if len(widths) == 1 else slice(lo, lo + width)
        term = jnp.dot(a_ref[...].astype(BF16), w_ref[rows, :], preferred_element_type=F32)
        part = term if part is None else part + term
        lo += width
    if nk == 1:
        _residual_update(part, h_ref, gpost_ref, gnext_ref, ho_ref, uo_ref)
        return
    k = pl.program_id(1)

    @pl.when(k == 0)
    def _():
        ho_ref[...] = part

    @pl.when(jnp.logical_and(k > 0, k < nk - 1))
    def _():
        ho_ref[...] += part

    @pl.when(k == nk - 1)
    def _():
        _residual_update(ho_ref[...] + part, h_ref, gpost_ref, gnext_ref, ho_ref, uo_ref)


def proj_residual(a_parts, w, h, g_post, g_next):
    m = h.shape[0]
    widths = tuple(a.shape[1] for a in a_parts)
    kdim = sum(widths)
    n = w.shape[1]
    tm = _pick(m, (512, 256, 128, 64, 32, 16, 8))
    tk = _pick(kdim, (2048, 1024, 512, 256, 128))
    nk = kdim // tk
    assert nk == 1 or len(a_parts) == 1
    has_next = g_next is not None
    row = pl.BlockSpec((tm, n), lambda i, k: (i, 0))
    vec = pl.BlockSpec((1, n), lambda i, k: (0, 0))
    if len(a_parts) == 1:
        a_specs = [pl.BlockSpec((tm, tk), lambda i, k: (i, k))]
    else:
        a_specs = [pl.BlockSpec((tm, wd), lambda i, k: (i, 0)) for wd in widths]
    in_specs = a_specs + [pl.BlockSpec((tk, n), lambda i, k: (k, 0)), row, vec]
    args = list(a_parts) + [w.astype(BF16), h, g_post.reshape(1, n).astype(F32)]
    out_shape = [jax.ShapeDtypeStruct((m, n), F32)]
    out_specs = [row]
    if has_next:
        in_specs.append(vec)
        args.append(g_next.reshape(1, n).astype(F32))
        out_shape.append(jax.ShapeDtypeStruct((m, n), BF16))
        out_specs.append(row)
    out = pl.pallas_call(
        functools.partial(_proj_res_body, nk=nk, has_next=has_next, widths=widths),
        grid=(m // tm, nk),
        in_specs=in_specs, out_specs=out_specs, out_shape=out_shape,
        compiler_params=_params(("parallel", "arbitrary")),
        name="proj_residual",
    )(*args)
    return (out[0], out[1]) if has_next else (out[0], None)


def _ple_body(*refs, has_next):
    if has_next:
        p_ref, wp_ref, wg_ref, h_ref, gpost_ref, gnext_ref, ho_ref, uo_ref = refs
    else:
        p_ref, wp_ref, wg_ref, h_ref, gpost_ref, ho_ref = refs
        gnext_ref = uo_ref = None
    e = jnp.dot(p_ref[...], wp_ref[...], preferred_element_type=F32)
    gate = jax.nn.sigmoid(jnp.dot(h_ref[...].astype(BF16), wg_ref[...], preferred_element_type=F32))
    _residual_update(e * gate, h_ref, gpost_ref, gnext_ref, ho_ref, uo_ref)


def ple_residual(p, w_proj, w_gate, h, g_post, g_next):
    m, n = h.shape
    pd = p.shape[1]
    tm = _pick(m, (512, 256, 128, 64, 32, 16, 8))
    has_next = g_next is not None
    row = pl.BlockSpec((tm, n), lambda i: (i, 0))
    vec = pl.BlockSpec((1, n), lambda i: (0, 0))
    once = pl.Buffered(1)
    in_specs = [pl.BlockSpec((tm, pd), lambda i: (i, 0)),
                pl.BlockSpec((pd, n), lambda i: (0, 0), pipeline_mode=once),
                pl.BlockSpec((n, n), lambda i: (0, 0), pipeline_mode=once), row, vec]
    args = [p.astype(BF16), w_proj.astype(BF16), w_gate.astype(BF16), h, g_post.reshape(1, n).astype(F32)]
    out_shape = [jax.ShapeDtypeStruct((m, n), F32)]
    out_specs = [row]
    if has_next:
        in_specs.append(vec)
        args.append(g_next.reshape(1, n).astype(F32))
        out_shape.append(jax.ShapeDtypeStruct((m, n), BF16))
        out_specs.append(row)
    out = pl.pallas_call(
        functools.partial(_ple_body, has_next=has_next),
        grid=(m // tm,),
        in_specs=in_specs, out_specs=out_specs, out_shape=out_shape,
        compiler_params=_params(("parallel",)),
        name="ple_residual",
    )(*args)
    return (out[0], out[1]) if has_next else (out[0], None)


ONES_ROWS = 16


def _with_ones_rows(vt):
    pad = jnp.zeros(vt.shape[:-2] + (ONES_ROWS, vt.shape[-1]), vt.dtype).at[..., 0, :].set(1)
    return jnp.concatenate([vt, pad], axis=-2)


def _score_stage(score_fns, m_ref, p_ref, alpha_ref):
    n = range(len(score_fns))
    ss = [fn() for fn in score_fns]
    m_old = [m_ref[i] for i in n]
    m_new = [jnp.maximum(m_old[i], jnp.max(ss[i], axis=0, keepdims=True).astype(F32)) for i in n]
    for i in n:
        alpha_ref[i] = jnp.exp2(m_old[i] - m_new[i])
        m_ref[i] = m_new[i]
    for i in n:
        p_ref[i] = jnp.exp2(ss[i] - m_new[i].astype(BF16))


def _value_stage(vt, acc_ref, p_ref, alpha_ref):
    n = range(p_ref.shape[0])
    pv = [jnp.dot(vt, p_ref[i], preferred_element_type=F32) for i in n]
    for i in n:
        acc_ref[i] = alpha_ref[i] * acc_ref[i] + pv[i]


def _diff_attn_body(lam_ref, k_ref, q_ref, cmask_ref, vt_ref, g_ref, o_ref, m_ref, acc_ref, qz_ref,
                    pa_ref, aa_ref, pb_ref, ab_ref, *, tq, nsp):
    i = pl.program_id(2)
    tl = tq // nsp
    m_ref[...] = jnp.full(m_ref.shape, NEG, F32)
    acc_ref[...] = jnp.zeros(acc_ref.shape, F32)
    for c in range(2):
        qz_ref[c] = q_ref[...] * cmask_ref[c]

    buf_a = (pa_ref, aa_ref)
    buf_b = (pb_ref, ab_ref)

    def scores(j, buf, masked=False):
        kj = k_ref[pl.ds(pl.multiple_of(j * tq, tq), tq), :]

        def score(c, sp):
            s = lax.dot_general(kj, qz_ref[c, sp * tl:(sp + 1) * tl, :], (_NT, ((), ())),
                                preferred_element_type=F32)
            if masked:
                row = lax.broadcasted_iota(I32, (tq, tl), 0)
                col = lax.broadcasted_iota(I32, (tq, tl), 1)
                s = jnp.where(row <= col + sp * tl, s, NEG)
            return s.astype(BF16)

        _score_stage([functools.partial(score, c, sp) for c in range(2) for sp in range(nsp)], m_ref, *buf)

    def values(j, buf):
        _value_stage(vt_ref[j], acc_ref, *buf)

    @pl.when(i == 0)
    def _():
        scores(0, buf_a, masked=True)
        values(0, buf_a)

    @pl.when(i > 0)
    def _():
        scores(0, buf_a)

        def pair(t, carry):
            c = 2 * t + 1
            scores(c, buf_b)
            values(c - 1, buf_a)
            scores(c + 1, buf_a)
            values(c, buf_b)
            return carry

        lax.fori_loop(0, (i - 1) // 2, pair, 0)

        @pl.when((i - 1) % 2 == 1)
        def _():
            scores(i - 1, buf_b)
            values(i - 2, buf_a)
            scores(i, buf_a, masked=True)
            values(i - 1, buf_b)
            values(i, buf_a)

        @pl.when((i - 1) % 2 == 0)
        def _():
            scores(i, buf_b, masked=True)
            values(i - 1, buf_a)
            values(i, buf_b)

    lam = lam_ref[0]
    for sp in range(nsp):
        a0 = acc_ref[sp]
        a1 = acc_ref[nsp + sp]
        out = (a0[:A_DV] / a0[A_DV:A_DV + 1] - lam * (a1[:A_DV] / a1[A_DV:A_DV + 1])).T
        y = out * lax.rsqrt(jnp.mean(out * out, axis=-1, keepdims=True) + DIFF_SUBLN_EPS) * g_ref[...]
        o_ref[sp * tl:(sp + 1) * tl, :] = y.astype(o_ref.dtype)


def diff_attention(q, k, v, lam, gain):
    b, t, width = q.shape
    h = width // A_DV
    tq = _pick(t, (512, 256, 128))
    nq = t // tq
    vt = jnp.transpose(v.astype(BF16).reshape(b, nq, tq, h, A_DV), (0, 3, 1, 4, 2))
    vt = _with_ones_rows(vt)
    dve = A_DV + ONES_ROWS
    nsp = 2 if tq >= 256 else 1
    lane = jnp.arange(A_DV)
    cmask = jnp.stack([lane < A_DC, lane >= A_DC]).astype(BF16).reshape(2, 1, A_DV)
    head = pl.BlockSpec((None, tq, A_DV), lambda bi, hi, qi: (bi, qi, hi))
    return pl.pallas_call(
        functools.partial(_diff_attn_body, tq=tq, nsp=nsp),
        grid=(b, h, nq),
        in_specs=[pl.BlockSpec(memory_space=pltpu.SMEM),
                  pl.BlockSpec((None, t, A_DV), lambda bi, hi, qi: (bi, 0, hi)),
                  head,
                  pl.BlockSpec((2, 1, A_DV), lambda bi, hi, qi: (0, 0, 0)),
                  pl.BlockSpec((None, None, nq, dve, tq), lambda bi, hi, qi: (bi, hi, 0, 0, 0)),
                  pl.BlockSpec((1, A_DV), lambda bi, hi, qi: (0, 0))],
        out_specs=head,
        out_shape=jax.ShapeDtypeStruct((b, t, width), BF16),
        scratch_shapes=[pltpu.VMEM((2 * nsp, 1, tq // nsp), F32),
                        pltpu.VMEM((2 * nsp, dve, tq // nsp), F32),
                        pltpu.VMEM((2, tq, A_DV), BF16)]
        + [pltpu.VMEM((2 * nsp, tq, tq // nsp), BF16), pltpu.VMEM((2 * nsp, 1, tq // nsp), F32)] * 2,
        compiler_params=_params(("parallel", "parallel", "arbitrary")),
        name="diff_attention",
    )(lam.reshape(1).astype(F32), k.astype(BF16), q.astype(BF16), cmask, vt, gain.reshape(1, A_DV).astype(F32))


def _split_dot(x, w):
    hi = x.astype(BF16)
    lo = (x - hi.astype(F32)).astype(BF16)
    return jnp.dot(hi, w, preferred_element_type=F32) + jnp.dot(lo, w, preferred_element_type=F32)


def _rwkv_prep_body(zr_ref, zk_ref, zv_ref, zl_ref, pr_ref, pk_ref, pv_ref, pl_ref,
                    mur_ref, muk_ref, muv_ref, mul_ref, w0_ref, a0_ref, kk_ref, ka_ref,
                    w2_ref, a2_ref, g2_ref, hsum_ref,
                    r_ref, lw_ref, k_ref, v_ref, an_ref, bn_ref, g_ref, *, tm, seq):
    first = (pl.program_id(0) * tm) % seq == 0

    def shifted(z_ref, p_ref, mu_ref):
        z = z_ref[...]
        last = jnp.where(first, 0.0, p_ref[SUBLANES - 1:SUBLANES, :])
        row = lax.broadcasted_iota(I32, z.shape, 0)
        prev = jnp.where(row == 0, last, pltpu.roll(z, 1, 0))
        return z + (prev - z) * mu_ref[...]

    r = shifted(zr_ref, pr_ref, mur_ref)
    k = shifted(zk_ref, pk_ref, muk_ref)
    v = shifted(zv_ref, pv_ref, muv_ref)
    lo = shifted(zl_ref, pl_ref, mul_ref)
    lw = jnp.dot(jnp.tanh(lo).astype(BF16), w2_ref[...], preferred_element_type=F32)
    la = jnp.dot(lo.astype(BF16), a2_ref[...], preferred_element_type=F32)
    g = jnp.dot(jax.nn.sigmoid(lo).astype(BF16), g2_ref[...], preferred_element_type=F32)
    w = -jax.nn.softplus(-(w0_ref[...] + lw)) - 0.5
    a = jax.nn.sigmoid(a0_ref[...] + la)
    kk = k * kk_ref[...]
    norm = jnp.sqrt(_split_dot(kk * kk, hsum_ref[...]))
    kk = kk / jnp.maximum(norm, 1e-12)
    r_ref[...] = r
    lw_ref[...] = -jnp.exp(w)
    k_ref[...] = k * (1.0 + (a - 1.0) * ka_ref[...])
    v_ref[...] = v
    an_ref[...] = -kk
    bn_ref[...] = kk * a
    g_ref[...] = g


def rwkv_prep(z, col0, width, n_lora, seq, mu, w0, w2, a0, a2, g2, k_k, k_a):
    m = z.shape[0]
    tm = _pick(seq, (256, 128, 64, 32, 16, 8))
    lw_ = _round_up(n_lora, 128)
    assert col0 % width == 0 and (col0 + 3 * width) % lw_ == 0 and col0 + 3 * width + lw_ <= z.shape[1]
    cb = col0 // width
    lb = (col0 + 3 * width) // lw_
    dl, al = w2.shape[0], a2.shape[0]

    def rows_of(w, start):
        return jnp.zeros((lw_, width), BF16).at[start:start + w.shape[0]].set(w.astype(BF16))

    hsum = jnp.kron(jnp.eye(width // B_HEAD, dtype=BF16), jnp.ones((B_HEAD, B_HEAD), BF16))
    vec = lambda x: x.reshape(1, -1).astype(F32)
    mul = jnp.pad(mu[3 * width:], (0, lw_ - n_lora))
    tile = lambda c, wd: pl.BlockSpec((tm, wd), lambda i: (i, c))
    before = lambda c, wd: pl.BlockSpec((SUBLANES, wd), lambda i: (jnp.maximum(i * (tm // SUBLANES) - 1, 0), c))
    par = lambda wd: pl.BlockSpec((1, wd), lambda i: (0, 0))
    mat = lambda r_, c_: pl.BlockSpec((r_, c_), lambda i: (0, 0))
    out = pl.BlockSpec((tm, width), lambda i: (i, 0))
    return pl.pallas_call(
        functools.partial(_rwkv_prep_body, tm=tm, seq=seq),
        grid=(m // tm,),
        in_specs=[tile(cb, width), tile(cb + 1, width), tile(cb + 2, width), tile(lb, lw_),
                  before(cb, width), before(cb + 1, width), before(cb + 2, width), before(lb, lw_),
                  par(width), par(width), par(width), par(lw_), par(width), par(width), par(width), par(width),
                  mat(lw_, width), mat(lw_, width), mat(lw_, width), mat(width, width)],
        out_specs=[out] * 7,
        out_shape=[jax.ShapeDtypeStruct((m, width), F32)] * 7,
        compiler_params=_params(("parallel",)),
        name="rwkv_prep",
    )(z, z, z, z, z, z, z, z,
      vec(mu[:width]), vec(mu[width:2 * width]), vec(mu[2 * width:3 * width]), vec(mul),
      vec(w0), vec(a0), vec(k_k), vec(k_a),
      rows_of(w2, 0), rows_of(a2, dl), rows_of(g2, dl + al), hsum)


def _dot(a, b, dims):
    return lax.dot_general(a.astype(BF16), b.astype(BF16), (dims, ((), ())), preferred_element_type=F32)


def _prefix_sums(tri, x):
    hi = x.astype(BF16)
    r1 = x - hi.astype(F32)
    mid = r1.astype(BF16)
    lo = (r1 - mid.astype(F32)).astype(BF16)
    return (jnp.dot(tri, hi, preferred_element_type=F32) + jnp.dot(tri, mid, preferred_element_type=F32)
            + jnp.dot(tri, lo, preferred_element_type=F32))


_NN = ((1,), (0,))
_NT = ((1,), (1,))
_TN = ((0,), (0,))


def _wkv_body(r_ref, lw_ref, k_ref, v_ref, a_ref, b_ref, g_ref, rk_ref, lng_ref, lnb_ref, y_ref, s_ref,
              *, nh, n, c):
    @pl.when(pl.program_id(1) == 0)
    def _():
        s_ref[...] = jnp.zeros(s_ref.shape, F32)

    row = lax.broadcasted_iota(I32, (c, c), 0)
    col = lax.broadcasted_iota(I32, (c, c), 1)
    incl = row >= col
    strict = row > col
    tri_bf = jnp.where(incl, 1.0, 0.0).astype(BF16)
    eye = jnp.where(row == col, 1.0, 0.0).astype(F32)
    nsq = int(math.log2(c)) - 1

    hs = range(nh)
    dot = _dot
    heads = lambda ref: [ref[:, h * n:(h + 1) * n] for h in hs]
    lw, r, k, v, a, b = (heads(ref) for ref in (lw_ref, r_ref, k_ref, v_ref, a_ref, b_ref))
    cum = [_prefix_sums(tri_bf, lw[h]) for h in hs]
    tot = [cum[h][c - 1:c, :] for h in hs]
    pinv = [jnp.exp(-cum[h]) for h in hs]
    pend = [jnp.exp(tot[h] - cum[h]) for h in hs]
    at = [a[h] * jnp.exp(cum[h] - lw[h]) for h in hs]
    rt = [r[h] * jnp.exp(cum[h]) for h in hs]
    bt = [b[h] * pinv[h] for h in hs]
    kt = [k[h] * pinv[h] for h in hs]
    a_ab = [jnp.where(strict, dot(at[h], bt[h], _NT), 0.0) for h in hs]
    a_ak = [jnp.where(strict, dot(at[h], kt[h], _NT), 0.0) for h in hs]
    a_rb = [jnp.where(incl, dot(rt[h], bt[h], _NT), 0.0) for h in hs]
    a_rk = [jnp.where(incl, dot(rt[h], kt[h], _NT), 0.0) for h in hs]
    x = a_ab
    minv = [eye + x[h] for h in hs]
    for _ in range(nsq):
        x = [dot(x[h], x[h], _NN) for h in hs]
        minv = [minv[h] + dot(minv[h], x[h], _NN) for h in hs]
    s0 = [s_ref[h] for h in hs]
    rhs = [dot(at[h], s0[h], _NT) + dot(a_ak[h], v[h], _NN) for h in hs]
    u = [dot(minv[h], rhs[h], _NN) for h in hs]
    y = [dot(rt[h], s0[h], _NT) + dot(a_rb[h], u[h], _NN) + dot(a_rk[h], v[h], _NN) for h in hs]
    for h in hs:
        s_ref[h] = (s0[h] * jnp.exp(tot[h]) + dot(u[h], b[h] * pend[h], _TN)
                    + dot(v[h], k[h] * pend[h], _TN))
    ones = jnp.ones((n, n), BF16)
    cols = [slice(h * n, (h + 1) * n) for h in hs]
    mean = [_split_dot(y[h], ones) * (1.0 / n) for h in hs]
    d = [y[h] - mean[h] for h in hs]
    var = [_split_dot(d[h] * d[h], ones) * (1.0 / n) for h in hs]
    rk = [_split_dot(r[h] * k[h] * rk_ref[:, cols[h]], ones) for h in hs]
    for h in hs:
        yn = d[h] * lax.rsqrt(var[h] + LNX_EPS)
        y_ref[:, cols[h]] = (yn * lng_ref[:, cols[h]] + lnb_ref[:, cols[h]] + rk[h] * v[h]) * g_ref[:, cols[h]]


def wkv7(r, lw, k, v, a, b, g, r_k, lnx_g, lnx_b, n=B_HEAD):
    bsz, t, width = r.shape
    nh = width // n
    c = min(WKV_CHUNK, t)
    spec = pl.BlockSpec((None, c, width), lambda bi, ci: (bi, ci, 0))
    par = pl.BlockSpec((1, width), lambda bi, ci: (0, 0))
    vec = lambda x: x.reshape(1, width).astype(F32)
    return pl.pallas_call(
        functools.partial(_wkv_body, nh=nh, n=n, c=c),
        grid=(bsz, t // c),
        in_specs=[spec] * 7 + [par] * 3,
        out_specs=spec,
        out_shape=jax.ShapeDtypeStruct((bsz, t, width), F32),
        scratch_shapes=[pltpu.VMEM((nh, n, n), F32)],
        compiler_params=_params(("parallel", "arbitrary")),
        name="wkv7_chunked",
    )(r, lw, k, v, a, b, g, vec(r_k), vec(lnx_g), vec(lnx_b))


def _dsa_index_body(ki_ref, qit_ref, wit_ref, bias_ref, key_ref, dig_ref, cut_ref, *, tq, tkc, topk, nheads, t):
    i = pl.program_id(1)
    nch = ((i + 1) * tq + tkc - 1) // tkc
    qpos = i * tq + lax.broadcasted_iota(I32, (1, tq), 1)

    def kpos_of(ci):
        return ci * tkc + lax.broadcasted_iota(I32, (tkc, 1), 0)

    def rows(ci):
        return pl.ds(pl.multiple_of(ci * tkc, tkc), tkc)

    def score_chunk(ci, carry):
        kc = ki_ref[rows(ci), :]
        acc = jnp.zeros((tkc, tq), F32)
        for h in range(nheads):
            s = jnp.dot(kc, qit_ref[h], preferred_element_type=F32)
            acc = acc + jnp.maximum(s, 0.0) * wit_ref[h]
        acc = jnp.where(kpos_of(ci) <= qpos, acc, -jnp.inf)
        bits = lax.bitcast_convert_type(acc, I32)
        key_ref[rows(ci), :] = jnp.where(bits >= 0, bits, bits ^ 0x7FFFFFFF)
        return carry

    lax.fori_loop(0, nch, score_chunk, 0)

    def count(pred):
        def body(ci, cnt):
            ones = pred(key_ref[rows(ci), :], kpos_of(ci))
            return cnt + jnp.sum(ones, axis=0, keepdims=True)
        return lax.fori_loop(0, nch, body, jnp.zeros((1, tq), I32))

    fold = 64
    assert (t // tkc) * (tkc // fold) <= 256

    def count_digit_ge(cand):
        cand_bf = cand.astype(F32).astype(BF16)

        def body(ci, acc):
            hit = jnp.where(dig_ref[rows(ci), :] >= cand_bf, jnp.ones((), BF16), jnp.zeros((), BF16))
            for s in range(tkc // fold):
                acc = acc + hit[s * fold:(s + 1) * fold]
            return acc

        acc = lax.fori_loop(0, nch, body, jnp.zeros((fold, tq), BF16))
        return jnp.sum(acc.astype(F32), axis=0, keepdims=True).astype(I32)

    prefix = jnp.zeros((1, tq), I32)
    above = jnp.zeros((1, tq), I32)
    n_ge = above
    for level in (3, 2, 1, 0):
        shift = 8 * level

        def build(ci, carry, shift=shift, level=level, prefix=prefix):
            u = key_ref[rows(ci), :] ^ INT_MIN
            digit = lax.shift_right_logical(u, jnp.int32(shift)) & 255
            if level < 3:
                same = lax.shift_right_logical(u, jnp.int32(shift + 8)) == lax.shift_right_logical(
                    prefix, jnp.int32(shift + 8))
                digit = jnp.where(same, digit, -1)
            dig_ref[rows(ci), :] = digit.astype(F32).astype(BF16)
            return carry

        lax.fori_loop(0, nch, build, 0)

        def digit_bit(step, d, above=above):
            cand = d | lax.shift_left(jnp.int32(1), 7 - step)
            return jnp.where(above + count_digit_ge(cand) >= topk, cand, d)

        d = lax.fori_loop(0, 8, digit_bit, jnp.zeros((1, tq), I32))
        if level == 0:
            n_ge = above + count_digit_ge(d)
        above = above + count_digit_ge(d + 1)
        prefix = prefix | lax.shift_left(d, jnp.int32(shift))

    thr = prefix ^ INT_MIN
    n_gt = above
    need = topk - n_gt
    cut_ref[...] = jnp.full((1, tq), t, I32)

    @pl.when(jnp.max(n_ge) > topk)
    def _():
        nbits = max(1, (t - 1).bit_length())

        def index_bit(step, x):
            cand = x | lax.shift_left(jnp.int32(1), nbits - 1 - step)
            g = count(lambda key, kpos: jnp.where(key == thr, jnp.where(kpos < cand, 1, 0), 0))
            return jnp.where(g < need, cand, x)

        cut_ref[...] = lax.fori_loop(0, nbits, index_bit, jnp.zeros((1, tq), I32))

    cutoff = cut_ref[...]

    def write_chunk(ci, carry):
        key = key_ref[rows(ci), :]
        kpos = kpos_of(ci)
        tie = jnp.where(kpos <= cutoff, 0.0, NEG)
        sel = jnp.where(key > thr, 0.0, jnp.where(key == thr, tie, NEG))
        bias_ref[rows(ci), :] = jnp.where(kpos <= qpos, sel, NEG).astype(BF16)
        return carry

    lax.fori_loop(0, nch, write_chunk, 0)

    def fill_chunk(ci, carry):
        bias_ref[rows(ci), :] = jnp.full((tkc, tq), NEG, BF16)
        return carry

    lax.fori_loop(nch, t // tkc, fill_chunk, 0)


def dsa_select(ki, qi, wi, topk):
    b, t, nh, d = qi.shape
    tq = _pick(t, (256, 128))
    tkc = _pick(t, (512, 256, 128))
    nq = t // tq
    qit = jnp.transpose(qi.astype(BF16).reshape(b, nq, tq, nh, d), (0, 1, 3, 4, 2))
    wit = jnp.transpose(wi.astype(F32).reshape(b, nq, tq, nh), (0, 1, 3, 2))[:, :, :, None, :]
    return pl.pallas_call(
        functools.partial(_dsa_index_body, tq=tq, tkc=tkc, topk=topk, nheads=nh, t=t),
        grid=(b, nq),
        in_specs=[pl.BlockSpec((None, t, d), lambda bi, qi_: (bi, 0, 0)),
                  pl.BlockSpec((None, None, nh, d, tq), lambda bi, qi_: (bi, qi_, 0, 0, 0)),
                  pl.BlockSpec((None, None, nh, 1, tq), lambda bi, qi_: (bi, qi_, 0, 0, 0))],
        out_specs=pl.BlockSpec((None, t, tq), lambda bi, qi_: (bi, 0, qi_)),
        out_shape=jax.ShapeDtypeStruct((b, t, t), BF16),
        scratch_shapes=[pltpu.VMEM((t, tq), I32), pltpu.VMEM((t, tq), BF16), pltpu.VMEM((1, tq), I32)],
        compiler_params=_params(("parallel", "arbitrary")),
        name="dsa_index_topk",
    )(ki.astype(BF16), qit, wit)


def _dsa_attn_body(k_ref, vt_ref, q_ref, bias_ref, o_ref, m_ref, acc_ref, pa_ref, aa_ref, pb_ref, ab_ref,
                   *, tq, tk, rep, d):
    i = pl.program_id(2)
    nch = ((i + 1) * tq + tk - 1) // tk
    m_ref[...] = jnp.full(m_ref.shape, NEG, F32)
    acc_ref[...] = jnp.zeros(acc_ref.shape, F32)
    buf_a = (pa_ref, aa_ref)
    buf_b = (pb_ref, ab_ref)

    def scores(c, buf):
        rows = pl.ds(pl.multiple_of(c * tk, tk), tk)
        kj = k_ref[rows, :]
        bj = bias_ref[rows, :]
        fns = [lambda r=r: lax.dot_general(kj, q_ref[:, r * d:(r + 1) * d], (_NT, ((), ())),
                                           preferred_element_type=F32).astype(BF16) + bj
               for r in range(rep)]
        _score_stage(fns, m_ref, *buf)

    def values(c, buf):
        _value_stage(vt_ref[c], acc_ref, *buf)

    scores(0, buf_a)

    def pair(t, carry):
        c = 2 * t + 1
        scores(c, buf_b)
        values(c - 1, buf_a)
        scores(c + 1, buf_a)
        values(c, buf_b)
        return carry

    lax.fori_loop(0, (nch - 1) // 2, pair, 0)
    last = nch - 1

    @pl.when(last % 2 == 1)
    def _():
        scores(last, buf_b)
        values(last - 1, buf_a)
        values(last, buf_b)

    @pl.when(last % 2 == 0)
    def _():
        values(last, buf_a)

    for r in range(rep):
        acc = acc_ref[r]
        o_ref[:, r * d:(r + 1) * d] = (acc[:d] / acc[d:d + 1]).T.astype(o_ref.dtype)


def dsa_attention(q, k, v, bias):
    b, t, qw = q.shape
    d = C_HEAD_DIM
    g = k.shape[2] // d
    rep = qw // (g * d)
    tq = _pick(t, (256, 128))
    tk = _pick(t, (512, 256, 128))
    nq, nk = t // tq, t // tk
    vt = jnp.transpose(v.astype(BF16).reshape(b, nk, tk, g, d), (0, 3, 1, 4, 2))
    vt = _with_ones_rows(vt)
    dve = d + ONES_ROWS
    group = pl.BlockSpec((None, tq, rep * d), lambda bi, gi, qi: (bi, qi, gi))
    return pl.pallas_call(
        functools.partial(_dsa_attn_body, tq=tq, tk=tk, rep=rep, d=d),
        grid=(b, g, nq),
        in_specs=[pl.BlockSpec((None, t, d), lambda bi, gi, qi: (bi, 0, gi)),
                  pl.BlockSpec((None, None, nk, dve, tk), lambda bi, gi, qi: (bi, gi, 0, 0, 0)),
                  group,
                  pl.BlockSpec((None, t, tq), lambda bi, gi, qi: (bi, 0, qi))],
        out_specs=group,
        out_shape=jax.ShapeDtypeStruct((b, t, qw), BF16),
        scratch_shapes=[pltpu.VMEM((rep, 1, tq), F32), pltpu.VMEM((rep, dve, tq), F32)]
        + [pltpu.VMEM((rep, tk, tq), BF16), pltpu.VMEM((rep, 1, tq), F32)] * 2,
        compiler_params=_params(("parallel", "parallel", "arbitrary")),
        name="dsa_attention",
    )(k.astype(BF16), vt, q.astype(BF16), bias)


LANES = 128


def _rope_tables(pos, head_dim, scale=1.0):
    rot = head_dim // ROPE_FRACTION
    half = rot // 2
    inv_freq = ROPE_THETA ** (-(jnp.arange(half, dtype=F32) * 2.0 / rot))
    ang = pos.astype(F32).reshape(-1, 1) * inv_freq
    cos, sin = jnp.cos(ang), jnp.sin(ang)
    zero = jnp.zeros_like(sin)
    rest = jnp.zeros((ang.shape[0], head_dim - rot), F32)
    tabs = (jnp.concatenate([cos, cos, rest + 1.0], axis=-1),
            jnp.concatenate([-sin, zero, rest], axis=-1),
            jnp.concatenate([zero, sin, rest], axis=-1))
    return tuple(jnp.tile(x * scale, (1, LANES // head_dim)) for x in tabs)


def _rope_body(z_ref, c_ref, up_ref, down_ref, o_ref, *, half, ncols):
    c, up, down = c_ref[...], up_ref[...], down_ref[...]
    for cb in range(ncols // LANES):
        cols = slice(cb * LANES, (cb + 1) * LANES)
        x = z_ref[:, cols]
        y = x * c + pltpu.roll(x, LANES - half, 1) * up + pltpu.roll(x, half, 1) * down
        o_ref[:, cols] = y.astype(o_ref.dtype)


def rope_cast(z, col0, ncols, head_dim, tables):
    m = z.shape[0]
    assert col0 % ncols == 0 and ncols % LANES == 0 and LANES % head_dim == 0
    tm = _pick(m, (512, 256, 128, 64, 32, 16, 8))
    tab = pl.BlockSpec((tm, LANES), lambda i: (i, 0))
    return pl.pallas_call(
        functools.partial(_rope_body, half=head_dim // ROPE_FRACTION // 2, ncols=ncols),
        grid=(m // tm,),
        in_specs=[pl.BlockSpec((tm, ncols), lambda i: (i, col0 // ncols)), tab, tab, tab],
        out_specs=pl.BlockSpec((tm, ncols), lambda i: (i, 0)),
        out_shape=jax.ShapeDtypeStruct((m, ncols), BF16),
        compiler_params=_params(("parallel",)),
        name="rope_cast",
    )(z, *tables)


def _rms_norm(x, g, eps=NORM_EPS):
    xf = x.astype(F32)
    return xf * lax.rsqrt(jnp.mean(xf * xf, axis=-1, keepdims=True) + eps) * g.astype(F32)


def _layer_norm(x, g, b, eps=LN_EPS):
    mu = jnp.mean(x, axis=-1, keepdims=True)
    var = jnp.mean(jnp.square(x - mu), axis=-1, keepdims=True)
    return (x - mu) * lax.rsqrt(var + eps) * g + b


def _partial_rope(x, pos):
    dh = x.shape[-1]
    rot = dh // ROPE_FRACTION
    half = rot // 2
    inv_freq = ROPE_THETA ** (-(jnp.arange(half, dtype=F32) * 2.0 / rot))
    ang = pos.astype(F32)[..., None] * inv_freq
    cos = jnp.cos(ang)[:, :, None, :]
    sin = jnp.sin(ang)[:, :, None, :]
    x1 = x[..., :half]
    x2 = x[..., half:rot]
    return jnp.concatenate([x1 * cos - x2 * sin, x2 * cos + x1 * sin, x[..., rot:]], axis=-1)


def _pad_cols(w, n):
    return jnp.pad(w, ((0, 0), (0, n - w.shape[1])))


def _round_up(n, m):
    return (n + m - 1) // m * m


def _mixer_ab(u, pos, w_in, lq1, lk1, lq2, lk2, subln_g, mu, w0, w2, a0, a2, g2,
              k_k, k_a, r_k, lnx_g, lnx_b, lambda_init):
    b, t, d = u.shape
    a_width = d // 2
    b_width = d // 2
    a_in = 3 * a_width
    n_in = w_in.shape[1]
    z = matmul(u.reshape(b * t, d), _pad_cols(w_in, _round_up(n_in, 512)))

    seq = lambda x: x.reshape(b, t, a_width)
    qa = rope_cast(z, 0, a_width, A_DC, _rope_tables(pos, A_DC, A_DC ** -0.5 * LOG2E))
    ka = rope_cast(z, a_width, a_width, A_DC, _rope_tables(pos, A_DC))
    va = z[:, 2 * a_width:a_in].astype(BF16)
    lam = jnp.exp(jnp.sum(lq1 * lk1)) - jnp.exp(jnp.sum(lq2 * lk2)) + lambda_init
    ya = diff_attention(seq(qa), seq(ka), seq(va), lam, subln_g * (1.0 - lambda_init))

    prep = rwkv_prep(z, a_in, b_width, n_in - a_in - 3 * b_width, t, mu, w0, w2, a0, a2, g2, k_k, k_a)
    yb = wkv7(*(x.reshape(b, t, b_width) for x in prep), r_k.reshape(-1), lnx_g, lnx_b)
    return ya, yb


def _mixer_c(u, pos, w_in, idx_k_g, idx_k_b, topk):
    b, t, d = u.shape
    c_heads = d // C_HEAD_DIM
    n_in = w_in.shape[1]
    c1 = c_heads * C_HEAD_DIM
    c2 = c1 + C_KV_HEADS * C_HEAD_DIM
    c3 = c2 + C_KV_HEADS * C_HEAD_DIM
    c4 = c3 + IDX_HEADS * IDX_DIM
    c5 = c4 + IDX_DIM
    z = matmul(u.reshape(b * t, d), _pad_cols(w_in, _round_up(n_in, 512)))
    seq = lambda x: x.reshape(b, t, x.shape[-1])
    tabs = _rope_tables(pos, C_HEAD_DIM)
    q = rope_cast(z, 0, c1, C_HEAD_DIM, _rope_tables(pos, C_HEAD_DIM, C_HEAD_DIM ** -0.5 * LOG2E))
    k = rope_cast(z, c1, c2 - c1, C_HEAD_DIM, tabs)
    v = z[:, c2:c3].astype(BF16)
    qi = rope_cast(z, c3, c4 - c3, IDX_DIM, _rope_tables(pos, IDX_DIM))
    ki = _partial_rope(_layer_norm(seq(z[:, c4:c5]), idx_k_g, idx_k_b)[:, :, None, :], pos)[:, :, 0]
    wi = seq(z[:, c5:n_in]) * ((IDX_HEADS * IDX_DIM) ** -0.5)
    bias = dsa_select(ki, qi.reshape(b, t, IDX_HEADS, IDX_DIM), wi, topk)
    return (dsa_attention(seq(q), seq(k), seq(v), bias),)


def kernel(x, p, positions, mix_pre_g, mix_post_g, mlp_pre_g, mlp_post_g, w_mlp_up, w_mlp_down, w_ple_proj, w_ple_gate, ple_post_g, ab_w_in, ab_w_out, diff_lq1, diff_lk1, diff_lq2, diff_lk2, diff_subln_g, rwkv_mu, rwkv_w0, rwkv_w2, rwkv_a0, rwkv_a2, rwkv_g2, rwkv_k_k, rwkv_k_a, rwkv_r_k, rwkv_lnx_g, rwkv_lnx_b, c_w_in, c_w_out, idx_k_g, idx_k_b):
    depth = mix_pre_g.shape[0]
    b, t, d = x.shape
    topk = min(TOPK_MAX, t // 4)
    flat = lambda z: z.reshape(b * t, z.shape[-1])
    h = flat(x)
    u = _rms_norm(h, mix_pre_g[0]).astype(BF16)
    for i in range(depth):
        j = i // 2
        u3 = u.reshape(b, t, d)
        if i % 2 == 0:
            lambda_init = 0.8 - 0.6 * math.exp(-0.3 * i)
            m = _mixer_ab(u3, positions, ab_w_in[j], diff_lq1[j], diff_lk1[j], diff_lq2[j],
                          diff_lk2[j], diff_subln_g[j], rwkv_mu[j], rwkv_w0[j], rwkv_w2[j], rwkv_a0[j],
                          rwkv_a2[j], rwkv_g2[j], rwkv_k_k[j], rwkv_k_a[j], rwkv_r_k[j], rwkv_lnx_g[j],
                          rwkv_lnx_b[j], lambda_init)
            w_out = ab_w_out[j]
        else:
            m = _mixer_c(u3, positions, c_w_in[j], idx_k_g[j], idx_k_b[j], topk)
            w_out = c_w_out[j]
        h, u = proj_residual([flat(part) for part in m], w_out, h, mix_post_g[i], mlp_pre_g[i])
        up = matmul(u, w_mlp_up[i], out_dtype=BF16, act="relu2")
        h, _ = proj_residual([up], w_mlp_down[i], h, mlp_post_g[i], None)
        g_next = mix_pre_g[i + 1] if i + 1 < depth else None
        h, u = ple_residual(flat(p[i]), w_ple_proj[i], w_ple_gate[i], h, ple_post_g[i], g_next)
    return h.reshape(b, t, d)
```

```python
import functools
import math

import jax
import jax.numpy as jnp
from jax import lax
from jax.experimental import pallas as pl
from jax.experimental.pallas import tpu as pltpu

F32 = jnp.float32
BF16 = jnp.bfloat16
I32 = jnp.int32

V7X_VMEM_LIMIT_BYTES = 56 * 1024 * 1024
NEG = -1e30
LOG2E = 1.4426950408889634
SUBLANES = 8
INT_MIN = -2147483648

NORM_EPS = 1e-6
ROPE_THETA = 500000.0
ROPE_FRACTION = 4
A_DV = 128
A_DC = 64
DIFF_SUBLN_EPS = 1e-5
B_HEAD = 64
LNX_EPS = 64e-5
C_HEAD_DIM = 128
C_KV_HEADS = 4
IDX_HEADS = 16
IDX_DIM = 64
TOPK_MAX = 256
LN_EPS = 1e-6
WKV_CHUNK = 64


def _params(sem):
    return pltpu.CompilerParams(dimension_semantics=sem, vmem_limit_bytes=V7X_VMEM_LIMIT_BYTES)


def _pick(n, prefs):
    for t in prefs:
        if n % t == 0:
            return t
    return n


def _mm_body(a_ref, w_ref, o_ref, *scratch, nk, act):
    def epilogue(acc):
        if act == "relu2":
            r = jnp.maximum(acc, 0.0)
            acc = r * r
        return acc.astype(o_ref.dtype)

    if nk == 1:
        o_ref[...] = epilogue(jnp.dot(a_ref[...], w_ref[...], preferred_element_type=F32))
        return
    (acc_ref,) = scratch
    k = pl.program_id(2)

    @pl.when(k == 0)
    def _():
        acc_ref[...] = jnp.zeros_like(acc_ref)

    acc_ref[...] += jnp.dot(a_ref[...], w_ref[...], preferred_element_type=F32)

    @pl.when(k == nk - 1)
    def _():
        o_ref[...] = epilogue(acc_ref[...])


def matmul(a, w, out_dtype=F32, act=None):
    a = a.astype(BF16)
    w = w.astype(BF16)
    m, kdim = a.shape
    n = w.shape[1]
    tn = _pick(n, (2048, 1024, 512, 256, 128))
    tile_elems = (4 << 20) // jnp.dtype(out_dtype).itemsize
    tm = _pick(m, tuple(r for r in (2048, 1024, 512, 256, 128, 64, 32, 16, 8) if r * tn <= tile_elems))
    tk = kdim if kdim <= 2048 else _pick(kdim, (2048, 1024, 512))
    nk = kdim // tk
    scratch = [pltpu.VMEM((tm, tn), F32)] if nk > 1 else []
    return pl.pallas_call(
        functools.partial(_mm_body, nk=nk, act=act),
        grid=(m // tm, n // tn, nk),
        in_specs=[pl.BlockSpec((tm, tk), lambda i, j, k: (i, k)),
                  pl.BlockSpec((tk, tn), lambda i, j, k: (k, j))],
        out_specs=pl.BlockSpec((tm, tn), lambda i, j, k: (i, j)),
        out_shape=jax.ShapeDtypeStruct((m, n), out_dtype),
        scratch_shapes=scratch,
        compiler_params=_params(("parallel", "parallel", "arbitrary")),
        name="dense_matmul",
    )(a, w)


def _rms_rows(x, g):
    return x * lax.rsqrt(jnp.mean(x * x, axis=-1, keepdims=True) + NORM_EPS) * g


def _residual_update(m, h_ref, gpost_ref, gnext_ref, ho_ref, uo_ref):
    h_new = h_ref[...] + _rms_rows(m, gpost_ref[...])
    ho_ref[...] = h_new
    if uo_ref is not None:
        uo_ref[...] = _rms_rows(h_new, gnext_ref[...]).astype(BF16)


def _proj_res_body(*refs, nk, has_next, widths):
    a_refs, refs = refs[:len(widths)], refs[len(widths):]
    if has_next:
        w_ref, h_ref, gpost_ref, gnext_ref, ho_ref, uo_ref = refs
    else:
        w_ref, h_ref, gpost_ref, ho_ref = refs
        gnext_ref = uo_ref = None
    part, lo = None, 0
    for a_ref, width in zip(a_refs, widths):
        rows = slice(None) if len(widths) == 1 else slice(lo, lo + width)
        term = jnp.dot(a_ref[...].astype(BF16), w_ref[rows, :], preferred_element_type=F32)
        part = term if part is None else part + term
        lo += width
    if nk == 1:
        _residual_update(part, h_ref, gpost_ref, gnext_ref, ho_ref, uo_ref)
        return
    k = pl.program_id(1)

    @pl.when(k == 0)
    def _():
        ho_ref[...] = part

    @pl.when(jnp.logical_and(k > 0, k < nk - 1))
    def _():
        ho_ref[...] += part

    @pl.when(k == nk - 1)
    def _():
        _residual_update(ho_ref[...] + part, h_ref, gpost_ref, gnext_ref, ho_ref, uo_ref)


def proj_residual(a_parts, w, h, g_post, g_next):
    m = h.shape[0]
    widths = tuple(a.shape[1] for a in a_parts)
    kdim = sum(widths)
    n = w.shape[1]
    tm = _pick(m, (512, 256, 128, 64, 32, 16, 8))
    tk = _pick(kdim, (2048, 1024, 512, 256, 128))
    nk = kdim // tk
    assert nk == 1 or len(a_parts) == 1
    has_next = g_next is not None
    row = pl.BlockSpec((tm, n), lambda i, k: (i, 0))
    vec = pl.BlockSpec((1, n), lambda i, k: (0, 0))
    if len(a_parts) == 1:
        a_specs = [pl.BlockSpec((tm, tk), lambda i, k: (i, k))]
    else:
        a_specs = [pl.BlockSpec((tm, wd), lambda i, k: (i, 0)) for wd in widths]
    in_specs = a_specs + [pl.BlockSpec((tk, n), lambda i, k: (k, 0)), row, vec]
    args = list(a_parts) + [w.astype(BF16), h, g_post.reshape(1, n).astype(F32)]
    out_shape = [jax.ShapeDtypeStruct((m, n), F32)]
    out_specs = [row]
    if has_next:
        in_specs.append(vec)
        args.append(g_next.reshape(1, n).astype(F32))
        out_shape.append(jax.ShapeDtypeStruct((m, n), BF16))
        out_specs.append(row)
    out = pl.pallas_call(
        functools.partial(_proj_res_body, nk=nk, has_next=has_next, widths=widths),
        grid=(m // tm, nk),
        in_specs=in_specs, out_specs=out_specs, out_shape=out_shape,
        compiler_params=_params(("parallel", "arbitrary")),
        name="proj_residual",
    )(*args)
    return (out[0], out[1]) if has_next else (out[0], None)


def _mlp_body(u_ref, wu_ref, wd_ref, h_ref, gpost_ref, ho_ref, *, nf):
    f = pl.program_id(1)
    hid = jnp.maximum(jnp.dot(u_ref[...], wu_ref[...], preferred_element_type=F32), 0.0)
    part = jnp.dot((hid * hid).astype(BF16), wd_ref[...], preferred_element_type=F32)

    @pl.when(f == 0)
    def _():
        ho_ref[...] = part

    @pl.when(jnp.logical_and(f > 0, f < nf - 1))
    def _():
        ho_ref[...] += part

    @pl.when(f == nf - 1)
    def _():
        _residual_update(ho_ref[...] + part, h_ref, gpost_ref, None, ho_ref, None)


def mlp_residual(u, w_up, w_down, h, g_post):
    m, d = u.shape
    ff = w_up.shape[1]
    n = w_down.shape[1]
    tm = _pick(m, (512, 256, 128, 64, 32, 16, 8))
    fc = _pick(ff, (1024, 512, 256, 128))
    nf = ff // fc
    assert nf >= 2
    row = pl.BlockSpec((tm, n), lambda i, f: (i, 0))
    return pl.pallas_call(
        functools.partial(_mlp_body, nf=nf),
        grid=(m // tm, nf),
        in_specs=[pl.BlockSpec((tm, d), lambda i, f: (i, 0)),
                  pl.BlockSpec((d, fc), lambda i, f: (0, f)),
                  pl.BlockSpec((fc, n), lambda i, f: (f, 0)),
                  row, pl.BlockSpec((1, n), lambda i, f: (0, 0))],
        out_specs=row,
        out_shape=jax.ShapeDtypeStruct((m, n), F32),
        compiler_params=_params(("parallel", "arbitrary")),
        name="mlp_residual",
    )(u.astype(BF16), w_up.astype(BF16), w_down.astype(BF16), h, g_post.reshape(1, n).astype(F32))


def _ple_body(*refs, has_next):
    if has_next:
        p_ref, wp_ref, wg_ref, h_ref, gpost_ref, gnext_ref, ho_ref, uo_ref = refs
    else:
        p_ref, wp_ref, wg_ref, h_ref, gpost_ref, ho_ref = refs
        gnext_ref = uo_ref = None
    e = jnp.dot(p_ref[...], wp_ref[...], preferred_element_type=F32)
    gate = jax.nn.sigmoid(jnp.dot(h_ref[...].astype(BF16), wg_ref[...], preferred_element_type=F32))
    _residual_update(e * gate, h_ref, gpost_ref, gnext_ref, ho_ref, uo_ref)


def ple_residual(p, w_proj, w_gate, h, g_post, g_next):
    m, n = h.shape
    pd = p.shape[1]
    tm = _pick(m, (512, 256, 128, 64, 32, 16, 8))
    has_next = g_next is not None
    row = pl.BlockSpec((tm, n), lambda i: (i, 0))
    vec = pl.BlockSpec((1, n), lambda i: (0, 0))
    once = pl.Buffered(1)
    in_specs = [pl.BlockSpec((tm, pd), lambda i: (i, 0)),
                pl.BlockSpec((pd, n), lambda i: (0, 0), pipeline_mode=once),
                pl.BlockSpec((n, n), lambda i: (0, 0), pipeline_mode=once), row, vec]
    args = [p.astype(BF16), w_proj.astype(BF16), w_gate.astype(BF16), h, g_post.reshape(1, n).astype(F32)]
    out_shape = [jax.ShapeDtypeStruct((m, n), F32)]
    out_specs = [row]
    if has_next:
        in_specs.append(vec)
        args.append(g_next.reshape(1, n).astype(F32))
        out_shape.append(jax.ShapeDtypeStruct((m, n), BF16))
        out_specs.append(row)
    out = pl.pallas_call(
        functools.partial(_ple_body, has_next=has_next),
        grid=(m // tm,),
        in_specs=in_specs, out_specs=out_specs, out_shape=out_shape,
        compiler_params=_params(("parallel",)),
        name="ple_residual",
    )(*args)
    return (out[0], out[1]) if has_next else (out[0], None)


ONES_ROWS = 16


def _with_ones_rows(vt):
    pad = jnp.zeros(vt.shape[:-2] + (ONES_ROWS, vt.shape[-1]), vt.dtype).at[..., 0, :].set(1)
    return jnp.concatenate([vt, pad], axis=-2)


def _score_stage(score_fns, m_ref, p_ref, alpha_ref):
    n = range(len(score_fns))
    ss = [fn() for fn in score_fns]
    m_old = [m_ref[i] for i in n]
    m_new = [jnp.maximum(m_old[i], jnp.max(ss[i], axis=0, keepdims=True).astype(F32)) for i in n]
    for i in n:
        alpha_ref[i] = jnp.exp2(m_old[i] - m_new[i])
        m_ref[i] = m_new[i]
    for i in n:
        p_ref[i] = jnp.exp2(ss[i] - m_new[i].astype(BF16))


def _value_stage(vt, acc_ref, p_ref, alpha_ref):
    n = range(p_ref.shape[0])
    pv = [jnp.dot(vt, p_ref[i], preferred_element_type=F32) for i in n]
    for i in n:
        acc_ref[i] = alpha_ref[i] * acc_ref[i] + pv[i]


def _diff_attn_body(lam_ref, k_ref, q_ref, cmask_ref, vt_ref, g_ref, o_ref, m_ref, acc_ref, qz_ref,
                    pa_ref, aa_ref, pb_ref, ab_ref, *, tq, nsp):
    i = pl.program_id(2)
    tl = tq // nsp
    m_ref[...] = jnp.full(m_ref.shape, NEG, F32)
    acc_ref[...] = jnp.zeros(acc_ref.shape, F32)
    for c in range(2):
        qz_ref[c] = q_ref[...] * cmask_ref[c]

    buf_a = (pa_ref, aa_ref)
    buf_b = (pb_ref, ab_ref)

    def scores(j, buf, masked=False):
        kj = k_ref[pl.ds(pl.multiple_of(j * tq, tq), tq), :]

        def score(c, sp):
            s = lax.dot_general(kj, qz_ref[c, sp * tl:(sp + 1) * tl, :], (_NT, ((), ())),
                                preferred_element_type=F32)
            if masked:
                row = lax.broadcasted_iota(I32, (tq, tl), 0)
                col = lax.broadcasted_iota(I32, (tq, tl), 1)
                s = jnp.where(row <= col + sp * tl, s, NEG)
            return s.astype(BF16)

        _score_stage([functools.partial(score, c, sp) for c in range(2) for sp in range(nsp)], m_ref, *buf)

    def values(j, buf):
        _value_stage(vt_ref[j], acc_ref, *buf)

    @pl.when(i == 0)
    def _():
        scores(0, buf_a, masked=True)
        values(0, buf_a)

    @pl.when(i > 0)
    def _():
        scores(0, buf_a)

        def pair(t, carry):
            c = 2 * t + 1
            scores(c, buf_b)
            values(c - 1, buf_a)
            scores(c + 1, buf_a)
            values(c, buf_b)
            return carry

        lax.fori_loop(0, (i - 1) // 2, pair, 0)

        @pl.when((i - 1) % 2 == 1)
        def _():
            scores(i - 1, buf_b)
            values(i - 2, buf_a)
            scores(i, buf_a, masked=True)
            values(i - 1, buf_b)
            values(i, buf_a)

        @pl.when((i - 1) % 2 == 0)
        def _():
            scores(i, buf_b, masked=True)
            values(i - 1, buf_a)
            values(i, buf_b)

    lam = lam_ref[0]
    for sp in range(nsp):
        a0 = acc_ref[sp]
        a1 = acc_ref[nsp + sp]
        out = (a0[:A_DV] / a0[A_DV:A_DV + 1] - lam * (a1[:A_DV] / a1[A_DV:A_DV + 1])).T
        y = out * lax.rsqrt(jnp.mean(out * out, axis=-1, keepdims=True) + DIFF_SUBLN_EPS) * g_ref[...]
        o_ref[sp * tl:(sp + 1) * tl, :] = y.astype(o_ref.dtype)


def diff_attention(q, k, v, lam, gain):
    b, t, width = q.shape
    h = width // A_DV
    tq = _pick(t, (512, 256, 128))
    nq = t // tq
    vt = jnp.transpose(v.astype(BF16).reshape(b, nq, tq, h, A_DV), (0, 3, 1, 4, 2))
    vt = _with_ones_rows(vt)
    dve = A_DV + ONES_ROWS
    nsp = 2 if tq >= 256 else 1
    lane = jnp.arange(A_DV)
    cmask = jnp.stack([lane < A_DC, lane >= A_DC]).astype(BF16).reshape(2, 1, A_DV)
    head = pl.BlockSpec((None, tq, A_DV), lambda bi, hi, qi: (bi, qi, hi))
    return pl.pallas_call(
        functools.partial(_diff_attn_body, tq=tq, nsp=nsp),
        grid=(b, h, nq),
        in_specs=[pl.BlockSpec(memory_space=pltpu.SMEM),
                  pl.BlockSpec((None, t, A_DV), lambda bi, hi, qi: (bi, 0, hi)),
                  head,
                  pl.BlockSpec((2, 1, A_DV), lambda bi, hi, qi: (0, 0, 0)),
                  pl.BlockSpec((None, None, nq, dve, tq), lambda bi, hi, qi: (bi, hi, 0, 0, 0)),
                  pl.BlockSpec((1, A_DV), lambda bi, hi, qi: (0, 0))],
        out_specs=head,
        out_shape=jax.ShapeDtypeStruct((b, t, width), BF16),
        scratch_shapes=[pltpu.VMEM((2 * nsp, 1, tq // nsp), F32),
                        pltpu.VMEM((2 * nsp, dve, tq // nsp), F32),
                        pltpu.VMEM((2, tq, A_DV), BF16)]
        + [pltpu.VMEM((2 * nsp, tq, tq // nsp), BF16), pltpu.VMEM((2 * nsp, 1, tq // nsp), F32)] * 2,
        compiler_params=_params(("parallel", "parallel", "arbitrary")),
        name="diff_attention",
    )(lam.reshape(1).astype(F32), k.astype(BF16), q.astype(BF16), cmask, vt, gain.reshape(1, A_DV).astype(F32))


def _split_dot(x, w):
    hi = x.astype(BF16)
    lo = (x - hi.astype(F32)).astype(BF16)
    return jnp.dot(hi, w, preferred_element_type=F32) + jnp.dot(lo, w, preferred_element_type=F32)


def _rwkv_prep_body(zr_ref, zk_ref, zv_ref, zl_ref, pr_ref, pk_ref, pv_ref, pl_ref,
                    mur_ref, muk_ref, muv_ref, mul_ref, w0_ref, a0_ref, kk_ref, ka_ref,
                    w2_ref, a2_ref, g2_ref, hsum_ref,
                    r_ref, lw_ref, k_ref, v_ref, an_ref, bn_ref, g_ref, *, tm, seq):
    first = (pl.program_id(0) * tm) % seq == 0

    def shifted(z_ref, p_ref, mu_ref):
        z = z_ref[...]
        last = jnp.where(first, 0.0, p_ref[SUBLANES - 1:SUBLANES, :])
        row = lax.broadcasted_iota(I32, z.shape, 0)
        prev = jnp.where(row == 0, last, pltpu.roll(z, 1, 0))
        return z + (prev - z) * mu_ref[...]

    r = shifted(zr_ref, pr_ref, mur_ref)
    k = shifted(zk_ref, pk_ref, muk_ref)
    v = shifted(zv_ref, pv_ref, muv_ref)
    lo = shifted(zl_ref, pl_ref, mul_ref)
    lw = jnp.dot(jnp.tanh(lo).astype(BF16), w2_ref[...], preferred_element_type=F32)
    la = jnp.dot(lo.astype(BF16), a2_ref[...], preferred_element_type=F32)
    g = jnp.dot(jax.nn.sigmoid(lo).astype(BF16), g2_ref[...], preferred_element_type=F32)
    w = -jax.nn.softplus(-(w0_ref[...] + lw)) - 0.5
    a = jax.nn.sigmoid(a0_ref[...] + la)
    kk = k * kk_ref[...]
    norm = jnp.sqrt(_split_dot(kk * kk, hsum_ref[...]))
    kk = kk / jnp.maximum(norm, 1e-12)
    r_ref[...] = r
    lw_ref[...] = -jnp.exp(w)
    k_ref[...] = k * (1.0 + (a - 1.0) * ka_ref[...])
    v_ref[...] = v
    an_ref[...] = -kk
    bn_ref[...] = kk * a
    g_ref[...] = g


def rwkv_prep(z, col0, width, n_lora, seq, mu, w0, w2, a0, a2, g2, k_k, k_a):
    m = z.shape[0]
    tm = _pick(seq, (256, 128, 64, 32, 16, 8))
    lw_ = _round_up(n_lora, 128)
    assert col0 % width == 0 and (col0 + 3 * width) % lw_ == 0 and col0 + 3 * width + lw_ <= z.shape[1]
    cb = col0 // width
    lb = (col0 + 3 * width) // lw_
    dl, al = w2.shape[0], a2.shape[0]

    def rows_of(w, start):
        return jnp.zeros((lw_, width), BF16).at[start:start + w.shape[0]].set(w.astype(BF16))

    hsum = jnp.kron(jnp.eye(width // B_HEAD, dtype=BF16), jnp.ones((B_HEAD, B_HEAD), BF16))
    vec = lambda x: x.reshape(1, -1).astype(F32)
    mul = jnp.pad(mu[3 * width:], (0, lw_ - n_lora))
    tile = lambda c, wd: pl.BlockSpec((tm, wd), lambda i: (i, c))
    before = lambda c, wd: pl.BlockSpec((SUBLANES, wd), lambda i: (jnp.maximum(i * (tm // SUBLANES) - 1, 0), c))
    par = lambda wd: pl.BlockSpec((1, wd), lambda i: (0, 0))
    mat = lambda r_, c_: pl.BlockSpec((r_, c_), lambda i: (0, 0))
    out = pl.BlockSpec((tm, width), lambda i: (i, 0))
    return pl.pallas_call(
        functools.partial(_rwkv_prep_body, tm=tm, seq=seq),
        grid=(m // tm,),
        in_specs=[tile(cb, width), tile(cb + 1, width), tile(cb + 2, width), tile(lb, lw_),
                  before(cb, width), before(cb + 1, width), before(cb + 2, width), before(lb, lw_),
                  par(width), par(width), par(width), par(lw_), par(width), par(width), par(width), par(width),
                  mat(lw_, width), mat(lw_, width), mat(lw_, width), mat(width, width)],
        out_specs=[out] * 7,
        out_shape=[jax.ShapeDtypeStruct((m, width), F32)] * 7,
        compiler_params=_params(("parallel",)),
        name="rwkv_prep",
    )(z, z, z, z, z, z, z, z,
      vec(mu[:width]), vec(mu[width:2 * width]), vec(mu[2 * width:3 * width]), vec(mul),
      vec(w0), vec(a0), vec(k_k), vec(k_a),
      rows_of(w2, 0), rows_of(a2, dl), rows_of(g2, dl + al), hsum)


def _dot(a, b, dims):
    return lax.dot_general(a.astype(BF16), b.astype(BF16), (dims, ((), ())), preferred_element_type=F32)


def _prefix_sums(tri, x):
    hi = x.astype(BF16)
    r1 = x - hi.astype(F32)
    mid = r1.astype(BF16)
    lo = (r1 - mid.astype(F32)).astype(BF16)
    return (jnp.dot(tri, hi, preferred_element_type=F32) + jnp.dot(tri, mid, preferred_element_type=F32)
            + jnp.dot(tri, lo, preferred_element_type=F32))


_NN = ((1,), (0,))
_NT = ((1,), (1,))
_TN = ((0,), (0,))


def _wkv_body(r_ref, lw_ref, k_ref, v_ref, a_ref, b_ref, g_ref, rk_ref, lng_ref, lnb_ref, y_ref, s_ref,
              *, nh, n, c):
    @pl.when(pl.program_id(1) == 0)
    def _():
        s_ref[...] = jnp.zeros(s_ref.shape, F32)

    row = lax.broadcasted_iota(I32, (c, c), 0)
    col = lax.broadcasted_iota(I32, (c, c), 1)
    incl = row >= col
    strict = row > col
    tri_bf = jnp.where(incl, 1.0, 0.0).astype(BF16)
    eye = jnp.where(row == col, 1.0, 0.0).astype(F32)
    nsq = int(math.log2(c)) - 1

    hs = range(nh)
    dot = _dot
    heads = lambda ref: [ref[:, h * n:(h + 1) * n] for h in hs]
    lw, r, k, v, a, b = (heads(ref) for ref in (lw_ref, r_ref, k_ref, v_ref, a_ref, b_ref))
    cum = [_prefix_sums(tri_bf, lw[h]) for h in hs]
    tot = [cum[h][c - 1:c, :] for h in hs]
    pinv = [jnp.exp(-cum[h]) for h in hs]
    pend = [jnp.exp(tot[h] - cum[h]) for h in hs]
    at = [a[h] * jnp.exp(cum[h] - lw[h]) for h in hs]
    rt = [r[h] * jnp.exp(cum[h]) for h in hs]
    bt = [b[h] * pinv[h] for h in hs]
    kt = [k[h] * pinv[h] for h in hs]
    a_ab = [jnp.where(strict, dot(at[h], bt[h], _NT), 0.0) for h in hs]
    a_ak = [jnp.where(strict, dot(at[h], kt[h], _NT), 0.0) for h in hs]
    a_rb = [jnp.where(incl, dot(rt[h], bt[h], _NT), 0.0) for h in hs]
    a_rk = [jnp.where(incl, dot(rt[h], kt[h], _NT), 0.0) for h in hs]
    x = a_ab
    minv = [eye + x[h] for h in hs]
    for _ in range(nsq):
        x = [dot(x[h], x[h], _NN) for h in hs]
        minv = [minv[h] + dot(minv[h], x[h], _NN) for h in hs]
    s0 = [s_ref[h] for h in hs]
    rhs = [dot(at[h], s0[h], _NT) + dot(a_ak[h], v[h], _NN) for h in hs]
    u = [dot(minv[h], rhs[h], _NN) for h in hs]
    y = [dot(rt[h], s0[h], _NT) + dot(a_rb[h], u[h], _NN) + dot(a_rk[h], v[h], _NN) for h in hs]
    for h in hs:
        s_ref[h] = (s0[h] * jnp.exp(tot[h]) + dot(u[h], b[h] * pend[h], _TN)
                    + dot(v[h], k[h] * pend[h], _TN))
    ones = jnp.ones((n, n), BF16)
    cols = [slice(h * n, (h + 1) * n) for h in hs]
    mean = [_split_dot(y[h], ones) * (1.0 / n) for h in hs]
    d = [y[h] - mean[h] for h in hs]
    var = [_split_dot(d[h] * d[h], ones) * (1.0 / n) for h in hs]
    rk = [_split_dot(r[h] * k[h] * rk_ref[:, cols[h]], ones) for h in hs]
    for h in hs:
        yn = d[h] * lax.rsqrt(var[h] + LNX_EPS)
        y_ref[:, cols[h]] = (yn * lng_ref[:, cols[h]] + lnb_ref[:, cols[h]] + rk[h] * v[h]) * g_ref[:, cols[h]]


def wkv7(r, lw, k, v, a, b, g, r_k, lnx_g, lnx_b, n=B_HEAD):
    bsz, t, width = r.shape
    nh = width // n
    c = min(WKV_CHUNK, t)
    spec = pl.BlockSpec((None, c, width), lambda bi, ci: (bi, ci, 0))
    par = pl.BlockSpec((1, width), lambda bi, ci: (0, 0))
    vec = lambda x: x.reshape(1, width).astype(F32)
    return pl.pallas_call(
        functools.partial(_wkv_body, nh=nh, n=n, c=c),
        grid=(bsz, t // c),
        in_specs=[spec] * 7 + [par] * 3,
        out_specs=spec,
        out_shape=jax.ShapeDtypeStruct((bsz, t, width), F32),
        scratch_shapes=[pltpu.VMEM((nh, n, n), F32)],
        compiler_params=_params(("parallel", "arbitrary")),
        name="wkv7_chunked",
    )(r, lw, k, v, a, b, g, vec(r_k), vec(lnx_g), vec(lnx_b))


def _dsa_index_body(ki_ref, qit_ref, wit_ref, bias_ref, key_ref, dig_ref, cut_ref, *, tq, tkc, topk, nheads, t):
    i = pl.program_id(1)
    nch = ((i + 1) * tq + tkc - 1) // tkc
    qpos = i * tq + lax.broadcasted_iota(I32, (1, tq), 1)

    def kpos_of(ci):
        return ci * tkc + lax.broadcasted_iota(I32, (tkc, 1), 0)

    def rows(ci):
        return pl.ds(pl.multiple_of(ci * tkc, tkc), tkc)

    def score_chunk(ci, carry):
        kc = ki_ref[rows(ci), :]
        acc = jnp.zeros((tkc, tq), F32)
        for h in range(nheads):
            s = jnp.dot(kc, qit_ref[h], preferred_element_type=F32)
            acc = acc + jnp.maximum(s, 0.0) * wit_ref[h]
        acc = jnp.where(kpos_of(ci) <= qpos, acc, -jnp.inf)
        bits = lax.bitcast_convert_type(acc, I32)
        key_ref[rows(ci), :] = jnp.where(bits >= 0, bits, bits ^ 0x7FFFFFFF)
        return carry

    lax.fori_loop(0, nch, score_chunk, 0)

    def count(pred):
        def body(ci, cnt):
            ones = pred(key_ref[rows(ci), :], kpos_of(ci))
            return cnt + jnp.sum(ones, axis=0, keepdims=True)
        return lax.fori_loop(0, nch, body, jnp.zeros((1, tq), I32))

    fold = 64
    assert (t // tkc) * (tkc // fold) <= 256

    def count_digit_ge(cand):
        cand_bf = cand.astype(F32).astype(BF16)

        def body(ci, acc):
            hit = jnp.where(dig_ref[rows(ci), :] >= cand_bf, jnp.ones((), BF16), jnp.zeros((), BF16))
            for s in range(tkc // fold):
                acc = acc + hit[s * fold:(s + 1) * fold]
            return acc

        acc = lax.fori_loop(0, nch, body, jnp.zeros((fold, tq), BF16))
        return jnp.sum(acc.astype(F32), axis=0, keepdims=True).astype(I32)

    prefix = jnp.zeros((1, tq), I32)
    above = jnp.zeros((1, tq), I32)
    n_ge = above
    for level in (3, 2, 1, 0):
        shift = 8 * level

        def build(ci, carry, shift=shift, level=level, prefix=prefix):
            u = key_ref[rows(ci), :] ^ INT_MIN
            digit = lax.shift_right_logical(u, jnp.int32(shift)) & 255
            if level < 3:
                same = lax.shift_right_logical(u, jnp.int32(shift + 8)) == lax.shift_right_logical(
                    prefix, jnp.int32(shift + 8))
                digit = jnp.where(same, digit, -1)
            dig_ref[rows(ci), :] = digit.astype(F32).astype(BF16)
            return carry

        lax.fori_loop(0, nch, build, 0)

        def digit_bit(step, d, above=above):
            cand = d | lax.shift_left(jnp.int32(1), 7 - step)
            return jnp.where(above + count_digit_ge(cand) >= topk, cand, d)

        d = lax.fori_loop(0, 8, digit_bit, jnp.zeros((1, tq), I32))
        if level == 0:
            n_ge = above + count_digit_ge(d)
        above = above + count_digit_ge(d + 1)
        prefix = prefix | lax.shift_left(d, jnp.int32(shift))

    thr = prefix ^ INT_MIN
    n_gt = above
    need = topk - n_gt
    cut_ref[...] = jnp.full((1, tq), t, I32)

    @pl.when(jnp.max(n_ge) > topk)
    def _():
        nbits = max(1, (t - 1).bit_length())

        def index_bit(step, x):
            cand = x | lax.shift_left(jnp.int32(1), nbits - 1 - step)
            g = count(lambda key, kpos: jnp.where(key == thr, jnp.where(kpos < cand, 1, 0), 0))
            return jnp.where(g < need, cand, x)

        cut_ref[...] = lax.fori_loop(0, nbits, index_bit, jnp.zeros((1, tq), I32))

    cutoff = cut_ref[...]

    def write_chunk(ci, carry):
        key = key_ref[rows(ci), :]
        kpos = kpos_of(ci)
        tie = jnp.where(kpos <= cutoff, 0.0, NEG)
        sel = jnp.where(key > thr, 0.0, jnp.where(key == thr, tie, NEG))
        bias_ref[rows(ci), :] = jnp.where(kpos <= qpos, sel, NEG).astype(BF16)
        return carry

    lax.fori_loop(0, nch, write_chunk, 0)

    def fill_chunk(ci, carry):
        bias_ref[rows(ci), :] = jnp.full((tkc, tq), NEG, BF16)
        return carry

    lax.fori_loop(nch, t // tkc, fill_chunk, 0)


def dsa_select(ki, qi, wi, topk):
    b, t, nh, d = qi.shape
    tq = _pick(t, (256, 128))
    tkc = _pick(t, (512, 256, 128))
    nq = t // tq
    qit = jnp.transpose(qi.astype(BF16).reshape(b, nq, tq, nh, d), (0, 1, 3, 4, 2))
    wit = jnp.transpose(wi.astype(F32).reshape(b, nq, tq, nh), (0, 1, 3, 2))[:, :, :, None, :]
    return pl.pallas_call(
        functools.partial(_dsa_index_body, tq=tq, tkc=tkc, topk=topk, nheads=nh, t=t),
        grid=(b, nq),
        in_specs=[pl.BlockSpec((None, t, d), lambda bi, qi_: (bi, 0, 0)),
                  pl.BlockSpec((None, None, nh, d, tq), lambda bi, qi_: (bi, qi_, 0, 0, 0)),
                  pl.BlockSpec((None, None, nh, 1, tq), lambda bi, qi_: (bi, qi_, 0, 0, 0))],
        out_specs=pl.BlockSpec((None, t, tq), lambda bi, qi_: (bi, 0, qi_)),
        out_shape=jax.ShapeDtypeStruct((b, t, t), BF16),
        scratch_shapes=[pltpu.VMEM((t, tq), I32), pltpu.VMEM((t, tq), BF16), pltpu.VMEM((1, tq), I32)],
        compiler_params=_params(("parallel", "arbitrary")),
        name="dsa_index_topk",
    )(ki.astype(BF16), qit, wit)


def _dsa_attn_body(k_ref, vt_ref, q_ref, bias_ref, o_ref, m_ref, acc_ref, pa_ref, aa_ref, pb_ref, ab_ref,
                   *, tq, tk, rep, d):
    i = pl.program_id(2)
    nch = ((i + 1) * tq + tk - 1) // tk
    m_ref[...] = jnp.full(m_ref.shape, NEG, F32)
    acc_ref[...] = jnp.zeros(acc_ref.shape, F32)
    buf_a = (pa_ref, aa_ref)
    buf_b = (pb_ref, ab_ref)

    def scores(c, buf):
        rows = pl.ds(pl.multiple_of(c * tk, tk), tk)
        kj = k_ref[rows, :]
        bj = bias_ref[rows, :]
        fns = [lambda r=r: lax.dot_general(kj, q_ref[:, r * d:(r + 1) * d], (_NT, ((), ())),
                                           preferred_element_type=F32).astype(BF16) + bj
               for r in range(rep)]
        _score_stage(fns, m_ref, *buf)

    def values(c, buf):
        _value_stage(vt_ref[c], acc_ref, *buf)

    scores(0, buf_a)

    def pair(t, carry):
        c = 2 * t + 1
        scores(c, buf_b)
        values(c - 1, buf_a)
        scores(c + 1, buf_a)
        values(c, buf_b)
        return carry

    lax.fori_loop(0, (nch - 1) // 2, pair, 0)
    last = nch - 1

    @pl.when(last % 2 == 1)
    def _():
        scores(last, buf_b)
        values(last - 1, buf_a)
        values(last, buf_b)

    @pl.when(last % 2 == 0)
    def _():
        values(last, buf_a)

    for r in range(rep):
        acc = acc_ref[r]
        o_ref[:, r * d:(r + 1) * d] = (acc[:d] / acc[d:d + 1]).T.astype(o_ref.dtype)


def dsa_attention(q, k, v, bias):
    b, t, qw = q.shape
    d = C_HEAD_DIM
    g = k.shape[2] // d
    rep = qw // (g * d)
    tq = _pick(t, (256, 128))
    tk = _pick(t, (512, 256, 128))
    nq, nk = t // tq, t // tk
    vt = jnp.transpose(v.astype(BF16).reshape(b, nk, tk, g, d), (0, 3, 1, 4, 2))
    vt = _with_ones_rows(vt)
    dve = d + ONES_ROWS
    group = pl.BlockSpec((None, tq, rep * d), lambda bi, gi, qi: (bi, qi, gi))
    return pl.pallas_call(
        functools.partial(_dsa_attn_body, tq=tq, tk=tk, rep=rep, d=d),
        grid=(b, g, nq),
        in_specs=[pl.BlockSpec((None, t, d), lambda bi, gi, qi: (bi, 0, gi)),
                  pl.BlockSpec((None, None, nk, dve, tk), lambda bi, gi, qi: (bi, gi, 0, 0, 0)),
                  group,
                  pl.BlockSpec((None, t, tq), lambda bi, gi, qi: (bi, 0, qi))],
        out_specs=group,
        out_shape=jax.ShapeDtypeStruct((b, t, qw), BF16),
        scratch_shapes=[pltpu.VMEM((rep, 1, tq), F32), pltpu.VMEM((rep, dve, tq), F32)]
        + [pltpu.VMEM((rep, tk, tq), BF16), pltpu.VMEM((rep, 1, tq), F32)] * 2,
        compiler_params=_params(("parallel", "parallel", "arbitrary")),
        name="dsa_attention",
    )(k.astype(BF16), vt, q.astype(BF16), bias)


LANES = 128


def _rope_tables(pos, head_dim, scale=1.0):
    rot = head_dim // ROPE_FRACTION
    half = rot // 2
    inv_freq = ROPE_THETA ** (-(jnp.arange(half, dtype=F32) * 2.0 / rot))
    ang = pos.astype(F32).reshape(-1, 1) * inv_freq
    cos, sin = jnp.cos(ang), jnp.sin(ang)
    zero = jnp.zeros_like(sin)
    rest = jnp.zeros((ang.shape[0], head_dim - rot), F32)
    tabs = (jnp.concatenate([cos, cos, rest + 1.0], axis=-1),
            jnp.concatenate([-sin, zero, rest], axis=-1),
            jnp.concatenate([zero, sin, rest], axis=-1))
    return tuple(jnp.tile(x * scale, (1, LANES // head_dim)) for x in tabs)


def _rope_body(z_ref, c_ref, up_ref, down_ref, o_ref, *, half, ncols):
    c, up, down = c_ref[...], up_ref[...], down_ref[...]
    for cb in range(ncols // LANES):
        cols = slice(cb * LANES, (cb + 1) * LANES)
        x = z_ref[:, cols]
        y = x * c + pltpu.roll(x, LANES - half, 1) * up + pltpu.roll(x, half, 1) * down
        o_ref[:, cols] = y.astype(o_ref.dtype)


def rope_cast(z, col0, ncols, head_dim, tables):
    m = z.shape[0]
    assert col0 % ncols == 0 and ncols % LANES == 0 and LANES % head_dim == 0
    tm = _pick(m, (512, 256, 128, 64, 32, 16, 8))
    tab = pl.BlockSpec((tm, LANES), lambda i: (i, 0))
    return pl.pallas_call(
        functools.partial(_rope_body, half=head_dim // ROPE_FRACTION // 2, ncols=ncols),
        grid=(m // tm,),
        in_specs=[pl.BlockSpec((tm, ncols), lambda i: (i, col0 // ncols)), tab, tab, tab],
        out_specs=pl.BlockSpec((tm, ncols), lambda i: (i, 0)),
        out_shape=jax.ShapeDtypeStruct((m, ncols), BF16),
        compiler_params=_params(("parallel",)),
        name="rope_cast",
    )(z, *tables)


def _rms_norm(x, g, eps=NORM_EPS):
    xf = x.astype(F32)
    return xf * lax.rsqrt(jnp.mean(xf * xf, axis=-1, keepdims=True) + eps) * g.astype(F32)


def _layer_norm(x, g, b, eps=LN_EPS):
    mu = jnp.mean(x, axis=-1, keepdims=True)
    var = jnp.mean(jnp.square(x - mu), axis=-1, keepdims=True)
    return (x - mu) * lax.rsqrt(var + eps) * g + b


def _partial_rope(x, pos):
    dh = x.shape[-1]
    rot = dh // ROPE_FRACTION
    half = rot // 2
    inv_freq = ROPE_THETA ** (-(jnp.arange(half, dtype=F32) * 2.0 / rot))
    ang = pos.astype(F32)[..., None] * inv_freq
    cos = jnp.cos(ang)[:, :, None, :]
    sin = jnp.sin(ang)[:, :, None, :]
    x1 = x[..., :half]
    x2 = x[..., half:rot]
    return jnp.concatenate([x1 * cos - x2 * sin, x2 * cos + x1 * sin, x[..., rot:]], axis=-1)


def _pad_cols(w, n):
    return jnp.pad(w, ((0, 0), (0, n - w.shape[1])))


def _round_up(n, m):
    return (n + m - 1) // m * m


def _mixer_ab(u, pos, w_in, lq1, lk1, lq2, lk2, subln_g, mu, w0, w2, a0, a2, g2,
              k_k, k_a, r_k, lnx_g, lnx_b, lambda_init):
    b, t, d = u.shape
    a_width = d // 2
    b_width = d // 2
    a_in = 3 * a_width
    n_in = w_in.shape[1]
    z = matmul(u.reshape(b * t, d), _pad_cols(w_in, _round_up(n_in, 512)))

    seq = lambda x: x.reshape(b, t, a_width)
    qa = rope_cast(z, 0, a_width, A_DC, _rope_tables(pos, A_DC, A_DC ** -0.5 * LOG2E))
    ka = rope_cast(z, a_width, a_width, A_DC, _rope_tables(pos, A_DC))
    va = z[:, 2 * a_width:a_in].astype(BF16)
    lam = jnp.exp(jnp.sum(lq1 * lk1)) - jnp.exp(jnp.sum(lq2 * lk2)) + lambda_init
    ya = diff_attention(seq(qa), seq(ka), seq(va), lam, subln_g * (1.0 - lambda_init))

    prep = rwkv_prep(z, a_in, b_width, n_in - a_in - 3 * b_width, t, mu, w0, w2, a0, a2, g2, k_k, k_a)
    yb = wkv7(*(x.reshape(b, t, b_width) for x in prep), r_k.reshape(-1), lnx_g, lnx_b)
    return ya, yb


def _mixer_c(u, pos, w_in, idx_k_g, idx_k_b, topk):
    b, t, d = u.shape
    c_heads = d // C_HEAD_DIM
    n_in = w_in.shape[1]
    c1 = c_heads * C_HEAD_DIM
    c2 = c1 + C_KV_HEADS * C_HEAD_DIM
    c3 = c2 + C_KV_HEADS * C_HEAD_DIM
    c4 = c3 + IDX_HEADS * IDX_DIM
    c5 = c4 + IDX_DIM
    z = matmul(u.reshape(b * t, d), _pad_cols(w_in, _round_up(n_in, 512)))
    seq = lambda x: x.reshape(b, t, x.shape[-1])
    tabs = _rope_tables(pos, C_HEAD_DIM)
    q = rope_cast(z, 0, c1, C_HEAD_DIM, _rope_tables(pos, C_HEAD_DIM, C_HEAD_DIM ** -0.5 * LOG2E))
    k = rope_cast(z, c1, c2 - c1, C_HEAD_DIM, tabs)
    v = z[:, c2:c3].astype(BF16)
    qi = rope_cast(z, c3, c4 - c3, IDX_DIM, _rope_tables(pos, IDX_DIM))
    ki = _partial_rope(_layer_norm(seq(z[:, c4:c5]), idx_k_g, idx_k_b)[:, :, None, :], pos)[:, :, 0]
    wi = seq(z[:, c5:n_in]) * ((IDX_HEADS * IDX_DIM) ** -0.5)
    bias = dsa_select(ki, qi.reshape(b, t, IDX_HEADS, IDX_DIM), wi, topk)
    return (dsa_attention(seq(q), seq(k), seq(v), bias),)


def kernel(x, p, positions, mix_pre_g, mix_post_g, mlp_pre_g, mlp_post_g, w_mlp_up, w_mlp_down, w_ple_proj, w_ple_gate, ple_post_g, ab_w_in, ab_w_out, diff_lq1, diff_lk1, diff_lq2, diff_lk2, diff_subln_g, rwkv_mu, rwkv_w0, rwkv_w2, rwkv_a0, rwkv_a2, rwkv_g2, rwkv_k_k, rwkv_k_a, rwkv_r_k, rwkv_lnx_g, rwkv_lnx_b, c_w_in, c_w_out, idx_k_g, idx_k_b):
    depth = mix_pre_g.shape[0]
    b, t, d = x.shape
    topk = min(TOPK_MAX, t // 4)
    flat = lambda z: z.reshape(b * t, z.shape[-1])
    h = flat(x)
    u = _rms_norm(h, mix_pre_g[0]).astype(BF16)
    for i in range(depth):
        j = i // 2
        u3 = u.reshape(b, t, d)
        if i % 2 == 0:
            lambda_init = 0.8 - 0.6 * math.exp(-0.3 * i)
            m = _mixer_ab(u3, positions, ab_w_in[j], diff_lq1[j], diff_lk1[j], diff_lq2[j],
                          diff_lk2[j], diff_subln_g[j], rwkv_mu[j], rwkv_w0[j], rwkv_w2[j], rwkv_a0[j],
                          rwkv_a2[j], rwkv_g2[j], rwkv_k_k[j], rwkv_k_a[j], rwkv_r_k[j], rwkv_lnx_g[j],
                          rwkv_lnx_b[j], lambda_init)
            w_out = ab_w_out[j]
        else:
            m = _mixer_c(u3, positions, c_w_in[j], idx_k_g[j], idx_k_b[j], topk)
            w_out = c_w_out[j]
        h, u = proj_residual([flat(part) for part in m], w_out, h, mix_post_g[i], mlp_pre_g[i])
        h = mlp_residual(u, w_mlp_up[i], w_mlp_down[i], h, mlp_post_g[i])
        g_next = mix_pre_g[i + 1] if i + 1 < depth else None
        h, u = ple_residual(flat(p[i]), w_ple_proj[i], w_ple_gate[i], h, ple_post_g[i], g_next)
    return h.reshape(b, t, d)
```

```python
import functools
import math

import jax
import jax.numpy as jnp
from jax import lax
from jax.experimental import pallas as pl
from jax.experimental.pallas import tpu as pltpu

F32 = jnp.float32
BF16 = jnp.bfloat16
I32 = jnp.int32

V7X_VMEM_LIMIT_BYTES = 56 * 1024 * 1024
NEG = -1e30
LOG2E = 1.4426950408889634
SUBLANES = 8
INT_MIN = -2147483648

NORM_EPS = 1e-6
ROPE_THETA = 500000.0
ROPE_FRACTION = 4
A_DV = 128
A_DC = 64
DIFF_SUBLN_EPS = 1e-5
B_HEAD = 64
LNX_EPS = 64e-5
C_HEAD_DIM = 128
C_KV_HEADS = 4
IDX_HEADS = 16
IDX_DIM = 64
TOPK_MAX = 256
LN_EPS = 1e-6
WKV_CHUNK = 64


def _params(sem):
    return pltpu.CompilerParams(dimension_semantics=sem, vmem_limit_bytes=V7X_VMEM_LIMIT_BYTES)


def _pick(n, prefs):
    for t in prefs:
        if n % t == 0:
            return t
    return n


def _mm_body(a_ref, w_ref, o_ref, *scratch, nk, act):
    def epilogue(acc):
        if act == "relu2":
            r = jnp.maximum(acc, 0.0)
            acc = r * r
        return acc.astype(o_ref.dtype)

    if nk == 1:
        o_ref[...] = epilogue(jnp.dot(a_ref[...], w_ref[...], preferred_element_type=F32))
        return
    (acc_ref,) = scratch
    k = pl.program_id(2)

    @pl.when(k == 0)
    def _():
        acc_ref[...] = jnp.zeros_like(acc_ref)

    acc_ref[...] += jnp.dot(a_ref[...], w_ref[...], preferred_element_type=F32)

    @pl.when(k == nk - 1)
    def _():
        o_ref[...] = epilogue(acc_ref[...])


def matmul(a, w, out_dtype=F32, act=None):
    a = a.astype(BF16)
    w = w.astype(BF16)
    m, kdim = a.shape
    n = w.shape[1]
    tn = _pick(n, (2048, 1024, 512, 256, 128))
    tile_elems = (4 << 20) // jnp.dtype(out_dtype).itemsize
    tm = _pick(m, tuple(r for r in (2048, 1024, 512, 256, 128, 64, 32, 16, 8) if r * tn <= tile_elems))
    tk = kdim if kdim <= 2048 else _pick(kdim, (2048, 1024, 512))
    nk = kdim // tk
    scratch = [pltpu.VMEM((tm, tn), F32)] if nk > 1 else []
    return pl.pallas_call(
        functools.partial(_mm_body, nk=nk, act=act),
        grid=(m // tm, n // tn, nk),
        in_specs=[pl.BlockSpec((tm, tk), lambda i, j, k: (i, k)),
                  pl.BlockSpec((tk, tn), lambda i, j, k: (k, j))],
        out_specs=pl.BlockSpec((tm, tn), lambda i, j, k: (i, j)),
        out_shape=jax.ShapeDtypeStruct((m, n), out_dtype),
        scratch_shapes=scratch,
        compiler_params=_params(("parallel", "parallel", "arbitrary")),
        name="dense_matmul",
    )(a, w)


def _rms_rows(x, g):
    return x * lax.rsqrt(jnp.mean(x * x, axis=-1, keepdims=True) + NORM_EPS) * g


def _residual_update(m, h_ref, gpost_ref, gnext_ref, ho_ref, uo_ref):
    h_new = h_ref[...] + _rms_rows(m, gpost_ref[...])
    ho_ref[...] = h_new
    if uo_ref is not None:
        uo_ref[...] = _rms_rows(h_new, gnext_ref[...]).astype(BF16)


def _proj_res_body(*refs, nk, has_next, widths):
    a_refs, refs = refs[:len(widths)], refs[len(widths):]
    if has_next:
        w_ref, h_ref, gpost_ref, gnext_ref, ho_ref, uo_ref = refs
    else:
        w_ref, h_ref, gpost_ref, ho_ref = refs
        gnext_ref = uo_ref = None
    part, lo = None, 0
    for a_ref, width in zip(a_refs, widths):
        rows = slice(None) if len(widths) == 1 else slice(lo, lo + width)
        term = jnp.dot(a_ref[...].astype(BF16), w_ref[rows, :], preferred_element_type=F32)
        part = term if part is None else part + term
        lo += width
    if nk == 1:
        _residual_update(part, h_ref, gpost_ref, gnext_ref, ho_ref, uo_ref)
        return
    k = pl.program_id(1)

    @pl.when(k == 0)
    def _():
        ho_ref[...] = part

    @pl.when(jnp.logical_and(k > 0, k < nk - 1))
    def _():
        ho_ref[...] += part

    @pl.when(k == nk - 1)
    def _():
        _residual_update(ho_ref[...] + part, h_ref, gpost_ref, gnext_ref, ho_ref, uo_ref)


def proj_residual(a_parts, w, h, g_post, g_next):
    m = h.shape[0]
    widths = tuple(a.shape[1] for a in a_parts)
    kdim = sum(widths)
    n = w.shape[1]
    tm = _pick(m, (512, 256, 128, 64, 32, 16, 8))
    tk = _pick(kdim, (2048, 1024, 512, 256, 128))
    nk = kdim // tk
    assert nk == 1 or len(a_parts) == 1
    has_next = g_next is not None
    row = pl.BlockSpec((tm, n), lambda i, k: (i, 0))
    vec = pl.BlockSpec((1, n), lambda i, k: (0, 0))
    if len(a_parts) == 1:
        a_specs = [pl.BlockSpec((tm, tk), lambda i, k: (i, k))]
    else:
        a_specs = [pl.BlockSpec((tm, wd), lambda i, k: (i, 0)) for wd in widths]
    in_specs = a_specs + [pl.BlockSpec((tk, n), lambda i, k: (k, 0)), row, vec]
    args = list(a_parts) + [w.astype(BF16), h, g_post.reshape(1, n).astype(F32)]
    out_shape = [jax.ShapeDtypeStruct((m, n), F32)]
    out_specs = [row]
    if has_next:
        in_specs.append(vec)
        args.append(g_next.reshape(1, n).astype(F32))
        out_shape.append(jax.ShapeDtypeStruct((m, n), BF16))
        out_specs.append(row)
    out = pl.pallas_call(
        functools.partial(_proj_res_body, nk=nk, has_next=has_next, widths=widths),
        grid=(m // tm, nk),
        in_specs=in_specs, out_specs=out_specs, out_shape=out_shape,
        compiler_params=_params(("parallel", "arbitrary")),
        name="proj_residual",
    )(*args)
    return (out[0], out[1]) if has_next else (out[0], None)


def _proj_normed_body(a_ref, w_ref, g_ref, o_ref, *, nk):
    k = pl.program_id(1)
    part = jnp.dot(a_ref[...], w_ref[...], preferred_element_type=F32)

    @pl.when(k == 0)
    def _():
        o_ref[...] = part

    @pl.when(jnp.logical_and(k > 0, k < nk - 1))
    def _():
        o_ref[...] += part

    @pl.when(k == nk - 1)
    def _():
        o_ref[...] = _rms_rows(o_ref[...] + part, g_ref[...])


def proj_normed(a, w, g):
    m, kdim = a.shape
    n = w.shape[1]
    tm = _pick(m, (1024, 512, 256, 128, 64, 32, 16, 8))
    tk = _pick(kdim, (2048, 1024, 512, 256, 128))
    nk = kdim // tk
    assert nk >= 2
    return pl.pallas_call(
        functools.partial(_proj_normed_body, nk=nk),
        grid=(m // tm, nk),
        in_specs=[pl.BlockSpec((tm, tk), lambda i, k: (i, k)), pl.BlockSpec((tk, n), lambda i, k: (k, 0)),
                  pl.BlockSpec((1, n), lambda i, k: (0, 0))],
        out_specs=pl.BlockSpec((tm, n), lambda i, k: (i, 0), pipeline_mode=pl.Buffered(1)),
        out_shape=jax.ShapeDtypeStruct((m, n), F32),
        compiler_params=_params(("parallel", "arbitrary")),
        name="proj_normed",
    )(a.astype(BF16), w.astype(BF16), g.reshape(1, n).astype(F32))


def _ple_body(*refs, has_next):
    if has_next:
        p_ref, wp_ref, wg_ref, h_ref, d_ref, gpost_ref, gnext_ref, ho_ref, uo_ref = refs
    else:
        p_ref, wp_ref, wg_ref, h_ref, d_ref, gpost_ref, ho_ref = refs
        gnext_ref = uo_ref = None
    h_in = h_ref[...] + d_ref[...]
    e = jnp.dot(p_ref[...], wp_ref[...], preferred_element_type=F32)
    gate = jax.nn.sigmoid(jnp.dot(h_in.astype(BF16), wg_ref[...], preferred_element_type=F32))
    h_new = h_in + _rms_rows(e * gate, gpost_ref[...])
    ho_ref[...] = h_new
    if uo_ref is not None:
        uo_ref[...] = _rms_rows(h_new, gnext_ref[...]).astype(BF16)


def ple_residual(p, w_proj, w_gate, h, delta, g_post, g_next):
    m, n = h.shape
    pd = p.shape[1]
    tm = _pick(m, (256, 128, 64, 32, 16, 8))
    has_next = g_next is not None
    row = pl.BlockSpec((tm, n), lambda i: (i, 0))
    vec = pl.BlockSpec((1, n), lambda i: (0, 0))
    once = pl.Buffered(1)
    in_specs = [pl.BlockSpec((tm, pd), lambda i: (i, 0)),
                pl.BlockSpec((pd, n), lambda i: (0, 0), pipeline_mode=once),
                pl.BlockSpec((n, n), lambda i: (0, 0), pipeline_mode=once), row, row, vec]
    args = [p.astype(BF16), w_proj.astype(BF16), w_gate.astype(BF16), h, delta,
            g_post.reshape(1, n).astype(F32)]
    out_shape = [jax.ShapeDtypeStruct((m, n), F32)]
    out_specs = [row]
    if has_next:
        in_specs.append(vec)
        args.append(g_next.reshape(1, n).astype(F32))
        out_shape.append(jax.ShapeDtypeStruct((m, n), BF16))
        out_specs.append(row)
    out = pl.pallas_call(
        functools.partial(_ple_body, has_next=has_next),
        grid=(m // tm,),
        in_specs=in_specs, out_specs=out_specs, out_shape=out_shape,
        compiler_params=_params(("parallel",)),
        name="ple_residual",
    )(*args)
    return (out[0], out[1]) if has_next else (out[0], None)


ONES_ROWS = 16


def _with_ones_rows(vt):
    pad = jnp.zeros(vt.shape[:-2] + (ONES_ROWS, vt.shape[-1]), vt.dtype).at[..., 0, :].set(1)
    return jnp.concatenate([vt, pad], axis=-2)


def _score_stage(score_fns, m_ref, p_ref, alpha_ref):
    n = range(len(score_fns))
    ss = [fn() for fn in score_fns]
    m_old = [m_ref[i] for i in n]
    m_new = [jnp.maximum(m_old[i], jnp.max(ss[i], axis=0, keepdims=True).astype(F32)) for i in n]
    for i in n:
        alpha_ref[i] = jnp.exp2(m_old[i] - m_new[i])
        m_ref[i] = m_new[i]
    for i in n:
        p_ref[i] = jnp.exp2(ss[i] - m_new[i].astype(BF16))


def _value_stage(vt, acc_ref, p_ref, alpha_ref):
    n = range(p_ref.shape[0])
    pv = [jnp.dot(vt, p_ref[i], preferred_element_type=F32) for i in n]
    for i in n:
        acc_ref[i] = alpha_ref[i] * acc_ref[i] + pv[i]


def _diff_attn_body(lam_ref, k_ref, q_ref, cmask_ref, vt_ref, g_ref, o_ref, m_ref, acc_ref, qz_ref,
                    pa_ref, aa_ref, pb_ref, ab_ref, *, tq, nsp):
    i = pl.program_id(2)
    tl = tq // nsp
    m_ref[...] = jnp.full(m_ref.shape, NEG, F32)
    acc_ref[...] = jnp.zeros(acc_ref.shape, F32)
    for c in range(2):
        qz_ref[c] = q_ref[...] * cmask_ref[c]

    buf_a = (pa_ref, aa_ref)
    buf_b = (pb_ref, ab_ref)

    def scores(j, buf, masked=False):
        kj = k_ref[pl.ds(pl.multiple_of(j * tq, tq), tq), :]

        def score(c, sp):
            s = lax.dot_general(kj, qz_ref[c, sp * tl:(sp + 1) * tl, :], (_NT, ((), ())),
                                preferred_element_type=F32)
            if masked:
                row = lax.broadcasted_iota(I32, (tq, tl), 0)
                col = lax.broadcasted_iota(I32, (tq, tl), 1)
                s = jnp.where(row <= col + sp * tl, s, NEG)
            return s.astype(BF16)

        _score_stage([functools.partial(score, c, sp) for c in range(2) for sp in range(nsp)], m_ref, *buf)

    def values(j, buf):
        _value_stage(vt_ref[j], acc_ref, *buf)

    @pl.when(i == 0)
    def _():
        scores(0, buf_a, masked=True)
        values(0, buf_a)

    @pl.when(i > 0)
    def _():
        scores(0, buf_a)

        def pair(t, carry):
            c = 2 * t + 1
            scores(c, buf_b)
            values(c - 1, buf_a)
            scores(c + 1, buf_a)
            values(c, buf_b)
            return carry

        lax.fori_loop(0, (i - 1) // 2, pair, 0)

        @pl.when((i - 1) % 2 == 1)
        def _():
            scores(i - 1, buf_b)
            values(i - 2, buf_a)
            scores(i, buf_a, masked=True)
            values(i - 1, buf_b)
            values(i, buf_a)

        @pl.when((i - 1) % 2 == 0)
        def _():
            scores(i, buf_b, masked=True)
            values(i - 1, buf_a)
            values(i, buf_b)

    lam = lam_ref[0]
    for sp in range(nsp):
        a0 = acc_ref[sp]
        a1 = acc_ref[nsp + sp]
        out = (a0[:A_DV] / a0[A_DV:A_DV + 1] - lam * (a1[:A_DV] / a1[A_DV:A_DV + 1])).T
        y = out * lax.rsqrt(jnp.mean(out * out, axis=-1, keepdims=True) + DIFF_SUBLN_EPS) * g_ref[...]
        o_ref[sp * tl:(sp + 1) * tl, :] = y.astype(o_ref.dtype)


def diff_attention(q, k, v, lam, gain):
    b, t, width = q.shape
    h = width // A_DV
    tq = _pick(t, (512, 256, 128))
    nq = t // tq
    vt = jnp.transpose(v.astype(BF16).reshape(b, nq, tq, h, A_DV), (0, 3, 1, 4, 2))
    vt = _with_ones_rows(vt)
    dve = A_DV + ONES_ROWS
    nsp = 2 if tq >= 256 else 1
    lane = jnp.arange(A_DV)
    cmask = jnp.stack([lane < A_DC, lane >= A_DC]).astype(BF16).reshape(2, 1, A_DV)
    head = pl.BlockSpec((None, tq, A_DV), lambda bi, hi, qi: (bi, qi, hi))
    return pl.pallas_call(
        functools.partial(_diff_attn_body, tq=tq, nsp=nsp),
        grid=(b, h, nq),
        in_specs=[pl.BlockSpec(memory_space=pltpu.SMEM),
                  pl.BlockSpec((None, t, A_DV), lambda bi, hi, qi: (bi, 0, hi)),
                  head,
                  pl.BlockSpec((2, 1, A_DV), lambda bi, hi, qi: (0, 0, 0)),
                  pl.BlockSpec((None, None, nq, dve, tq), lambda bi, hi, qi: (bi, hi, 0, 0, 0)),
                  pl.BlockSpec((1, A_DV), lambda bi, hi, qi: (0, 0))],
        out_specs=head,
        out_shape=jax.ShapeDtypeStruct((b, t, width), BF16),
        scratch_shapes=[pltpu.VMEM((2 * nsp, 1, tq // nsp), F32),
                        pltpu.VMEM((2 * nsp, dve, tq // nsp), F32),
                        pltpu.VMEM((2, tq, A_DV), BF16)]
        + [pltpu.VMEM((2 * nsp, tq, tq // nsp), BF16), pltpu.VMEM((2 * nsp, 1, tq // nsp), F32)] * 2,
        compiler_params=_params(("parallel", "parallel", "arbitrary")),
        name="diff_attention",
    )(lam.reshape(1).astype(F32), k.astype(BF16), q.astype(BF16), cmask, vt, gain.reshape(1, A_DV).astype(F32))


def _split_dot(x, w):
    hi = x.astype(BF16)
    lo = (x - hi.astype(F32)).astype(BF16)
    return jnp.dot(hi, w, preferred_element_type=F32) + jnp.dot(lo, w, preferred_element_type=F32)


def _rwkv_prep_body(zr_ref, zk_ref, zv_ref, zl_ref, pr_ref, pk_ref, pv_ref, pl_ref,
                    mur_ref, muk_ref, muv_ref, mul_ref, w0_ref, a0_ref, kk_ref, ka_ref,
                    w2_ref, a2_ref, g2_ref, hsum_ref,
                    r_ref, lw_ref, k_ref, v_ref, an_ref, bn_ref, g_ref, *, tm, seq):
    first = (pl.program_id(0) * tm) % seq == 0

    def shifted(z_ref, p_ref, mu_ref):
        z = z_ref[...]
        last = jnp.where(first, 0.0, p_ref[SUBLANES - 1:SUBLANES, :])
        row = lax.broadcasted_iota(I32, z.shape, 0)
        prev = jnp.where(row == 0, last, pltpu.roll(z, 1, 0))
        return z + (prev - z) * mu_ref[...]

    r = shifted(zr_ref, pr_ref, mur_ref)
    k = shifted(zk_ref, pk_ref, muk_ref)
    v = shifted(zv_ref, pv_ref, muv_ref)
    lo = shifted(zl_ref, pl_ref, mul_ref)
    lw = jnp.dot(jnp.tanh(lo).astype(BF16), w2_ref[...], preferred_element_type=F32)
    la = jnp.dot(lo.astype(BF16), a2_ref[...], preferred_element_type=F32)
    g = jnp.dot(jax.nn.sigmoid(lo).astype(BF16), g2_ref[...], preferred_element_type=F32)
    w = -jax.nn.softplus(-(w0_ref[...] + lw)) - 0.5
    a = jax.nn.sigmoid(a0_ref[...] + la)
    kk = k * kk_ref[...]
    norm = jnp.sqrt(_split_dot(kk * kk, hsum_ref[...]))
    kk = kk / jnp.maximum(norm, 1e-12)
    r_ref[...] = r
    lw_ref[...] = -jnp.exp(w)
    k_ref[...] = k * (1.0 + (a - 1.0) * ka_ref[...])
    v_ref[...] = v
    an_ref[...] = -kk
    bn_ref[...] = kk * a
    g_ref[...] = g


def rwkv_prep(z, col0, width, n_lora, seq, mu, w0, w2, a0, a2, g2, k_k, k_a):
    m = z.shape[0]
    tm = _pick(seq, (256, 128, 64, 32, 16, 8))
    lw_ = _round_up(n_lora, 128)
    assert col0 % width == 0 and (col0 + 3 * width) % lw_ == 0 and col0 + 3 * width + lw_ <= z.shape[1]
    cb = col0 // width
    lb = (col0 + 3 * width) // lw_
    dl, al = w2.shape[0], a2.shape[0]

    def rows_of(w, start):
        return jnp.zeros((lw_, width), BF16).at[start:start + w.shape[0]].set(w.astype(BF16))

    hsum = jnp.kron(jnp.eye(width // B_HEAD, dtype=BF16), jnp.ones((B_HEAD, B_HEAD), BF16))
    vec = lambda x: x.reshape(1, -1).astype(F32)
    mul = jnp.pad(mu[3 * width:], (0, lw_ - n_lora))
    tile = lambda c, wd: pl.BlockSpec((tm, wd), lambda i: (i, c))
    before = lambda c, wd: pl.BlockSpec((SUBLANES, wd), lambda i: (jnp.maximum(i * (tm // SUBLANES) - 1, 0), c))
    par = lambda wd: pl.BlockSpec((1, wd), lambda i: (0, 0))
    mat = lambda r_, c_: pl.BlockSpec((r_, c_), lambda i: (0, 0))
    out = pl.BlockSpec((tm, width), lambda i: (i, 0))
    return pl.pallas_call(
        functools.partial(_rwkv_prep_body, tm=tm, seq=seq),
        grid=(m // tm,),
        in_specs=[tile(cb, width), tile(cb + 1, width), tile(cb + 2, width), tile(lb, lw_),
                  before(cb, width), before(cb + 1, width), before(cb + 2, width), before(lb, lw_),
                  par(width), par(width), par(width), par(lw_), par(width), par(width), par(width), par(width),
                  mat(lw_, width), mat(lw_, width), mat(lw_, width), mat(width, width)],
        out_specs=[out] * 7,
        out_shape=[jax.ShapeDtypeStruct((m, width), F32)] * 7,
        compiler_params=_params(("parallel",)),
        name="rwkv_prep",
    )(z, z, z, z, z, z, z, z,
      vec(mu[:width]), vec(mu[width:2 * width]), vec(mu[2 * width:3 * width]), vec(mul),
      vec(w0), vec(a0), vec(k_k), vec(k_a),
      rows_of(w2, 0), rows_of(a2, dl), rows_of(g2, dl + al), hsum)


def _dot(a, b, dims):
    return lax.dot_general(a.astype(BF16), b.astype(BF16), (dims, ((), ())), preferred_element_type=F32)


def _prefix_sums(tri, x):
    hi = x.astype(BF16)
    r1 = x - hi.astype(F32)
    mid = r1.astype(BF16)
    lo = (r1 - mid.astype(F32)).astype(BF16)
    return (jnp.dot(tri, hi, preferred_element_type=F32) + jnp.dot(tri, mid, preferred_element_type=F32)
            + jnp.dot(tri, lo, preferred_element_type=F32))


_NN = ((1,), (0,))
_NT = ((1,), (1,))
_TN = ((0,), (0,))


def _wkv_body(r_ref, lw_ref, k_ref, v_ref, a_ref, b_ref, g_ref, rk_ref, lng_ref, lnb_ref, y_ref, s_ref,
              *, nh, n, c):
    @pl.when(pl.program_id(1) == 0)
    def _():
        s_ref[...] = jnp.zeros(s_ref.shape, F32)

    row = lax.broadcasted_iota(I32, (c, c), 0)
    col = lax.broadcasted_iota(I32, (c, c), 1)
    incl = row >= col
    strict = row > col
    tri_bf = jnp.where(incl, 1.0, 0.0).astype(BF16)
    eye = jnp.where(row == col, 1.0, 0.0).astype(F32)
    nsq = int(math.log2(c)) - 1

    hs = range(nh)
    dot = _dot
    heads = lambda ref: [ref[:, h * n:(h + 1) * n] for h in hs]
    lw, r, k, v, a, b = (heads(ref) for ref in (lw_ref, r_ref, k_ref, v_ref, a_ref, b_ref))
    cum = [_prefix_sums(tri_bf, lw[h]) for h in hs]
    tot = [cum[h][c - 1:c, :] for h in hs]
    pinv = [jnp.exp(-cum[h]) for h in hs]
    pend = [jnp.exp(tot[h] - cum[h]) for h in hs]
    at = [a[h] * jnp.exp(cum[h] - lw[h]) for h in hs]
    rt = [r[h] * jnp.exp(cum[h]) for h in hs]
    bt = [b[h] * pinv[h] for h in hs]
    kt = [k[h] * pinv[h] for h in hs]
    a_ab = [jnp.where(strict, dot(at[h], bt[h], _NT), 0.0) for h in hs]
    a_ak = [jnp.where(strict, dot(at[h], kt[h], _NT), 0.0) for h in hs]
    a_rb = [jnp.where(incl, dot(rt[h], bt[h], _NT), 0.0) for h in hs]
    a_rk = [jnp.where(incl, dot(rt[h], kt[h], _NT), 0.0) for h in hs]
    x = a_ab
    minv = [eye + x[h] for h in hs]
    for _ in range(nsq):
        x = [dot(x[h], x[h], _NN) for h in hs]
        minv = [minv[h] + dot(minv[h], x[h], _NN) for h in hs]
    s0 = [s_ref[h] for h in hs]
    rhs = [dot(at[h], s0[h], _NT) + dot(a_ak[h], v[h], _NN) for h in hs]
    u = [dot(minv[h], rhs[h], _NN) for h in hs]
    y = [dot(rt[h], s0[h], _NT) + dot(a_rb[h], u[h], _NN) + dot(a_rk[h], v[h], _NN) for h in hs]
    for h in hs:
        s_ref[h] = (s0[h] * jnp.exp(tot[h]) + dot(u[h], b[h] * pend[h], _TN)
                    + dot(v[h], k[h] * pend[h], _TN))
    ones = jnp.ones((n, n), BF16)
    cols = [slice(h * n, (h + 1) * n) for h in hs]
    mean = [_split_dot(y[h], ones) * (1.0 / n) for h in hs]
    d = [y[h] - mean[h] for h in hs]
    var = [_split_dot(d[h] * d[h], ones) * (1.0 / n) for h in hs]
    rk = [_split_dot(r[h] * k[h] * rk_ref[:, cols[h]], ones) for h in hs]
    for h in hs:
        yn = d[h] * lax.rsqrt(var[h] + LNX_EPS)
        y_ref[:, cols[h]] = (yn * lng_ref[:, cols[h]] + lnb_ref[:, cols[h]] + rk[h] * v[h]) * g_ref[:, cols[h]]


def wkv7(r, lw, k, v, a, b, g, r_k, lnx_g, lnx_b, n=B_HEAD):
    bsz, t, width = r.shape
    nh = width // n
    c = min(WKV_CHUNK, t)
    spec = pl.BlockSpec((None, c, width), lambda bi, ci: (bi, ci, 0))
    par = pl.BlockSpec((1, width), lambda bi, ci: (0, 0))
    vec = lambda x: x.reshape(1, width).astype(F32)
    return pl.pallas_call(
        functools.partial(_wkv_body, nh=nh, n=n, c=c),
        grid=(bsz, t // c),
        in_specs=[spec] * 7 + [par] * 3,
        out_specs=spec,
        out_shape=jax.ShapeDtypeStruct((bsz, t, width), F32),
        scratch_shapes=[pltpu.VMEM((nh, n, n), F32)],
        compiler_params=_params(("parallel", "arbitrary")),
        name="wkv7_chunked",
    )(r, lw, k, v, a, b, g, vec(r_k), vec(lnx_g), vec(lnx_b))


def _dsa_index_body(ki_ref, qit_ref, wit_ref, bias_ref, key_ref, dig_ref, cut_ref, *, tq, tkc, topk, nheads, t):
    i = pl.program_id(1)
    nch = ((i + 1) * tq + tkc - 1) // tkc
    qpos = i * tq + lax.broadcasted_iota(I32, (1, tq), 1)

    def kpos_of(ci):
        return ci * tkc + lax.broadcasted_iota(I32, (tkc, 1), 0)

    def rows(ci):
        return pl.ds(pl.multiple_of(ci * tkc, tkc), tkc)

    def score_chunk(ci, carry):
        kc = ki_ref[rows(ci), :]
        acc = jnp.zeros((tkc, tq), F32)
        for h in range(nheads):
            s = jnp.dot(kc, qit_ref[h], preferred_element_type=F32)
            acc = acc + jnp.maximum(s, 0.0) * wit_ref[h]
        acc = jnp.where(kpos_of(ci) <= qpos, acc, -jnp.inf)
        bits = lax.bitcast_convert_type(acc, I32)
        key_ref[rows(ci), :] = jnp.where(bits >= 0, bits, bits ^ 0x7FFFFFFF)
        return carry

    lax.fori_loop(0, nch, score_chunk, 0)

    def count(pred):
        def body(ci, cnt):
            ones = pred(key_ref[rows(ci), :], kpos_of(ci))
            return cnt + jnp.sum(ones, axis=0, keepdims=True)
        return lax.fori_loop(0, nch, body, jnp.zeros((1, tq), I32))

    fold = 64
    assert (t // tkc) * (tkc // fold) <= 256

    def count_digit_ge(cand):
        cand_bf = cand.astype(F32).astype(BF16)

        def body(ci, acc):
            hit = jnp.where(dig_ref[rows(ci), :] >= cand_bf, jnp.ones((), BF16), jnp.zeros((), BF16))
            for s in range(tkc // fold):
                acc = acc + hit[s * fold:(s + 1) * fold]
            return acc

        acc = lax.fori_loop(0, nch, body, jnp.zeros((fold, tq), BF16))
        return jnp.sum(acc.astype(F32), axis=0, keepdims=True).astype(I32)

    prefix = jnp.zeros((1, tq), I32)
    above = jnp.zeros((1, tq), I32)
    n_ge = above
    for level in (3, 2, 1, 0):
        shift = 8 * level

        def build(ci, carry, shift=shift, level=level, prefix=prefix):
            u = key_ref[rows(ci), :] ^ INT_MIN
            digit = lax.shift_right_logical(u, jnp.int32(shift)) & 255
            if level < 3:
                same = lax.shift_right_logical(u, jnp.int32(shift + 8)) == lax.shift_right_logical(
                    prefix, jnp.int32(shift + 8))
                digit = jnp.where(same, digit, -1)
            dig_ref[rows(ci), :] = digit.astype(F32).astype(BF16)
            return carry

        lax.fori_loop(0, nch, build, 0)

        def digit_bit(step, d, above=above):
            cand = d | lax.shift_left(jnp.int32(1), 7 - step)
            return jnp.where(above + count_digit_ge(cand) >= topk, cand, d)

        d = lax.fori_loop(0, 8, digit_bit, jnp.zeros((1, tq), I32))
        if level == 0:
            n_ge = above + count_digit_ge(d)
        above = above + count_digit_ge(d + 1)
        prefix = prefix | lax.shift_left(d, jnp.int32(shift))

    thr = prefix ^ INT_MIN
    n_gt = above
    need = topk - n_gt
    cut_ref[...] = jnp.full((1, tq), t, I32)

    @pl.when(jnp.max(n_ge) > topk)
    def _():
        nbits = max(1, (t - 1).bit_length())

        def index_bit(step, x):
            cand = x | lax.shift_left(jnp.int32(1), nbits - 1 - step)
            g = count(lambda key, kpos: jnp.where(key == thr, jnp.where(kpos < cand, 1, 0), 0))
            return jnp.where(g < need, cand, x)

        cut_ref[...] = lax.fori_loop(0, nbits, index_bit, jnp.zeros((1, tq), I32))

    cutoff = cut_ref[...]

    def write_chunk(ci, carry):
        key = key_ref[rows(ci), :]
        kpos = kpos_of(ci)
        tie = jnp.where(kpos <= cutoff, 0.0, NEG)
        sel = jnp.where(key > thr, 0.0, jnp.where(key == thr, tie, NEG))
        bias_ref[rows(ci), :] = jnp.where(kpos <= qpos, sel, NEG).astype(BF16)
        return carry

    lax.fori_loop(0, nch, write_chunk, 0)

    def fill_chunk(ci, carry):
        bias_ref[rows(ci), :] = jnp.full((tkc, tq), NEG, BF16)
        return carry

    lax.fori_loop(nch, t // tkc, fill_chunk, 0)


def dsa_select(ki, qi, wi, topk):
    b, t, nh, d = qi.shape
    tq = _pick(t, (256, 128))
    tkc = _pick(t, (512, 256, 128))
    nq = t // tq
    qit = jnp.transpose(qi.astype(BF16).reshape(b, nq, tq, nh, d), (0, 1, 3, 4, 2))
    wit = jnp.transpose(wi.astype(F32).reshape(b, nq, tq, nh), (0, 1, 3, 2))[:, :, :, None, :]
    return pl.pallas_call(
        functools.partial(_dsa_index_body, tq=tq, tkc=tkc, topk=topk, nheads=nh, t=t),
        grid=(b, nq),
        in_specs=[pl.BlockSpec((None, t, d), lambda bi, qi_: (bi, 0, 0)),
                  pl.BlockSpec((None, None, nh, d, tq), lambda bi, qi_: (bi, qi_, 0, 0, 0)),
                  pl.BlockSpec((None, None, nh, 1, tq), lambda bi, qi_: (bi, qi_, 0, 0, 0))],
        out_specs=pl.BlockSpec((None, t, tq), lambda bi, qi_: (bi, 0, qi_)),
        out_shape=jax.ShapeDtypeStruct((b, t, t), BF16),
        scratch_shapes=[pltpu.VMEM((t, tq), I32), pltpu.VMEM((t, tq), BF16), pltpu.VMEM((1, tq), I32)],
        compiler_params=_params(("parallel", "arbitrary")),
        name="dsa_index_topk",
    )(ki.astype(BF16), qit, wit)


def _dsa_attn_body(k_ref, vt_ref, q_ref, bias_ref, o_ref, m_ref, acc_ref, pa_ref, aa_ref, pb_ref, ab_ref,
                   *, tq, tk, rep, d):
    i = pl.program_id(2)
    nch = ((i + 1) * tq + tk - 1) // tk
    m_ref[...] = jnp.full(m_ref.shape, NEG, F32)
    acc_ref[...] = jnp.zeros(acc_ref.shape, F32)
    buf_a = (pa_ref, aa_ref)
    buf_b = (pb_ref, ab_ref)

    def scores(c, buf):
        rows = pl.ds(pl.multiple_of(c * tk, tk), tk)
        kj = k_ref[rows, :]
        bj = bias_ref[rows, :]
        fns = [lambda r=r: lax.dot_general(kj, q_ref[:, r * d:(r + 1) * d], (_NT, ((), ())),
                                           preferred_element_type=F32).astype(BF16) + bj
               for r in range(rep)]
        _score_stage(fns, m_ref, *buf)

    def values(c, buf):
        _value_stage(vt_ref[c], acc_ref, *buf)

    scores(0, buf_a)

    def pair(t, carry):
        c = 2 * t + 1
        scores(c, buf_b)
        values(c - 1, buf_a)
        scores(c + 1, buf_a)
        values(c, buf_b)
        return carry

    lax.fori_loop(0, (nch - 1) // 2, pair, 0)
    last = nch - 1

    @pl.when(last % 2 == 1)
    def _():
        scores(last, buf_b)
        values(last - 1, buf_a)
        values(last, buf_b)

    @pl.when(last % 2 == 0)
    def _():
        values(last, buf_a)

    for r in range(rep):
        acc = acc_ref[r]
        o_ref[:, r * d:(r + 1) * d] = (acc[:d] / acc[d:d + 1]).T.astype(o_ref.dtype)


def dsa_attention(q, k, v, bias):
    b, t, qw = q.shape
    d = C_HEAD_DIM
    g = k.shape[2] // d
    rep = qw // (g * d)
    tq = _pick(t, (256, 128))
    tk = _pick(t, (512, 256, 128))
    nq, nk = t // tq, t // tk
    vt = jnp.transpose(v.astype(BF16).reshape(b, nk, tk, g, d), (0, 3, 1, 4, 2))
    vt = _with_ones_rows(vt)
    dve = d + ONES_ROWS
    group = pl.BlockSpec((None, tq, rep * d), lambda bi, gi, qi: (bi, qi, gi))
    return pl.pallas_call(
        functools.partial(_dsa_attn_body, tq=tq, tk=tk, rep=rep, d=d),
        grid=(b, g, nq),
        in_specs=[pl.BlockSpec((None, t, d), lambda bi, gi, qi: (bi, 0, gi)),
                  pl.BlockSpec((None, None, nk, dve, tk), lambda bi, gi, qi: (bi, gi, 0, 0, 0)),
                  group,
                  pl.BlockSpec((None, t, tq), lambda bi, gi, qi: (bi, 0, qi))],
        out_specs=group,
        out_shape=jax.ShapeDtypeStruct((b, t, qw), BF16),
        scratch_shapes=[pltpu.VMEM((rep, 1, tq), F32), pltpu.VMEM((rep, dve, tq), F32)]
        + [pltpu.VMEM((rep, tk, tq), BF16), pltpu.VMEM((rep, 1, tq), F32)] * 2,
        compiler_params=_params(("parallel", "parallel", "arbitrary")),
        name="dsa_attention",
    )(k.astype(BF16), vt, q.astype(BF16), bias)


LANES = 128


def _rope_tables(pos, head_dim, scale=1.0):
    rot = head_dim // ROPE_FRACTION
    half = rot // 2
    inv_freq = ROPE_THETA ** (-(jnp.arange(half, dtype=F32) * 2.0 / rot))
    ang = pos.astype(F32).reshape(-1, 1) * inv_freq
    cos, sin = jnp.cos(ang), jnp.sin(ang)
    zero = jnp.zeros_like(sin)
    rest = jnp.zeros((ang.shape[0], head_dim - rot), F32)
    tabs = (jnp.concatenate([cos, cos, rest + 1.0], axis=-1),
            jnp.concatenate([-sin, zero, rest], axis=-1),
            jnp.concatenate([zero, sin, rest], axis=-1))
    return tuple(jnp.tile(x * scale, (1, LANES // head_dim)) for x in tabs)


def _rope_body(z_ref, c_ref, up_ref, down_ref, o_ref, *, half, ncols):
    c, up, down = c_ref[...], up_ref[...], down_ref[...]
    for cb in range(ncols // LANES):
        cols = slice(cb * LANES, (cb + 1) * LANES)
        x = z_ref[:, cols]
        y = x * c + pltpu.roll(x, LANES - half, 1) * up + pltpu.roll(x, half, 1) * down
        o_ref[:, cols] = y.astype(o_ref.dtype)


def rope_cast(z, col0, ncols, head_dim, tables):
    m = z.shape[0]
    assert col0 % ncols == 0 and ncols % LANES == 0 and LANES % head_dim == 0
    tm = _pick(m, (512, 256, 128, 64, 32, 16, 8))
    tab = pl.BlockSpec((tm, LANES), lambda i: (i, 0))
    return pl.pallas_call(
        functools.partial(_rope_body, half=head_dim // ROPE_FRACTION // 2, ncols=ncols),
        grid=(m // tm,),
        in_specs=[pl.BlockSpec((tm, ncols), lambda i: (i, col0 // ncols)), tab, tab, tab],
        out_specs=pl.BlockSpec((tm, ncols), lambda i: (i, 0)),
        out_shape=jax.ShapeDtypeStruct((m, ncols), BF16),
        compiler_params=_params(("parallel",)),
        name="rope_cast",
    )(z, *tables)


def _rms_norm(x, g, eps=NORM_EPS):
    xf = x.astype(F32)
    return xf * lax.rsqrt(jnp.mean(xf * xf, axis=-1, keepdims=True) + eps) * g.astype(F32)


def _layer_norm(x, g, b, eps=LN_EPS):
    mu = jnp.mean(x, axis=-1, keepdims=True)
    var = jnp.mean(jnp.square(x - mu), axis=-1, keepdims=True)
    return (x - mu) * lax.rsqrt(var + eps) * g + b


def _partial_rope(x, pos):
    dh = x.shape[-1]
    rot = dh // ROPE_FRACTION
    half = rot // 2
    inv_freq = ROPE_THETA ** (-(jnp.arange(half, dtype=F32) * 2.0 / rot))
    ang = pos.astype(F32)[..., None] * inv_freq
    cos = jnp.cos(ang)[:, :, None, :]
    sin = jnp.sin(ang)[:, :, None, :]
    x1 = x[..., :half]
    x2 = x[..., half:rot]
    return jnp.concatenate([x1 * cos - x2 * sin, x2 * cos + x1 * sin, x[..., rot:]], axis=-1)


def _pad_cols(w, n):
    return jnp.pad(w, ((0, 0), (0, n - w.shape[1])))


def _round_up(n, m):
    return (n + m - 1) // m * m


def _mixer_ab(u, pos, w_in, lq1, lk1, lq2, lk2, subln_g, mu, w0, w2, a0, a2, g2,
              k_k, k_a, r_k, lnx_g, lnx_b, lambda_init):
    b, t, d = u.shape
    a_width = d // 2
    b_width = d // 2
    a_in = 3 * a_width
    n_in = w_in.shape[1]
    z = matmul(u.reshape(b * t, d), _pad_cols(w_in, _round_up(n_in, 512)))

    seq = lambda x: x.reshape(b, t, a_width)
    qa = rope_cast(z, 0, a_width, A_DC, _rope_tables(pos, A_DC, A_DC ** -0.5 * LOG2E))
    ka = rope_cast(z, a_width, a_width, A_DC, _rope_tables(pos, A_DC))
    va = z[:, 2 * a_width:a_in].astype(BF16)
    lam = jnp.exp(jnp.sum(lq1 * lk1)) - jnp.exp(jnp.sum(lq2 * lk2)) + lambda_init
    ya = diff_attention(seq(qa), seq(ka), seq(va), lam, subln_g * (1.0 - lambda_init))

    prep = rwkv_prep(z, a_in, b_width, n_in - a_in - 3 * b_width, t, mu, w0, w2, a0, a2, g2, k_k, k_a)
    yb = wkv7(*(x.reshape(b, t, b_width) for x in prep), r_k.reshape(-1), lnx_g, lnx_b)
    return ya, yb


def _mixer_c(u, pos, w_in, idx_k_g, idx_k_b, topk):
    b, t, d = u.shape
    c_heads = d // C_HEAD_DIM
    n_in = w_in.shape[1]
    c1 = c_heads * C_HEAD_DIM
    c2 = c1 + C_KV_HEADS * C_HEAD_DIM
    c3 = c2 + C_KV_HEADS * C_HEAD_DIM
    c4 = c3 + IDX_HEADS * IDX_DIM
    c5 = c4 + IDX_DIM
    z = matmul(u.reshape(b * t, d), _pad_cols(w_in, _round_up(n_in, 512)))
    seq = lambda x: x.reshape(b, t, x.shape[-1])
    tabs = _rope_tables(pos, C_HEAD_DIM)
    q = rope_cast(z, 0, c1, C_HEAD_DIM, _rope_tables(pos, C_HEAD_DIM, C_HEAD_DIM ** -0.5 * LOG2E))
    k = rope_cast(z, c1, c2 - c1, C_HEAD_DIM, tabs)
    v = z[:, c2:c3].astype(BF16)
    qi = rope_cast(z, c3, c4 - c3, IDX_DIM, _rope_tables(pos, IDX_DIM))
    ki = _partial_rope(_layer_norm(seq(z[:, c4:c5]), idx_k_g, idx_k_b)[:, :, None, :], pos)[:, :, 0]
    wi = seq(z[:, c5:n_in]) * ((IDX_HEADS * IDX_DIM) ** -0.5)
    bias = dsa_select(ki, qi.reshape(b, t, IDX_HEADS, IDX_DIM), wi, topk)
    return (dsa_attention(seq(q), seq(k), seq(v), bias),)


def kernel(x, p, positions, mix_pre_g, mix_post_g, mlp_pre_g, mlp_post_g, w_mlp_up, w_mlp_down, w_ple_proj, w_ple_gate, ple_post_g, ab_w_in, ab_w_out, diff_lq1, diff_lk1, diff_lq2, diff_lk2, diff_subln_g, rwkv_mu, rwkv_w0, rwkv_w2, rwkv_a0, rwkv_a2, rwkv_g2, rwkv_k_k, rwkv_k_a, rwkv_r_k, rwkv_lnx_g, rwkv_lnx_b, c_w_in, c_w_out, idx_k_g, idx_k_b):
    depth = mix_pre_g.shape[0]
    b, t, d = x.shape
    topk = min(TOPK_MAX, t // 4)
    flat = lambda z: z.reshape(b * t, z.shape[-1])
    h = flat(x)
    u = _rms_norm(h, mix_pre_g[0]).astype(BF16)
    for i in range(depth):
        j = i // 2
        u3 = u.reshape(b, t, d)
        if i % 2 == 0:
            lambda_init = 0.8 - 0.6 * math.exp(-0.3 * i)
            m = _mixer_ab(u3, positions, ab_w_in[j], diff_lq1[j], diff_lk1[j], diff_lq2[j],
                          diff_lk2[j], diff_subln_g[j], rwkv_mu[j], rwkv_w0[j], rwkv_w2[j], rwkv_a0[j],
                          rwkv_a2[j], rwkv_g2[j], rwkv_k_k[j], rwkv_k_a[j], rwkv_r_k[j], rwkv_lnx_g[j],
                          rwkv_lnx_b[j], lambda_init)
            w_out = ab_w_out[j]
        else:
            m = _mixer_c(u3, positions, c_w_in[j], idx_k_g[j], idx_k_b[j], topk)
            w_out = c_w_out[j]
        h, u = proj_residual([flat(part) for part in m], w_out, h, mix_post_g[i], mlp_pre_g[i])
        up = matmul(u, w_mlp_up[i], out_dtype=BF16, act="relu2")
        delta = proj_normed(up, w_mlp_down[i], mlp_post_g[i])
        g_next = mix_pre_g[i + 1] if i + 1 < depth else None
        h, u = ple_residual(flat(p[i]), w_ple_proj[i], w_ple_gate[i], h, delta, ple_post_g[i], g_next)
    return h.reshape(b, t, d)
```

```python
import functools
import math

import jax
import jax.numpy as jnp
from jax import lax
from jax.experimental import pallas as pl
from jax.experimental.pallas import tpu as pltpu

F32 = jnp.float32
BF16 = jnp.bfloat16
I32 = jnp.int32

V7X_VMEM_LIMIT_BYTES = 56 * 1024 * 1024
NEG = -1e30
LOG2E = 1.4426950408889634
SUBLANES = 8
INT_MIN = -2147483648

NORM_EPS = 1e-6
ROPE_THETA = 500000.0
ROPE_FRACTION = 4
A_DV = 128
A_DC = 64
DIFF_SUBLN_EPS = 1e-5
B_HEAD = 64
LNX_EPS = 64e-5
C_HEAD_DIM = 128
C_KV_HEADS = 4
IDX_HEADS = 16
IDX_DIM = 64
TOPK_MAX = 256
LN_EPS = 1e-6
WKV_CHUNK = 64


def _params(sem):
    return pltpu.CompilerParams(dimension_semantics=sem, vmem_limit_bytes=V7X_VMEM_LIMIT_BYTES)


def _pick(n, prefs):
    for t in prefs:
        if n % t == 0:
            return t
    return n


def _mm_body(a_ref, w_ref, o_ref, *scratch, nk, act):
    def epilogue(acc):
        if act == "relu2":
            r = jnp.maximum(acc, 0.0)
            acc = r * r
        return acc.astype(o_ref.dtype)

    if nk == 1:
        o_ref[...] = epilogue(jnp.dot(a_ref[...], w_ref[...], preferred_element_type=F32))
        return
    (acc_ref,) = scratch
    k = pl.program_id(2)

    @pl.when(k == 0)
    def _():
        acc_ref[...] = jnp.zeros_like(acc_ref)

    acc_ref[...] += jnp.dot(a_ref[...], w_ref[...], preferred_element_type=F32)

    @pl.when(k == nk - 1)
    def _():
        o_ref[...] = epilogue(acc_ref[...])


def matmul(a, w, out_dtype=F32, act=None):
    a = a.astype(BF16)
    w = w.astype(BF16)
    m, kdim = a.shape
    n = w.shape[1]
    tn = _pick(n, (2048, 1024, 512, 256, 128))
    tile_elems = (4 << 20) // jnp.dtype(out_dtype).itemsize
    tm = _pick(m, tuple(r for r in (2048, 1024, 512, 256, 128, 64, 32, 16, 8) if r * tn <= tile_elems))
    tk = kdim if kdim <= 2048 else _pick(kdim, (2048, 1024, 512))
    nk = kdim // tk
    scratch = [pltpu.VMEM((tm, tn), F32)] if nk > 1 else []
    return pl.pallas_call(
        functools.partial(_mm_body, nk=nk, act=act),
        grid=(m // tm, n // tn, nk),
        in_specs=[pl.BlockSpec((tm, tk), lambda i, j, k: (i, k)),
                  pl.BlockSpec((tk, tn), lambda i, j, k: (k, j))],
        out_specs=pl.BlockSpec((tm, tn), lambda i, j, k: (i, j)),
        out_shape=jax.ShapeDtypeStruct((m, n), out_dtype),
        scratch_shapes=scratch,
        compiler_params=_params(("parallel", "parallel", "arbitrary")),
        name="dense_matmul",
    )(a, w)


def _rms_rows(x, g):
    return x * lax.rsqrt(jnp.mean(x * x, axis=-1, keepdims=True) + NORM_EPS) * g


def _residual_update(m, h_ref, gpost_ref, gnext_ref, ho_ref, uo_ref):
    h_new = h_ref[...] + _rms_rows(m, gpost_ref[...])
    ho_ref[...] = h_new
    if uo_ref is not None:
        uo_ref[...] = _rms_rows(h_new, gnext_ref[...]).astype(BF16)


def _proj_res_body(*refs, nk, has_next, widths):
    a_refs, refs = refs[:len(widths)], refs[len(widths):]
    if has_next:
        w_ref, h_ref, gpost_ref, gnext_ref, ho_ref, uo_ref = refs
    else:
        w_ref, h_ref, gpost_ref, ho_ref = refs
        gnext_ref = uo_ref = None
    part, lo = None, 0
    for a_ref, width in zip(a_refs, widths):
        rows = slice(None) if len(widths) == 1 else slice(lo, lo + width)
        term = jnp.dot(a_ref[...].astype(BF16), w_ref[rows, :], preferred_element_type=F32)
        part = term if part is None else part + term
        lo += width
    if nk == 1:
        _residual_update(part, h_ref, gpost_ref, gnext_ref, ho_ref, uo_ref)
        return
    k = pl.program_id(1)

    @pl.when(k == 0)
    def _():
        ho_ref[...] = part

    @pl.when(jnp.logical_and(k > 0, k < nk - 1))
    def _():
        ho_ref[...] += part

    @pl.when(k == nk - 1)
    def _():
        _residual_update(ho_ref[...] + part, h_ref, gpost_ref, gnext_ref, ho_ref, uo_ref)


def proj_residual(a_parts, w, h, g_post, g_next):
    m = h.shape[0]
    widths = tuple(a.shape[1] for a in a_parts)
    kdim = sum(widths)
    n = w.shape[1]
    tm = _pick(m, (512, 256, 128, 64, 32, 16, 8))
    tk = _pick(kdim, (2048, 1024, 512, 256, 128))
    nk = kdim // tk
    assert nk == 1 or len(a_parts) == 1
    has_next = g_next is not None
    row = pl.BlockSpec((tm, n), lambda i, k: (i, 0))
    vec = pl.BlockSpec((1, n), lambda i, k: (0, 0))
    if len(a_parts) == 1:
        a_specs = [pl.BlockSpec((tm, tk), lambda i, k: (i, k))]
    else:
        a_specs = [pl.BlockSpec((tm, wd), lambda i, k: (i, 0)) for wd in widths]
    in_specs = a_specs + [pl.BlockSpec((tk, n), lambda i, k: (k, 0)), row, vec]
    args = list(a_parts) + [w.astype(BF16), h, g_post.reshape(1, n).astype(F32)]
    out_shape = [jax.ShapeDtypeStruct((m, n), F32)]
    out_specs = [row]
    if has_next:
        in_specs.append(vec)
        args.append(g_next.reshape(1, n).astype(F32))
        out_shape.append(jax.ShapeDtypeStruct((m, n), BF16))
        out_specs.append(row)
    out = pl.pallas_call(
        functools.partial(_proj_res_body, nk=nk, has_next=has_next, widths=widths),
        grid=(m // tm, nk),
        in_specs=in_specs, out_specs=out_specs, out_shape=out_shape,
        compiler_params=_params(("parallel", "arbitrary")),
        name="proj_residual",
    )(*args)
    return (out[0], out[1]) if has_next else (out[0], None)


def _proj_normed_body(a_ref, w_ref, g_ref, o_ref, *, nk):
    k = pl.program_id(1)
    part = jnp.dot(a_ref[...], w_ref[...], preferred_element_type=F32)

    @pl.when(k == 0)
    def _():
        o_ref[...] = part

    @pl.when(jnp.logical_and(k > 0, k < nk - 1))
    def _():
        o_ref[...] += part

    @pl.when(k == nk - 1)
    def _():
        o_ref[...] = _rms_rows(o_ref[...] + part, g_ref[...])


def proj_normed(a, w, g):
    m, kdim = a.shape
    n = w.shape[1]
    tm = _pick(m, (1024, 512, 256, 128, 64, 32, 16, 8))
    tk = _pick(kdim, (2048, 1024, 512, 256, 128))
    nk = kdim // tk
    assert nk >= 2
    return pl.pallas_call(
        functools.partial(_proj_normed_body, nk=nk),
        grid=(m // tm, nk),
        in_specs=[pl.BlockSpec((tm, tk), lambda i, k: (i, k)), pl.BlockSpec((tk, n), lambda i, k: (k, 0)),
                  pl.BlockSpec((1, n), lambda i, k: (0, 0))],
        out_specs=pl.BlockSpec((tm, n), lambda i, k: (i, 0)),
        out_shape=jax.ShapeDtypeStruct((m, n), F32),
        compiler_params=_params(("parallel", "arbitrary")),
        name="proj_normed",
    )(a.astype(BF16), w.astype(BF16), g.reshape(1, n).astype(F32))


def _ple_body(*refs, has_next):
    if has_next:
        p_ref, wp_ref, wg_ref, h_ref, d_ref, gpost_ref, gnext_ref, ho_ref, uo_ref = refs
    else:
        p_ref, wp_ref, wg_ref, h_ref, d_ref, gpost_ref, ho_ref = refs
        gnext_ref = uo_ref = None
    h_in = h_ref[...] + d_ref[...]
    e = jnp.dot(p_ref[...], wp_ref[...], preferred_element_type=F32)
    gate = jax.nn.sigmoid(jnp.dot(h_in.astype(BF16), wg_ref[...], preferred_element_type=F32))
    h_new = h_in + _rms_rows(e * gate, gpost_ref[...])
    ho_ref[...] = h_new
    if uo_ref is not None:
        uo_ref[...] = _rms_rows(h_new, gnext_ref[...]).astype(BF16)


def ple_residual(p, w_proj, w_gate, h, delta, g_post, g_next):
    m, n = h.shape
    pd = p.shape[1]
    tm = _pick(m, (256, 128, 64, 32, 16, 8))
    has_next = g_next is not None
    row = pl.BlockSpec((tm, n), lambda i: (i, 0))
    vec = pl.BlockSpec((1, n), lambda i: (0, 0))
    once = pl.Buffered(1)
    in_specs = [pl.BlockSpec((tm, pd), lambda i: (i, 0)),
                pl.BlockSpec((pd, n), lambda i: (0, 0), pipeline_mode=once),
                pl.BlockSpec((n, n), lambda i: (0, 0), pipeline_mode=once), row, row, vec]
    args = [p.astype(BF16), w_proj.astype(BF16), w_gate.astype(BF16), h, delta,
            g_post.reshape(1, n).astype(F32)]
    out_shape = [jax.ShapeDtypeStruct((m, n), F32)]
    out_specs = [row]
    if has_next:
        in_specs.append(vec)
        args.append(g_next.reshape(1, n).astype(F32))
        out_shape.append(jax.ShapeDtypeStruct((m, n), BF16))
        out_specs.append(row)
    out = pl.pallas_call(
        functools.partial(_ple_body, has_next=has_next),
        grid=(m // tm,),
        in_specs=in_specs, out_specs=out_specs, out_shape=out_shape,
        compiler_params=_params(("parallel",)),
        name="ple_residual",
    )(*args)
    return (out[0], out[1]) if has_next else (out[0], None)


ONES_ROWS = 16


def _with_ones_rows(vt):
    pad = jnp.zeros(vt.shape[:-2] + (ONES_ROWS, vt.shape[-1]), vt.dtype).at[..., 0, :].set(1)
    return jnp.concatenate([vt, pad], axis=-2)


def _score_stage(score_fns, m_ref, p_ref, alpha_ref):
    n = range(len(score_fns))
    ss = [fn() for fn in score_fns]
    m_old = [m_ref[i] for i in n]
    m_new = [jnp.maximum(m_old[i], jnp.max(ss[i], axis=0, keepdims=True).astype(F32)) for i in n]
    for i in n:
        alpha_ref[i] = jnp.exp2(m_old[i] - m_new[i])
        m_ref[i] = m_new[i]
    for i in n:
        p_ref[i] = jnp.exp2(ss[i] - m_new[i].astype(BF16))


def _value_stage(vt, acc_ref, p_ref, alpha_ref):
    n = range(p_ref.shape[0])
    pv = [jnp.dot(vt, p_ref[i], preferred_element_type=F32) for i in n]
    for i in n:
        acc_ref[i] = alpha_ref[i] * acc_ref[i] + pv[i]


def _diff_attn_body(lam_ref, k_ref, q_ref, cmask_ref, vt_ref, g_ref, o_ref, m_ref, acc_ref, qz_ref,
                    pa_ref, aa_ref, pb_ref, ab_ref, *, tq, nsp):
    i = pl.program_id(2)
    tl = tq // nsp
    m_ref[...] = jnp.full(m_ref.shape, NEG, F32)
    acc_ref[...] = jnp.zeros(acc_ref.shape, F32)
    for c in range(2):
        qz_ref[c] = q_ref[...] * cmask_ref[c]

    buf_a = (pa_ref, aa_ref)
    buf_b = (pb_ref, ab_ref)

    def scores(j, buf, masked=False):
        kj = k_ref[pl.ds(pl.multiple_of(j * tq, tq), tq), :]

        def score(c, sp):
            s = lax.dot_general(kj, qz_ref[c, sp * tl:(sp + 1) * tl, :], (_NT, ((), ())),
                                preferred_element_type=F32)
            if masked:
                row = lax.broadcasted_iota(I32, (tq, tl), 0)
                col = lax.broadcasted_iota(I32, (tq, tl), 1)
                s = jnp.where(row <= col + sp * tl, s, NEG)
            return s.astype(BF16)

        _score_stage([functools.partial(score, c, sp) for c in range(2) for sp in range(nsp)], m_ref, *buf)

    def values(j, buf):
        _value_stage(vt_ref[j], acc_ref, *buf)

    @pl.when(i == 0)
    def _():
        scores(0, buf_a, masked=True)
        values(0, buf_a)

    @pl.when(i > 0)
    def _():
        scores(0, buf_a)

        def pair(t, carry):
            c = 2 * t + 1
            scores(c, buf_b)
            values(c - 1, buf_a)
            scores(c + 1, buf_a)
            values(c, buf_b)
            return carry

        lax.fori_loop(0, (i - 1) // 2, pair, 0)

        @pl.when((i - 1) % 2 == 1)
        def _():
            scores(i - 1, buf_b)
            values(i - 2, buf_a)
            scores(i, buf_a, masked=True)
            values(i - 1, buf_b)
            values(i, buf_a)

        @pl.when((i - 1) % 2 == 0)
        def _():
            scores(i, buf_b, masked=True)
            values(i - 1, buf_a)
            values(i, buf_b)

    lam = lam_ref[0]
    for sp in range(nsp):
        a0 = acc_ref[sp]
        a1 = acc_ref[nsp + sp]
        out = (a0[:A_DV] / a0[A_DV:A_DV + 1] - lam * (a1[:A_DV] / a1[A_DV:A_DV + 1])).T
        y = out * lax.rsqrt(jnp.mean(out * out, axis=-1, keepdims=True) + DIFF_SUBLN_EPS) * g_ref[...]
        o_ref[sp * tl:(sp + 1) * tl, :] = y.astype(o_ref.dtype)


def diff_attention(q, k, v, lam, gain):
    b, t, width = q.shape
    h = width // A_DV
    tq = _pick(t, (512, 256, 128))
    nq = t // tq
    vt = jnp.transpose(v.astype(BF16).reshape(b, nq, tq, h, A_DV), (0, 3, 1, 4, 2))
    vt = _with_ones_rows(vt)
    dve = A_DV + ONES_ROWS
    nsp = 2 if tq >= 256 else 1
    lane = jnp.arange(A_DV)
    cmask = jnp.stack([lane < A_DC, lane >= A_DC]).astype(BF16).reshape(2, 1, A_DV)
    head = pl.BlockSpec((None, tq, A_DV), lambda bi, hi, qi: (bi, qi, hi))
    return pl.pallas_call(
        functools.partial(_diff_attn_body, tq=tq, nsp=nsp),
        grid=(b, h, nq),
        in_specs=[pl.BlockSpec(memory_space=pltpu.SMEM),
                  pl.BlockSpec((None, t, A_DV), lambda bi, hi, qi: (bi, 0, hi)),
                  head,
                  pl.BlockSpec((2, 1, A_DV), lambda bi, hi, qi: (0, 0, 0)),
                  pl.BlockSpec((None, None, nq, dve, tq), lambda bi, hi, qi: (bi, hi, 0, 0, 0)),
                  pl.BlockSpec((1, A_DV), lambda bi, hi, qi: (0, 0))],
        out_specs=head,
        out_shape=jax.ShapeDtypeStruct((b, t, width), BF16),
        scratch_shapes=[pltpu.VMEM((2 * nsp, 1, tq // nsp), F32),
                        pltpu.VMEM((2 * nsp, dve, tq // nsp), F32),
                        pltpu.VMEM((2, tq, A_DV), BF16)]
        + [pltpu.VMEM((2 * nsp, tq, tq // nsp), BF16), pltpu.VMEM((2 * nsp, 1, tq // nsp), F32)] * 2,
        compiler_params=_params(("parallel", "parallel", "arbitrary")),
        name="diff_attention",
    )(lam.reshape(1).astype(F32), k.astype(BF16), q.astype(BF16), cmask, vt, gain.reshape(1, A_DV).astype(F32))


def _split_dot(x, w):
    hi = x.astype(BF16)
    lo = (x - hi.astype(F32)).astype(BF16)
    return jnp.dot(hi, w, preferred_element_type=F32) + jnp.dot(lo, w, preferred_element_type=F32)


def _rwkv_prep_body(zr_ref, zk_ref, zv_ref, zl_ref, pr_ref, pk_ref, pv_ref, pl_ref,
                    mur_ref, muk_ref, muv_ref, mul_ref, w0_ref, a0_ref, kk_ref, ka_ref,
                    w2_ref, a2_ref, g2_ref, hsum_ref,
                    r_ref, lw_ref, k_ref, v_ref, an_ref, bn_ref, g_ref, *, tm, seq):
    first = (pl.program_id(0) * tm) % seq == 0

    def shifted(z_ref, p_ref, mu_ref):
        z = z_ref[...]
        last = jnp.where(first, 0.0, p_ref[SUBLANES - 1:SUBLANES, :])
        row = lax.broadcasted_iota(I32, z.shape, 0)
        prev = jnp.where(row == 0, last, pltpu.roll(z, 1, 0))
        return z + (prev - z) * mu_ref[...]

    r = shifted(zr_ref, pr_ref, mur_ref)
    k = shifted(zk_ref, pk_ref, muk_ref)
    v = shifted(zv_ref, pv_ref, muv_ref)
    lo = shifted(zl_ref, pl_ref, mul_ref)
    lw = jnp.dot(jnp.tanh(lo).astype(BF16), w2_ref[...], preferred_element_type=F32)
    la = jnp.dot(lo.astype(BF16), a2_ref[...], preferred_element_type=F32)
    g = jnp.dot(jax.nn.sigmoid(lo).astype(BF16), g2_ref[...], preferred_element_type=F32)
    w = -jax.nn.softplus(-(w0_ref[...] + lw)) - 0.5
    a = jax.nn.sigmoid(a0_ref[...] + la)
    kk = k * kk_ref[...]
    norm = jnp.sqrt(_split_dot(kk * kk, hsum_ref[...]))
    kk = kk / jnp.maximum(norm, 1e-12)
    r_ref[...] = r
    lw_ref[...] = -jnp.exp(w)
    k_ref[...] = k * (1.0 + (a - 1.0) * ka_ref[...])
    v_ref[...] = v
    an_ref[...] = -kk
    bn_ref[...] = kk * a
    g_ref[...] = g


def rwkv_prep(z, col0, width, n_lora, seq, mu, w0, w2, a0, a2, g2, k_k, k_a):
    m = z.shape[0]
    tm = _pick(seq, (256, 128, 64, 32, 16, 8))
    lw_ = _round_up(n_lora, 128)
    assert col0 % width == 0 and (col0 + 3 * width) % lw_ == 0 and col0 + 3 * width + lw_ <= z.shape[1]
    cb = col0 // width
    lb = (col0 + 3 * width) // lw_
    dl, al = w2.shape[0], a2.shape[0]

    def rows_of(w, start):
        return jnp.zeros((lw_, width), BF16).at[start:start + w.shape[0]].set(w.astype(BF16))

    hsum = jnp.kron(jnp.eye(width // B_HEAD, dtype=BF16), jnp.ones((B_HEAD, B_HEAD), BF16))
    vec = lambda x: x.reshape(1, -1).astype(F32)
    mul = jnp.pad(mu[3 * width:], (0, lw_ - n_lora))
    tile = lambda c, wd: pl.BlockSpec((tm, wd), lambda i: (i, c))
    before = lambda c, wd: pl.BlockSpec((SUBLANES, wd), lambda i: (jnp.maximum(i * (tm // SUBLANES) - 1, 0), c))
    par = lambda wd: pl.BlockSpec((1, wd), lambda i: (0, 0))
    mat = lambda r_, c_: pl.BlockSpec((r_, c_), lambda i: (0, 0))
    out = pl.BlockSpec((tm, width), lambda i: (i, 0))
    return pl.pallas_call(
        functools.partial(_rwkv_prep_body, tm=tm, seq=seq),
        grid=(m // tm,),
        in_specs=[tile(cb, width), tile(cb + 1, width), tile(cb + 2, width), tile(lb, lw_),
                  before(cb, width), before(cb + 1, width), before(cb + 2, width), before(lb, lw_),
                  par(width), par(width), par(width), par(lw_), par(width), par(width), par(width), par(width),
                  mat(lw_, width), mat(lw_, width), mat(lw_, width), mat(width, width)],
        out_specs=[out] * 7,
        out_shape=[jax.ShapeDtypeStruct((m, width), F32)] * 7,
        compiler_params=_params(("parallel",)),
        name="rwkv_prep",
    )(z, z, z, z, z, z, z, z,
      vec(mu[:width]), vec(mu[width:2 * width]), vec(mu[2 * width:3 * width]), vec(mul),
      vec(w0), vec(a0), vec(k_k), vec(k_a),
      rows_of(w2, 0), rows_of(a2, dl), rows_of(g2, dl + al), hsum)


def _dot(a, b, dims):
    return lax.dot_general(a.astype(BF16), b.astype(BF16), (dims, ((), ())), preferred_element_type=F32)


def _prefix_sums(tri, x):
    hi = x.astype(BF16)
    r1 = x - hi.astype(F32)
    mid = r1.astype(BF16)
    lo = (r1 - mid.astype(F32)).astype(BF16)
    return (jnp.dot(tri, hi, preferred_element_type=F32) + jnp.dot(tri, mid, preferred_element_type=F32)
            + jnp.dot(tri, lo, preferred_element_type=F32))


_NN = ((1,), (0,))
_NT = ((1,), (1,))
_TN = ((0,), (0,))


def _wkv_body(r_ref, lw_ref, k_ref, v_ref, a_ref, b_ref, g_ref, rk_ref, lng_ref, lnb_ref, y_ref, s_ref,
              *, nh, n, c):
    @pl.when(pl.program_id(1) == 0)
    def _():
        s_ref[...] = jnp.zeros(s_ref.shape, F32)

    row = lax.broadcasted_iota(I32, (c, c), 0)
    col = lax.broadcasted_iota(I32, (c, c), 1)
    incl = row >= col
    strict = row > col
    tri_bf = jnp.where(incl, 1.0, 0.0).astype(BF16)
    eye = jnp.where(row == col, 1.0, 0.0).astype(F32)
    nsq = int(math.log2(c)) - 1

    hs = range(nh)
    dot = _dot
    heads = lambda ref: [ref[:, h * n:(h + 1) * n] for h in hs]
    lw, r, k, v, a, b = (heads(ref) for ref in (lw_ref, r_ref, k_ref, v_ref, a_ref, b_ref))
    cum = [_prefix_sums(tri_bf, lw[h]) for h in hs]
    tot = [cum[h][c - 1:c, :] for h in hs]
    pinv = [jnp.exp(-cum[h]) for h in hs]
    pend = [jnp.exp(tot[h] - cum[h]) for h in hs]
    at = [a[h] * jnp.exp(cum[h] - lw[h]) for h in hs]
    rt = [r[h] * jnp.exp(cum[h]) for h in hs]
    bt = [b[h] * pinv[h] for h in hs]
    kt = [k[h] * pinv[h] for h in hs]
    a_ab = [jnp.where(strict, dot(at[h], bt[h], _NT), 0.0) for h in hs]
    a_ak = [jnp.where(strict, dot(at[h], kt[h], _NT), 0.0) for h in hs]
    a_rb = [jnp.where(incl, dot(rt[h], bt[h], _NT), 0.0) for h in hs]
    a_rk = [jnp.where(incl, dot(rt[h], kt[h], _NT), 0.0) for h in hs]
    x = a_ab
    minv = [eye + x[h] for h in hs]
    for _ in range(nsq):
        x = [dot(x[h], x[h], _NN) for h in hs]
        minv = [minv[h] + dot(minv[h], x[h], _NN) for h in hs]
    s0 = [s_ref[h] for h in hs]
    rhs = [dot(at[h], s0[h], _NT) + dot(a_ak[h], v[h], _NN) for h in hs]
    u = [dot(minv[h], rhs[h], _NN) for h in hs]
    y = [dot(rt[h], s0[h], _NT) + dot(a_rb[h], u[h], _NN) + dot(a_rk[h], v[h], _NN) for h in hs]
    for h in hs:
        s_ref[h] = (s0[h] * jnp.exp(tot[h]) + dot(u[h], b[h] * pend[h], _TN)
                    + dot(v[h], k[h] * pend[h], _TN))
    ones = jnp.ones((n, n), BF16)
    cols = [slice(h * n, (h + 1) * n) for h in hs]
    mean = [_split_dot(y[h], ones) * (1.0 / n) for h in hs]
    d = [y[h] - mean[h] for h in hs]
    var = [_split_dot(d[h] * d[h], ones) * (1.0 / n) for h in hs]
    rk = [_split_dot(r[h] * k[h] * rk_ref[:, cols[h]], ones) for h in hs]
    for h in hs:
        yn = d[h] * lax.rsqrt(var[h] + LNX_EPS)
        y_ref[:, cols[h]] = (yn * lng_ref[:, cols[h]] + lnb_ref[:, cols[h]] + rk[h] * v[h]) * g_ref[:, cols[h]]


def wkv7(r, lw, k, v, a, b, g, r_k, lnx_g, lnx_b, n=B_HEAD):
    bsz, t, width = r.shape
    nh = width // n
    c = min(WKV_CHUNK, t)
    spec = pl.BlockSpec((None, c, width), lambda bi, ci: (bi, ci, 0))
    par = pl.BlockSpec((1, width), lambda bi, ci: (0, 0))
    vec = lambda x: x.reshape(1, width).astype(F32)
    return pl.pallas_call(
        functools.partial(_wkv_body, nh=nh, n=n, c=c),
        grid=(bsz, t // c),
        in_specs=[spec] * 7 + [par] * 3,
        out_specs=spec,
        out_shape=jax.ShapeDtypeStruct((bsz, t, width), F32),
        scratch_shapes=[pltpu.VMEM((nh, n, n), F32)],
        compiler_params=_params(("parallel", "arbitrary")),
        name="wkv7_chunked",
    )(r, lw, k, v, a, b, g, vec(r_k), vec(lnx_g), vec(lnx_b))


def _dsa_index_body(ki_ref, qit_ref, wit_ref, bias_ref, key_ref, dig_ref, cut_ref, *, tq, tkc, topk, nheads, t):
    i = pl.program_id(1)
    nch = ((i + 1) * tq + tkc - 1) // tkc
    qpos = i * tq + lax.broadcasted_iota(I32, (1, tq), 1)

    def kpos_of(ci):
        return ci * tkc + lax.broadcasted_iota(I32, (tkc, 1), 0)

    def rows(ci):
        return pl.ds(pl.multiple_of(ci * tkc, tkc), tkc)

    def score_chunk(ci, carry):
        kc = ki_ref[rows(ci), :]
        acc = jnp.zeros((tkc, tq), F32)
        for h in range(nheads):
            s = jnp.dot(kc, qit_ref[h], preferred_element_type=F32)
            acc = acc + jnp.maximum(s, 0.0) * wit_ref[h]
        acc = jnp.where(kpos_of(ci) <= qpos, acc, -jnp.inf)
        bits = lax.bitcast_convert_type(acc, I32)
        key_ref[rows(ci), :] = jnp.where(bits >= 0, bits, bits ^ 0x7FFFFFFF)
        return carry

    lax.fori_loop(0, nch, score_chunk, 0)

    def count(pred):
        def body(ci, cnt):
            ones = pred(key_ref[rows(ci), :], kpos_of(ci))
            return cnt + jnp.sum(ones, axis=0, keepdims=True)
        return lax.fori_loop(0, nch, body, jnp.zeros((1, tq), I32))

    fold = 64
    assert (t // tkc) * (tkc // fold) <= 256

    def count_digit_ge(cand):
        cand_bf = cand.astype(F32).astype(BF16)

        def body(ci, acc):
            hit = jnp.where(dig_ref[rows(ci), :] >= cand_bf, jnp.ones((), BF16), jnp.zeros((), BF16))
            for s in range(tkc // fold):
                acc = acc + hit[s * fold:(s + 1) * fold]
            return acc

        acc = lax.fori_loop(0, nch, body, jnp.zeros((fold, tq), BF16))
        return jnp.sum(acc.astype(F32), axis=0, keepdims=True).astype(I32)

    prefix = jnp.zeros((1, tq), I32)
    above = jnp.zeros((1, tq), I32)
    n_ge = above
    for level in (3, 2, 1, 0):
        shift = 8 * level

        def build(ci, carry, shift=shift, level=level, prefix=prefix):
            u = key_ref[rows(ci), :] ^ INT_MIN
            digit = lax.shift_right_logical(u, jnp.int32(shift)) & 255
            if level < 3:
                same = lax.shift_right_logical(u, jnp.int32(shift + 8)) == lax.shift_right_logical(
                    prefix, jnp.int32(shift + 8))
                digit = jnp.where(same, digit, -1)
            dig_ref[rows(ci), :] = digit.astype(F32).astype(BF16)
            return carry

        lax.fori_loop(0, nch, build, 0)

        def digit_bit(step, d, above=above):
            cand = d | lax.shift_left(jnp.int32(1), 7 - step)
            return jnp.where(above + count_digit_ge(cand) >= topk, cand, d)

        d = lax.fori_loop(0, 8, digit_bit, jnp.zeros((1, tq), I32))
        if level == 0:
            n_ge = above + count_digit_ge(d)
        above = above + count_digit_ge(d + 1)
        prefix = prefix | lax.shift_left(d, jnp.int32(shift))

    thr = prefix ^ INT_MIN
    n_gt = above
    need = topk - n_gt
    cut_ref[...] = jnp.full((1, tq), t, I32)

    @pl.when(jnp.max(n_ge) > topk)
    def _():
        nbits = max(1, (t - 1).bit_length())

        def index_bit(step, x):
            cand = x | lax.shift_left(jnp.int32(1), nbits - 1 - step)
            g = count(lambda key, kpos: jnp.where(key == thr, jnp.where(kpos < cand, 1, 0), 0))
            return jnp.where(g < need, cand, x)

        cut_ref[...] = lax.fori_loop(0, nbits, index_bit, jnp.zeros((1, tq), I32))

    cutoff = cut_ref[...]

    def write_chunk(ci, carry):
        key = key_ref[rows(ci), :]
        kpos = kpos_of(ci)
        tie = jnp.where(kpos <= cutoff, 0.0, NEG)
        sel = jnp.where(key > thr, 0.0, jnp.where(key == thr, tie, NEG))
        bias_ref[rows(ci), :] = jnp.where(kpos <= qpos, sel, NEG).astype(BF16)
        return carry

    lax.fori_loop(0, nch, write_chunk, 0)

    def fill_chunk(ci, carry):
        bias_ref[rows(ci), :] = jnp.full((tkc, tq), NEG, BF16)
        return carry

    lax.fori_loop(nch, t // tkc, fill_chunk, 0)


def dsa_select(ki, qi, wi, topk):
    b, t, nh, d = qi.shape
    tq = _pick(t, (256, 128))
    tkc = _pick(t, (512, 256, 128))
    nq = t // tq
    qit = jnp.transpose(qi.astype(BF16).reshape(b, nq, tq, nh, d), (0, 1, 3, 4, 2))
    wit = jnp.transpose(wi.astype(F32).reshape(b, nq, tq, nh), (0, 1, 3, 2))[:, :, :, None, :]
    return pl.pallas_call(
        functools.partial(_dsa_index_body, tq=tq, tkc=tkc, topk=topk, nheads=nh, t=t),
        grid=(b, nq),
        in_specs=[pl.BlockSpec((None, t, d), lambda bi, qi_: (bi, 0, 0)),
                  pl.BlockSpec((None, None, nh, d, tq), lambda bi, qi_: (bi, qi_, 0, 0, 0)),
                  pl.BlockSpec((None, None, nh, 1, tq), lambda bi, qi_: (bi, qi_, 0, 0, 0))],
        out_specs=pl.BlockSpec((None, t, tq), lambda bi, qi_: (bi, 0, qi_)),
        out_shape=jax.ShapeDtypeStruct((b, t, t), BF16),
        scratch_shapes=[pltpu.VMEM((t, tq), I32), pltpu.VMEM((t, tq), BF16), pltpu.VMEM((1, tq), I32)],
        compiler_params=_params(("parallel", "arbitrary")),
        name="dsa_index_topk",
    )(ki.astype(BF16), qit, wit)


def _dsa_attn_body(k_ref, vt_ref, q_ref, bias_ref, o_ref, m_ref, acc_ref, pa_ref, aa_ref, pb_ref, ab_ref,
                   *, tq, tk, rep, d):
    i = pl.program_id(2)
    nch = ((i + 1) * tq + tk - 1) // tk
    m_ref[...] = jnp.full(m_ref.shape, NEG, F32)
    acc_ref[...] = jnp.zeros(acc_ref.shape, F32)
    buf_a = (pa_ref, aa_ref)
    buf_b = (pb_ref, ab_ref)

    def scores(c, buf):
        rows = pl.ds(pl.multiple_of(c * tk, tk), tk)
        kj = k_ref[rows, :]
        bj = bias_ref[rows, :]
        fns = [lambda r=r: lax.dot_general(kj, q_ref[:, r * d:(r + 1) * d], (_NT, ((), ())),
                                           preferred_element_type=F32).astype(BF16) + bj
               for r in range(rep)]
        _score_stage(fns, m_ref, *buf)

    def values(c, buf):
        _value_stage(vt_ref[c], acc_ref, *buf)

    scores(0, buf_a)

    def pair(t, carry):
        c = 2 * t + 1
        scores(c, buf_b)
        values(c - 1, buf_a)
        scores(c + 1, buf_a)
        values(c, buf_b)
        return carry

    lax.fori_loop(0, (nch - 1) // 2, pair, 0)
    last = nch - 1

    @pl.when(last % 2 == 1)
    def _():
        scores(last, buf_b)
        values(last - 1, buf_a)
        values(last, buf_b)

    @pl.when(last % 2 == 0)
    def _():
        values(last, buf_a)

    for r in range(rep):
        acc = acc_ref[r]
        o_ref[:, r * d:(r + 1) * d] = (acc[:d] / acc[d:d + 1]).T.astype(o_ref.dtype)


def dsa_attention(q, k, v, bias):
    b, t, qw = q.shape
    d = C_HEAD_DIM
    g = k.shape[2] // d
    rep = qw // (g * d)
    tq = _pick(t, (256, 128))
    tk = _pick(t, (512, 256, 128))
    nq, nk = t // tq, t // tk
    vt = jnp.transpose(v.astype(BF16).reshape(b, nk, tk, g, d), (0, 3, 1, 4, 2))
    vt = _with_ones_rows(vt)
    dve = d + ONES_ROWS
    group = pl.BlockSpec((None, tq, rep * d), lambda bi, gi, qi: (bi, qi, gi))
    return pl.pallas_call(
        functools.partial(_dsa_attn_body, tq=tq, tk=tk, rep=rep, d=d),
        grid=(b, g, nq),
        in_specs=[pl.BlockSpec((None, t, d), lambda bi, gi, qi: (bi, 0, gi)),
                  pl.BlockSpec((None, None, nk, dve, tk), lambda bi, gi, qi: (bi, gi, 0, 0, 0)),
                  group,
                  pl.BlockSpec((None, t, tq), lambda bi, gi, qi: (bi, 0, qi))],
        out_specs=group,
        out_shape=jax.ShapeDtypeStruct((b, t, qw), BF16),
        scratch_shapes=[pltpu.VMEM((rep, 1, tq), F32), pltpu.VMEM((rep, dve, tq), F32)]
        + [pltpu.VMEM((rep, tk, tq), BF16), pltpu.VMEM((rep, 1, tq), F32)] * 2,
        compiler_params=_params(("parallel", "parallel", "arbitrary")),
        name="dsa_attention",
    )(k.astype(BF16), vt, q.astype(BF16), bias)


LANES = 128


def _rope_tables(pos, head_dim, scale=1.0):
    rot = head_dim // ROPE_FRACTION
    half = rot // 2
    inv_freq = ROPE_THETA ** (-(jnp.arange(half, dtype=F32) * 2.0 / rot))
    ang = pos.astype(F32).reshape(-1, 1) * inv_freq
    cos, sin = jnp.cos(ang), jnp.sin(ang)
    zero = jnp.zeros_like(sin)
    rest = jnp.zeros((ang.shape[0], head_dim - rot), F32)
    tabs = (jnp.concatenate([cos, cos, rest + 1.0], axis=-1),
            jnp.concatenate([-sin, zero, rest], axis=-1),
            jnp.concatenate([zero, sin, rest], axis=-1))
    return tuple(jnp.tile(x * scale, (1, LANES // head_dim)) for x in tabs)


def _rope_body(z_ref, c_ref, up_ref, down_ref, o_ref, *, half, ncols):
    c, up, down = c_ref[...], up_ref[...], down_ref[...]
    for cb in range(ncols // LANES):
        cols = slice(cb * LANES, (cb + 1) * LANES)
        x = z_ref[:, cols]
        y = x * c + pltpu.roll(x, LANES - half, 1) * up + pltpu.roll(x, half, 1) * down
        o_ref[:, cols] = y.astype(o_ref.dtype)


def rope_cast(z, col0, ncols, head_dim, tables):
    m = z.shape[0]
    assert col0 % ncols == 0 and ncols % LANES == 0 and LANES % head_dim == 0
    tm = _pick(m, (512, 256, 128, 64, 32, 16, 8))
    tab = pl.BlockSpec((tm, LANES), lambda i: (i, 0))
    return pl.pallas_call(
        functools.partial(_rope_body, half=head_dim // ROPE_FRACTION // 2, ncols=ncols),
        grid=(m // tm,),
        in_specs=[pl.BlockSpec((tm, ncols), lambda i: (i, col0 // ncols)), tab, tab, tab],
        out_specs=pl.BlockSpec((tm, ncols), lambda i: (i, 0)),
        out_shape=jax.ShapeDtypeStruct((m, ncols), BF16),
        compiler_params=_params(("parallel",)),
        name="rope_cast",
    )(z, *tables)


def _rms_norm(x, g, eps=NORM_EPS):
    xf = x.astype(F32)
    return xf * lax.rsqrt(jnp.mean(xf * xf, axis=-1, keepdims=True) + eps) * g.astype(F32)


def _layer_norm(x, g, b, eps=LN_EPS):
    mu = jnp.mean(x, axis=-1, keepdims=True)
    var = jnp.mean(jnp.square(x - mu), axis=-1, keepdims=True)
    return (x - mu) * lax.rsqrt(var + eps) * g + b


def _partial_rope(x, pos):
    dh = x.shape[-1]
    rot = dh // ROPE_FRACTION
    half = rot // 2
    inv_freq = ROPE_THETA ** (-(jnp.arange(half, dtype=F32) * 2.0 / rot))
    ang = pos.astype(F32)[..., None] * inv_freq
    cos = jnp.cos(ang)[:, :, None, :]
    sin = jnp.sin(ang)[:, :, None, :]
    x1 = x[..., :half]
    x2 = x[..., half:rot]
    return jnp.concatenate([x1 * cos - x2 * sin, x2 * cos + x1 * sin, x[..., rot:]], axis=-1)


def _pad_cols(w, n):
    return jnp.pad(w, ((0, 0), (0, n - w.shape[1])))


def _round_up(n, m):
    return (n + m - 1) // m * m


def _mixer_ab(u, pos, w_in, lq1, lk1, lq2, lk2, subln_g, mu, w0, w2, a0, a2, g2,
              k_k, k_a, r_k, lnx_g, lnx_b, lambda_init):
    b, t, d = u.shape
    a_width = d // 2
    b_width = d // 2
    a_in = 3 * a_width
    n_in = w_in.shape[1]
    z = matmul(u.reshape(b * t, d), _pad_cols(w_in, _round_up(n_in, 512)))

    seq = lambda x: x.reshape(b, t, a_width)
    qa = rope_cast(z, 0, a_width, A_DC, _rope_tables(pos, A_DC, A_DC ** -0.5 * LOG2E))
    ka = rope_cast(z, a_width, a_width, A_DC, _rope_tables(pos, A_DC))
    va = z[:, 2 * a_width:a_in].astype(BF16)
    lam = jnp.exp(jnp.sum(lq1 * lk1)) - jnp.exp(jnp.sum(lq2 * lk2)) + lambda_init
    ya = diff_attention(seq(qa), seq(ka), seq(va), lam, subln_g * (1.0 - lambda_init))

    prep = rwkv_prep(z, a_in, b_width, n_in - a_in - 3 * b_width, t, mu, w0, w2, a0, a2, g2, k_k, k_a)
    yb = wkv7(*(x.reshape(b, t, b_width) for x in prep), r_k.reshape(-1), lnx_g, lnx_b)
    return ya, yb


def _mixer_c(u, pos, w_in, idx_k_g, idx_k_b, topk):
    b, t, d = u.shape
    c_heads = d // C_HEAD_DIM
    n_in = w_in.shape[1]
    c1 = c_heads * C_HEAD_DIM
    c2 = c1 + C_KV_HEADS * C_HEAD_DIM
    c3 = c2 + C_KV_HEADS * C_HEAD_DIM
    c4 = c3 + IDX_HEADS * IDX_DIM
    c5 = c4 + IDX_DIM
    z = matmul(u.reshape(b * t, d), _pad_cols(w_in, _round_up(n_in, 512)))
    seq = lambda x: x.reshape(b, t, x.shape[-1])
    tabs = _rope_tables(pos, C_HEAD_DIM)
    q = rope_cast(z, 0, c1, C_HEAD_DIM, _rope_tables(pos, C_HEAD_DIM, C_HEAD_DIM ** -0.5 * LOG2E))
    k = rope_cast(z, c1, c2 - c1, C_HEAD_DIM, tabs)
    v = z[:, c2:c3].astype(BF16)
    qi = rope_cast(z, c3, c4 - c3, IDX_DIM, _rope_tables(pos, IDX_DIM))
    ki = _partial_rope(_layer_norm(seq(z[:, c4:c5]), idx_k_g, idx_k_b)[:, :, None, :], pos)[:, :, 0]
    wi = seq(z[:, c5:n_in]) * ((IDX_HEADS * IDX_DIM) ** -0.5)
    bias = dsa_select(ki, qi.reshape(b, t, IDX_HEADS, IDX_DIM), wi, topk)
    return (dsa_attention(seq(q), seq(k), seq(v), bias),)


def kernel(x, p, positions, mix_pre_g, mix_post_g, mlp_pre_g, mlp_post_g, w_mlp_up, w_mlp_down, w_ple_proj, w_ple_gate, ple_post_g, ab_w_in, ab_w_out, diff_lq1, diff_lk1, diff_lq2, diff_lk2, diff_subln_g, rwkv_mu, rwkv_w0, rwkv_w2, rwkv_a0, rwkv_a2, rwkv_g2, rwkv_k_k, rwkv_k_a, rwkv_r_k, rwkv_lnx_g, rwkv_lnx_b, c_w_in, c_w_out, idx_k_g, idx_k_b):
    depth = mix_pre_g.shape[0]
    b, t, d = x.shape
    topk = min(TOPK_MAX, t // 4)
    flat = lambda z: z.reshape(b * t, z.shape[-1])
    h = flat(x)
    u = _rms_norm(h, mix_pre_g[0]).astype(BF16)
    for i in range(depth):
        j = i // 2
        u3 = u.reshape(b, t, d)
        if i % 2 == 0:
            lambda_init = 0.8 - 0.6 * math.exp(-0.3 * i)
            m = _mixer_ab(u3, positions, ab_w_in[j], diff_lq1[j], diff_lk1[j], diff_lq2[j],
                          diff_lk2[j], diff_subln_g[j], rwkv_mu[j], rwkv_w0[j], rwkv_w2[j], rwkv_a0[j],
                          rwkv_a2[j], rwkv_g2[j], rwkv_k_k[j], rwkv_k_a[j], rwkv_r_k[j], rwkv_lnx_g[j],
                          rwkv_lnx_b[j], lambda_init)
            w_out = ab_w_out[j]
        else:
            m = _mixer_c(u3, positions, c_w_in[j], idx_k_g[j], idx_k_b[j], topk)
            w_out = c_w_out[j]
        h, u = proj_residual([flat(part) for part in m], w_out, h, mix_post_g[i], mlp_pre_g[i])
        up = matmul(u, w_mlp_up[i], out_dtype=BF16, act="relu2")
        delta = proj_normed(up, w_mlp_down[i], mlp_post_g[i])
        g_next = mix_pre_g[i + 1] if i + 1 < depth else None
        h, u = ple_residual(flat(p[i]), w_ple_proj[i], w_ple_gate[i], h, delta, ple_post_g[i], g_next)
    return h.reshape(b, t, d)
```
